```python
import math
import jax, jax.numpy as jnp
from jax import lax
import numpy as np

D_MODEL = 2048
BATCH = 4
SEQ = 2048
DEPTH = 4
DEC_BATCH = 128
DEC_SEQ = 4
PAST_LEN = 16384
PAGE_SIZE = 128

N_EVEN = (DEPTH + 1) // 2
N_ODD = DEPTH // 2
D_FF = 5632
RW_WIDTH = D_MODEL // 2
RW_HEAD = 64
RW_HEADS = RW_WIDTH // RW_HEAD
D_DECAY_LORA = 96
D_AAA_LORA = 96
D_GATE_LORA = 256
P_RW = 3 * RW_WIDTH + D_DECAY_LORA + D_AAA_LORA + D_GATE_LORA
RW_GN_EPS = 64e-5
HG_WIDTH = D_MODEL // 2
HG_EXPAND = 128
HG_HEADS = HG_WIDTH // HG_EXPAND
HG_VDIM = HG_WIDTH // HG_HEADS
HG_KEY_WIDTH = HG_HEADS * HG_EXPAND
P_HG = 2 * HG_KEY_WIDTH + 2 * HG_WIDTH
P_EVEN = P_RW + P_HG
HG_CHUNK = 64
HG_NORM_EPS = 1e-5
LRU_WIDTH = D_MODEL
LRU_BLOCKS = 8
LRU_BLOCK = LRU_WIDTH // LRU_BLOCKS
CONV_W = 4
LRU_C = 8.0
NORM_EPS = 1e-6

kernel_name = "rwkv7_hgrn2_rglru_macaron_step"


def rmsnorm(x, g, eps=NORM_EPS):
    xf = x.astype(jnp.float32)
    y = xf * lax.rsqrt(jnp.mean(xf * xf, axis=-1, keepdims=True) + eps)
    return (y * g.astype(jnp.float32)).astype(x.dtype)


def swiglu_half(x, norm_g, w_gu, w_down):
    h = rmsnorm(x, norm_g) @ w_gu
    gate, up = jnp.split(h, 2, axis=-1)
    return (jax.nn.silu(gate) * up) @ w_down


def rwkv7_scan(S0, r, w, k, v, kk, b):
    def step(S, inp):
        r_t, w_t, k_t, v_t, kk_t, b_t = inp
        sa = jnp.einsum('bhvk,bhk->bhv', S, -kk_t)
        S = (S * w_t[:, :, None, :] + sa[..., None] * b_t[:, :, None, :]
             + v_t[..., None] * k_t[:, :, None, :])
        return S, jnp.einsum('bhvk,bhk->bhv', S, r_t)
    xs = tuple(jnp.moveaxis(a, 1, 0) for a in (r, w, k, v, kk, b))
    S, o = lax.scan(step, S0, xs)
    return jnp.moveaxis(o, 0, 1), S


def hgrn2_chunked(S0, q, k, v, logf):
    B, T, H, _ = q.shape
    c = math.gcd(T, HG_CHUNK)
    n = T // c

    def to_chunks(a):
        return a.reshape(B, n, c, H, a.shape[-1]).transpose(1, 0, 3, 2, 4)

    mask = jnp.tril(jnp.ones((c, c), dtype=bool))

    def step(S, inp):
        qc, kc, vc, gc = inp
        G = jnp.cumsum(gc, axis=2)
        diff = G[:, :, :, None, :] - G[:, :, None, :, :]
        decay = jnp.exp(jnp.where(mask[:, :, None], diff, -jnp.inf))
        A = jnp.einsum('bhtk,bhsk,bhtsk->bhts', qc, kc, decay)
        o = (jnp.einsum('bhts,bhsv->bhtv', A, vc)
             + jnp.einsum('bhtk,bhkv->bhtv', qc * jnp.exp(G), S))
        G_last = G[:, :, -1:, :]
        S = (jnp.exp(G_last[:, :, 0, :])[..., None] * S
             + jnp.einsum('bhsk,bhsv->bhkv', kc * jnp.exp(G_last - G), vc))
        return S, o

    S, o = lax.scan(step, S0, tuple(to_chunks(a) for a in (q, k, v, logf)))
    o = o.transpose(1, 0, 3, 2, 4).reshape(B, T, H, -1)
    return o, S


def rglru_scan(h0, a, b):
    def comb(x, y):
        a1, b1 = x
        a2, b2 = y
        return a1 * a2, a2 * b1 + b2
    A, Bc = lax.associative_scan(comb, (a, b), axis=1)
    h = A * h0[:, None, :] + Bc
    return h, h[:, -1]


def even_mixer(xn, shift, S_rw, S_hg, p, e):
    f32 = jnp.float32
    B, T, _ = xn.shape
    dt = xn.dtype
    proj = xn @ p['ev_w_in'][e]
    p_rw, p_hg = proj[..., :P_RW], proj[..., P_RW:]
    prev = jnp.concatenate([shift[:, None].astype(dt), p_rw[:, :-1]], axis=1)
    z = p_rw + (prev - p_rw) * p['rw_mu'][e]
    new_shift = p_rw[:, -1]
    idx = [RW_WIDTH, 2 * RW_WIDTH, 3 * RW_WIDTH, 3 * RW_WIDTH + D_DECAY_LORA,
           3 * RW_WIDTH + D_DECAY_LORA + D_AAA_LORA]
    r, k, v, zw, za, zg = jnp.split(z, idx, axis=-1)
    w_log = -jax.nn.softplus(-(p['rw_w0'][e] + jnp.tanh(zw) @ p['rw_w2'][e])) - 0.5
    decay = jnp.exp(-jnp.exp(w_log.astype(f32)))
    a = jax.nn.sigmoid(p['rw_a0'][e] + za @ p['rw_a2'][e])
    g = jax.nn.sigmoid(zg) @ p['rw_g2'][e]

    def heads(t):
        return t.reshape(B, T, RW_HEADS, RW_HEAD).astype(f32)

    r_h, k_h, v_h, a_h, w_h = heads(r), heads(k), heads(v), heads(a), heads(decay)
    kk = k_h * p['rw_k_k'][e].reshape(RW_HEADS, RW_HEAD).astype(f32)
    kk = kk / jnp.maximum(jnp.sqrt(jnp.sum(kk * kk, axis=-1, keepdims=True)), 1e-12)
    k_h = k_h * (1.0 + (a_h - 1.0) * p['rw_k_a'][e].reshape(RW_HEADS, RW_HEAD).astype(f32))
    o, S_rw_new = rwkv7_scan(S_rw.astype(f32), r_h, w_h, k_h, v_h, kk, kk * a_h)
    mu = jnp.mean(o, axis=-1, keepdims=True)
    var = jnp.mean(jnp.square(o - mu), axis=-1, keepdims=True)
    o = ((o - mu) * lax.rsqrt(var + RW_GN_EPS)
         * p['rw_ln_w'][e].reshape(RW_HEADS, RW_HEAD).astype(f32)
         + p['rw_ln_b'][e].reshape(RW_HEADS, RW_HEAD).astype(f32))
    bonus = jnp.sum(r_h * k_h * p['rw_r_k'][e].astype(f32), axis=-1, keepdims=True)
    o = o + bonus * v_h
    o_rw = o.reshape(B, T, RW_WIDTH).astype(dt) * g
    q, f, i, og = jnp.split(p_hg, [HG_KEY_WIDTH, 2 * HG_KEY_WIDTH, 2 * HG_KEY_WIDTH + HG_WIDTH], axis=-1)
    lbs = jnp.cumsum(jax.nn.softmax(p['hg_lb'].astype(f32), axis=0), axis=0)
    lb = (lbs - lbs[0])[e]
    logf = jnp.logaddexp(jnp.log(lb), jnp.log1p(-lb) + jax.nn.log_sigmoid(f.astype(f32)))
    k_hg = -jnp.expm1(logf)
    qh = jax.nn.silu(q.astype(f32)).reshape(B, T, HG_HEADS, HG_EXPAND)
    o_hg, S_hg_new = hgrn2_chunked(
        S_hg.astype(f32), qh, k_hg.reshape(B, T, HG_HEADS, HG_EXPAND),
        i.astype(f32).reshape(B, T, HG_HEADS, HG_VDIM), logf.reshape(B, T, HG_HEADS, HG_EXPAND))
    o_hg = (o_hg * lax.rsqrt(jnp.mean(o_hg * o_hg, axis=-1, keepdims=True) + HG_NORM_EPS)
            * p['hg_norm'][e].reshape(HG_HEADS, HG_VDIM).astype(f32))
    o_hg = o_hg.reshape(B, T, HG_WIDTH).astype(dt) * jax.nn.silu(og)
    y = jnp.concatenate([o_rw, o_hg], axis=-1) @ p['ev_w_out'][e]
    return y, new_shift, S_rw_new, S_hg_new


def odd_mixer(xn, conv_buf, h0, p, o_i):
    f32 = jnp.float32
    B, T, _ = xn.shape
    dt = xn.dtype
    proj = xn @ p['od_w_in'][o_i]
    gate, xb = jnp.split(proj, 2, axis=-1)
    xc = jnp.concatenate([conv_buf.astype(dt), xb], axis=1)
    w = p['conv_w'][o_i]
    xconv = p['conv_b'][o_i] + sum(xc[:, j:j + T] * w[j] for j in range(CONV_W))
    new_buf = xc[:, -(CONV_W - 1):]
    blocks = xconv.reshape(B, T, LRU_BLOCKS, LRU_BLOCK)
    ga = jnp.einsum('btnj,njk->btnk', blocks, p['lru_wa'][o_i]).reshape(B, T, LRU_WIDTH) + p['lru_ba'][o_i]
    gx = jnp.einsum('btnj,njk->btnk', blocks, p['lru_wx'][o_i]).reshape(B, T, LRU_WIDTH) + p['lru_bx'][o_i]
    log_a = (-LRU_C * jax.nn.sigmoid(ga.astype(f32))
             * jax.nn.softplus(-p['lru_lambda'][o_i].astype(f32)))
    a = jnp.exp(log_a)
    mult = jnp.sqrt(-jnp.expm1(2.0 * log_a))
    b = mult * jax.nn.sigmoid(gx.astype(f32)) * xconv.astype(f32)
    h, h_last = rglru_scan(h0.astype(f32), a, b)
    y = (h.astype(dt) * jax.nn.gelu(gate)) @ p['od_w_out'][o_i]
    return y, new_buf, h_last


def trunk(x, shift, S_rw, S_hg, conv_buf, h_lru, p):
    shifts, rws, hgs, convs, lrus = [], [], [], [], []
    for l in range(DEPTH):
        x = x + 0.5 * swiglu_half(x, p['ffn1_norm'][l], p['ffn1_w_gu'][l], p['ffn1_w_down'][l])
        xn = rmsnorm(x, p['mix_norm'][l])
        if l % 2 == 0:
            e = l // 2
            y, s_new, rw_new, hg_new = even_mixer(xn, shift[:, e], S_rw[:, e], S_hg[:, e], p, e)
            shifts.append(s_new)
            rws.append(rw_new)
            hgs.append(hg_new)
        else:
            o_i = l // 2
            y, c_new, h_new = odd_mixer(xn, conv_buf[:, o_i], h_lru[:, o_i], p, o_i)
            convs.append(c_new)
            lrus.append(h_new)
        x = x + y
        x = x + 0.5 * swiglu_half(x, p['ffn2_norm'][l], p['ffn2_w_gu'][l], p['ffn2_w_down'][l])
    y = rmsnorm(x, p['final_norm'])
    return (y,
            jnp.stack(shifts, axis=1).astype(shift.dtype),
            jnp.stack(rws, axis=1).astype(S_rw.dtype),
            jnp.stack(hgs, axis=1).astype(S_hg.dtype),
            jnp.stack(convs, axis=1).astype(conv_buf.dtype),
            jnp.stack(lrus, axis=1).astype(h_lru.dtype))


def setup_inputs(seed: int = 0) -> dict:
    key = jax.random.key(seed)
    ks = iter(jax.random.split(key, 48))

    def nrm(shape, scale):
        return jax.random.normal(next(ks), shape, jnp.float32) * scale

    def gain(shape):
        return 1.0 + nrm(shape, 0.01)

    a0 = jax.random.uniform(next(ks), (N_ODD, LRU_WIDTH), jnp.float32, 0.9, 0.999)
    s = a0 ** (1.0 / LRU_C)
    lru_lambda = jnp.log(s) - jnp.log1p(-s)
    return {
        'x_prompt': nrm((BATCH, SEQ, D_MODEL), 1.0),
        'x_sample': nrm((DEC_BATCH, DEC_SEQ, D_MODEL), 1.0),
        'state_rwkv_shift': nrm((DEC_BATCH, N_EVEN, P_RW), 1.0),
        'state_rwkv': nrm((DEC_BATCH, N_EVEN, RW_HEADS, RW_HEAD, RW_HEAD), 0.3),
        'state_hgrn': nrm((DEC_BATCH, N_EVEN, HG_HEADS, HG_EXPAND, HG_VDIM), 0.3),
        'state_conv': nrm((DEC_BATCH, N_ODD, CONV_W - 1, LRU_WIDTH), 1.0),
        'state_lru': nrm((DEC_BATCH, N_ODD, LRU_WIDTH), 0.5),
        'ffn1_norm': gain((DEPTH, D_MODEL)),
        'ffn1_w_gu': nrm((DEPTH, D_MODEL, 2 * D_FF), D_MODEL ** -0.5),
        'ffn1_w_down': nrm((DEPTH, D_FF, D_MODEL), D_FF ** -0.5),
        'mix_norm': gain((DEPTH, D_MODEL)),
        'ffn2_norm': gain((DEPTH, D_MODEL)),
        'ffn2_w_gu': nrm((DEPTH, D_MODEL, 2 * D_FF), D_MODEL ** -0.5),
        'ffn2_w_down': nrm((DEPTH, D_FF, D_MODEL), D_FF ** -0.5),
        'ev_w_in': nrm((N_EVEN, D_MODEL, P_EVEN), D_MODEL ** -0.5),
        'rw_mu': jax.random.uniform(next(ks), (N_EVEN, P_RW), jnp.float32),
        'rw_w0': jax.random.uniform(next(ks), (N_EVEN, RW_WIDTH), jnp.float32, -6.0, 1.0),
        'rw_w2': nrm((N_EVEN, D_DECAY_LORA, RW_WIDTH), 0.5 * D_DECAY_LORA ** -0.5),
        'rw_a0': nrm((N_EVEN, RW_WIDTH), 0.1),
        'rw_a2': nrm((N_EVEN, D_AAA_LORA, RW_WIDTH), D_AAA_LORA ** -0.5),
        'rw_g2': nrm((N_EVEN, D_GATE_LORA, RW_WIDTH), D_GATE_LORA ** -0.5),
        'rw_k_k': 0.85 + nrm((N_EVEN, RW_WIDTH), 0.05),
        'rw_k_a': 1.0 + nrm((N_EVEN, RW_WIDTH), 0.05),
        'rw_r_k': nrm((N_EVEN, RW_HEADS, RW_HEAD), 0.1),
        'rw_ln_w': gain((N_EVEN, RW_WIDTH)),
        'rw_ln_b': nrm((N_EVEN, RW_WIDTH), 0.01),
        'hg_lb': nrm((N_EVEN, HG_KEY_WIDTH), 0.5),
        'hg_norm': gain((N_EVEN, HG_WIDTH)),
        'ev_w_out': nrm((N_EVEN, RW_WIDTH + HG_WIDTH, D_MODEL), (RW_WIDTH + HG_WIDTH) ** -0.5),
        'od_w_in': nrm((N_ODD, D_MODEL, 2 * LRU_WIDTH), D_MODEL ** -0.5),
        'conv_w': nrm((N_ODD, CONV_W, LRU_WIDTH), CONV_W ** -0.5),
        'conv_b': nrm((N_ODD, LRU_WIDTH), 0.01),
        'lru_wa': nrm((N_ODD, LRU_BLOCKS, LRU_BLOCK, LRU_BLOCK), LRU_BLOCK ** -0.5),
        'lru_ba': nrm((N_ODD, LRU_WIDTH), 0.01),
        'lru_wx': nrm((N_ODD, LRU_BLOCKS, LRU_BLOCK, LRU_BLOCK), LRU_BLOCK ** -0.5),
        'lru_bx': nrm((N_ODD, LRU_WIDTH), 0.01),
        'lru_lambda': lru_lambda,
        'od_w_out': nrm((N_ODD, LRU_WIDTH, D_MODEL), LRU_WIDTH ** -0.5),
        'final_norm': gain((D_MODEL,)),
    }


def reference(x_prompt, x_sample, state_rwkv_shift, state_rwkv, state_hgrn, state_conv, state_lru,
              ffn1_norm, ffn1_w_gu, ffn1_w_down, mix_norm, ffn2_norm, ffn2_w_gu, ffn2_w_down,
              ev_w_in, rw_mu, rw_w0, rw_w2, rw_a0, rw_a2, rw_g2, rw_k_k, rw_k_a, rw_r_k,
              rw_ln_w, rw_ln_b, hg_lb, hg_norm, ev_w_out,
              od_w_in, conv_w, conv_b, lru_wa, lru_ba, lru_wx, lru_bx, lru_lambda, od_w_out,
              final_norm):
    p = dict(ffn1_norm=ffn1_norm, ffn1_w_gu=ffn1_w_gu, ffn1_w_down=ffn1_w_down, mix_norm=mix_norm,
             ffn2_norm=ffn2_norm, ffn2_w_gu=ffn2_w_gu, ffn2_w_down=ffn2_w_down,
             ev_w_in=ev_w_in, rw_mu=rw_mu, rw_w0=rw_w0, rw_w2=rw_w2, rw_a0=rw_a0, rw_a2=rw_a2,
             rw_g2=rw_g2, rw_k_k=rw_k_k, rw_k_a=rw_k_a, rw_r_k=rw_r_k, rw_ln_w=rw_ln_w,
             rw_ln_b=rw_ln_b, hg_lb=hg_lb, hg_norm=hg_norm, ev_w_out=ev_w_out,
             od_w_in=od_w_in, conv_w=conv_w, conv_b=conv_b, lru_wa=lru_wa, lru_ba=lru_ba,
             lru_wx=lru_wx, lru_bx=lru_bx, lru_lambda=lru_lambda, od_w_out=od_w_out,
             final_norm=final_norm)
    Bp = x_prompt.shape[0]
    dt = x_prompt.dtype
    z_shift = jnp.zeros((Bp,) + state_rwkv_shift.shape[1:], dt)
    z_rw = jnp.zeros((Bp,) + state_rwkv.shape[1:], dt)
    z_hg = jnp.zeros((Bp,) + state_hgrn.shape[1:], dt)
    z_conv = jnp.zeros((Bp,) + state_conv.shape[1:], dt)
    z_lru = jnp.zeros((Bp,) + state_lru.shape[1:], dt)
    y_prompt, p_shift, p_rwkv, p_hgrn, p_conv, p_lru = trunk(
        x_prompt, z_shift, z_rw, z_hg, z_conv, z_lru, p)
    y_sample, s_shift, s_rwkv, s_hgrn, s_conv, s_lru = trunk(
        x_sample, state_rwkv_shift, state_rwkv, state_hgrn, state_conv, state_lru, p)
    return (y_prompt, y_sample, p_shift, p_rwkv, p_hgrn, p_conv, p_lru,
            s_shift, s_rwkv, s_hgrn, s_conv, s_lru)
```

```python
import functools

import jax
import jax.numpy as jnp
from jax import lax
from jax.experimental import pallas as pl
from jax.experimental.pallas import tpu as pltpu

F32 = jnp.float32
BF16 = jnp.bfloat16

NORM_EPS = 1e-6
RW_GN_EPS = 64e-5
HG_NORM_EPS = 1e-5
LRU_C = 8.0
RW_HEAD = 64
HG_DIM = 128
LRU_BLOCKS = 8
CONV_W = 4
LANES = 128
SUBLANES = 8
VMEM_LIMIT = 56 * 1024 * 1024
MAX_BM = 1088


def _cparams(*sem):
    return pltpu.CompilerParams(dimension_semantics=sem, vmem_limit_bytes=VMEM_LIMIT)


def _pick_block(n, cap, mult):
    best = None
    for d in range(mult, min(n, cap) + 1, mult):
        if n % d == 0:
            best = d
    assert best is not None, (n, cap, mult)
    return best


def _dot(a, b):
    return jnp.dot(a, b, preferred_element_type=F32)


def _dot_nt(a, b):
    return lax.dot_general(a, b, (((1,), (1,)), ((), ())), preferred_element_type=F32)


def _dot_tn(a, b):
    return lax.dot_general(a, b, (((0,), (0,)), ((), ())), preferred_element_type=F32)


def _softplus(x):
    return jnp.maximum(x, 0.0) + jnp.log1p(jnp.exp(-jnp.abs(x)))


def _neg_expm1(y):
    return -jnp.tanh(0.5 * y) * (jnp.exp(y) + 1.0)


def _rms_rows(x, g):
    ms = jnp.mean(x * x, axis=-1, keepdims=True)
    return x * lax.rsqrt(ms + NORM_EPS) * g


def _ffn_body(x_ref, g_ref, wg_ref, wu_ref, wd_ref, o_ref, xn_ref):
    @pl.when(pl.program_id(1) == 0)
    def _():
        x = x_ref[...]
        xn_ref[...] = _rms_rows(x, g_ref[...]).astype(BF16)
        o_ref[...] = x

    xn = xn_ref[...]
    gate = _dot(xn, wg_ref[...])
    up = _dot(xn, wu_ref[...])
    act = (0.5 * (gate * jax.nn.sigmoid(gate)) * up).astype(BF16)
    o_ref[...] += _dot(act, wd_ref[...])


def _ffn(x, norm_g, w_gu, w_down, layer):
    m, d = x.shape
    f = w_down.shape[1]
    bm = _pick_block(m, MAX_BM, 16)
    bf = _pick_block(f, 512, LANES)
    nf = f // bf
    return pl.pallas_call(
        _ffn_body,
        grid=(m // bm, nf),
        in_specs=[
            pl.BlockSpec((bm, d), lambda i, j: (i, 0), pipeline_mode=pl.Buffered(1)),
            pl.BlockSpec((None, 1, d), lambda i, j: (layer, 0, 0)),
            pl.BlockSpec((None, d, bf), lambda i, j: (layer, 0, j)),
            pl.BlockSpec((None, d, bf), lambda i, j: (layer, 0, j + nf)),
            pl.BlockSpec((None, bf, d), lambda i, j: (layer, j, 0)),
        ],
        out_specs=pl.BlockSpec((bm, d), lambda i, j: (i, 0)),
        out_shape=jax.ShapeDtypeStruct((m, d), F32),
        scratch_shapes=[pltpu.VMEM((bm, d), BF16)],
        compiler_params=_cparams("parallel", "arbitrary"),
        name="ffn",
    )(x, norm_g[:, None, :], w_gu, w_gu, w_down)


def _norm_proj_body(x_ref, g_ref, w_ref, o_ref, xn_ref):
    @pl.when(pl.program_id(1) == 0)
    def _():
        xn_ref[...] = _rms_rows(x_ref[...], g_ref[...]).astype(BF16)

    o_ref[...] = _dot(xn_ref[...], w_ref[...])


def _norm_proj(x, norm_g, layer, w, bn):
    m, d = x.shape
    n = w.shape[-1]
    bm = _pick_block(m, MAX_BM, 16)
    return pl.pallas_call(
        _norm_proj_body,
        grid=(m // bm, n // bn),
        in_specs=[
            pl.BlockSpec((bm, d), lambda i, j: (i, 0)),
            pl.BlockSpec((None, 1, d), lambda i, j: (layer, 0, 0)),
            pl.BlockSpec((d, bn), lambda i, j: (0, j)),
        ],
        out_specs=pl.BlockSpec((bm, bn), lambda i, j: (i, j)),
        out_shape=jax.ShapeDtypeStruct((m, n), F32),
        scratch_shapes=[pltpu.VMEM((bm, d), BF16)],
        compiler_params=_cparams("parallel", "arbitrary"),
        name="norm_proj",
    )(x, norm_g[:, None, :], w)


def _proj_res_body(x_ref, y_ref, w_ref, o_ref):
    o_ref[...] = x_ref[...] + _dot(y_ref[...], w_ref[...])


def _proj_res(x, y, w):
    m, d = x.shape
    k = y.shape[1]
    bm = _pick_block(m, MAX_BM, 16)
    bn = _pick_block(d, 1024, LANES)
    return pl.pallas_call(
        _proj_res_body,
        grid=(m // bm, d // bn),
        in_specs=[
            pl.BlockSpec((bm, bn), lambda i, j: (i, j)),
            pl.BlockSpec((bm, k), lambda i, j: (i, 0)),
            pl.BlockSpec((k, bn), lambda i, j: (0, j)),
        ],
        out_specs=pl.BlockSpec((bm, bn), lambda i, j: (i, j)),
        out_shape=jax.ShapeDtypeStruct((m, d), F32),
        compiler_params=_cparams("parallel", "arbitrary"),
        name="proj_res",
    )(x, y, w)


def _proj_res_even_body(x_ref, orw_ref, g_ref, ohg_ref, w1_ref, w2_ref, o_ref):
    y1 = (orw_ref[...] * g_ref[...]).astype(BF16)
    o_ref[...] = x_ref[...] + _dot(y1, w1_ref[...]) + _dot(ohg_ref[...], w2_ref[...])


def _proj_res_even(x, o_rw, g, o_hg, w):
    m, d = x.shape
    rw = o_rw.shape[1]
    hg = o_hg.shape[1]
    assert rw == hg
    bm = _pick_block(m, MAX_BM, 16)
    bn = _pick_block(d, 1024, LANES)
    return pl.pallas_call(
        _proj_res_even_body,
        grid=(m // bm, d // bn),
        in_specs=[
            pl.BlockSpec((bm, bn), lambda i, j: (i, j)),
            pl.BlockSpec((bm, rw), lambda i, j: (i, 0)),
            pl.BlockSpec((bm, rw), lambda i, j: (i, 0)),
            pl.BlockSpec((bm, hg), lambda i, j: (i, 0)),
            pl.BlockSpec((rw, bn), lambda i, j: (0, j)),
            pl.BlockSpec((hg, bn), lambda i, j: (1, j)),
        ],
        out_specs=pl.BlockSpec((bm, bn), lambda i, j: (i, j)),
        out_shape=jax.ShapeDtypeStruct((m, d), F32),
        compiler_params=_cparams("parallel", "arbitrary"),
        name="proj_res_even",
    )(x, o_rw, g, o_hg, w, w)


def _final_norm_body(x_ref, g_ref, o_ref):
    o_ref[...] = _rms_rows(x_ref[...], g_ref[...])


def _final_norm(x, g):
    m, d = x.shape
    bm = _pick_block(m, MAX_BM, 16)
    return pl.pallas_call(
        _final_norm_body,
        grid=(m // bm,),
        in_specs=[pl.BlockSpec((bm, d), lambda i: (i, 0)), pl.BlockSpec((1, d), lambda i: (0, 0))],
        out_specs=pl.BlockSpec((bm, d), lambda i: (i, 0)),
        out_shape=jax.ShapeDtypeStruct((m, d), F32),
        compiler_params=_cparams("parallel"),
        name="final_norm",
    )(x, g[None, :])


def _rwkv_prep_body(cur_ref, prevblk_ref, shiftp_ref, shifts_ref, mu_ref, w0_ref, w2_ref, a0_ref, a2_ref,
                    g2_ref, r_ref, k_ref, v_ref, w_ref, a_ref, g_ref, *, rw, dwp, dap, n_prompt_blk, blk_per_seq):
    i = pl.program_id(0)
    cur = cur_ref[...]
    prevblk = prevblk_ref[...]
    rows = cur.shape[0]
    seq = jnp.minimum(i // blk_per_seq, shiftp_ref.shape[0] - 1)
    first = (i % blk_per_seq) == 0
    row0 = jnp.where(first, shiftp_ref[pl.ds(seq, 1), :], prevblk[rows - 1:rows, :])
    rid = lax.broadcasted_iota(jnp.int32, (rows, 1), 0)
    prev_p = jnp.where(rid == 0, row0, pltpu.roll(cur, 1, 0))
    prev_s = jnp.where(i == n_prompt_blk, shifts_ref[...], prevblk)
    prev = jnp.where(i >= n_prompt_blk, prev_s, prev_p)

    z = cur + (prev - cur) * mu_ref[...]
    r_ref[...] = z[:, :rw]
    k_ref[...] = z[:, rw:2 * rw]
    v_ref[...] = z[:, 2 * rw:3 * rw]
    o = 3 * rw
    zw = z[:, o:o + dwp]
    za = z[:, o + dwp:o + dwp + dap]
    zg = z[:, o + dwp + dap:]
    lw = w0_ref[...] + _dot(jnp.tanh(zw).astype(BF16), w2_ref[...])
    w_log = -_softplus(-lw) - 0.5
    w_ref[...] = jnp.exp(-jnp.exp(w_log))
    a_ref[...] = jax.nn.sigmoid(a0_ref[...] + _dot(za.astype(BF16), a2_ref[...]))
    g_ref[...] = _dot(jax.nn.sigmoid(zg).astype(BF16), g2_ref[...])


def _rwkv_prep(p_rw, shift_p, shift_s, mu, w0, w2, a0, a2, g2, *, rw, mp, tp, bs):
    m, prw = p_rw.shape
    assert mp % bs == 0 and tp % bs == 0 and (m - mp) % bs == 0
    dwp, dap = w2.shape[0], a2.shape[0]
    full = lambda a: pl.BlockSpec(a.shape, lambda i: (0,) * a.ndim)
    out_spec = pl.BlockSpec((bs, rw), lambda i: (i, 0))
    body = functools.partial(_rwkv_prep_body, rw=rw, dwp=dwp, dap=dap, n_prompt_blk=mp // bs,
                             blk_per_seq=tp // bs)
    return pl.pallas_call(
        body,
        grid=(m // bs,),
        in_specs=[
            pl.BlockSpec((bs, prw), lambda i: (i, 0)),
            pl.BlockSpec((bs, prw), lambda i: (jnp.maximum(i - 1, 0), 0)),
            full(shift_p), full(shift_s), full(mu), full(w0), full(w2), full(a0), full(a2), full(g2),
        ],
        out_specs=[out_spec] * 6,
        out_shape=[jax.ShapeDtypeStruct((m, rw), F32)] * 6,
        compiler_params=_cparams("arbitrary"),
        name="rwkv_prep",
    )(p_rw, p_rw, shift_p, shift_s, mu, w0, w2, a0, a2, g2)


def _rwkv_scan_body(r_ref, w_ref, k_ref, a_ref, v_ref, kk_p_ref, ka_p_ref, rk_p_ref, lnw_ref, lnb_ref, s0_ref,
                    o_ref, s_ref, nkk_s, b_s, km_s, *, vsplit):
    tc_len, nv, lanes = v_ref.shape

    @pl.when(pl.program_id(1) == 0)
    def _():
        s_ref[...] = s0_ref[...]

    kvec = k_ref[...]
    a_ = a_ref[...]
    kk = kvec * kk_p_ref[...][None]
    nrm = jnp.sqrt(jnp.sum(kk * kk, axis=1, keepdims=True))
    kk = kk / jnp.maximum(nrm, 1e-12)
    nkk_s[...] = -kk
    b_s[...] = kk * a_
    km_s[...] = kvec * (1.0 + (a_ - 1.0) * ka_p_ref[...][None])

    def step(t, carry):
        nkk = nkk_s[t]
        b = b_s[t]
        km = km_s[t]
        w = w_ref[t]
        r = r_ref[t]
        vt = v_ref[t]
        for vi in range(nv):
            sv = s_ref[vi]
            sa = jnp.sum(sv * nkk, axis=0, keepdims=True)
            sv = sv * w + sa * b + vt[vi:vi + 1, :] * km
            s_ref[vi] = sv
            o_ref[t, pl.ds(vi, 1), :] = jnp.sum(sv * r, axis=0, keepdims=True)
        return carry

    lax.fori_loop(0, tc_len, step, 0)

    o = o_ref[...]
    n_val = nv * (2 if vsplit else 1)

    def head_sum(x):
        if vsplit:
            x2 = x.reshape(tc_len * nv, lanes)
            x = (x2 + pltpu.roll(x2, lanes // 2, 1)).reshape(tc_len, nv, lanes)
        return jnp.sum(x, axis=1, keepdims=True)

    mu = head_sum(o) / n_val
    d = o - mu
    var = head_sum(d * d) / n_val
    o = d * lax.rsqrt(var + RW_GN_EPS) * lnw_ref[...][None] + lnb_ref[...][None]
    bonus = jnp.sum(r_ref[...] * km_s[...] * rk_p_ref[...][None], axis=1, keepdims=True)
    o_ref[...] = o + bonus * v_ref[...]


def _rwkv_scan(r, w, k, a, v, kk_p, ka_p, rk_p, lnw, lnb, s0, *, tc, vsplit):
    t, kc, ln = r.shape
    nv = v.shape[1]
    g = ln // LANES
    kspec = pl.BlockSpec((tc, kc, LANES), lambda gi, ti: (ti, 0, gi))
    vspec = pl.BlockSpec((tc, nv, LANES), lambda gi, ti: (ti, 0, gi))
    pk = pl.BlockSpec((kc, LANES), lambda gi, ti: (0, 0))
    pv = pl.BlockSpec((nv, LANES), lambda gi, ti: (0, 0))
    sspec = pl.BlockSpec((nv, kc, LANES), lambda gi, ti: (0, 0, gi))
    return pl.pallas_call(
        functools.partial(_rwkv_scan_body, vsplit=vsplit),
        grid=(g, t // tc),
        in_specs=[kspec, kspec, kspec, kspec, vspec, pk, pk, pk, pv, pv, sspec],
        out_specs=[vspec, sspec],
        out_shape=[jax.ShapeDtypeStruct((t, nv, ln), F32), jax.ShapeDtypeStruct((nv, kc, ln), F32)],
        scratch_shapes=[pltpu.VMEM((tc, kc, LANES), F32)] * 3,
        compiler_params=_cparams("parallel", "arbitrary"),
        name="rwkv_scan",
    )(r, w, k, a, v, kk_p, ka_p, rk_p, lnw, lnb, s0)


def _split3(x):
    x1 = x.astype(BF16)
    r1 = x - x1.astype(F32)
    x2 = r1.astype(BF16)
    x3 = (r1 - x2.astype(F32)).astype(BF16)
    return x1, x2, x3


def _hgrn_chunk(q, f, val, og, st_list, loglb, log1mlb, nrm_w, seq_len):
    c = q.shape[0]
    nseq = c // seq_len
    lg = seq_len.bit_length() - 1
    assert (1 << lg) == seq_len
    ls = jnp.minimum(f, 0.0) - jnp.log1p(jnp.exp(-jnp.abs(f)))
    b_ = log1mlb + ls
    g = jnp.maximum(loglb, b_) + jnp.log1p(jnp.exp(-jnp.abs(loglb - b_)))
    kk = _neg_expm1(g)
    qs = q * jax.nn.sigmoid(q)
    row = lax.broadcasted_iota(jnp.int32, (c, 1), 0)
    col = lax.broadcasted_iota(jnp.int32, (1, c), 1)
    tpos = row & (seq_len - 1)
    gc = g
    for j in range(lg):
        s = 1 << j
        gc = gc + jnp.where(tpos >= s, pltpu.roll(gc, s, 0), 0.0)
    g1, g2, g3 = _split3(gc)
    amat = jnp.where(row == col, _dot_nt(qs.astype(BF16), kk.astype(BF16)), 0.0)
    for j in range(lg):
        ref_idx = ((row >> (j + 1)) << (j + 1)) + ((1 << j) - 1)
        sel = jnp.where(col == ref_idx, 1.0, 0.0).astype(BF16)
        gref = _dot(sel, g1) + _dot(sel, g2) + _dot(sel, g3)
        e = jnp.exp(-jnp.abs(gc - gref))
        is_q = ((row >> j) & 1) == 1
        qb = jnp.where(is_q, qs * e, 0.0).astype(BF16)
        kb = jnp.where(is_q, 0.0, kk * e).astype(BF16)
        same = (row >> (j + 1)) == (col >> (j + 1))
        amat = amat + jnp.where(same, _dot_nt(qb, kb), 0.0)
    vb = val.astype(BF16)
    o = _dot(amat.astype(BF16), vb)
    qg = (qs * jnp.exp(gc)).astype(BF16)
    new_st = []
    for s_i in range(nseq):
        st = st_list[s_i]
        o_int = _dot_nt(qg, st.astype(BF16))
        glast = gc[s_i * seq_len + seq_len - 1:s_i * seq_len + seq_len, :]
        if nseq == 1:
            o = o + o_int
            k2 = kk * jnp.exp(glast - gc)
        else:
            in_s = (row >> lg) == s_i
            o = o + jnp.where(in_s, o_int, 0.0)
            k2 = jnp.where(in_s, kk * jnp.exp(jnp.minimum(glast - gc, 0.0)), 0.0)
        new_st.append(st * jnp.exp(glast) + _dot_tn(vb, k2.astype(BF16)))
    on = o * lax.rsqrt(jnp.mean(o * o, axis=-1, keepdims=True) + HG_NORM_EPS) * nrm_w
    y = on * (og * jax.nn.sigmoid(og))
    return y, new_st


def _hgrn_prompt_body(q_ref, f_ref, i_ref, og_ref, s0_ref, loglb_ref, log1m_ref, nw_ref, o_ref, sout_ref, st_ref,
                      *, chunk):
    nh = st_ref.shape[0]
    tb = pl.program_id(1)

    @pl.when(tb == 0)
    def _():
        for h in range(nh):
            st_ref[h] = s0_ref[h].T

    def body(ci, carry):
        r0 = pl.multiple_of(ci * chunk, chunk)
        for h in range(nh):
            cs = slice(h * HG_DIM, (h + 1) * HG_DIM)
            y, new_st = _hgrn_chunk(q_ref[pl.ds(r0, chunk), cs], f_ref[pl.ds(r0, chunk), cs],
                                    i_ref[pl.ds(r0, chunk), cs], og_ref[pl.ds(r0, chunk), cs],
                                    [st_ref[h]], loglb_ref[:, cs], log1m_ref[:, cs], nw_ref[:, cs], chunk)
            st_ref[h] = new_st[0]
            o_ref[pl.ds(r0, chunk), cs] = y.astype(BF16)
        return carry

    lax.fori_loop(0, q_ref.shape[0] // chunk, body, 0)

    @pl.when(tb == pl.num_programs(1) - 1)
    def _():
        for h in range(nh):
            sout_ref[h] = st_ref[h].T


def _hgrn_prompt(p_hg, s0, loglb, log1m, nw, *, bp, tp, chunk):
    hgw = p_hg.shape[1] // 4
    nh = hgw // HG_DIM
    tb = _pick_block(tp, 512, chunk)
    nt = tp // tb
    cspec = lambda c: pl.BlockSpec((tb, hgw), lambda b, t: (b * nt + t, c))
    pspec = pl.BlockSpec((1, hgw), lambda b, t: (0, 0))
    sspec = pl.BlockSpec((None, nh, HG_DIM, HG_DIM), lambda b, t: (b, 0, 0, 0))
    return pl.pallas_call(
        functools.partial(_hgrn_prompt_body, chunk=chunk),
        grid=(bp, nt),
        in_specs=[cspec(0), cspec(1), cspec(2), cspec(3), sspec, pspec, pspec, pspec],
        out_specs=[pl.BlockSpec((tb, hgw), lambda b, t: (b * nt + t, 0)), sspec],
        out_shape=[jax.ShapeDtypeStruct((bp * tp, hgw), BF16), jax.ShapeDtypeStruct(s0.shape, F32)],
        scratch_shapes=[pltpu.VMEM((nh, HG_DIM, HG_DIM), F32)],
        compiler_params=_cparams("parallel", "arbitrary"),
        name="hgrn_prompt",
    )(p_hg, p_hg, p_hg, p_hg, s0, loglb, log1m, nw)


def _hgrn_sample_body(q_ref, f_ref, i_ref, og_ref, s0_ref, loglb_ref, log1m_ref, nw_ref, o_ref, sout_ref,
                      *, seq_len, nseq):
    nh = s0_ref.shape[1]
    c = seq_len * nseq

    def body(gi, carry):
        r0 = pl.multiple_of(gi * c, c)
        for h in range(nh):
            cs = slice(h * HG_DIM, (h + 1) * HG_DIM)
            sts = [s0_ref[gi * nseq + s_i, h].T for s_i in range(nseq)]
            y, new_st = _hgrn_chunk(q_ref[pl.ds(r0, c), cs], f_ref[pl.ds(r0, c), cs], i_ref[pl.ds(r0, c), cs],
                                    og_ref[pl.ds(r0, c), cs], sts, loglb_ref[:, cs], log1m_ref[:, cs],
                                    nw_ref[:, cs], seq_len)
            for s_i in range(nseq):
                sout_ref[gi * nseq + s_i, h] = new_st[s_i].T
            o_ref[pl.ds(r0, c), cs] = y.astype(BF16)
        return carry

    lax.fori_loop(0, q_ref.shape[0] // c, body, 0)


def _hgrn_sample(p_hg_bm, s0, layer, loglb, log1m, nw, *, bs, ts):
    hgw = p_hg_bm.shape[1] // 4
    nh = hgw // HG_DIM
    nseq = max(1, 16 // ts)
    assert bs % nseq == 0
    bb = _pick_block(bs, 8, nseq)
    rows = bb * ts
    cspec = lambda c: pl.BlockSpec((rows, hgw), lambda i: (i, c))
    pspec = pl.BlockSpec((1, hgw), lambda i: (0, 0))
    return pl.pallas_call(
        functools.partial(_hgrn_sample_body, seq_len=ts, nseq=nseq),
        grid=(bs // bb,),
        in_specs=[cspec(0), cspec(1), cspec(2), cspec(3),
                  pl.BlockSpec((bb, None, nh, HG_DIM, HG_DIM), lambda i: (i, layer, 0, 0, 0)),
                  pspec, pspec, pspec],
        out_specs=[pl.BlockSpec((rows, hgw), lambda i: (i, 0)),
                   pl.BlockSpec((bb, nh, HG_DIM, HG_DIM), lambda i: (i, 0, 0, 0))],
        out_shape=[jax.ShapeDtypeStruct((bs * ts, hgw), BF16),
                   jax.ShapeDtypeStruct((bs, nh, HG_DIM, HG_DIM), F32)],
        compiler_params=_cparams("parallel"),
        name="hgrn_sample",
    )(p_hg_bm, p_hg_bm, p_hg_bm, p_hg_bm, s0, loglb, log1m, nw)


def _lru_gates(xconv, wa_ref, wx_ref, ba, bx, lam):
    l = xconv.shape[1]
    blk = l // LRU_BLOCKS
    ga, gx = [], []
    for n in range(LRU_BLOCKS):
        xb = xconv[:, n * blk:(n + 1) * blk].astype(BF16)
        ga.append(_dot(xb, wa_ref[n]))
        gx.append(_dot(xb, wx_ref[n]))
    ga = jnp.concatenate(ga, axis=1) + ba
    gx = jnp.concatenate(gx, axis=1) + bx
    log_a = -LRU_C * jax.nn.sigmoid(ga) * _softplus(-lam)
    a = jnp.exp(log_a)
    mult = jnp.sqrt(_neg_expm1(2.0 * log_a))
    return a, mult * jax.nn.sigmoid(gx) * xconv


def _gelu_tanh(x):
    c = 0.7978845608028654
    return 0.5 * x * (1.0 + jnp.tanh(c * (x + 0.044715 * (x * x * x))))


def _odd_prompt_body(gate_ref, xb_ref, conv0_ref, h0_ref, cw_ref, cb_ref, wa_ref, wx_ref, ba_ref, bx_ref, lam_ref,
                     y_ref, convo_ref, ho_ref, ext_ref, hc_ref, a_s, b_s, h_s):
    rows = xb_ref.shape[0]
    hist = SUBLANES

    @pl.when(pl.program_id(1) == 0)
    def _():
        ext_ref[0:hist, :] = conv0_ref[...]
        hc_ref[...] = h0_ref[...]

    ext_ref[hist:hist + rows, :] = xb_ref[...]
    cw = cw_ref[...]
    xconv = cw[0:1, :] * ext_ref[pl.ds(hist - (CONV_W - 1), rows), :]
    for j in range(1, CONV_W):
        xconv = xconv + cw[j:j + 1, :] * ext_ref[pl.ds(hist - (CONV_W - 1) + j, rows), :]
    xconv = cb_ref[...] + xconv
    a, b = _lru_gates(xconv, wa_ref, wx_ref, ba_ref[...], bx_ref[...], lam_ref[...])
    r8 = lax.broadcasted_iota(jnp.int32, (rows, 1), 0) & (SUBLANES - 1)
    for s in (1, 2, 4):
        m = r8 >= s
        a_sh = pltpu.roll(a, s, 0)
        b_sh = pltpu.roll(b, s, 0)
        b = jnp.where(m, a * b_sh + b, b)
        a = jnp.where(m, a * a_sh, a)
    a_s[...] = a
    b_s[...] = b

    def tile(j, hc):
        r0 = pl.multiple_of(j * SUBLANES, SUBLANES)
        h = a_s[pl.ds(r0, SUBLANES), :] * hc + b_s[pl.ds(r0, SUBLANES), :]
        h_s[pl.ds(r0, SUBLANES), :] = h
        return h[SUBLANES - 1:SUBLANES, :]

    hc = lax.fori_loop(0, rows // SUBLANES, tile, hc_ref[...])
    hc_ref[...] = hc
    ho_ref[...] = hc
    y_ref[...] = (h_s[...] * _gelu_tanh(gate_ref[...])).astype(BF16)
    tail = ext_ref[rows:rows + hist, :]
    ext_ref[0:hist, :] = tail
    convo_ref[...] = tail


def _odd_prompt(p_od, conv0, h0, layer, cw, cb, wa, wx, ba, bx, lam, *, bp, tp):
    l = p_od.shape[1] // 2
    rows = _pick_block(tp, 256, SUBLANES)
    nt = tp // rows
    lsel = lambda a: pl.BlockSpec((None,) + a.shape[1:], lambda b, t: (layer,) + (0,) * (a.ndim - 1))
    return pl.pallas_call(
        _odd_prompt_body,
        grid=(bp, nt),
        in_specs=[
            pl.BlockSpec((rows, l), lambda b, t: (b * nt + t, 0)),
            pl.BlockSpec((rows, l), lambda b, t: (b * nt + t, 1)),
            pl.BlockSpec((None, SUBLANES, l), lambda b, t: (b, 0, 0)),
            pl.BlockSpec((None, 1, l), lambda b, t: (b, 0, 0)),
            lsel(cw), lsel(cb), lsel(wa), lsel(wx), lsel(ba), lsel(bx), lsel(lam),
        ],
        out_specs=[
            pl.BlockSpec((rows, l), lambda b, t: (b * nt + t, 0)),
            pl.BlockSpec((None, SUBLANES, l), lambda b, t: (b, 0, 0)),
            pl.BlockSpec((None, 1, l), lambda b, t: (b, 0, 0)),
        ],
        out_shape=[jax.ShapeDtypeStruct((bp * tp, l), BF16), jax.ShapeDtypeStruct((bp, SUBLANES, l), F32),
                   jax.ShapeDtypeStruct((bp, 1, l), F32)],
        scratch_shapes=[pltpu.VMEM((rows + SUBLANES, l), F32), pltpu.VMEM((1, l), F32),
                        pltpu.VMEM((rows, l), F32), pltpu.VMEM((rows, l), F32), pltpu.VMEM((rows, l), F32)],
        compiler_params=_cparams("parallel", "arbitrary"),
        name="odd_prompt",
    )(p_od, p_od, conv0, h0, cw, cb, wa, wx, ba, bx, lam)


def _odd_sample_body(gate_ref, xb_ref, conv0_ref, h0_ref, cw_ref, cb_ref, wa_ref, wx_ref, ba_ref, bx_ref, lam_ref,
                     y_ref, convo_ref, ho_ref):
    @pl.when(pl.program_id(0) == 0)
    def _():
        convo_ref[...] = conv0_ref[...]
        ho_ref[...] = h0_ref[...]

    xb = xb_ref[...]
    cw = cw_ref[...]
    xconv = cw[CONV_W - 1:CONV_W, :] * xb
    for j in range(CONV_W - 1):
        xconv = xconv + cw[j:j + 1, :] * convo_ref[j]
    xconv = cb_ref[...] + xconv
    a, b = _lru_gates(xconv, wa_ref, wx_ref, ba_ref[...], bx_ref[...], lam_ref[...])
    h = a * ho_ref[...] + b
    ho_ref[...] = h
    y_ref[...] = (h * _gelu_tanh(gate_ref[...])).astype(BF16)
    for j in range(CONV_W - 2):
        convo_ref[j] = convo_ref[j + 1]
    convo_ref[CONV_W - 2] = xb


def _odd_sample(p_od, conv0_t, h0, layer, cw, cb, wa, wx, ba, bx, lam, *, mp, bs, ts):
    l = p_od.shape[1] // 2
    assert mp % bs == 0
    off = mp // bs
    lsel = lambda a: pl.BlockSpec((None,) + a.shape[1:], lambda t: (layer,) + (0,) * (a.ndim - 1))
    full = lambda a: pl.BlockSpec(a.shape, lambda t: (0,) * a.ndim)
    return pl.pallas_call(
        _odd_sample_body,
        grid=(ts,),
        in_specs=[
            pl.BlockSpec((bs, l), lambda t: (off + t, 0)),
            pl.BlockSpec((bs, l), lambda t: (off + t, 1)),
            full(conv0_t), full(h0),
            lsel(cw), lsel(cb), lsel(wa), lsel(wx), lsel(ba), lsel(bx), lsel(lam),
        ],
        out_specs=[pl.BlockSpec((bs, l), lambda t: (t, 0)), full(conv0_t), full(h0)],
        out_shape=[jax.ShapeDtypeStruct((bs * ts, l), BF16), jax.ShapeDtypeStruct(conv0_t.shape, F32),
                   jax.ShapeDtypeStruct(h0.shape, F32)],
        compiler_params=_cparams("arbitrary"),
        name="odd_sample",
    )(p_od, p_od, conv0_t, h0, cw, cb, wa, wx, ba, bx, lam)


def _pad_last(a, n):
    return jnp.pad(a, [(0, 0)] * (a.ndim - 1) + [(0, n - a.shape[-1])])


def _round_up(n, m):
    return (n + m - 1) // m * m


class _RwLayout:
    def __init__(self, rw, dw, da, dg):
        self.rw, self.dw, self.da, self.dg = rw, dw, da, dg
        self.dwp, self.dap, self.dgp = (_round_up(d, LANES) for d in (dw, da, dg))
        self.width = 3 * rw + dw + da + dg
        self.padded = 3 * rw + self.dwp + self.dap + self.dgp

    def pad(self, a):
        o = 3 * self.rw
        parts = [a[..., :o], _pad_last(a[..., o:o + self.dw], self.dwp),
                 _pad_last(a[..., o + self.dw:o + self.dw + self.da], self.dap),
                 _pad_last(a[..., o + self.dw + self.da:self.width], self.dgp)]
        return jnp.concatenate(parts, axis=-1)

    def unpad(self, a):
        o = 3 * self.rw
        parts = [a[..., :o], a[..., o:o + self.dw], a[..., o + self.dwp:o + self.dwp + self.da],
                 a[..., o + self.dwp + self.dap:o + self.dwp + self.dap + self.dg]]
        return jnp.concatenate(parts, axis=-1)


def kernel(x_prompt, x_sample, state_rwkv_shift, state_rwkv, state_hgrn, state_conv, state_lru, ffn1_norm, ffn1_w_gu, ffn1_w_down, mix_norm, ffn2_norm, ffn2_w_gu, ffn2_w_down, ev_w_in, rw_mu, rw_w0, rw_w2, rw_a0, rw_a2, rw_g2, rw_k_k, rw_k_a, rw_r_k, rw_ln_w, rw_ln_b, hg_lb, hg_norm, ev_w_out, od_w_in, conv_w, conv_b, lru_wa, lru_ba, lru_wx, lru_bx, lru_lambda, od_w_out, final_norm):
    bp, tp, d = x_prompt.shape
    bs, ts, _ = x_sample.shape
    depth = ffn1_norm.shape[0]
    mp, ms = bp * tp, bs * ts
    rw = rw_w0.shape[1]
    nh_rw = rw // RW_HEAD
    hgw = hg_norm.shape[1]
    nh_hg = hgw // HG_DIM
    lay = _RwLayout(rw, rw_w2.shape[1], rw_a2.shape[1], rw_g2.shape[1])
    assert bp * nh_rw * 2 == LANES and (bs * nh_rw) % LANES == 0
    dt = x_prompt.dtype

    w1_gu, w1_dn = ffn1_w_gu.astype(BF16), ffn1_w_down.astype(BF16)
    w2_gu, w2_dn = ffn2_w_gu.astype(BF16), ffn2_w_down.astype(BF16)
    ev_out = ev_w_out.astype(BF16)
    od_in, od_out = od_w_in.astype(BF16), od_w_out.astype(BF16)
    wa_bf, wx_bf = lru_wa.astype(BF16), lru_wx.astype(BF16)

    x = jnp.concatenate([x_prompt.reshape(mp, d), x_sample.transpose(1, 0, 2).reshape(ms, d)], axis=0)

    lbs = jnp.cumsum(jax.nn.softmax(hg_lb.astype(F32), axis=0), axis=0)
    lb_all = lbs - lbs[0]

    def k_lanes_p(a):
        a = a.reshape(bp, tp, nh_rw, RW_HEAD).transpose(1, 3, 0, 2).reshape(tp, RW_HEAD, bp * nh_rw)
        return jnp.concatenate([a, a], axis=-1)

    def v_lanes_p(a):
        return a.reshape(bp, tp, nh_rw, 2, RW_HEAD // 2).transpose(1, 4, 3, 0, 2).reshape(tp, RW_HEAD // 2, LANES)

    def v_lanes_p_inv(a):
        return a.reshape(tp, RW_HEAD // 2, 2, bp, nh_rw).transpose(3, 0, 4, 2, 1).reshape(mp, rw)

    def k_lanes_s(a):
        return a.reshape(ts, bs, nh_rw, RW_HEAD).transpose(0, 3, 1, 2).reshape(ts, RW_HEAD, bs * nh_rw)

    def k_lanes_s_inv(a):
        return a.reshape(ts, RW_HEAD, bs, nh_rw).transpose(0, 2, 3, 1).reshape(ms, rw)

    def kparam(p):
        return jnp.tile(p.reshape(nh_rw, RW_HEAD).T, (1, LANES // nh_rw))

    def vparam_p(p):
        a = p.reshape(nh_rw, 2, RW_HEAD // 2).transpose(2, 1, 0)[:, :, None, :]
        return jnp.broadcast_to(a, (RW_HEAD // 2, 2, bp, nh_rw)).reshape(RW_HEAD // 2, LANES)

    shifts_p, shifts_s, rws_p, rws_s, hgs_p, hgs_s, convs_p, convs_s, lrus_p, lrus_s = ([] for _ in range(10))

    for l in range(depth):
        x = _ffn(x, ffn1_norm, w1_gu, w1_dn, l)
        if l % 2 == 0:
            e = l // 2
            w_in = ev_w_in[e]
            w_rw = lay.pad(w_in[:, :lay.width]).astype(BF16)
            w_hg = w_in[:, lay.width:].astype(BF16)
            p_rw = _norm_proj(x, mix_norm, l, w_rw, _pick_block(lay.padded, 1792, LANES))
            p_hg = _norm_proj(x, mix_norm, l, w_hg, _pick_block(4 * hgw, 1024, LANES))
            shift_p = jnp.zeros((bp, lay.padded), F32)
            shift_s = lay.pad(state_rwkv_shift[:, e].astype(F32))
            r_, k_, v_, wdec, a_, g_ = _rwkv_prep(
                p_rw, shift_p, shift_s, lay.pad(rw_mu[e])[None], rw_w0[e][None],
                _pad_last(rw_w2[e].T, lay.dwp).T.astype(BF16), rw_a0[e][None],
                _pad_last(rw_a2[e].T, lay.dap).T.astype(BF16),
                _pad_last(rw_g2[e].T, lay.dgp).T.astype(BF16), rw=rw, mp=mp, tp=tp, bs=bs)
            shifts_p.append(lay.unpad(p_rw[:mp].reshape(bp, tp, lay.padded)[:, -1]))
            shifts_s.append(lay.unpad(p_rw[mp + (ts - 1) * bs:]))
            kk_p, ka_p = kparam(rw_k_k[e]), kparam(rw_k_a[e])
            rk_p = kparam(rw_r_k[e].reshape(rw))
            o_p, s_p = _rwkv_scan(
                k_lanes_p(r_[:mp]), k_lanes_p(wdec[:mp]), k_lanes_p(k_[:mp]), k_lanes_p(a_[:mp]),
                v_lanes_p(v_[:mp]), kk_p, ka_p, rk_p, vparam_p(rw_ln_w[e]), vparam_p(rw_ln_b[e]),
                jnp.zeros((RW_HEAD // 2, RW_HEAD, LANES), F32), tc=_pick_block(tp, 64, 1), vsplit=True)
            rws_p.append(s_p.reshape(RW_HEAD // 2, RW_HEAD, 2, bp, nh_rw).transpose(3, 4, 2, 0, 1)
                         .reshape(bp, nh_rw, RW_HEAD, RW_HEAD))
            s0_s = state_rwkv[:, e].astype(F32).transpose(2, 3, 0, 1).reshape(RW_HEAD, RW_HEAD, bs * nh_rw)
            o_s, s_s = _rwkv_scan(
                k_lanes_s(r_[mp:]), k_lanes_s(wdec[mp:]), k_lanes_s(k_[mp:]), k_lanes_s(a_[mp:]),
                k_lanes_s(v_[mp:]), kk_p, ka_p, rk_p, kparam(rw_ln_w[e]), kparam(rw_ln_b[e]),
                s0_s, tc=ts, vsplit=False)
            rws_s.append(s_s.reshape(RW_HEAD, RW_HEAD, bs, nh_rw).transpose(2, 3, 0, 1))
            o_rw = jnp.concatenate([v_lanes_p_inv(o_p), k_lanes_s_inv(o_s)], axis=0)
            lb = lb_all[e][None]
            loglb, log1m = jnp.log(lb), jnp.log1p(-lb)
            nw = hg_norm[e][None]
            chunk = _pick_block(tp, 64, 1)
            oh_p, hs_p = _hgrn_prompt(p_hg, jnp.zeros((bp, nh_hg, HG_DIM, HG_DIM), F32), loglb, log1m, nw,
                                      bp=bp, tp=tp, chunk=chunk)
            p_hg_bm = p_hg[mp:].reshape(ts, bs, 4 * hgw).transpose(1, 0, 2).reshape(ms, 4 * hgw)
            oh_s, hs_s = _hgrn_sample(p_hg_bm, state_hgrn.astype(F32), e, loglb, log1m, nw, bs=bs, ts=ts)
            oh_s = oh_s.reshape(bs, ts, hgw).transpose(1, 0, 2).reshape(ms, hgw)
            hgs_p.append(hs_p)
            hgs_s.append(hs_s)
            o_hg = jnp.concatenate([oh_p, oh_s], axis=0)
            x = _proj_res_even(x, o_rw, g_, o_hg, ev_out[e])
        else:
            o_i = l // 2
            p_od = _norm_proj(x, mix_norm, l, od_in[o_i], _pick_block(od_in.shape[-1], 1024, LANES))
            lsz = p_od.shape[1] // 2
            vec = lambda a: a[:, None, :]
            args = (conv_w, vec(conv_b), wa_bf, wx_bf, vec(lru_ba), vec(lru_bx), vec(lru_lambda))
            conv0_p = jnp.zeros((bp, SUBLANES, lsz), F32)
            y_p, c_p, h_p = _odd_prompt(p_od, conv0_p, jnp.zeros((bp, 1, lsz), F32), o_i, *args, bp=bp, tp=tp)
            conv0_s = state_conv[:, o_i].astype(F32).transpose(1, 0, 2)
            y_s, c_s, h_s = _odd_sample(p_od, conv0_s, state_lru[:, o_i].astype(F32), o_i, *args,
                                        mp=mp, bs=bs, ts=ts)
            convs_p.append(c_p[:, SUBLANES - (CONV_W - 1):])
            convs_s.append(c_s.transpose(1, 0, 2))
            lrus_p.append(h_p[:, 0])
            lrus_s.append(h_s)
            x = _proj_res(x, jnp.concatenate([y_p, y_s], axis=0), od_out[o_i])
        x = _ffn(x, ffn2_norm, w2_gu, w2_dn, l)

    y = _final_norm(x, final_norm)
    y_prompt = y[:mp].reshape(bp, tp, d)
    y_sample = y[mp:].reshape(ts, bs, d).transpose(1, 0, 2)
    st = lambda xs: jnp.stack(xs, axis=1).astype(dt)
    return (y_prompt, y_sample, st(shifts_p), st(rws_p), st(hgs_p), st(convs_p), st(lrus_p),
            st(shifts_s), st(rws_s), st(hgs_s), st(convs_s), st(lrus_s))
```

```python
import functools
import math

import jax
import jax.numpy as jnp
from jax import lax
from jax.experimental import pallas as pl
from jax.experimental.pallas import tpu as pltpu

F32 = jnp.float32
BF16 = jnp.bfloat16

NORM_EPS = 1e-6
RW_GN_EPS = 64e-5
HG_NORM_EPS = 1e-5
LRU_C = 8.0
RW_HEAD = 64
HG_DIM = 128
LRU_BLOCKS = 8
CONV_W = 4
LANES = 128
SUBLANES = 8
VMEM_LIMIT = 56 * 1024 * 1024
MAX_BM = 1088
HG_CHUNK = 64


def _cparams(*sem):
    return pltpu.CompilerParams(dimension_semantics=sem, vmem_limit_bytes=VMEM_LIMIT)


def _pick_block(n, cap, mult):
    best = None
    for d in range(mult, min(n, cap) + 1, mult):
        if n % d == 0:
            best = d
    assert best is not None, (n, cap, mult)
    return best


def _gcd(a, b):
    return math.gcd(a, b)


def _dot(a, b):
    return jnp.dot(a, b, preferred_element_type=F32)


def _dot_nt(a, b):
    return lax.dot_general(a, b, (((1,), (1,)), ((), ())), preferred_element_type=F32)


def _dot_tn(a, b):
    return lax.dot_general(a, b, (((0,), (0,)), ((), ())), preferred_element_type=F32)


def _softplus(x):
    return jnp.maximum(x, 0.0) + jnp.log1p(jnp.exp(-jnp.abs(x)))


def _neg_expm1(y):
    return -jnp.tanh(0.5 * y) * (jnp.exp(y) + 1.0)


def _rms_rows(x, g):
    ms = jnp.mean(x * x, axis=-1, keepdims=True)
    return x * lax.rsqrt(ms + NORM_EPS) * g


def _ffn_body(x_ref, g_ref, wg_ref, wu_ref, wd_ref, o_ref, xn_ref):
    @pl.when(pl.program_id(1) == 0)
    def _():
        x = x_ref[...]
        xn_ref[...] = _rms_rows(x, g_ref[...]).astype(BF16)
        o_ref[...] = x

    xn = xn_ref[...]
    gate = _dot(xn, wg_ref[...])
    up = _dot(xn, wu_ref[...])
    act = (0.5 * (gate * jax.nn.sigmoid(gate)) * up).astype(BF16)
    o_ref[...] += _dot(act, wd_ref[...])


def _ffn(x, norm_g, w_gu, w_down, layer):
    m, d = x.shape
    f = w_down.shape[1]
    bm = _pick_block(m, MAX_BM, 16)
    bf = _pick_block(f, 512, LANES)
    nf = f // bf
    return pl.pallas_call(
        _ffn_body,
        grid=(m // bm, nf),
        in_specs=[
            pl.BlockSpec((bm, d), lambda i, j: (i, 0), pipeline_mode=pl.Buffered(1)),
            pl.BlockSpec((None, 1, d), lambda i, j: (layer, 0, 0)),
            pl.BlockSpec((None, d, bf), lambda i, j: (layer, 0, j)),
            pl.BlockSpec((None, d, bf), lambda i, j: (layer, 0, j + nf)),
            pl.BlockSpec((None, bf, d), lambda i, j: (layer, j, 0)),
        ],
        out_specs=pl.BlockSpec((bm, d), lambda i, j: (i, 0)),
        out_shape=jax.ShapeDtypeStruct((m, d), F32),
        scratch_shapes=[pltpu.VMEM((bm, d), BF16)],
        compiler_params=_cparams("parallel", "arbitrary"),
        name="ffn",
    )(x, norm_g[:, None, :], w_gu, w_gu, w_down)


def _norm_proj_body(x_ref, g_ref, w_ref, o_ref, xn_ref):
    @pl.when(pl.program_id(1) == 0)
    def _():
        xn_ref[...] = _rms_rows(x_ref[...], g_ref[...]).astype(BF16)

    o_ref[...] = _dot(xn_ref[...], w_ref[...])


def _norm_proj(x, norm_g, layer, w, bn):
    m, d = x.shape
    n = w.shape[-1]
    bm = _pick_block(m, MAX_BM, 16)
    return pl.pallas_call(
        _norm_proj_body,
        grid=(m // bm, n // bn),
        in_specs=[
            pl.BlockSpec((bm, d), lambda i, j: (i, 0)),
            pl.BlockSpec((None, 1, d), lambda i, j: (layer, 0, 0)),
            pl.BlockSpec((d, bn), lambda i, j: (0, j)),
        ],
        out_specs=pl.BlockSpec((bm, bn), lambda i, j: (i, j)),
        out_shape=jax.ShapeDtypeStruct((m, n), F32),
        scratch_shapes=[pltpu.VMEM((bm, d), BF16)],
        compiler_params=_cparams("parallel", "arbitrary"),
        name="norm_proj",
    )(x, norm_g[:, None, :], w)


def _proj_res_body(x_ref, y_ref, w_ref, o_ref):
    o_ref[...] = x_ref[...] + _dot(y_ref[...], w_ref[...])


def _proj_res(x, y, w):
    m, d = x.shape
    k = y.shape[1]
    bm = _pick_block(m, MAX_BM, 16)
    bn = _pick_block(d, 1024, LANES)
    return pl.pallas_call(
        _proj_res_body,
        grid=(m // bm, d // bn),
        in_specs=[
            pl.BlockSpec((bm, bn), lambda i, j: (i, j)),
            pl.BlockSpec((bm, k), lambda i, j: (i, 0)),
            pl.BlockSpec((k, bn), lambda i, j: (0, j)),
        ],
        out_specs=pl.BlockSpec((bm, bn), lambda i, j: (i, j)),
        out_shape=jax.ShapeDtypeStruct((m, d), F32),
        compiler_params=_cparams("parallel", "arbitrary"),
        name="proj_res",
    )(x, y, w)


def _proj_res_even_body(x_ref, orw_ref, g_ref, ohg_ref, w1_ref, w2_ref, o_ref):
    y1 = (orw_ref[...] * g_ref[...]).astype(BF16)
    o_ref[...] = x_ref[...] + _dot(y1, w1_ref[...]) + _dot(ohg_ref[...], w2_ref[...])


def _proj_res_even(x, o_rw, g, o_hg, w):
    m, d = x.shape
    rw = o_rw.shape[1]
    hg = o_hg.shape[1]
    assert rw == hg
    bm = _pick_block(m, MAX_BM, 16)
    bn = _pick_block(d, 1024, LANES)
    return pl.pallas_call(
        _proj_res_even_body,
        grid=(m // bm, d // bn),
        in_specs=[
            pl.BlockSpec((bm, bn), lambda i, j: (i, j)),
            pl.BlockSpec((bm, rw), lambda i, j: (i, 0)),
            pl.BlockSpec((bm, rw), lambda i, j: (i, 0)),
            pl.BlockSpec((bm, hg), lambda i, j: (i, 0)),
            pl.BlockSpec((rw, bn), lambda i, j: (0, j)),
            pl.BlockSpec((hg, bn), lambda i, j: (1, j)),
        ],
        out_specs=pl.BlockSpec((bm, bn), lambda i, j: (i, j)),
        out_shape=jax.ShapeDtypeStruct((m, d), F32),
        compiler_params=_cparams("parallel", "arbitrary"),
        name="proj_res_even",
    )(x, o_rw, g, o_hg, w, w)


def _final_norm_body(x_ref, g_ref, op_ref, os_ref, *, n_prompt_blk):
    y = _rms_rows(x_ref[...], g_ref[...])
    i = pl.program_id(0)

    @pl.when(i < n_prompt_blk)
    def _():
        op_ref[...] = y

    @pl.when(i >= n_prompt_blk)
    def _():
        os_ref[...] = y


def _final_norm(x, g, mp):
    m, d = x.shape
    ms = m - mp
    bm = _pick_block(_gcd(mp, ms), MAX_BM, SUBLANES)
    npb = mp // bm
    return pl.pallas_call(
        functools.partial(_final_norm_body, n_prompt_blk=npb),
        grid=(m // bm,),
        in_specs=[pl.BlockSpec((bm, d), lambda i: (i, 0)), pl.BlockSpec((1, d), lambda i: (0, 0))],
        out_specs=[pl.BlockSpec((bm, d), lambda i: (jnp.minimum(i, npb - 1), 0)),
                   pl.BlockSpec((bm, d), lambda i: (jnp.maximum(i - npb, 0), 0))],
        out_shape=[jax.ShapeDtypeStruct((mp, d), F32), jax.ShapeDtypeStruct((ms, d), F32)],
        compiler_params=_cparams("arbitrary"),
        name="final_norm",
    )(x, g[None, :])


def _rwkv_prep_body(cur_ref, prevblk_ref, shiftp_ref, shifts_ref, mu_ref, w0_ref, w2_ref, a0_ref, a2_ref,
                    g2_ref, rp_ref, kp_ref, vp_ref, wp_ref, ap_ref, rs_ref, ks_ref, vs_ref, ws_ref, as_ref,
                    g_ref, *, rw, dwp, dap, n_prompt_blk, blk_per_seq):
    i = pl.program_id(0)
    cur = cur_ref[...]
    prevblk = prevblk_ref[...]
    rows = cur.shape[0]
    seq = jnp.minimum(i // blk_per_seq, shiftp_ref.shape[0] - 1)
    first = (i % blk_per_seq) == 0
    row0 = jnp.where(first, shiftp_ref[pl.ds(seq, 1), :], prevblk[rows - 1:rows, :])
    rid = lax.broadcasted_iota(jnp.int32, (rows, 1), 0)
    prev_p = jnp.where(rid == 0, row0, pltpu.roll(cur, 1, 0))
    prev_s = jnp.where(i == n_prompt_blk, shifts_ref[...], prevblk)
    prev = jnp.where(i >= n_prompt_blk, prev_s, prev_p)

    z = cur + (prev - cur) * mu_ref[...]
    o = 3 * rw
    zw = z[:, o:o + dwp]
    za = z[:, o + dwp:o + dwp + dap]
    zg = z[:, o + dwp + dap:]
    lw = w0_ref[...] + _dot(jnp.tanh(zw).astype(BF16), w2_ref[...])
    w_log = -_softplus(-lw) - 0.5
    decay = jnp.exp(-jnp.exp(w_log))
    a = jax.nn.sigmoid(a0_ref[...] + _dot(za.astype(BF16), a2_ref[...]))
    g_ref[...] = _dot(jax.nn.sigmoid(zg).astype(BF16), g2_ref[...])

    def write(r_ref, k_ref, v_ref, w_ref, a_ref):
        r_ref[...] = z[:, :rw]
        k_ref[...] = z[:, rw:2 * rw]
        v_ref[...] = z[:, 2 * rw:3 * rw]
        w_ref[...] = decay
        a_ref[...] = a

    @pl.when(i < n_prompt_blk)
    def _():
        write(rp_ref, kp_ref, vp_ref, wp_ref, ap_ref)

    @pl.when(i >= n_prompt_blk)
    def _():
        write(rs_ref, ks_ref, vs_ref, ws_ref, as_ref)


def _rwkv_prep(p_rw, shift_p, shift_s, mu, w0, w2, a0, a2, g2, *, rw, mp, tp, bs):
    m, prw = p_rw.shape
    ms = m - mp
    assert mp % bs == 0 and tp % bs == 0 and ms % bs == 0
    npb = mp // bs
    dwp, dap = w2.shape[0], a2.shape[0]
    full = lambda a: pl.BlockSpec(a.shape, lambda i: (0,) * a.ndim)
    spec_p = pl.BlockSpec((bs, rw), lambda i: (jnp.minimum(i, npb - 1), 0))
    spec_s = pl.BlockSpec((bs, rw), lambda i: (jnp.maximum(i - npb, 0), 0))
    body = functools.partial(_rwkv_prep_body, rw=rw, dwp=dwp, dap=dap, n_prompt_blk=npb, blk_per_seq=tp // bs)
    outs = pl.pallas_call(
        body,
        grid=(m // bs,),
        in_specs=[
            pl.BlockSpec((bs, prw), lambda i: (i, 0)),
            pl.BlockSpec((bs, prw), lambda i: (jnp.maximum(i - 1, 0), 0)),
            full(shift_p), full(shift_s), full(mu), full(w0), full(w2), full(a0), full(a2), full(g2),
        ],
        out_specs=[spec_p] * 5 + [spec_s] * 5 + [pl.BlockSpec((bs, rw), lambda i: (i, 0))],
        out_shape=[jax.ShapeDtypeStruct((mp, rw), F32)] * 5 + [jax.ShapeDtypeStruct((ms, rw), F32)] * 5
        + [jax.ShapeDtypeStruct((m, rw), F32)],
        compiler_params=_cparams("arbitrary"),
        name="rwkv_prep",
    )(p_rw, p_rw, shift_p, shift_s, mu, w0, w2, a0, a2, g2)
    return outs[:5], outs[5:10], outs[10]


def _rwkv_scan_body(r_ref, w_ref, k_ref, a_ref, v_ref, kk_p_ref, ka_p_ref, rk_p_ref, lnw_ref, lnb_ref, s0_ref,
                    o_ref, sout_ref, s_ref, r_s, w_s, nkk_s, b_s, km_s, *, vsplit, state_rows):
    tc_len, nv, lanes = v_ref.shape
    kc = s_ref.shape[1]
    ti = pl.program_id(1)

    @pl.when(ti == 0)
    def _():
        if state_rows:
            s_ref[...] = s0_ref[...].reshape(lanes, nv * kc).T.reshape(nv, kc, lanes)
        else:
            s_ref[...] = s0_ref[...]

    def widen(x):
        return jnp.concatenate([x, x], axis=-1) if vsplit else x

    kvec = widen(k_ref[...])
    a_ = widen(a_ref[...])
    r_s[...] = widen(r_ref[...])
    w_s[...] = widen(w_ref[...])
    kk = kvec * kk_p_ref[...][None]
    nrm = jnp.sqrt(jnp.sum(kk * kk, axis=1, keepdims=True))
    kk = kk / jnp.maximum(nrm, 1e-12)
    nkk_s[...] = -kk
    b_s[...] = kk * a_
    km_s[...] = kvec * (1.0 + (a_ - 1.0) * ka_p_ref[...][None])

    def step(t, carry):
        nkk = nkk_s[t]
        b = b_s[t]
        km = km_s[t]
        w = w_s[t]
        r = r_s[t]
        vt = v_ref[t]
        for vi in range(nv):
            sv = s_ref[vi]
            sa = jnp.sum(sv * nkk, axis=0, keepdims=True)
            sv = sv * w + sa * b + vt[vi:vi + 1, :] * km
            s_ref[vi] = sv
            o_ref[t, pl.ds(vi, 1), :] = jnp.sum(sv * r, axis=0, keepdims=True)
        return carry

    lax.fori_loop(0, tc_len, step, 0)

    o = o_ref[...]
    n_val = nv * (2 if vsplit else 1)

    def head_sum(x):
        if vsplit:
            x2 = x.reshape(tc_len * nv, lanes)
            x = (x2 + pltpu.roll(x2, lanes // 2, 1)).reshape(tc_len, nv, lanes)
        return jnp.sum(x, axis=1, keepdims=True)

    mu = head_sum(o) / n_val
    d = o - mu
    var = head_sum(d * d) / n_val
    o = d * lax.rsqrt(var + RW_GN_EPS) * lnw_ref[...][None] + lnb_ref[...][None]
    bonus = jnp.sum(r_s[...] * km_s[...] * rk_p_ref[...][None], axis=1, keepdims=True)
    o_ref[...] = o + bonus * v_ref[...]

    @pl.when(ti == pl.num_programs(1) - 1)
    def _():
        if state_rows:
            sout_ref[...] = s_ref[...].reshape(nv * kc, lanes).T.reshape(sout_ref.shape)
        else:
            sout_ref[...] = s_ref[...]


def _rwkv_scan(r, w, k, a, v, kk_p, ka_p, rk_p, lnw, lnb, s0, *, tc, vsplit, layer=None, s_prev=None):
    t, kc, kl = r.shape
    nv = v.shape[1]
    ln = v.shape[2]
    g = ln // LANES
    kspec = pl.BlockSpec((tc, kc, kl // g), lambda gi, ti: (ti, 0, gi))
    vspec = pl.BlockSpec((tc, nv, LANES), lambda gi, ti: (ti, 0, gi))
    pk = pl.BlockSpec((kc, LANES), lambda gi, ti: (0, 0))
    pv = pl.BlockSpec((nv, LANES), lambda gi, ti: (0, 0))
    state_rows = not vsplit
    operands = [r, w, k, a, v, kk_p, ka_p, rk_p, lnw, lnb, s0]
    in_specs = [kspec, kspec, kspec, kspec, vspec, pk, pk, pk, pv, pv]
    aliases = {}
    if state_rows:
        nb, _, nh, sz = s0.shape
        assert sz == nv * kc and nb * nh == ln
        sspec = pl.BlockSpec((LANES // nh, None, nh, sz), lambda gi, ti: (gi, layer, 0, 0))
        in_specs.append(sspec)
        if s_prev is not None:
            operands.append(s_prev)
            in_specs.append(pl.BlockSpec(memory_space=pl.ANY))
            aliases = {len(operands) - 1: 1}
        s_shape = s0.shape
    else:
        sspec = pl.BlockSpec((nv, kc, LANES), lambda gi, ti: (0, 0, gi))
        in_specs.append(sspec)
        s_shape = (nv, kc, ln)

    def body(*refs):
        n_in = len(operands)
        ins = refs[:11]
        _rwkv_scan_body(*ins, *refs[n_in:], vsplit=vsplit, state_rows=state_rows)

    return pl.pallas_call(
        body,
        grid=(g, t // tc),
        in_specs=in_specs,
        out_specs=[vspec, sspec],
        out_shape=[jax.ShapeDtypeStruct((t, nv, ln), F32), jax.ShapeDtypeStruct(s_shape, F32)],
        scratch_shapes=[pltpu.VMEM((nv, kc, LANES), F32)] + [pltpu.VMEM((tc, kc, LANES), F32)] * 5,
        input_output_aliases=aliases,
        compiler_params=_cparams("parallel", "arbitrary"),
        name="rwkv_scan",
    )(*operands)


def _ref_rows(gc, j):
    c, w = gc.shape
    b = 1 << j
    nt = c // SUBLANES
    gv = gc.reshape(nt, SUBLANES, w)
    if 2 * b <= SUBLANES:
        sub = lax.broadcasted_iota(jnp.int32, (1, SUBLANES, 1), 1)
        out = None
        for g0 in range(0, SUBLANES, 2 * b):
            piece = jnp.broadcast_to(gv[:, g0 + b - 1:g0 + b, :], gv.shape)
            out = piece if out is None else jnp.where(sub >= g0, piece, out)
        return out.reshape(c, w)
    tiles_per_group = 2 * b // SUBLANES
    pieces = []
    for ti in range(nt):
        src = (ti // tiles_per_group) * tiles_per_group + b // SUBLANES - 1
        pieces.append(jnp.broadcast_to(gv[src, SUBLANES - 1:SUBLANES, :], (SUBLANES, w)))
    return jnp.concatenate(pieces, axis=0)


def _hgrn_chunk(q, f, val, og, get_st, set_st, loglb, log1mlb, nrm_w, seq_len):
    c, width = q.shape
    nh = width // HG_DIM
    nseq = c // seq_len
    lg = seq_len.bit_length() - 1
    assert (1 << lg) == seq_len and c % SUBLANES == 0
    ls = jnp.minimum(f, 0.0) - jnp.log1p(jnp.exp(-jnp.abs(f)))
    b_ = log1mlb + ls
    g = jnp.maximum(loglb, b_) + jnp.log1p(jnp.exp(-jnp.abs(loglb - b_)))
    kk = _neg_expm1(g)
    qs = q * jax.nn.sigmoid(q)
    row = lax.broadcasted_iota(jnp.int32, (c, 1), 0)
    col = lax.broadcasted_iota(jnp.int32, (1, c), 1)
    tpos = row & (seq_len - 1)
    gc = g
    for j in range(lg):
        s = 1 << j
        gc = gc + jnp.where(tpos >= s, pltpu.roll(gc, s, 0), 0.0)
    qbs, kbs, sames = [qs.astype(BF16)], [kk.astype(BF16)], [row == col]
    for j in range(lg):
        e = jnp.exp(-jnp.abs(gc - _ref_rows(gc, j)))
        is_q = ((row >> j) & 1) == 1
        qbs.append(jnp.where(is_q, qs * e, 0.0).astype(BF16))
        kbs.append(jnp.where(is_q, 0.0, kk * e).astype(BF16))
        sames.append((row >> (j + 1)) == (col >> (j + 1)))
    vb = val.astype(BF16)
    qg = (qs * jnp.exp(gc)).astype(BF16)
    k2s, eglast, in_seq = [], [], []
    for s_i in range(nseq):
        glast = gc[s_i * seq_len + seq_len - 1:s_i * seq_len + seq_len, :]
        eglast.append(jnp.exp(glast))
        if nseq == 1:
            in_seq.append(None)
            k2s.append((kk * jnp.exp(glast - gc)).astype(BF16))
        else:
            in_s = (row >> lg) == s_i
            in_seq.append(in_s)
            k2s.append(jnp.where(in_s, kk * jnp.exp(jnp.minimum(glast - gc, 0.0)), 0.0).astype(BF16))
    outs = []
    for h in range(nh):
        cs = slice(h * HG_DIM, (h + 1) * HG_DIM)
        amat = None
        for qb, kb, same in zip(qbs, kbs, sames):
            term = jnp.where(same, _dot_nt(qb[:, cs], kb[:, cs]), 0.0)
            amat = term if amat is None else amat + term
        o = _dot(amat.astype(BF16), vb[:, cs])
        for s_i in range(nseq):
            st = get_st(s_i, h)
            o_int = _dot_nt(qg[:, cs], st.astype(BF16))
            o = o + (o_int if nseq == 1 else jnp.where(in_seq[s_i], o_int, 0.0))
            set_st(s_i, h, st * eglast[s_i][:, cs] + _dot_tn(vb[:, cs], k2s[s_i][:, cs]))
        outs.append(o * lax.rsqrt(jnp.mean(o * o, axis=-1, keepdims=True) + HG_NORM_EPS))
    on = jnp.concatenate(outs, axis=1) * nrm_w
    return on * (og * jax.nn.sigmoid(og))


def _hgrn_prompt_body(q_ref, f_ref, i_ref, og_ref, s0_ref, loglb_ref, log1m_ref, nw_ref, o_ref, sout_ref, st_ref,
                      *, chunk):
    nh = st_ref.shape[0]
    tb = pl.program_id(1)

    @pl.when(tb == 0)
    def _():
        for h in range(nh):
            st_ref[h] = s0_ref[h].T

    def set_st(s_i, h, v):
        st_ref[h] = v

    def body(ci, carry):
        rows = pl.ds(pl.multiple_of(ci * chunk, chunk), chunk)
        y = _hgrn_chunk(q_ref[rows, :], f_ref[rows, :], i_ref[rows, :], og_ref[rows, :],
                        lambda s_i, h: st_ref[h], set_st, loglb_ref[...], log1m_ref[...], nw_ref[...], chunk)
        o_ref[rows, :] = y.astype(BF16)
        return carry

    lax.fori_loop(0, q_ref.shape[0] // chunk, body, 0)

    @pl.when(tb == pl.num_programs(1) - 1)
    def _():
        for h in range(nh):
            sout_ref[h] = st_ref[h].T


def _hgrn_prompt(p_hg, s0, loglb, log1m, nw, *, bp, tp, chunk):
    hgw = p_hg.shape[1] // 4
    nh = hgw // HG_DIM
    tb = _pick_block(tp, 512, chunk)
    nt = tp // tb
    cspec = lambda c: pl.BlockSpec((tb, hgw), lambda b, t: (b * nt + t, c))
    pspec = pl.BlockSpec((1, hgw), lambda b, t: (0, 0))
    sspec = pl.BlockSpec((None, nh, HG_DIM, HG_DIM), lambda b, t: (b, 0, 0, 0))
    return pl.pallas_call(
        functools.partial(_hgrn_prompt_body, chunk=chunk),
        grid=(bp, nt),
        in_specs=[cspec(0), cspec(1), cspec(2), cspec(3), sspec, pspec, pspec, pspec],
        out_specs=[pl.BlockSpec((tb, hgw), lambda b, t: (b * nt + t, 0)), sspec],
        out_shape=[jax.ShapeDtypeStruct((bp * tp, hgw), BF16), jax.ShapeDtypeStruct(s0.shape, F32)],
        scratch_shapes=[pltpu.VMEM((nh, HG_DIM, HG_DIM), F32)],
        compiler_params=_cparams("parallel", "arbitrary"),
        name="hgrn_prompt",
    )(p_hg, p_hg, p_hg, p_hg, s0, loglb, log1m, nw)


def _hgrn_sample_body(q_ref, f_ref, i_ref, og_ref, s0_ref, loglb_ref, log1m_ref, nw_ref, *rest, seq_len, nseq):
    o_ref, sout_ref = rest[-2:]
    c = seq_len * nseq

    def body(gi, carry):
        rows = pl.ds(pl.multiple_of(gi * c, c), c)

        def set_st(s_i, h, v):
            sout_ref[gi * nseq + s_i, h] = v.T

        y = _hgrn_chunk(q_ref[rows, :], f_ref[rows, :], i_ref[rows, :], og_ref[rows, :],
                        lambda s_i, h: s0_ref[gi * nseq + s_i, h].T, set_st,
                        loglb_ref[...], log1m_ref[...], nw_ref[...], seq_len)
        o_ref[rows, :] = y.astype(BF16)
        return carry

    lax.fori_loop(0, q_ref.shape[0] // c, body, 0)


def _hgrn_sample(p_hg_bm, s0, layer, loglb, log1m, nw, *, bs, ts, s_prev=None):
    hgw = p_hg_bm.shape[1] // 4
    nh = hgw // HG_DIM
    nseq = max(1, 16 // ts)
    assert bs % nseq == 0
    bb = _pick_block(bs, 8, nseq)
    rows = bb * ts
    cspec = lambda c: pl.BlockSpec((rows, hgw), lambda i: (i, c))
    pspec = pl.BlockSpec((1, hgw), lambda i: (0, 0))
    sspec = pl.BlockSpec((bb, None, nh, HG_DIM, HG_DIM), lambda i: (i, layer, 0, 0, 0))
    operands = [p_hg_bm, p_hg_bm, p_hg_bm, p_hg_bm, s0, loglb, log1m, nw]
    in_specs = [cspec(0), cspec(1), cspec(2), cspec(3), sspec, pspec, pspec, pspec]
    aliases = {}
    if s_prev is not None:
        operands.append(s_prev)
        in_specs.append(pl.BlockSpec(memory_space=pl.ANY))
        aliases = {len(operands) - 1: 1}
    return pl.pallas_call(
        functools.partial(_hgrn_sample_body, seq_len=ts, nseq=nseq),
        grid=(bs // bb,),
        in_specs=in_specs,
        out_specs=[pl.BlockSpec((rows, hgw), lambda i: (i, 0)), sspec],
        out_shape=[jax.ShapeDtypeStruct((bs * ts, hgw), BF16), jax.ShapeDtypeStruct(s0.shape, F32)],
        input_output_aliases=aliases,
        compiler_params=_cparams("parallel"),
        name="hgrn_sample",
    )(*operands)


def _lru_gates(xconv, wa_ref, wx_ref, ba, bx, lam):
    l = xconv.shape[1]
    blk = l // LRU_BLOCKS
    ga, gx = [], []
    for n in range(LRU_BLOCKS):
        xb = xconv[:, n * blk:(n + 1) * blk].astype(BF16)
        ga.append(_dot(xb, wa_ref[n]))
        gx.append(_dot(xb, wx_ref[n]))
    ga = jnp.concatenate(ga, axis=1) + ba
    gx = jnp.concatenate(gx, axis=1) + bx
    log_a = -LRU_C * jax.nn.sigmoid(ga) * _softplus(-lam)
    a = jnp.exp(log_a)
    mult = jnp.sqrt(_neg_expm1(2.0 * log_a))
    return a, mult * jax.nn.sigmoid(gx) * xconv


def _gelu_tanh(x):
    c = 0.7978845608028654
    return 0.5 * x * (1.0 + jnp.tanh(c * (x + 0.044715 * (x * x * x))))


def _odd_prompt_body(gate_ref, xb_ref, conv0_ref, h0_ref, cw_ref, cb_ref, wa_ref, wx_ref, ba_ref, bx_ref, lam_ref,
                     y_ref, convo_ref, ho_ref, ext_ref, hc_ref, a_s, b_s, h_s):
    rows = xb_ref.shape[0]
    hist = SUBLANES

    @pl.when(pl.program_id(1) == 0)
    def _():
        ext_ref[0:hist, :] = conv0_ref[...]
        hc_ref[...] = h0_ref[...]

    ext_ref[hist:hist + rows, :] = xb_ref[...]
    cw = cw_ref[...]
    xconv = cw[0:1, :] * ext_ref[pl.ds(hist - (CONV_W - 1), rows), :]
    for j in range(1, CONV_W):
        xconv = xconv + cw[j:j + 1, :] * ext_ref[pl.ds(hist - (CONV_W - 1) + j, rows), :]
    xconv = cb_ref[...] + xconv
    a, b = _lru_gates(xconv, wa_ref, wx_ref, ba_ref[...], bx_ref[...], lam_ref[...])
    r8 = lax.broadcasted_iota(jnp.int32, (rows, 1), 0) & (SUBLANES - 1)
    for s in (1, 2, 4):
        m = r8 >= s
        a_sh = pltpu.roll(a, s, 0)
        b_sh = pltpu.roll(b, s, 0)
        b = jnp.where(m, a * b_sh + b, b)
        a = jnp.where(m, a * a_sh, a)
    a_s[...] = a
    b_s[...] = b

    def tile(j, hc):
        r0 = pl.multiple_of(j * SUBLANES, SUBLANES)
        h = a_s[pl.ds(r0, SUBLANES), :] * hc + b_s[pl.ds(r0, SUBLANES), :]
        h_s[pl.ds(r0, SUBLANES), :] = h
        return h[SUBLANES - 1:SUBLANES, :]

    hc = lax.fori_loop(0, rows // SUBLANES, tile, hc_ref[...])
    hc_ref[...] = hc
    ho_ref[...] = hc
    y_ref[...] = (h_s[...] * _gelu_tanh(gate_ref[...])).astype(BF16)
    tail = ext_ref[rows:rows + hist, :]
    ext_ref[0:hist, :] = tail
    convo_ref[...] = tail


def _odd_prompt(p_od, conv0, h0, layer, cw, cb, wa, wx, ba, bx, lam, *, bp, tp):
    l = p_od.shape[1] // 2
    rows = _pick_block(tp, 256, SUBLANES)
    nt = tp // rows
    lsel = lambda a: pl.BlockSpec((None,) + a.shape[1:], lambda b, t: (layer,) + (0,) * (a.ndim - 1))
    return pl.pallas_call(
        _odd_prompt_body,
        grid=(bp, nt),
        in_specs=[
            pl.BlockSpec((rows, l), lambda b, t: (b * nt + t, 0)),
            pl.BlockSpec((rows, l), lambda b, t: (b * nt + t, 1)),
            pl.BlockSpec((None, SUBLANES, l), lambda b, t: (b, 0, 0)),
            pl.BlockSpec((None, 1, l), lambda b, t: (b, 0, 0)),
            lsel(cw), lsel(cb), lsel(wa), lsel(wx), lsel(ba), lsel(bx), lsel(lam),
        ],
        out_specs=[
            pl.BlockSpec((rows, l), lambda b, t: (b * nt + t, 0)),
            pl.BlockSpec((None, SUBLANES, l), lambda b, t: (b, 0, 0)),
            pl.BlockSpec((None, 1, l), lambda b, t: (b, 0, 0)),
        ],
        out_shape=[jax.ShapeDtypeStruct((bp * tp, l), BF16), jax.ShapeDtypeStruct((bp, SUBLANES, l), F32),
                   jax.ShapeDtypeStruct((bp, 1, l), F32)],
        scratch_shapes=[pltpu.VMEM((rows + SUBLANES, l), F32), pltpu.VMEM((1, l), F32),
                        pltpu.VMEM((rows, l), F32), pltpu.VMEM((rows, l), F32), pltpu.VMEM((rows, l), F32)],
        compiler_params=_cparams("parallel", "arbitrary"),
        name="odd_prompt",
    )(p_od, p_od, conv0, h0, cw, cb, wa, wx, ba, bx, lam)


def _odd_sample_body(gate_ref, xb_ref, conv0_ref, h0_ref, cw_ref, cb_ref, wa_ref, wx_ref, ba_ref, bx_ref, lam_ref,
                     y_ref, convo_ref, ho_ref):
    @pl.when(pl.program_id(0) == 0)
    def _():
        convo_ref[...] = conv0_ref[...]
        ho_ref[...] = h0_ref[...]

    xb = xb_ref[...]
    cw = cw_ref[...]
    xconv = cw[CONV_W - 1:CONV_W, :] * xb
    for j in range(CONV_W - 1):
        xconv = xconv + cw[j:j + 1, :] * convo_ref[j]
    xconv = cb_ref[...] + xconv
    a, b = _lru_gates(xconv, wa_ref, wx_ref, ba_ref[...], bx_ref[...], lam_ref[...])
    h = a * ho_ref[...] + b
    ho_ref[...] = h
    y_ref[...] = (h * _gelu_tanh(gate_ref[...])).astype(BF16)
    for j in range(CONV_W - 2):
        convo_ref[j] = convo_ref[j + 1]
    convo_ref[CONV_W - 2] = xb


def _odd_sample(p_od, conv0_t, h0, layer, cw, cb, wa, wx, ba, bx, lam, *, mp, bs, ts):
    l = p_od.shape[1] // 2
    assert mp % bs == 0
    off = mp // bs
    lsel = lambda a: pl.BlockSpec((None,) + a.shape[1:], lambda t: (layer,) + (0,) * (a.ndim - 1))
    full = lambda a: pl.BlockSpec(a.shape, lambda t: (0,) * a.ndim)
    return pl.pallas_call(
        _odd_sample_body,
        grid=(ts,),
        in_specs=[
            pl.BlockSpec((bs, l), lambda t: (off + t, 0)),
            pl.BlockSpec((bs, l), lambda t: (off + t, 1)),
            full(conv0_t), full(h0),
            lsel(cw), lsel(cb), lsel(wa), lsel(wx), lsel(ba), lsel(bx), lsel(lam),
        ],
        out_specs=[pl.BlockSpec((bs, l), lambda t: (t, 0)), full(conv0_t), full(h0)],
        out_shape=[jax.ShapeDtypeStruct((bs * ts, l), BF16), jax.ShapeDtypeStruct(conv0_t.shape, F32),
                   jax.ShapeDtypeStruct(h0.shape, F32)],
        compiler_params=_cparams("arbitrary"),
        name="odd_sample",
    )(p_od, p_od, conv0_t, h0, cw, cb, wa, wx, ba, bx, lam)


def _pad_last(a, n):
    return jnp.pad(a, [(0, 0)] * (a.ndim - 1) + [(0, n - a.shape[-1])])


def _round_up(n, m):
    return (n + m - 1) // m * m


class _RwLayout:
    def __init__(self, rw, dw, da, dg):
        self.rw, self.dw, self.da, self.dg = rw, dw, da, dg
        self.dwp, self.dap, self.dgp = (_round_up(d, LANES) for d in (dw, da, dg))
        self.width = 3 * rw + dw + da + dg
        self.padded = 3 * rw + self.dwp + self.dap + self.dgp

    def pad(self, a):
        o = 3 * self.rw
        parts = [a[..., :o], _pad_last(a[..., o:o + self.dw], self.dwp),
                 _pad_last(a[..., o + self.dw:o + self.dw + self.da], self.dap),
                 _pad_last(a[..., o + self.dw + self.da:self.width], self.dgp)]
        return jnp.concatenate(parts, axis=-1)

    def unpad(self, a):
        o = 3 * self.rw
        parts = [a[..., :o], a[..., o:o + self.dw], a[..., o + self.dwp:o + self.dwp + self.da],
                 a[..., o + self.dwp + self.dap:o + self.dwp + self.dap + self.dg]]
        return jnp.concatenate(parts, axis=-1)


def kernel(x_prompt, x_sample, state_rwkv_shift, state_rwkv, state_hgrn, state_conv, state_lru, ffn1_norm, ffn1_w_gu, ffn1_w_down, mix_norm, ffn2_norm, ffn2_w_gu, ffn2_w_down, ev_w_in, rw_mu, rw_w0, rw_w2, rw_a0, rw_a2, rw_g2, rw_k_k, rw_k_a, rw_r_k, rw_ln_w, rw_ln_b, hg_lb, hg_norm, ev_w_out, od_w_in, conv_w, conv_b, lru_wa, lru_ba, lru_wx, lru_bx, lru_lambda, od_w_out, final_norm):
    bp, tp, d = x_prompt.shape
    bs, ts, _ = x_sample.shape
    depth = ffn1_norm.shape[0]
    n_even = ev_w_in.shape[0]
    mp, ms = bp * tp, bs * ts
    rw = rw_w0.shape[1]
    nh_rw = rw // RW_HEAD
    hgw = hg_norm.shape[1]
    nh_hg = hgw // HG_DIM
    lay = _RwLayout(rw, rw_w2.shape[1], rw_a2.shape[1], rw_g2.shape[1])
    assert bp * nh_rw * 2 == LANES and (bs * nh_rw) % LANES == 0
    dt = x_prompt.dtype

    w1_gu, w1_dn = ffn1_w_gu.astype(BF16), ffn1_w_down.astype(BF16)
    w2_gu, w2_dn = ffn2_w_gu.astype(BF16), ffn2_w_down.astype(BF16)
    ev_out = ev_w_out.astype(BF16)
    od_in, od_out = od_w_in.astype(BF16), od_w_out.astype(BF16)
    wa_bf, wx_bf = lru_wa.astype(BF16), lru_wx.astype(BF16)

    x = jnp.concatenate([x_prompt.reshape(mp, d), x_sample.transpose(1, 0, 2).reshape(ms, d)], axis=0)

    lbs = jnp.cumsum(jax.nn.softmax(hg_lb.astype(F32), axis=0), axis=0)
    lb_all = lbs - lbs[0]

    def k_lanes_p(a):
        return a.reshape(bp, tp, nh_rw, RW_HEAD).transpose(1, 3, 0, 2).reshape(tp, RW_HEAD, bp * nh_rw)

    def v_lanes_p(a):
        return a.reshape(bp, tp, nh_rw, 2, RW_HEAD // 2).transpose(1, 4, 3, 0, 2).reshape(tp, RW_HEAD // 2, LANES)

    def v_lanes_p_inv(a):
        return a.reshape(tp, RW_HEAD // 2, 2, bp, nh_rw).transpose(3, 0, 4, 2, 1).reshape(mp, rw)

    def k_lanes_s(a):
        return a.reshape(ts, bs, nh_rw, RW_HEAD).transpose(0, 3, 1, 2).reshape(ts, RW_HEAD, bs * nh_rw)

    def k_lanes_s_inv(a):
        return a.reshape(ts, RW_HEAD, bs, nh_rw).transpose(0, 2, 3, 1).reshape(ms, rw)

    def kparam(p):
        return jnp.tile(p.reshape(nh_rw, RW_HEAD).T, (1, LANES // nh_rw))

    def vparam_p(p):
        a = p.reshape(nh_rw, 2, RW_HEAD // 2).transpose(2, 1, 0)[:, :, None, :]
        return jnp.broadcast_to(a, (RW_HEAD // 2, 2, bp, nh_rw)).reshape(RW_HEAD // 2, LANES)

    shifts_p, shifts_s, rws_p, hgs_p, convs_p, convs_s, lrus_p, lrus_s = ([] for _ in range(8))
    rw_rows = state_rwkv.astype(F32).reshape(bs, n_even, nh_rw, RW_HEAD * RW_HEAD)
    rw_rows_new = None
    hg_new = None

    for l in range(depth):
        x = _ffn(x, ffn1_norm, w1_gu, w1_dn, l)
        if l % 2 == 0:
            e = l // 2
            w_in = ev_w_in[e]
            w_rw = lay.pad(w_in[:, :lay.width]).astype(BF16)
            w_hg = w_in[:, lay.width:].astype(BF16)
            p_rw = _norm_proj(x, mix_norm, l, w_rw, _pick_block(lay.padded, 1792, LANES))
            p_hg = _norm_proj(x, mix_norm, l, w_hg, _pick_block(4 * hgw, 1024, LANES))
            shift_p = jnp.zeros((bp, lay.padded), F32)
            shift_s = lay.pad(state_rwkv_shift[:, e].astype(F32))
            rkvwa_p, rkvwa_s, g_ = _rwkv_prep(
                p_rw, shift_p, shift_s, lay.pad(rw_mu[e])[None], rw_w0[e][None],
                _pad_last(rw_w2[e].T, lay.dwp).T.astype(BF16), rw_a0[e][None],
                _pad_last(rw_a2[e].T, lay.dap).T.astype(BF16),
                _pad_last(rw_g2[e].T, lay.dgp).T.astype(BF16), rw=rw, mp=mp, tp=tp, bs=bs)
            shifts_p.append(lay.unpad(p_rw[tp - 1:mp:tp]))
            shifts_s.append(lay.unpad(p_rw[mp + (ts - 1) * bs:]))
            kk_p, ka_p = kparam(rw_k_k[e]), kparam(rw_k_a[e])
            rk_p = kparam(rw_r_k[e].reshape(rw))
            r_p, k_p, v_p, w_p, a_p = rkvwa_p
            o_p, s_p = _rwkv_scan(
                k_lanes_p(r_p), k_lanes_p(w_p), k_lanes_p(k_p), k_lanes_p(a_p), v_lanes_p(v_p),
                kk_p, ka_p, rk_p, vparam_p(rw_ln_w[e]), vparam_p(rw_ln_b[e]),
                jnp.zeros((RW_HEAD // 2, RW_HEAD, LANES), F32), tc=_pick_block(tp, 64, 1), vsplit=True)
            rws_p.append(s_p.reshape(RW_HEAD // 2, RW_HEAD, 2, bp, nh_rw).transpose(3, 4, 2, 0, 1)
                         .reshape(bp, nh_rw, RW_HEAD, RW_HEAD))
            r_s, k_s, v_s, w_s, a_s = rkvwa_s
            o_s, rw_rows_new = _rwkv_scan(
                k_lanes_s(r_s), k_lanes_s(w_s), k_lanes_s(k_s), k_lanes_s(a_s), k_lanes_s(v_s),
                kk_p, ka_p, rk_p, kparam(rw_ln_w[e]), kparam(rw_ln_b[e]),
                rw_rows, tc=ts, vsplit=False, layer=e, s_prev=rw_rows_new)
            o_rw = jnp.concatenate([v_lanes_p_inv(o_p), k_lanes_s_inv(o_s)], axis=0)
            lb = lb_all[e][None]
            loglb, log1m = jnp.log(lb), jnp.log1p(-lb)
            nw = hg_norm[e][None]
            chunk = _pick_block(tp, HG_CHUNK, 1)
            oh_p, hs_p = _hgrn_prompt(p_hg, jnp.zeros((bp, nh_hg, HG_DIM, HG_DIM), F32), loglb, log1m, nw,
                                      bp=bp, tp=tp, chunk=chunk)
            p_hg_bm = p_hg[mp:].reshape(ts, bs, 4 * hgw).transpose(1, 0, 2).reshape(ms, 4 * hgw)
            oh_s, hg_new = _hgrn_sample(p_hg_bm, state_hgrn.astype(F32), e, loglb, log1m, nw, bs=bs, ts=ts,
                                        s_prev=hg_new)
            oh_s = oh_s.reshape(bs, ts, hgw).transpose(1, 0, 2).reshape(ms, hgw)
            hgs_p.append(hs_p)
            o_hg = jnp.concatenate([oh_p, oh_s], axis=0)
            x = _proj_res_even(x, o_rw, g_, o_hg, ev_out[e])
        else:
            o_i = l // 2
            p_od = _norm_proj(x, mix_norm, l, od_in[o_i], _pick_block(od_in.shape[-1], 1024, LANES))
            lsz = p_od.shape[1] // 2
            vec = lambda a: a[:, None, :]
            args = (conv_w, vec(conv_b), wa_bf, wx_bf, vec(lru_ba), vec(lru_bx), vec(lru_lambda))
            conv0_p = jnp.zeros((bp, SUBLANES, lsz), F32)
            y_p, c_p, h_p = _odd_prompt(p_od, conv0_p, jnp.zeros((bp, 1, lsz), F32), o_i, *args, bp=bp, tp=tp)
            conv0_s = state_conv[:, o_i].astype(F32).transpose(1, 0, 2)
            y_s, c_s, h_s = _odd_sample(p_od, conv0_s, state_lru[:, o_i].astype(F32), o_i, *args,
                                        mp=mp, bs=bs, ts=ts)
            convs_p.append(c_p[:, SUBLANES - (CONV_W - 1):])
            convs_s.append(c_s.transpose(1, 0, 2))
            lrus_p.append(h_p[:, 0])
            lrus_s.append(h_s)
            x = _proj_res(x, jnp.concatenate([y_p, y_s], axis=0), od_out[o_i])
        x = _ffn(x, ffn2_norm, w2_gu, w2_dn, l)

    y_p, y_s = _final_norm(x, final_norm, mp)
    y_prompt = y_p.reshape(bp, tp, d)
    y_sample = y_s.reshape(ts, bs, d).transpose(1, 0, 2)
    st = lambda xs: jnp.stack(xs, axis=1).astype(dt)
    sample_rwkv = rw_rows_new.reshape(state_rwkv.shape).astype(dt)
    return (y_prompt, y_sample, st(shifts_p), st(rws_p), st(hgs_p), st(convs_p), st(lrus_p),
            st(shifts_s), sample_rwkv, hg_new.astype(dt), st(convs_s), st(lrus_s))
```

```python
import functools
import math

import jax
import jax.numpy as jnp
from jax import lax
from jax.experimental import pallas as pl
from jax.experimental.pallas import tpu as pltpu

F32 = jnp.float32
BF16 = jnp.bfloat16

NORM_EPS = 1e-6
RW_GN_EPS = 64e-5
HG_NORM_EPS = 1e-5
LRU_C = 8.0
RW_HEAD = 64
HG_DIM = 128
LRU_BLOCKS = 8
CONV_W = 4
LANES = 128
SUBLANES = 8
VMEM_LIMIT = 56 * 1024 * 1024
MAX_BM = 1088
HG_CHUNK = 64


def _cparams(*sem):
    return pltpu.CompilerParams(dimension_semantics=sem, vmem_limit_bytes=VMEM_LIMIT)


def _pick_block(n, cap, mult):
    best = None
    for d in range(mult, min(n, cap) + 1, mult):
        if n % d == 0:
            best = d
    assert best is not None, (n, cap, mult)
    return best


def _gcd(a, b):
    return math.gcd(a, b)


def _dot(a, b):
    return jnp.dot(a, b, preferred_element_type=F32)


def _dot_nt(a, b):
    return lax.dot_general(a, b, (((1,), (1,)), ((), ())), preferred_element_type=F32)


def _dot_tn(a, b):
    return lax.dot_general(a, b, (((0,), (0,)), ((), ())), preferred_element_type=F32)


def _softplus(x):
    return jnp.maximum(x, 0.0) + jnp.log1p(jnp.exp(-jnp.abs(x)))


def _neg_expm1(y):
    return -jnp.tanh(0.5 * y) * (jnp.exp(y) + 1.0)


def _rms_rows(x, g):
    ms = jnp.mean(x * x, axis=-1, keepdims=True)
    return x * lax.rsqrt(ms + NORM_EPS) * g


def _ffn_body(x_ref, g_ref, wg_ref, wu_ref, wd_ref, *rest, cast_next):
    if cast_next:
        sgu_ref, sdn_ref, o_ref, cgu_ref, cdn_ref, xn_ref = rest
        cgu_ref[...] = sgu_ref[...].astype(BF16)
        cdn_ref[...] = sdn_ref[...].astype(BF16)
    else:
        o_ref, xn_ref = rest

    @pl.when(pl.program_id(1) == 0)
    def _():
        x = x_ref[...]
        xn_ref[...] = _rms_rows(x, g_ref[...]).astype(BF16)
        o_ref[...] = x

    xn = xn_ref[...]
    gate = _dot(xn, wg_ref[...])
    up = _dot(xn, wu_ref[...])
    act = (0.5 * (gate * jax.nn.sigmoid(gate)) * up).astype(BF16)
    o_ref[...] += _dot(act, wd_ref[...])


def _ffn(x, norm_g, layer, w_gu, w_down, next_src=None):
    m, d = x.shape
    f = w_down.shape[0]
    bm = _pick_block(m, MAX_BM, 16)
    bf = _pick_block(f, 512, LANES)
    ni, nf = m // bm, f // bf
    in_specs = [
        pl.BlockSpec((bm, d), lambda i, j: (i, 0), pipeline_mode=pl.Buffered(1)),
        pl.BlockSpec((None, 1, d), lambda i, j: (layer, 0, 0)),
        pl.BlockSpec((d, bf), lambda i, j: (0, j)),
        pl.BlockSpec((d, bf), lambda i, j: (0, j + nf)),
        pl.BlockSpec((bf, d), lambda i, j: (j, 0)),
    ]
    operands = [x, norm_g[:, None, :], w_gu, w_gu, w_down]
    out_specs = [pl.BlockSpec((bm, d), lambda i, j: (i, 0))]
    out_shape = [jax.ShapeDtypeStruct((m, d), F32)]
    if next_src is not None:
        s_gu, s_dn, nl = next_src
        assert d % ni == 0 and (2 * f) % nf == 0 and (d // ni) % 16 == 0 and (d // ni) % LANES == 0
        gu_blk, dn_blk = (d // ni, 2 * f // nf), (f // nf, d // ni)
        in_specs += [pl.BlockSpec((None,) + gu_blk, lambda i, j: (nl, i, j)),
                     pl.BlockSpec((None,) + dn_blk, lambda i, j: (nl, j, i))]
        operands += [s_gu, s_dn]
        out_specs += [pl.BlockSpec(gu_blk, lambda i, j: (i, j)), pl.BlockSpec(dn_blk, lambda i, j: (j, i))]
        out_shape += [jax.ShapeDtypeStruct((d, 2 * f), BF16), jax.ShapeDtypeStruct((f, d), BF16)]
    return pl.pallas_call(
        functools.partial(_ffn_body, cast_next=next_src is not None),
        grid=(ni, nf),
        in_specs=in_specs,
        out_specs=out_specs,
        out_shape=out_shape,
        scratch_shapes=[pltpu.VMEM((bm, d), BF16)],
        compiler_params=_cparams("parallel", "arbitrary"),
        name="ffn",
    )(*operands)


def _norm_proj_body(x_ref, g_ref, w_ref, o_ref, xn_ref):
    @pl.when(pl.program_id(1) == 0)
    def _():
        xn_ref[...] = _rms_rows(x_ref[...], g_ref[...]).astype(BF16)

    o_ref[...] = _dot(xn_ref[...], w_ref[...])


def _norm_proj(x, norm_g, layer, w, bn):
    m, d = x.shape
    n = w.shape[-1]
    bm = _pick_block(m, MAX_BM, 16)
    return pl.pallas_call(
        _norm_proj_body,
        grid=(m // bm, n // bn),
        in_specs=[
            pl.BlockSpec((bm, d), lambda i, j: (i, 0)),
            pl.BlockSpec((None, 1, d), lambda i, j: (layer, 0, 0)),
            pl.BlockSpec((d, bn), lambda i, j: (0, j)),
        ],
        out_specs=pl.BlockSpec((bm, bn), lambda i, j: (i, j)),
        out_shape=jax.ShapeDtypeStruct((m, n), F32),
        scratch_shapes=[pltpu.VMEM((bm, d), BF16)],
        compiler_params=_cparams("parallel", "arbitrary"),
        name="norm_proj",
    )(x, norm_g[:, None, :], w)


def _proj_res_body(x_ref, y_ref, w_ref, o_ref):
    o_ref[...] = x_ref[...] + _dot(y_ref[...], w_ref[...])


def _proj_res(x, y, w):
    m, d = x.shape
    k = y.shape[1]
    bm = _pick_block(m, MAX_BM, 16)
    bn = _pick_block(d, 1024, LANES)
    return pl.pallas_call(
        _proj_res_body,
        grid=(m // bm, d // bn),
        in_specs=[
            pl.BlockSpec((bm, bn), lambda i, j: (i, j)),
            pl.BlockSpec((bm, k), lambda i, j: (i, 0)),
            pl.BlockSpec((k, bn), lambda i, j: (0, j)),
        ],
        out_specs=pl.BlockSpec((bm, bn), lambda i, j: (i, j)),
        out_shape=jax.ShapeDtypeStruct((m, d), F32),
        compiler_params=_cparams("parallel", "arbitrary"),
        name="proj_res",
    )(x, y, w)


def _proj_res_even_body(x_ref, orw_ref, g_ref, ohg_ref, w1_ref, w2_ref, o_ref):
    y1 = (orw_ref[...] * g_ref[...]).astype(BF16)
    o_ref[...] = x_ref[...] + _dot(y1, w1_ref[...]) + _dot(ohg_ref[...], w2_ref[...])


def _proj_res_even(x, o_rw, g, o_hg, w):
    m, d = x.shape
    rw = o_rw.shape[1]
    hg = o_hg.shape[1]
    assert rw == hg
    bm = _pick_block(m, MAX_BM, 16)
    bn = _pick_block(d, 1024, LANES)
    return pl.pallas_call(
        _proj_res_even_body,
        grid=(m // bm, d // bn),
        in_specs=[
            pl.BlockSpec((bm, bn), lambda i, j: (i, j)),
            pl.BlockSpec((bm, rw), lambda i, j: (i, 0)),
            pl.BlockSpec((bm, rw), lambda i, j: (i, 0)),
            pl.BlockSpec((bm, hg), lambda i, j: (i, 0)),
            pl.BlockSpec((rw, bn), lambda i, j: (0, j)),
            pl.BlockSpec((hg, bn), lambda i, j: (1, j)),
        ],
        out_specs=pl.BlockSpec((bm, bn), lambda i, j: (i, j)),
        out_shape=jax.ShapeDtypeStruct((m, d), F32),
        compiler_params=_cparams("parallel", "arbitrary"),
        name="proj_res_even",
    )(x, o_rw, g, o_hg, w, w)


def _final_norm_body(x_ref, g_ref, op_ref, os_ref, *, n_prompt_blk):
    y = _rms_rows(x_ref[...], g_ref[...])
    i = pl.program_id(0)

    @pl.when(i < n_prompt_blk)
    def _():
        op_ref[...] = y

    @pl.when(i >= n_prompt_blk)
    def _():
        os_ref[...] = y


def _final_norm(x, g, mp):
    m, d = x.shape
    ms = m - mp
    bm = _pick_block(_gcd(mp, ms), MAX_BM, SUBLANES)
    npb = mp // bm
    return pl.pallas_call(
        functools.partial(_final_norm_body, n_prompt_blk=npb),
        grid=(m // bm,),
        in_specs=[pl.BlockSpec((bm, d), lambda i: (i, 0)), pl.BlockSpec((1, d), lambda i: (0, 0))],
        out_specs=[pl.BlockSpec((bm, d), lambda i: (jnp.minimum(i, npb - 1), 0)),
                   pl.BlockSpec((bm, d), lambda i: (jnp.maximum(i - npb, 0), 0))],
        out_shape=[jax.ShapeDtypeStruct((mp, d), F32), jax.ShapeDtypeStruct((ms, d), F32)],
        compiler_params=_cparams("arbitrary"),
        name="final_norm",
    )(x, g[None, :])


def _rwkv_prep_body(cur_ref, prevblk_ref, shiftp_ref, shifts_ref, mu_ref, w0_ref, w2_ref, a0_ref, a2_ref,
                    g2_ref, rp_ref, kp_ref, vp_ref, wp_ref, ap_ref, rs_ref, ks_ref, vs_ref, ws_ref, as_ref,
                    g_ref, *, rw, dwp, dap, n_prompt_blk, blk_per_seq):
    i = pl.program_id(0)
    cur = cur_ref[...]
    prevblk = prevblk_ref[...]
    rows = cur.shape[0]
    seq = jnp.minimum(i // blk_per_seq, shiftp_ref.shape[0] - 1)
    first = (i % blk_per_seq) == 0
    row0 = jnp.where(first, shiftp_ref[pl.ds(seq, 1), :], prevblk[rows - 1:rows, :])
    rid = lax.broadcasted_iota(jnp.int32, (rows, 1), 0)
    prev_p = jnp.where(rid == 0, row0, pltpu.roll(cur, 1, 0))
    prev_s = jnp.where(i == n_prompt_blk, shifts_ref[...], prevblk)
    prev = jnp.where(i >= n_prompt_blk, prev_s, prev_p)

    z = cur + (prev - cur) * mu_ref[...]
    o = 3 * rw
    zw = z[:, o:o + dwp]
    za = z[:, o + dwp:o + dwp + dap]
    zg = z[:, o + dwp + dap:]
    lw = w0_ref[...] + _dot(jnp.tanh(zw).astype(BF16), w2_ref[...])
    w_log = -_softplus(-lw) - 0.5
    decay = jnp.exp(-jnp.exp(w_log))
    a = jax.nn.sigmoid(a0_ref[...] + _dot(za.astype(BF16), a2_ref[...]))
    g_ref[...] = _dot(jax.nn.sigmoid(zg).astype(BF16), g2_ref[...])

    def write(r_ref, k_ref, v_ref, w_ref, a_ref):
        r_ref[...] = z[:, :rw]
        k_ref[...] = z[:, rw:2 * rw]
        v_ref[...] = z[:, 2 * rw:3 * rw]
        w_ref[...] = decay
        a_ref[...] = a

    @pl.when(i < n_prompt_blk)
    def _():
        write(rp_ref, kp_ref, vp_ref, wp_ref, ap_ref)

    @pl.when(i >= n_prompt_blk)
    def _():
        write(rs_ref, ks_ref, vs_ref, ws_ref, as_ref)


def _rwkv_prep(p_rw, shift_p, shift_s, mu, w0, w2, a0, a2, g2, *, rw, mp, tp, bs):
    m, prw = p_rw.shape
    ms = m - mp
    assert mp % bs == 0 and tp % bs == 0 and ms % bs == 0
    npb = mp // bs
    dwp, dap = w2.shape[0], a2.shape[0]
    full = lambda a: pl.BlockSpec(a.shape, lambda i: (0,) * a.ndim)
    bps = tp // bs
    bp = mp // tp

    def prompt_idx(i):
        ic = jnp.minimum(i, npb - 1)
        return ic % bps, ic // bps

    spec_p = pl.BlockSpec((bs, rw), prompt_idx)
    spec_s = pl.BlockSpec((bs, rw), lambda i: (jnp.maximum(i - npb, 0), 0))
    body = functools.partial(_rwkv_prep_body, rw=rw, dwp=dwp, dap=dap, n_prompt_blk=npb, blk_per_seq=tp // bs)
    outs = pl.pallas_call(
        body,
        grid=(m // bs,),
        in_specs=[
            pl.BlockSpec((bs, prw), lambda i: (i, 0)),
            pl.BlockSpec((bs, prw), lambda i: (jnp.maximum(i - 1, 0), 0)),
            full(shift_p), full(shift_s), full(mu), full(w0), full(w2), full(a0), full(a2), full(g2),
        ],
        out_specs=[spec_p] * 5 + [spec_s] * 5 + [pl.BlockSpec((bs, rw), lambda i: (i, 0))],
        out_shape=[jax.ShapeDtypeStruct((tp, bp * rw), F32)] * 5 + [jax.ShapeDtypeStruct((ms, rw), F32)] * 5
        + [jax.ShapeDtypeStruct((m, rw), F32)],
        compiler_params=_cparams("arbitrary"),
        name="rwkv_prep",
    )(p_rw, p_rw, shift_p, shift_s, mu, w0, w2, a0, a2, g2)
    return outs[:5], outs[5:10], outs[10]


def _rwkv_scan_body(r_ref, w_ref, k_ref, a_ref, v_ref, kk_p_ref, ka_p_ref, rk_p_ref, lnw_ref, lnb_ref, s0_ref,
                    o_ref, sout_ref, s_ref, r_s, w_s, nkk_s, b_s, km_s, *, vsplit, state_rows):
    tc_len, nv, lanes = v_ref.shape
    kc = s_ref.shape[1]
    ti = pl.program_id(1)

    @pl.when(ti == 0)
    def _():
        if state_rows:
            s_ref[...] = s0_ref[...].reshape(lanes, nv * kc).T.reshape(nv, kc, lanes)
        else:
            s_ref[...] = s0_ref[...]

    def widen(x):
        return jnp.concatenate([x, x], axis=-1) if vsplit else x

    kvec = widen(k_ref[...])
    a_ = widen(a_ref[...])
    r_s[...] = widen(r_ref[...])
    w_s[...] = widen(w_ref[...])
    kk = kvec * kk_p_ref[...][None]
    nrm = jnp.sqrt(jnp.sum(kk * kk, axis=1, keepdims=True))
    kk = kk / jnp.maximum(nrm, 1e-12)
    nkk_s[...] = -kk
    b_s[...] = kk * a_
    km_s[...] = kvec * (1.0 + (a_ - 1.0) * ka_p_ref[...][None])

    def step(t, carry):
        nkk = nkk_s[t]
        b = b_s[t]
        km = km_s[t]
        w = w_s[t]
        r = r_s[t]
        vt = v_ref[t]
        for vi in range(nv):
            sv = s_ref[vi]
            sa = jnp.sum(sv * nkk, axis=0, keepdims=True)
            sv = sv * w + sa * b + vt[vi:vi + 1, :] * km
            s_ref[vi] = sv
            o_ref[t, pl.ds(vi, 1), :] = jnp.sum(sv * r, axis=0, keepdims=True)
        return carry

    lax.fori_loop(0, tc_len, step, 0)

    o = o_ref[...]
    n_val = nv * (2 if vsplit else 1)

    def head_sum(x):
        if vsplit:
            x2 = x.reshape(tc_len * nv, lanes)
            x = (x2 + pltpu.roll(x2, lanes // 2, 1)).reshape(tc_len, nv, lanes)
        return jnp.sum(x, axis=1, keepdims=True)

    mu = head_sum(o) / n_val
    d = o - mu
    var = head_sum(d * d) / n_val
    o = d * lax.rsqrt(var + RW_GN_EPS) * lnw_ref[...][None] + lnb_ref[...][None]
    bonus = jnp.sum(r_s[...] * km_s[...] * rk_p_ref[...][None], axis=1, keepdims=True)
    o_ref[...] = o + bonus * v_ref[...]

    @pl.when(ti == pl.num_programs(1) - 1)
    def _():
        if state_rows:
            sout_ref[...] = s_ref[...].reshape(nv * kc, lanes).T.reshape(sout_ref.shape)
        else:
            sout_ref[...] = s_ref[...]


def _rwkv_scan(r, w, k, a, v, kk_p, ka_p, rk_p, lnw, lnb, s0, *, tc, vsplit, layer=None, s_prev=None):
    t, kc, kl = r.shape
    nv = v.shape[1]
    ln = v.shape[2]
    g = ln // LANES
    kspec = pl.BlockSpec((tc, kc, kl // g), lambda gi, ti: (ti, 0, gi))
    vspec = pl.BlockSpec((tc, nv, LANES), lambda gi, ti: (ti, 0, gi))
    pk = pl.BlockSpec((kc, LANES), lambda gi, ti: (0, 0))
    pv = pl.BlockSpec((nv, LANES), lambda gi, ti: (0, 0))
    state_rows = not vsplit
    operands = [r, w, k, a, v, kk_p, ka_p, rk_p, lnw, lnb, s0]
    in_specs = [kspec, kspec, kspec, kspec, vspec, pk, pk, pk, pv, pv]
    aliases = {}
    if state_rows:
        nb, _, nh, sz = s0.shape
        assert sz == nv * kc and nb * nh == ln
        sspec = pl.BlockSpec((LANES // nh, None, nh, sz), lambda gi, ti: (gi, layer, 0, 0))
        in_specs.append(sspec)
        if s_prev is not None:
            operands.append(s_prev)
            in_specs.append(pl.BlockSpec(memory_space=pl.ANY))
            aliases = {len(operands) - 1: 1}
        s_shape = s0.shape
    else:
        sspec = pl.BlockSpec((nv, kc, LANES), lambda gi, ti: (0, 0, gi))
        in_specs.append(sspec)
        s_shape = (nv, kc, ln)

    def body(*refs):
        n_in = len(operands)
        ins = refs[:11]
        _rwkv_scan_body(*ins, *refs[n_in:], vsplit=vsplit, state_rows=state_rows)

    return pl.pallas_call(
        body,
        grid=(g, t // tc),
        in_specs=in_specs,
        out_specs=[vspec, sspec],
        out_shape=[jax.ShapeDtypeStruct((t, nv, ln), F32), jax.ShapeDtypeStruct(s_shape, F32)],
        scratch_shapes=[pltpu.VMEM((nv, kc, LANES), F32)] + [pltpu.VMEM((tc, kc, LANES), F32)] * 5,
        input_output_aliases=aliases,
        compiler_params=_cparams("parallel", "arbitrary"),
        name="rwkv_scan",
    )(*operands)


def _ref_rows(gc, j):
    c, w = gc.shape
    b = 1 << j
    nt = c // SUBLANES
    gv = gc.reshape(nt, SUBLANES, w)
    if 2 * b <= SUBLANES:
        sub = lax.broadcasted_iota(jnp.int32, (1, SUBLANES, 1), 1)
        out = None
        for g0 in range(0, SUBLANES, 2 * b):
            piece = jnp.broadcast_to(gv[:, g0 + b - 1:g0 + b, :], gv.shape)
            out = piece if out is None else jnp.where(sub >= g0, piece, out)
        return out.reshape(c, w)
    tiles_per_group = 2 * b // SUBLANES
    pieces = []
    for ti in range(nt):
        src = (ti // tiles_per_group) * tiles_per_group + b // SUBLANES - 1
        pieces.append(jnp.broadcast_to(gv[src, SUBLANES - 1:SUBLANES, :], (SUBLANES, w)))
    return jnp.concatenate(pieces, axis=0)


def _hgrn_chunk(q, f, val, og, get_st, set_st, loglb, log1mlb, nrm_w, seq_len):
    c, width = q.shape
    nh = width // HG_DIM
    nseq = c // seq_len
    lg = seq_len.bit_length() - 1
    assert (1 << lg) == seq_len and c % SUBLANES == 0
    ls = jnp.minimum(f, 0.0) - jnp.log1p(jnp.exp(-jnp.abs(f)))
    b_ = log1mlb + ls
    g = jnp.maximum(loglb, b_) + jnp.log1p(jnp.exp(-jnp.abs(loglb - b_)))
    kk = _neg_expm1(g)
    qs = q * jax.nn.sigmoid(q)
    row = lax.broadcasted_iota(jnp.int32, (c, 1), 0)
    col = lax.broadcasted_iota(jnp.int32, (1, c), 1)
    tpos = row & (seq_len - 1)
    gc = g
    for j in range(lg):
        s = 1 << j
        gc = gc + jnp.where(tpos >= s, pltpu.roll(gc, s, 0), 0.0)
    qbs, kbs, sames = [qs.astype(BF16)], [kk.astype(BF16)], [row == col]
    for j in range(lg):
        e = jnp.exp(-jnp.abs(gc - _ref_rows(gc, j)))
        is_q = ((row >> j) & 1) == 1
        qbs.append(jnp.where(is_q, qs * e, 0.0).astype(BF16))
        kbs.append(jnp.where(is_q, 0.0, kk * e).astype(BF16))
        sames.append((row >> (j + 1)) == (col >> (j + 1)))
    vb = val.astype(BF16)
    qg = (qs * jnp.exp(gc)).astype(BF16)
    k2s, eglast, in_seq = [], [], []
    for s_i in range(nseq):
        glast = gc[s_i * seq_len + seq_len - 1:s_i * seq_len + seq_len, :]
        eglast.append(jnp.exp(glast))
        if nseq == 1:
            in_seq.append(None)
            k2s.append((kk * jnp.exp(glast - gc)).astype(BF16))
        else:
            in_s = (row >> lg) == s_i
            in_seq.append(in_s)
            k2s.append(jnp.where(in_s, kk * jnp.exp(jnp.minimum(glast - gc, 0.0)), 0.0).astype(BF16))
    outs = []
    for h in range(nh):
        cs = slice(h * HG_DIM, (h + 1) * HG_DIM)
        amat = None
        for qb, kb, same in zip(qbs, kbs, sames):
            term = jnp.where(same, _dot_nt(qb[:, cs], kb[:, cs]), 0.0)
            amat = term if amat is None else amat + term
        o = _dot(amat.astype(BF16), vb[:, cs])
        for s_i in range(nseq):
            st = get_st(s_i, h)
            o_int = _dot_nt(qg[:, cs], st.astype(BF16))
            o = o + (o_int if nseq == 1 else jnp.where(in_seq[s_i], o_int, 0.0))
            set_st(s_i, h, st * eglast[s_i][:, cs] + _dot_tn(vb[:, cs], k2s[s_i][:, cs]))
        outs.append(o * lax.rsqrt(jnp.mean(o * o, axis=-1, keepdims=True) + HG_NORM_EPS))
    on = jnp.concatenate(outs, axis=1) * nrm_w
    return on * (og * jax.nn.sigmoid(og))


def _hgrn_prompt_body(q_ref, f_ref, i_ref, og_ref, s0_ref, loglb_ref, log1m_ref, nw_ref, o_ref, sout_ref, st_ref,
                      *, chunk):
    nh = st_ref.shape[0]
    tb = pl.program_id(1)

    @pl.when(tb == 0)
    def _():
        for h in range(nh):
            st_ref[h] = s0_ref[h].T

    def set_st(s_i, h, v):
        st_ref[h] = v

    def body(ci, carry):
        rows = pl.ds(pl.multiple_of(ci * chunk, chunk), chunk)
        y = _hgrn_chunk(q_ref[rows, :], f_ref[rows, :], i_ref[rows, :], og_ref[rows, :],
                        lambda s_i, h: st_ref[h], set_st, loglb_ref[...], log1m_ref[...], nw_ref[...], chunk)
        o_ref[rows, :] = y.astype(BF16)
        return carry

    lax.fori_loop(0, q_ref.shape[0] // chunk, body, 0)

    @pl.when(tb == pl.num_programs(1) - 1)
    def _():
        for h in range(nh):
            sout_ref[h] = st_ref[h].T


def _hgrn_prompt(p_hg, s0, loglb, log1m, nw, *, bp, tp, chunk):
    hgw = p_hg.shape[1] // 4
    nh = hgw // HG_DIM
    tb = _pick_block(tp, 512, chunk)
    nt = tp // tb
    cspec = lambda c: pl.BlockSpec((tb, hgw), lambda b, t: (b * nt + t, c))
    pspec = pl.BlockSpec((1, hgw), lambda b, t: (0, 0))
    sspec = pl.BlockSpec((None, nh, HG_DIM, HG_DIM), lambda b, t: (b, 0, 0, 0))
    return pl.pallas_call(
        functools.partial(_hgrn_prompt_body, chunk=chunk),
        grid=(bp, nt),
        in_specs=[cspec(0), cspec(1), cspec(2), cspec(3), sspec, pspec, pspec, pspec],
        out_specs=[pl.BlockSpec((tb, hgw), lambda b, t: (b * nt + t, 0)), sspec],
        out_shape=[jax.ShapeDtypeStruct((bp * tp, hgw), BF16), jax.ShapeDtypeStruct(s0.shape, F32)],
        scratch_shapes=[pltpu.VMEM((nh, HG_DIM, HG_DIM), F32)],
        compiler_params=_cparams("parallel", "arbitrary"),
        name="hgrn_prompt",
    )(p_hg, p_hg, p_hg, p_hg, s0, loglb, log1m, nw)


def _hgrn_sample_body(q_ref, f_ref, i_ref, og_ref, s0_ref, loglb_ref, log1m_ref, nw_ref, *rest, seq_len, nseq):
    o_ref, sout_ref = rest[-2:]
    c = seq_len * nseq

    def body(gi, carry):
        rows = pl.ds(pl.multiple_of(gi * c, c), c)

        def set_st(s_i, h, v):
            sout_ref[gi * nseq + s_i, h] = v.T

        y = _hgrn_chunk(q_ref[rows, :], f_ref[rows, :], i_ref[rows, :], og_ref[rows, :],
                        lambda s_i, h: s0_ref[gi * nseq + s_i, h].T, set_st,
                        loglb_ref[...], log1m_ref[...], nw_ref[...], seq_len)
        o_ref[rows, :] = y.astype(BF16)
        return carry

    lax.fori_loop(0, q_ref.shape[0] // c, body, 0)


def _hgrn_sample(p_hg_bm, s0, layer, loglb, log1m, nw, *, bs, ts, s_prev=None):
    hgw = p_hg_bm.shape[1] // 4
    nh = hgw // HG_DIM
    nseq = max(1, 16 // ts)
    assert bs % nseq == 0
    bb = _pick_block(bs, 8, nseq)
    rows = bb * ts
    cspec = lambda c: pl.BlockSpec((rows, hgw), lambda i: (i, c))
    pspec = pl.BlockSpec((1, hgw), lambda i: (0, 0))
    sspec = pl.BlockSpec((bb, None, nh, HG_DIM, HG_DIM), lambda i: (i, layer, 0, 0, 0))
    operands = [p_hg_bm, p_hg_bm, p_hg_bm, p_hg_bm, s0, loglb, log1m, nw]
    in_specs = [cspec(0), cspec(1), cspec(2), cspec(3), sspec, pspec, pspec, pspec]
    aliases = {}
    if s_prev is not None:
        operands.append(s_prev)
        in_specs.append(pl.BlockSpec(memory_space=pl.ANY))
        aliases = {len(operands) - 1: 1}
    return pl.pallas_call(
        functools.partial(_hgrn_sample_body, seq_len=ts, nseq=nseq),
        grid=(bs // bb,),
        in_specs=in_specs,
        out_specs=[pl.BlockSpec((rows, hgw), lambda i: (i, 0)), sspec],
        out_shape=[jax.ShapeDtypeStruct((bs * ts, hgw), BF16), jax.ShapeDtypeStruct(s0.shape, F32)],
        input_output_aliases=aliases,
        compiler_params=_cparams("parallel"),
        name="hgrn_sample",
    )(*operands)


def _lru_gates(xconv, wa_ref, wx_ref, ba, bx, lam):
    l = xconv.shape[1]
    blk = l // LRU_BLOCKS
    ga, gx = [], []
    for n in range(LRU_BLOCKS):
        xb = xconv[:, n * blk:(n + 1) * blk].astype(BF16)
        ga.append(_dot(xb, wa_ref[n]))
        gx.append(_dot(xb, wx_ref[n]))
    ga = jnp.concatenate(ga, axis=1) + ba
    gx = jnp.concatenate(gx, axis=1) + bx
    log_a = -LRU_C * jax.nn.sigmoid(ga) * _softplus(-lam)
    a = jnp.exp(log_a)
    mult = jnp.sqrt(_neg_expm1(2.0 * log_a))
    return a, mult * jax.nn.sigmoid(gx) * xconv


def _gelu_tanh(x):
    c = 0.7978845608028654
    return 0.5 * x * (1.0 + jnp.tanh(c * (x + 0.044715 * (x * x * x))))


def _odd_prompt_body(gate_ref, xb_ref, conv0_ref, h0_ref, cw_ref, cb_ref, wa_ref, wx_ref, ba_ref, bx_ref, lam_ref,
                     y_ref, convo_ref, ho_ref, ext_ref, hc_ref, a_s, b_s, h_s):
    rows = xb_ref.shape[0]
    hist = SUBLANES

    @pl.when(pl.program_id(1) == 0)
    def _():
        ext_ref[0:hist, :] = conv0_ref[...]
        hc_ref[...] = h0_ref[...]

    ext_ref[hist:hist + rows, :] = xb_ref[...]
    cw = cw_ref[...]
    xconv = cw[0:1, :] * ext_ref[pl.ds(hist - (CONV_W - 1), rows), :]
    for j in range(1, CONV_W):
        xconv = xconv + cw[j:j + 1, :] * ext_ref[pl.ds(hist - (CONV_W - 1) + j, rows), :]
    xconv = cb_ref[...] + xconv
    a, b = _lru_gates(xconv, wa_ref, wx_ref, ba_ref[...], bx_ref[...], lam_ref[...])
    r8 = lax.broadcasted_iota(jnp.int32, (rows, 1), 0) & (SUBLANES - 1)
    for s in (1, 2, 4):
        m = r8 >= s
        a_sh = pltpu.roll(a, s, 0)
        b_sh = pltpu.roll(b, s, 0)
        b = jnp.where(m, a * b_sh + b, b)
        a = jnp.where(m, a * a_sh, a)
    a_s[...] = a
    b_s[...] = b

    def tile(j, hc):
        r0 = pl.multiple_of(j * SUBLANES, SUBLANES)
        h = a_s[pl.ds(r0, SUBLANES), :] * hc + b_s[pl.ds(r0, SUBLANES), :]
        h_s[pl.ds(r0, SUBLANES), :] = h
        return h[SUBLANES - 1:SUBLANES, :]

    hc = lax.fori_loop(0, rows // SUBLANES, tile, hc_ref[...])
    hc_ref[...] = hc
    ho_ref[...] = hc
    y_ref[...] = (h_s[...] * _gelu_tanh(gate_ref[...])).astype(BF16)
    tail = ext_ref[rows:rows + hist, :]
    ext_ref[0:hist, :] = tail
    convo_ref[...] = tail


def _odd_prompt(p_od, conv0, h0, layer, cw, cb, wa, wx, ba, bx, lam, *, bp, tp):
    l = p_od.shape[1] // 2
    rows = _pick_block(tp, 256, SUBLANES)
    nt = tp // rows
    lsel = lambda a: pl.BlockSpec((None,) + a.shape[1:], lambda b, t: (layer,) + (0,) * (a.ndim - 1))
    return pl.pallas_call(
        _odd_prompt_body,
        grid=(bp, nt),
        in_specs=[
            pl.BlockSpec((rows, l), lambda b, t: (b * nt + t, 0)),
            pl.BlockSpec((rows, l), lambda b, t: (b * nt + t, 1)),
            pl.BlockSpec((None, SUBLANES, l), lambda b, t: (b, 0, 0)),
            pl.BlockSpec((None, 1, l), lambda b, t: (b, 0, 0)),
            lsel(cw), lsel(cb), lsel(wa), lsel(wx), lsel(ba), lsel(bx), lsel(lam),
        ],
        out_specs=[
            pl.BlockSpec((rows, l), lambda b, t: (b * nt + t, 0)),
            pl.BlockSpec((None, SUBLANES, l), lambda b, t: (b, 0, 0)),
            pl.BlockSpec((None, 1, l), lambda b, t: (b, 0, 0)),
        ],
        out_shape=[jax.ShapeDtypeStruct((bp * tp, l), BF16), jax.ShapeDtypeStruct((bp, SUBLANES, l), F32),
                   jax.ShapeDtypeStruct((bp, 1, l), F32)],
        scratch_shapes=[pltpu.VMEM((rows + SUBLANES, l), F32), pltpu.VMEM((1, l), F32),
                        pltpu.VMEM((rows, l), F32), pltpu.VMEM((rows, l), F32), pltpu.VMEM((rows, l), F32)],
        compiler_params=_cparams("parallel", "arbitrary"),
        name="odd_prompt",
    )(p_od, p_od, conv0, h0, cw, cb, wa, wx, ba, bx, lam)


def _odd_sample_body(gate_ref, xb_ref, conv0_ref, h0_ref, cw_ref, cb_ref, wa_ref, wx_ref, ba_ref, bx_ref, lam_ref,
                     y_ref, convo_ref, ho_ref):
    @pl.when(pl.program_id(0) == 0)
    def _():
        convo_ref[...] = conv0_ref[...]
        ho_ref[...] = h0_ref[...]

    xb = xb_ref[...]
    cw = cw_ref[...]
    xconv = cw[CONV_W - 1:CONV_W, :] * xb
    for j in range(CONV_W - 1):
        xconv = xconv + cw[j:j + 1, :] * convo_ref[j]
    xconv = cb_ref[...] + xconv
    a, b = _lru_gates(xconv, wa_ref, wx_ref, ba_ref[...], bx_ref[...], lam_ref[...])
    h = a * ho_ref[...] + b
    ho_ref[...] = h
    y_ref[...] = (h * _gelu_tanh(gate_ref[...])).astype(BF16)
    for j in range(CONV_W - 2):
        convo_ref[j] = convo_ref[j + 1]
    convo_ref[CONV_W - 2] = xb


def _odd_sample(p_od, conv0_t, h0, layer, cw, cb, wa, wx, ba, bx, lam, *, mp, bs, ts):
    l = p_od.shape[1] // 2
    assert mp % bs == 0
    off = mp // bs
    lsel = lambda a: pl.BlockSpec((None,) + a.shape[1:], lambda t: (layer,) + (0,) * (a.ndim - 1))
    full = lambda a: pl.BlockSpec(a.shape, lambda t: (0,) * a.ndim)
    return pl.pallas_call(
        _odd_sample_body,
        grid=(ts,),
        in_specs=[
            pl.BlockSpec((bs, l), lambda t: (off + t, 0)),
            pl.BlockSpec((bs, l), lambda t: (off + t, 1)),
            full(conv0_t), full(h0),
            lsel(cw), lsel(cb), lsel(wa), lsel(wx), lsel(ba), lsel(bx), lsel(lam),
        ],
        out_specs=[pl.BlockSpec((bs, l), lambda t: (t, 0)), full(conv0_t), full(h0)],
        out_shape=[jax.ShapeDtypeStruct((bs * ts, l), BF16), jax.ShapeDtypeStruct(conv0_t.shape, F32),
                   jax.ShapeDtypeStruct(h0.shape, F32)],
        compiler_params=_cparams("arbitrary"),
        name="odd_sample",
    )(p_od, p_od, conv0_t, h0, cw, cb, wa, wx, ba, bx, lam)


def _pad_last(a, n):
    return jnp.pad(a, [(0, 0)] * (a.ndim - 1) + [(0, n - a.shape[-1])])


def _round_up(n, m):
    return (n + m - 1) // m * m


class _RwLayout:
    def __init__(self, rw, dw, da, dg):
        self.rw, self.dw, self.da, self.dg = rw, dw, da, dg
        self.dwp, self.dap, self.dgp = (_round_up(d, LANES) for d in (dw, da, dg))
        self.width = 3 * rw + dw + da + dg
        self.padded = 3 * rw + self.dwp + self.dap + self.dgp

    def pad(self, a):
        o = 3 * self.rw
        parts = [a[..., :o], _pad_last(a[..., o:o + self.dw], self.dwp),
                 _pad_last(a[..., o + self.dw:o + self.dw + self.da], self.dap),
                 _pad_last(a[..., o + self.dw + self.da:self.width], self.dgp)]
        return jnp.concatenate(parts, axis=-1)

    def unpad(self, a):
        o = 3 * self.rw
        parts = [a[..., :o], a[..., o:o + self.dw], a[..., o + self.dwp:o + self.dwp + self.da],
                 a[..., o + self.dwp + self.dap:o + self.dwp + self.dap + self.dg]]
        return jnp.concatenate(parts, axis=-1)


def kernel(x_prompt, x_sample, state_rwkv_shift, state_rwkv, state_hgrn, state_conv, state_lru, ffn1_norm, ffn1_w_gu, ffn1_w_down, mix_norm, ffn2_norm, ffn2_w_gu, ffn2_w_down, ev_w_in, rw_mu, rw_w0, rw_w2, rw_a0, rw_a2, rw_g2, rw_k_k, rw_k_a, rw_r_k, rw_ln_w, rw_ln_b, hg_lb, hg_norm, ev_w_out, od_w_in, conv_w, conv_b, lru_wa, lru_ba, lru_wx, lru_bx, lru_lambda, od_w_out, final_norm):
    bp, tp, d = x_prompt.shape
    bs, ts, _ = x_sample.shape
    depth = ffn1_norm.shape[0]
    n_even = ev_w_in.shape[0]
    mp, ms = bp * tp, bs * ts
    rw = rw_w0.shape[1]
    nh_rw = rw // RW_HEAD
    hgw = hg_norm.shape[1]
    nh_hg = hgw // HG_DIM
    lay = _RwLayout(rw, rw_w2.shape[1], rw_a2.shape[1], rw_g2.shape[1])
    assert bp * nh_rw * 2 == LANES and (bs * nh_rw) % LANES == 0
    dt = x_prompt.dtype

    ffn_src = {1: (ffn1_norm, ffn1_w_gu, ffn1_w_down), 2: (ffn2_norm, ffn2_w_gu, ffn2_w_down)}
    ffn_w = (ffn1_w_gu[0].astype(BF16), ffn1_w_down[0].astype(BF16))

    def ffn_step(x, l, which, ffn_w):
        norm_g = ffn_src[which][0]
        nxt = (l, 2) if which == 1 else (l + 1, 1)
        if nxt[0] >= depth:
            return _ffn(x, norm_g, l, *ffn_w)[0], None
        x, c_gu, c_dn = _ffn(x, norm_g, l, *ffn_w, next_src=(ffn_src[nxt[1]][1], ffn_src[nxt[1]][2], nxt[0]))
        return x, (c_gu, c_dn)

    ev_out = ev_w_out.astype(BF16)
    od_in, od_out = od_w_in.astype(BF16), od_w_out.astype(BF16)
    wa_bf, wx_bf = lru_wa.astype(BF16), lru_wx.astype(BF16)

    x = jnp.concatenate([x_prompt.reshape(mp, d), x_sample.transpose(1, 0, 2).reshape(ms, d)], axis=0)

    lbs = jnp.cumsum(jax.nn.softmax(hg_lb.astype(F32), axis=0), axis=0)
    lb_all = lbs - lbs[0]

    def k_lanes_p(a):
        return a.reshape(tp, bp * nh_rw, RW_HEAD).swapaxes(1, 2)

    def v_lanes_p(a):
        return a.reshape(tp, bp, nh_rw, 2, RW_HEAD // 2).transpose(0, 4, 3, 1, 2).reshape(tp, RW_HEAD // 2, LANES)

    def v_lanes_p_inv(a):
        return a.reshape(tp, RW_HEAD // 2, 2, bp, nh_rw).transpose(3, 0, 4, 2, 1).reshape(mp, rw)

    def k_lanes_s(a):
        return a.reshape(ts, bs, nh_rw, RW_HEAD).transpose(0, 3, 1, 2).reshape(ts, RW_HEAD, bs * nh_rw)

    def k_lanes_s_inv(a):
        return a.reshape(ts, RW_HEAD, bs, nh_rw).transpose(0, 2, 3, 1).reshape(ms, rw)

    def kparam(p):
        return jnp.tile(p.reshape(nh_rw, RW_HEAD).T, (1, LANES // nh_rw))

    def vparam_p(p):
        a = p.reshape(nh_rw, 2, RW_HEAD // 2).transpose(2, 1, 0)[:, :, None, :]
        return jnp.broadcast_to(a, (RW_HEAD // 2, 2, bp, nh_rw)).reshape(RW_HEAD // 2, LANES)

    shifts_p, shifts_s, rws_p, hgs_p, convs_p, convs_s, lrus_p, lrus_s = ([] for _ in range(8))
    rw_rows = state_rwkv.astype(F32).reshape(bs, n_even, nh_rw, RW_HEAD * RW_HEAD)
    rw_rows_new = None
    hg_new = None

    for l in range(depth):
        x, ffn_w = ffn_step(x, l, 1, ffn_w)
        if l % 2 == 0:
            e = l // 2
            w_in = ev_w_in[e]
            w_rw = lay.pad(w_in[:, :lay.width]).astype(BF16)
            w_hg = w_in[:, lay.width:].astype(BF16)
            p_rw = _norm_proj(x, mix_norm, l, w_rw, _pick_block(lay.padded, 1792, LANES))
            p_hg = _norm_proj(x, mix_norm, l, w_hg, _pick_block(4 * hgw, 1024, LANES))
            shift_p = jnp.zeros((bp, lay.padded), F32)
            shift_s = lay.pad(state_rwkv_shift[:, e].astype(F32))
            rkvwa_p, rkvwa_s, g_ = _rwkv_prep(
                p_rw, shift_p, shift_s, lay.pad(rw_mu[e])[None], rw_w0[e][None],
                _pad_last(rw_w2[e].T, lay.dwp).T.astype(BF16), rw_a0[e][None],
                _pad_last(rw_a2[e].T, lay.dap).T.astype(BF16),
                _pad_last(rw_g2[e].T, lay.dgp).T.astype(BF16), rw=rw, mp=mp, tp=tp, bs=bs)
            shifts_p.append(lay.unpad(p_rw[tp - 1:mp:tp]))
            shifts_s.append(lay.unpad(p_rw[mp + (ts - 1) * bs:]))
            kk_p, ka_p = kparam(rw_k_k[e]), kparam(rw_k_a[e])
            rk_p = kparam(rw_r_k[e].reshape(rw))
            r_p, k_p, v_p, w_p, a_p = rkvwa_p
            o_p, s_p = _rwkv_scan(
                k_lanes_p(r_p), k_lanes_p(w_p), k_lanes_p(k_p), k_lanes_p(a_p), v_lanes_p(v_p),
                kk_p, ka_p, rk_p, vparam_p(rw_ln_w[e]), vparam_p(rw_ln_b[e]),
                jnp.zeros((RW_HEAD // 2, RW_HEAD, LANES), F32), tc=_pick_block(tp, 64, 1), vsplit=True)
            rws_p.append(s_p.reshape(RW_HEAD // 2, RW_HEAD, 2, bp, nh_rw).transpose(3, 4, 2, 0, 1)
                         .reshape(bp, nh_rw, RW_HEAD, RW_HEAD))
            r_s, k_s, v_s, w_s, a_s = rkvwa_s
            o_s, rw_rows_new = _rwkv_scan(
                k_lanes_s(r_s), k_lanes_s(w_s), k_lanes_s(k_s), k_lanes_s(a_s), k_lanes_s(v_s),
                kk_p, ka_p, rk_p, kparam(rw_ln_w[e]), kparam(rw_ln_b[e]),
                rw_rows, tc=ts, vsplit=False, layer=e, s_prev=rw_rows_new)
            o_rw = jnp.concatenate([v_lanes_p_inv(o_p), k_lanes_s_inv(o_s)], axis=0)
            lb = lb_all[e][None]
            loglb, log1m = jnp.log(lb), jnp.log1p(-lb)
            nw = hg_norm[e][None]
            chunk = _pick_block(tp, HG_CHUNK, 1)
            oh_p, hs_p = _hgrn_prompt(p_hg, jnp.zeros((bp, nh_hg, HG_DIM, HG_DIM), F32), loglb, log1m, nw,
                                      bp=bp, tp=tp, chunk=chunk)
            p_hg_bm = p_hg[mp:].reshape(ts, bs, 4 * hgw).transpose(1, 0, 2).reshape(ms, 4 * hgw)
            oh_s, hg_new = _hgrn_sample(p_hg_bm, state_hgrn.astype(F32), e, loglb, log1m, nw, bs=bs, ts=ts,
                                        s_prev=hg_new)
            oh_s = oh_s.reshape(bs, ts, hgw).transpose(1, 0, 2).reshape(ms, hgw)
            hgs_p.append(hs_p)
            o_hg = jnp.concatenate([oh_p, oh_s], axis=0)
            x = _proj_res_even(x, o_rw, g_, o_hg, ev_out[e])
        else:
            o_i = l // 2
            p_od = _norm_proj(x, mix_norm, l, od_in[o_i], _pick_block(od_in.shape[-1], 1024, LANES))
            lsz = p_od.shape[1] // 2
            vec = lambda a: a[:, None, :]
            args = (conv_w, vec(conv_b), wa_bf, wx_bf, vec(lru_ba), vec(lru_bx), vec(lru_lambda))
            conv0_p = jnp.zeros((bp, SUBLANES, lsz), F32)
            y_p, c_p, h_p = _odd_prompt(p_od, conv0_p, jnp.zeros((bp, 1, lsz), F32), o_i, *args, bp=bp, tp=tp)
            conv0_s = state_conv[:, o_i].astype(F32).transpose(1, 0, 2)
            y_s, c_s, h_s = _odd_sample(p_od, conv0_s, state_lru[:, o_i].astype(F32), o_i, *args,
                                        mp=mp, bs=bs, ts=ts)
            convs_p.append(c_p[:, SUBLANES - (CONV_W - 1):])
            convs_s.append(c_s.transpose(1, 0, 2))
            lrus_p.append(h_p[:, 0])
            lrus_s.append(h_s)
            x = _proj_res(x, jnp.concatenate([y_p, y_s], axis=0), od_out[o_i])
        x, ffn_w = ffn_step(x, l, 2, ffn_w)

    y_p, y_s = _final_norm(x, final_norm, mp)
    y_prompt = y_p.reshape(bp, tp, d)
    y_sample = y_s.reshape(ts, bs, d).transpose(1, 0, 2)
    st = lambda xs: jnp.stack(xs, axis=1).astype(dt)
    sample_rwkv = rw_rows_new.reshape(state_rwkv.shape).astype(dt)
    return (y_prompt, y_sample, st(shifts_p), st(rws_p), st(hgs_p), st(convs_p), st(lrus_p),
            st(shifts_s), sample_rwkv, hg_new.astype(dt), st(convs_s), st(lrus_s))
```

```python
import functools
import math

import jax
import jax.numpy as jnp
from jax import lax
from jax.experimental import pallas as pl
from jax.experimental.pallas import tpu as pltpu

F32 = jnp.float32
BF16 = jnp.bfloat16

NORM_EPS = 1e-6
RW_GN_EPS = 64e-5
HG_NORM_EPS = 1e-5
LRU_C = 8.0
RW_HEAD = 64
HG_DIM = 128
LRU_BLOCKS = 8
CONV_W = 4
LANES = 128
SUBLANES = 8
VMEM_LIMIT = 56 * 1024 * 1024
MAX_BM = 1088
HG_CHUNK = 64


def _cparams(*sem):
    return pltpu.CompilerParams(dimension_semantics=sem, vmem_limit_bytes=VMEM_LIMIT)


def _pick_block(n, cap, mult):
    best = None
    for d in range(mult, min(n, cap) + 1, mult):
        if n % d == 0:
            best = d
    assert best is not None, (n, cap, mult)
    return best


def _gcd(a, b):
    return math.gcd(a, b)


def _dot(a, b):
    return jnp.dot(a, b, preferred_element_type=F32)


def _dot_nt(a, b):
    return lax.dot_general(a, b, (((1,), (1,)), ((), ())), preferred_element_type=F32)


def _dot_tn(a, b):
    return lax.dot_general(a, b, (((0,), (0,)), ((), ())), preferred_element_type=F32)


def _softplus(x):
    return jnp.maximum(x, 0.0) + jnp.log1p(jnp.exp(-jnp.abs(x)))


def _neg_expm1(y):
    return 1.0 - jnp.exp(y)


def _rms_rows(x, g):
    ms = jnp.mean(x * x, axis=-1, keepdims=True)
    return x * lax.rsqrt(ms + NORM_EPS) * g


def _ffn_body(x_ref, g_ref, wg_ref, wu_ref, wd_ref, *rest, cast_next):
    if cast_next:
        sgu_ref, sdn_ref, o_ref, cgu_ref, cdn_ref, xn_ref = rest
        cgu_ref[...] = sgu_ref[...].astype(BF16)
        cdn_ref[...] = sdn_ref[...].astype(BF16)
    else:
        o_ref, xn_ref = rest

    @pl.when(pl.program_id(1) == 0)
    def _():
        x = x_ref[...]
        xn_ref[...] = _rms_rows(x, g_ref[...]).astype(BF16)
        o_ref[...] = x

    xn = xn_ref[...]
    gate = _dot(xn, wg_ref[...])
    up = _dot(xn, wu_ref[...])
    act = (0.5 * (gate * jax.nn.sigmoid(gate)) * up).astype(BF16)
    o_ref[...] += _dot(act, wd_ref[...])


def _ffn(x, norm_g, layer, w_gu, w_down, next_src=None):
    m, d = x.shape
    f = w_down.shape[0]
    bm = _pick_block(m, MAX_BM, 16)
    bf = _pick_block(f, 512, LANES)
    ni, nf = m // bm, f // bf
    in_specs = [
        pl.BlockSpec((bm, d), lambda i, j: (i, 0), pipeline_mode=pl.Buffered(1)),
        pl.BlockSpec((None, 1, d), lambda i, j: (layer, 0, 0)),
        pl.BlockSpec((d, bf), lambda i, j: (0, j)),
        pl.BlockSpec((d, bf), lambda i, j: (0, j + nf)),
        pl.BlockSpec((bf, d), lambda i, j: (j, 0)),
    ]
    operands = [x, norm_g[:, None, :], w_gu, w_gu, w_down]
    out_specs = [pl.BlockSpec((bm, d), lambda i, j: (i, 0))]
    out_shape = [jax.ShapeDtypeStruct((m, d), F32)]
    if next_src is not None:
        s_gu, s_dn, nl = next_src
        assert d % ni == 0 and (2 * f) % nf == 0 and (d // ni) % 16 == 0 and (d // ni) % LANES == 0
        gu_blk, dn_blk = (d // ni, 2 * f // nf), (f // nf, d // ni)
        in_specs += [pl.BlockSpec((None,) + gu_blk, lambda i, j: (nl, i, j)),
                     pl.BlockSpec((None,) + dn_blk, lambda i, j: (nl, j, i))]
        operands += [s_gu, s_dn]
        out_specs += [pl.BlockSpec(gu_blk, lambda i, j: (i, j)), pl.BlockSpec(dn_blk, lambda i, j: (j, i))]
        out_shape += [jax.ShapeDtypeStruct((d, 2 * f), BF16), jax.ShapeDtypeStruct((f, d), BF16)]
    return pl.pallas_call(
        functools.partial(_ffn_body, cast_next=next_src is not None),
        grid=(ni, nf),
        in_specs=in_specs,
        out_specs=out_specs,
        out_shape=out_shape,
        scratch_shapes=[pltpu.VMEM((bm, d), BF16)],
        compiler_params=_cparams("parallel", "arbitrary"),
        name="ffn",
    )(*operands)


def _norm_proj_body(x_ref, g_ref, w_ref, o_ref, xn_ref):
    @pl.when(pl.program_id(1) == 0)
    def _():
        xn_ref[...] = _rms_rows(x_ref[...], g_ref[...]).astype(BF16)

    o_ref[...] = _dot(xn_ref[...], w_ref[...])


def _norm_proj(x, norm_g, layer, w, bn):
    m, d = x.shape
    n = w.shape[-1]
    bm = _pick_block(m, MAX_BM, 16)
    return pl.pallas_call(
        _norm_proj_body,
        grid=(m // bm, n // bn),
        in_specs=[
            pl.BlockSpec((bm, d), lambda i, j: (i, 0)),
            pl.BlockSpec((None, 1, d), lambda i, j: (layer, 0, 0)),
            pl.BlockSpec((d, bn), lambda i, j: (0, j)),
        ],
        out_specs=pl.BlockSpec((bm, bn), lambda i, j: (i, j)),
        out_shape=jax.ShapeDtypeStruct((m, n), F32),
        scratch_shapes=[pltpu.VMEM((bm, d), BF16)],
        compiler_params=_cparams("parallel", "arbitrary"),
        name="norm_proj",
    )(x, norm_g[:, None, :], w)


def _proj_res_body(x_ref, y_ref, w_ref, o_ref):
    o_ref[...] = x_ref[...] + _dot(y_ref[...], w_ref[...])


def _proj_res(x, y, w):
    m, d = x.shape
    k = y.shape[1]
    bm = _pick_block(m, MAX_BM, 16)
    bn = _pick_block(d, 1024, LANES)
    return pl.pallas_call(
        _proj_res_body,
        grid=(m // bm, d // bn),
        in_specs=[
            pl.BlockSpec((bm, bn), lambda i, j: (i, j)),
            pl.BlockSpec((bm, k), lambda i, j: (i, 0)),
            pl.BlockSpec((k, bn), lambda i, j: (0, j)),
        ],
        out_specs=pl.BlockSpec((bm, bn), lambda i, j: (i, j)),
        out_shape=jax.ShapeDtypeStruct((m, d), F32),
        compiler_params=_cparams("parallel", "arbitrary"),
        name="proj_res",
    )(x, y, w)


def _proj_res_even_body(x_ref, orw_ref, g_ref, ohg_ref, w1_ref, w2_ref, o_ref):
    y1 = (orw_ref[...] * g_ref[...]).astype(BF16)
    o_ref[...] = x_ref[...] + _dot(y1, w1_ref[...]) + _dot(ohg_ref[...], w2_ref[...])


def _proj_res_even(x, o_rw, g, o_hg, w):
    m, d = x.shape
    rw = o_rw.shape[1]
    hg = o_hg.shape[1]
    assert rw == hg
    bm = _pick_block(m, MAX_BM, 16)
    bn = _pick_block(d, 1024, LANES)
    return pl.pallas_call(
        _proj_res_even_body,
        grid=(m // bm, d // bn),
        in_specs=[
            pl.BlockSpec((bm, bn), lambda i, j: (i, j)),
            pl.BlockSpec((bm, rw), lambda i, j: (i, 0)),
            pl.BlockSpec((bm, rw), lambda i, j: (i, 0)),
            pl.BlockSpec((bm, hg), lambda i, j: (i, 0)),
            pl.BlockSpec((rw, bn), lambda i, j: (0, j)),
            pl.BlockSpec((hg, bn), lambda i, j: (1, j)),
        ],
        out_specs=pl.BlockSpec((bm, bn), lambda i, j: (i, j)),
        out_shape=jax.ShapeDtypeStruct((m, d), F32),
        compiler_params=_cparams("parallel", "arbitrary"),
        name="proj_res_even",
    )(x, o_rw, g, o_hg, w, w)


def _final_norm_body(x_ref, g_ref, op_ref, os_ref, *, n_prompt_blk):
    y = _rms_rows(x_ref[...], g_ref[...])
    i = pl.program_id(0)

    @pl.when(i < n_prompt_blk)
    def _():
        op_ref[...] = y

    @pl.when(i >= n_prompt_blk)
    def _():
        os_ref[...] = y


def _final_norm(x, g, mp):
    m, d = x.shape
    ms = m - mp
    bm = _pick_block(_gcd(mp, ms), MAX_BM, SUBLANES)
    npb = mp // bm
    return pl.pallas_call(
        functools.partial(_final_norm_body, n_prompt_blk=npb),
        grid=(m // bm,),
        in_specs=[pl.BlockSpec((bm, d), lambda i: (i, 0)), pl.BlockSpec((1, d), lambda i: (0, 0))],
        out_specs=[pl.BlockSpec((bm, d), lambda i: (jnp.minimum(i, npb - 1), 0)),
                   pl.BlockSpec((bm, d), lambda i: (jnp.maximum(i - npb, 0), 0))],
        out_shape=[jax.ShapeDtypeStruct((mp, d), F32), jax.ShapeDtypeStruct((ms, d), F32)],
        compiler_params=_cparams("arbitrary"),
        name="final_norm",
    )(x, g[None, :])


def _rwkv_prep_body(cur_ref, prevblk_ref, shiftp_ref, shifts_ref, mu_ref, w0_ref, w2_ref, a0_ref, a2_ref,
                    g2_ref, rp_ref, kp_ref, vp_ref, wp_ref, ap_ref, rs_ref, ks_ref, vs_ref, ws_ref, as_ref,
                    g_ref, *, rw, dwp, dap, n_prompt_blk, blk_per_seq):
    i = pl.program_id(0)
    cur = cur_ref[...]
    prevblk = prevblk_ref[...]
    rows = cur.shape[0]
    seq = jnp.minimum(i // blk_per_seq, shiftp_ref.shape[0] - 1)
    first = (i % blk_per_seq) == 0
    row0 = jnp.where(first, shiftp_ref[pl.ds(seq, 1), :], prevblk[rows - 1:rows, :])
    rid = lax.broadcasted_iota(jnp.int32, (rows, 1), 0)
    prev_p = jnp.where(rid == 0, row0, pltpu.roll(cur, 1, 0))
    prev_s = jnp.where(i == n_prompt_blk, shifts_ref[...], prevblk)
    prev = jnp.where(i >= n_prompt_blk, prev_s, prev_p)

    z = cur + (prev - cur) * mu_ref[...]
    o = 3 * rw
    zw = z[:, o:o + dwp]
    za = z[:, o + dwp:o + dwp + dap]
    zg = z[:, o + dwp + dap:]
    lw = w0_ref[...] + _dot(jnp.tanh(zw).astype(BF16), w2_ref[...])
    w_log = -_softplus(-lw) - 0.5
    decay = jnp.exp(-jnp.exp(w_log))
    a = jax.nn.sigmoid(a0_ref[...] + _dot(za.astype(BF16), a2_ref[...]))
    g_ref[...] = _dot(jax.nn.sigmoid(zg).astype(BF16), g2_ref[...])

    def write(r_ref, k_ref, v_ref, w_ref, a_ref):
        r_ref[...] = z[:, :rw]
        k_ref[...] = z[:, rw:2 * rw]
        v_ref[...] = z[:, 2 * rw:3 * rw]
        w_ref[...] = decay
        a_ref[...] = a

    @pl.when(i < n_prompt_blk)
    def _():
        write(rp_ref, kp_ref, vp_ref, wp_ref, ap_ref)

    @pl.when(i >= n_prompt_blk)
    def _():
        write(rs_ref, ks_ref, vs_ref, ws_ref, as_ref)


def _rwkv_prep(p_rw, shift_p, shift_s, mu, w0, w2, a0, a2, g2, *, rw, mp, tp, bs):
    m, prw = p_rw.shape
    ms = m - mp
    assert mp % bs == 0 and tp % bs == 0 and ms % bs == 0
    npb = mp // bs
    dwp, dap = w2.shape[0], a2.shape[0]
    full = lambda a: pl.BlockSpec(a.shape, lambda i: (0,) * a.ndim)
    bps = tp // bs
    bp = mp // tp

    def prompt_idx(i):
        ic = jnp.minimum(i, npb - 1)
        return ic % bps, ic // bps

    spec_p = pl.BlockSpec((bs, rw), prompt_idx)
    spec_s = pl.BlockSpec((bs, rw), lambda i: (jnp.maximum(i - npb, 0), 0))
    body = functools.partial(_rwkv_prep_body, rw=rw, dwp=dwp, dap=dap, n_prompt_blk=npb, blk_per_seq=tp // bs)
    outs = pl.pallas_call(
        body,
        grid=(m // bs,),
        in_specs=[
            pl.BlockSpec((bs, prw), lambda i: (i, 0)),
            pl.BlockSpec((bs, prw), lambda i: (jnp.maximum(i - 1, 0), 0)),
            full(shift_p), full(shift_s), full(mu), full(w0), full(w2), full(a0), full(a2), full(g2),
        ],
        out_specs=[spec_p] * 5 + [spec_s] * 5 + [pl.BlockSpec((bs, rw), lambda i: (i, 0))],
        out_shape=[jax.ShapeDtypeStruct((tp, bp * rw), F32)] * 5 + [jax.ShapeDtypeStruct((ms, rw), F32)] * 5
        + [jax.ShapeDtypeStruct((m, rw), F32)],
        compiler_params=_cparams("arbitrary"),
        name="rwkv_prep",
    )(p_rw, p_rw, shift_p, shift_s, mu, w0, w2, a0, a2, g2)
    return outs[:5], outs[5:10], outs[10]


def _rwkv_scan_body(r_ref, w_ref, k_ref, a_ref, v_ref, kk_p_ref, ka_p_ref, rk_p_ref, lnw_ref, lnb_ref, s0_ref,
                    o_ref, sout_ref, s_ref, r_s, w_s, nkk_s, b_s, km_s, *, vsplit, state_rows):
    tc_len, nv, lanes = v_ref.shape
    kc = s_ref.shape[1]
    ti = pl.program_id(1)

    @pl.when(ti == 0)
    def _():
        if state_rows:
            s_ref[...] = s0_ref[...].reshape(lanes, nv * kc).T.reshape(nv, kc, lanes)
        else:
            s_ref[...] = s0_ref[...]

    def widen(x):
        return jnp.concatenate([x, x], axis=-1) if vsplit else x

    kvec = widen(k_ref[...])
    a_ = widen(a_ref[...])
    r_s[...] = widen(r_ref[...])
    w_s[...] = widen(w_ref[...])
    kk = kvec * kk_p_ref[...][None]
    nrm = jnp.sqrt(jnp.sum(kk * kk, axis=1, keepdims=True))
    kk = kk / jnp.maximum(nrm, 1e-12)
    nkk_s[...] = -kk
    b_s[...] = kk * a_
    km_s[...] = kvec * (1.0 + (a_ - 1.0) * ka_p_ref[...][None])

    def step(t, carry):
        nkk = nkk_s[t]
        b = b_s[t]
        km = km_s[t]
        w = w_s[t]
        r = r_s[t]
        vt = v_ref[t]
        for vi in range(nv):
            sv = s_ref[vi]
            sa = jnp.sum(sv * nkk, axis=0, keepdims=True)
            sv = sv * w + sa * b + vt[vi:vi + 1, :] * km
            s_ref[vi] = sv
            o_ref[t, pl.ds(vi, 1), :] = jnp.sum(sv * r, axis=0, keepdims=True)
        return carry

    lax.fori_loop(0, tc_len, step, 0)

    o = o_ref[...]
    n_val = nv * (2 if vsplit else 1)

    def head_sum(x):
        if vsplit:
            x2 = x.reshape(tc_len * nv, lanes)
            x = (x2 + pltpu.roll(x2, lanes // 2, 1)).reshape(tc_len, nv, lanes)
        return jnp.sum(x, axis=1, keepdims=True)

    mu = head_sum(o) / n_val
    d = o - mu
    var = head_sum(d * d) / n_val
    o = d * lax.rsqrt(var + RW_GN_EPS) * lnw_ref[...][None] + lnb_ref[...][None]
    bonus = jnp.sum(r_s[...] * km_s[...] * rk_p_ref[...][None], axis=1, keepdims=True)
    o_ref[...] = o + bonus * v_ref[...]

    @pl.when(ti == pl.num_programs(1) - 1)
    def _():
        if state_rows:
            sout_ref[...] = s_ref[...].reshape(nv * kc, lanes).T.reshape(sout_ref.shape)
        else:
            sout_ref[...] = s_ref[...]


def _rwkv_scan(r, w, k, a, v, kk_p, ka_p, rk_p, lnw, lnb, s0, *, tc, vsplit, layer=None, after=None):
    t, kc, kl = r.shape
    nv = v.shape[1]
    ln = v.shape[2]
    g = ln // LANES
    kspec = pl.BlockSpec((tc, kc, kl // g), lambda gi, ti: (ti, 0, gi))
    vspec = pl.BlockSpec((tc, nv, LANES), lambda gi, ti: (ti, 0, gi))
    pk = pl.BlockSpec((kc, LANES), lambda gi, ti: (0, 0))
    pv = pl.BlockSpec((nv, LANES), lambda gi, ti: (0, 0))
    state_rows = not vsplit
    operands = [r, w, k, a, v, kk_p, ka_p, rk_p, lnw, lnb, s0]
    in_specs = [kspec, kspec, kspec, kspec, vspec, pk, pk, pk, pv, pv]
    aliases = {}
    if state_rows:
        nb, _, nh, sz = s0.shape
        assert sz == nv * kc and nb * nh == ln
        sspec = pl.BlockSpec((LANES // nh, None, nh, sz), lambda gi, ti: (gi, layer, 0, 0))
        in_specs.append(sspec)
        aliases = {10: 1}
        s_shape = s0.shape
    else:
        sspec = pl.BlockSpec((nv, kc, LANES), lambda gi, ti: (0, 0, gi))
        in_specs.append(sspec)
        s_shape = (nv, kc, ln)
    if after is not None:
        operands.append(after)
        in_specs.append(pl.BlockSpec(memory_space=pl.ANY))

    def body(*refs):
        n_in = len(operands)
        ins = refs[:11]
        _rwkv_scan_body(*ins, *refs[n_in:], vsplit=vsplit, state_rows=state_rows)

    return pl.pallas_call(
        body,
        grid=(g, t // tc),
        in_specs=in_specs,
        out_specs=[vspec, sspec],
        out_shape=[jax.ShapeDtypeStruct((t, nv, ln), F32), jax.ShapeDtypeStruct(s_shape, F32)],
        scratch_shapes=[pltpu.VMEM((nv, kc, LANES), F32)] + [pltpu.VMEM((tc, kc, LANES), F32)] * 5,
        input_output_aliases=aliases,
        compiler_params=_cparams("parallel", "arbitrary"),
        name="rwkv_scan",
    )(*operands)


def _ref_rows(gc, j):
    c, w = gc.shape
    b = 1 << j
    nt = c // SUBLANES
    gv = gc.reshape(nt, SUBLANES, w)
    if 2 * b <= SUBLANES:
        sub = lax.broadcasted_iota(jnp.int32, (1, SUBLANES, 1), 1)
        out = None
        for g0 in range(0, SUBLANES, 2 * b):
            piece = jnp.broadcast_to(gv[:, g0 + b - 1:g0 + b, :], gv.shape)
            out = piece if out is None else jnp.where(sub >= g0, piece, out)
        return out.reshape(c, w)
    tiles_per_group = 2 * b // SUBLANES
    pieces = []
    for ti in range(nt):
        src = (ti // tiles_per_group) * tiles_per_group + b // SUBLANES - 1
        pieces.append(jnp.broadcast_to(gv[src, SUBLANES - 1:SUBLANES, :], (SUBLANES, w)))
    return jnp.concatenate(pieces, axis=0)


def _hgrn_chunk(q, f, val, og, get_st, set_st, loglb, log1mlb, nrm_w, seq_len):
    c, width = q.shape
    nh = width // HG_DIM
    nseq = c // seq_len
    lg = seq_len.bit_length() - 1
    assert (1 << lg) == seq_len and c % SUBLANES == 0
    ls = jnp.minimum(f, 0.0) - jnp.log1p(jnp.exp(-jnp.abs(f)))
    b_ = log1mlb + ls
    g = jnp.maximum(loglb, b_) + jnp.log1p(jnp.exp(-jnp.abs(loglb - b_)))
    kk = _neg_expm1(g)
    qs = q * jax.nn.sigmoid(q)
    row = lax.broadcasted_iota(jnp.int32, (c, 1), 0)
    col = lax.broadcasted_iota(jnp.int32, (1, c), 1)
    tpos = row & (seq_len - 1)
    gc = g
    for j in range(lg):
        s = 1 << j
        gc = gc + jnp.where(tpos >= s, pltpu.roll(gc, s, 0), 0.0)
    qbs, kbs, sames = [qs.astype(BF16)], [kk.astype(BF16)], [row == col]
    for j in range(lg):
        e = jnp.exp(-jnp.abs(gc - _ref_rows(gc, j)))
        is_q = ((row >> j) & 1) == 1
        qbs.append(jnp.where(is_q, qs * e, 0.0).astype(BF16))
        kbs.append(jnp.where(is_q, 0.0, kk * e).astype(BF16))
        sames.append((row >> (j + 1)) == (col >> (j + 1)))
    vb = val.astype(BF16)
    qg = (qs * jnp.exp(gc)).astype(BF16)
    k2s, eglast, in_seq = [], [], []
    for s_i in range(nseq):
        glast = gc[s_i * seq_len + seq_len - 1:s_i * seq_len + seq_len, :]
        eglast.append(jnp.exp(glast))
        if nseq == 1:
            in_seq.append(None)
            k2s.append((kk * jnp.exp(glast - gc)).astype(BF16))
        else:
            in_s = (row >> lg) == s_i
            in_seq.append(in_s)
            k2s.append(jnp.where(in_s, kk * jnp.exp(jnp.minimum(glast - gc, 0.0)), 0.0).astype(BF16))
    outs = []
    for h in range(nh):
        cs = slice(h * HG_DIM, (h + 1) * HG_DIM)
        amat = None
        for qb, kb, same in zip(qbs, kbs, sames):
            term = jnp.where(same, _dot_nt(qb[:, cs], kb[:, cs]), 0.0)
            amat = term if amat is None else amat + term
        o = _dot(amat.astype(BF16), vb[:, cs])
        for s_i in range(nseq):
            st = get_st(s_i, h)
            o_int = _dot_nt(qg[:, cs], st.astype(BF16))
            o = o + (o_int if nseq == 1 else jnp.where(in_seq[s_i], o_int, 0.0))
            set_st(s_i, h, st * eglast[s_i][:, cs] + _dot_tn(vb[:, cs], k2s[s_i][:, cs]))
        outs.append(o * lax.rsqrt(jnp.mean(o * o, axis=-1, keepdims=True) + HG_NORM_EPS))
    on = jnp.concatenate(outs, axis=1) * nrm_w
    return on * (og * jax.nn.sigmoid(og))


def _hgrn_prompt_body(q_ref, f_ref, i_ref, og_ref, s0_ref, loglb_ref, log1m_ref, nw_ref, o_ref, sout_ref, st_ref,
                      *, chunk):
    nh = st_ref.shape[0]
    tb = pl.program_id(1)

    @pl.when(tb == 0)
    def _():
        for h in range(nh):
            st_ref[h] = s0_ref[h].T

    def set_st(s_i, h, v):
        st_ref[h] = v

    def body(ci, carry):
        rows = pl.ds(pl.multiple_of(ci * chunk, chunk), chunk)
        y = _hgrn_chunk(q_ref[rows, :], f_ref[rows, :], i_ref[rows, :], og_ref[rows, :],
                        lambda s_i, h: st_ref[h], set_st, loglb_ref[...], log1m_ref[...], nw_ref[...], chunk)
        o_ref[rows, :] = y.astype(BF16)
        return carry

    lax.fori_loop(0, q_ref.shape[0] // chunk, body, 0)

    @pl.when(tb == pl.num_programs(1) - 1)
    def _():
        for h in range(nh):
            sout_ref[h] = st_ref[h].T


def _hgrn_prompt(p_hg, s0, loglb, log1m, nw, *, bp, tp, chunk):
    hgw = p_hg.shape[1] // 4
    nh = hgw // HG_DIM
    tb = _pick_block(tp, 512, chunk)
    nt = tp // tb
    cspec = lambda c: pl.BlockSpec((tb, hgw), lambda b, t: (b * nt + t, c))
    pspec = pl.BlockSpec((1, hgw), lambda b, t: (0, 0))
    sspec = pl.BlockSpec((None, nh, HG_DIM, HG_DIM), lambda b, t: (b, 0, 0, 0))
    return pl.pallas_call(
        functools.partial(_hgrn_prompt_body, chunk=chunk),
        grid=(bp, nt),
        in_specs=[cspec(0), cspec(1), cspec(2), cspec(3), sspec, pspec, pspec, pspec],
        out_specs=[pl.BlockSpec((tb, hgw), lambda b, t: (b * nt + t, 0)), sspec],
        out_shape=[jax.ShapeDtypeStruct((bp * tp, hgw), BF16), jax.ShapeDtypeStruct(s0.shape, F32)],
        scratch_shapes=[pltpu.VMEM((nh, HG_DIM, HG_DIM), F32)],
        compiler_params=_cparams("parallel", "arbitrary"),
        name="hgrn_prompt",
    )(p_hg, p_hg, p_hg, p_hg, s0, loglb, log1m, nw)


def _hgrn_sample_body(q_ref, f_ref, i_ref, og_ref, s0_ref, loglb_ref, log1m_ref, nw_ref, *rest, seq_len, nseq):
    o_ref, sout_ref = rest[-2:]
    c = seq_len * nseq

    def body(gi, carry):
        rows = pl.ds(pl.multiple_of(gi * c, c), c)

        def set_st(s_i, h, v):
            sout_ref[gi * nseq + s_i, h] = v.T

        y = _hgrn_chunk(q_ref[rows, :], f_ref[rows, :], i_ref[rows, :], og_ref[rows, :],
                        lambda s_i, h: s0_ref[gi * nseq + s_i, h].T, set_st,
                        loglb_ref[...], log1m_ref[...], nw_ref[...], seq_len)
        o_ref[rows, :] = y.astype(BF16)
        return carry

    lax.fori_loop(0, q_ref.shape[0] // c, body, 0)


def _hgrn_sample(p_hg_bm, s0, layer, loglb, log1m, nw, *, bs, ts, s_prev=None, after=None):
    hgw = p_hg_bm.shape[1] // 4
    nh = hgw // HG_DIM
    nseq = max(1, 16 // ts)
    assert bs % nseq == 0
    bb = _pick_block(bs, 8, nseq)
    rows = bb * ts
    cspec = lambda c: pl.BlockSpec((rows, hgw), lambda i: (i, c))
    pspec = pl.BlockSpec((1, hgw), lambda i: (0, 0))
    sspec = pl.BlockSpec((bb, None, nh, HG_DIM, HG_DIM), lambda i: (i, layer, 0, 0, 0))
    operands = [p_hg_bm, p_hg_bm, p_hg_bm, p_hg_bm, s0, loglb, log1m, nw]
    in_specs = [cspec(0), cspec(1), cspec(2), cspec(3), sspec, pspec, pspec, pspec]
    aliases = {}
    if s_prev is not None:
        operands.append(s_prev)
        in_specs.append(pl.BlockSpec(memory_space=pl.ANY))
        aliases = {len(operands) - 1: 1}
    if after is not None:
        operands.append(after)
        in_specs.append(pl.BlockSpec(memory_space=pl.ANY))
    return pl.pallas_call(
        functools.partial(_hgrn_sample_body, seq_len=ts, nseq=nseq),
        grid=(bs // bb,),
        in_specs=in_specs,
        out_specs=[pl.BlockSpec((rows, hgw), lambda i: (i, 0)), sspec],
        out_shape=[jax.ShapeDtypeStruct((bs * ts, hgw), BF16), jax.ShapeDtypeStruct(s0.shape, F32)],
        input_output_aliases=aliases,
        compiler_params=_cparams("parallel"),
        name="hgrn_sample",
    )(*operands)


def _lru_gates(xconv, wa_ref, wx_ref, ba, bx, lam):
    l = xconv.shape[1]
    blk = l // LRU_BLOCKS
    ga, gx = [], []
    for n in range(LRU_BLOCKS):
        xb = xconv[:, n * blk:(n + 1) * blk].astype(BF16)
        ga.append(_dot(xb, wa_ref[n]))
        gx.append(_dot(xb, wx_ref[n]))
    ga = jnp.concatenate(ga, axis=1) + ba
    gx = jnp.concatenate(gx, axis=1) + bx
    log_a = -LRU_C * jax.nn.sigmoid(ga) * _softplus(-lam)
    a = jnp.exp(log_a)
    mult = jnp.sqrt(1.0 - a * a)
    return a, mult * jax.nn.sigmoid(gx) * xconv


def _gelu_tanh(x):
    c = 0.7978845608028654
    return 0.5 * x * (1.0 + jnp.tanh(c * (x + 0.044715 * (x * x * x))))


def _odd_prompt_body(gate_ref, xb_ref, conv0_ref, h0_ref, cw_ref, cb_ref, wa_ref, wx_ref, ba_ref, bx_ref, lam_ref,
                     y_ref, convo_ref, ho_ref, ext_ref, hc_ref, a_s, b_s, h_s):
    rows = xb_ref.shape[0]
    hist = SUBLANES

    @pl.when(pl.program_id(1) == 0)
    def _():
        ext_ref[0:hist, :] = conv0_ref[...]
        hc_ref[...] = h0_ref[...]

    ext_ref[hist:hist + rows, :] = xb_ref[...]
    cw = cw_ref[...]
    xconv = cw[0:1, :] * ext_ref[pl.ds(hist - (CONV_W - 1), rows), :]
    for j in range(1, CONV_W):
        xconv = xconv + cw[j:j + 1, :] * ext_ref[pl.ds(hist - (CONV_W - 1) + j, rows), :]
    xconv = cb_ref[...] + xconv
    a, b = _lru_gates(xconv, wa_ref, wx_ref, ba_ref[...], bx_ref[...], lam_ref[...])
    width = a.shape[1]
    a = a.reshape(rows // SUBLANES, SUBLANES, width)
    b = b.reshape(rows // SUBLANES, SUBLANES, width)
    r8 = lax.broadcasted_iota(jnp.int32, (1, SUBLANES, 1), 1)
    for s in (1, 2, 4):
        m = r8 >= s
        a_sh = pltpu.roll(a, s, 1)
        b_sh = pltpu.roll(b, s, 1)
        b = jnp.where(m, a * b_sh + b, b)
        a = jnp.where(m, a * a_sh, a)
    a_s[...] = a.reshape(rows, width)
    b_s[...] = b.reshape(rows, width)

    def tile(j, hc):
        r0 = pl.multiple_of(j * SUBLANES, SUBLANES)
        h = a_s[pl.ds(r0, SUBLANES), :] * hc + b_s[pl.ds(r0, SUBLANES), :]
        h_s[pl.ds(r0, SUBLANES), :] = h
        return h[SUBLANES - 1:SUBLANES, :]

    hc = lax.fori_loop(0, rows // SUBLANES, tile, hc_ref[...])
    hc_ref[...] = hc
    ho_ref[...] = hc
    y_ref[...] = (h_s[...] * _gelu_tanh(gate_ref[...])).astype(BF16)
    tail = ext_ref[rows:rows + hist, :]
    ext_ref[0:hist, :] = tail
    convo_ref[...] = tail


def _odd_prompt(p_od, conv0, h0, layer, cw, cb, wa, wx, ba, bx, lam, *, bp, tp):
    l = p_od.shape[1] // 2
    rows = _pick_block(tp, 256, SUBLANES)
    nt = tp // rows
    lsel = lambda a: pl.BlockSpec((None,) + a.shape[1:], lambda b, t: (layer,) + (0,) * (a.ndim - 1))
    return pl.pallas_call(
        _odd_prompt_body,
        grid=(bp, nt),
        in_specs=[
            pl.BlockSpec((rows, l), lambda b, t: (b * nt + t, 0)),
            pl.BlockSpec((rows, l), lambda b, t: (b * nt + t, 1)),
            pl.BlockSpec((None, SUBLANES, l), lambda b, t: (b, 0, 0)),
            pl.BlockSpec((None, 1, l), lambda b, t: (b, 0, 0)),
            lsel(cw), lsel(cb), lsel(wa), lsel(wx), lsel(ba), lsel(bx), lsel(lam),
        ],
        out_specs=[
            pl.BlockSpec((rows, l), lambda b, t: (b * nt + t, 0)),
            pl.BlockSpec((None, SUBLANES, l), lambda b, t: (b, 0, 0)),
            pl.BlockSpec((None, 1, l), lambda b, t: (b, 0, 0)),
        ],
        out_shape=[jax.ShapeDtypeStruct((bp * tp, l), BF16), jax.ShapeDtypeStruct((bp, SUBLANES, l), F32),
                   jax.ShapeDtypeStruct((bp, 1, l), F32)],
        scratch_shapes=[pltpu.VMEM((rows + SUBLANES, l), F32), pltpu.VMEM((1, l), F32),
                        pltpu.VMEM((rows, l), F32), pltpu.VMEM((rows, l), F32), pltpu.VMEM((rows, l), F32)],
        compiler_params=_cparams("parallel", "arbitrary"),
        name="odd_prompt",
    )(p_od, p_od, conv0, h0, cw, cb, wa, wx, ba, bx, lam)


def _odd_sample_body(gate_ref, xb_ref, conv0_ref, h0_ref, cw_ref, cb_ref, wa_ref, wx_ref, ba_ref, bx_ref, lam_ref,
                     y_ref, convo_ref, ho_ref):
    @pl.when(pl.program_id(0) == 0)
    def _():
        convo_ref[...] = conv0_ref[...]
        ho_ref[...] = h0_ref[...]

    xb = xb_ref[...]
    cw = cw_ref[...]
    xconv = cw[CONV_W - 1:CONV_W, :] * xb
    for j in range(CONV_W - 1):
        xconv = xconv + cw[j:j + 1, :] * convo_ref[j]
    xconv = cb_ref[...] + xconv
    a, b = _lru_gates(xconv, wa_ref, wx_ref, ba_ref[...], bx_ref[...], lam_ref[...])
    h = a * ho_ref[...] + b
    ho_ref[...] = h
    y_ref[...] = (h * _gelu_tanh(gate_ref[...])).astype(BF16)
    for j in range(CONV_W - 2):
        convo_ref[j] = convo_ref[j + 1]
    convo_ref[CONV_W - 2] = xb


def _odd_sample(p_od, conv0_t, h0, layer, cw, cb, wa, wx, ba, bx, lam, *, mp, bs, ts):
    l = p_od.shape[1] // 2
    assert mp % bs == 0
    off = mp // bs
    lsel = lambda a: pl.BlockSpec((None,) + a.shape[1:], lambda t: (layer,) + (0,) * (a.ndim - 1))
    full = lambda a: pl.BlockSpec(a.shape, lambda t: (0,) * a.ndim)
    return pl.pallas_call(
        _odd_sample_body,
        grid=(ts,),
        in_specs=[
            pl.BlockSpec((bs, l), lambda t: (off + t, 0)),
            pl.BlockSpec((bs, l), lambda t: (off + t, 1)),
            full(conv0_t), full(h0),
            lsel(cw), lsel(cb), lsel(wa), lsel(wx), lsel(ba), lsel(bx), lsel(lam),
        ],
        out_specs=[pl.BlockSpec((bs, l), lambda t: (t, 0)), full(conv0_t), full(h0)],
        out_shape=[jax.ShapeDtypeStruct((bs * ts, l), BF16), jax.ShapeDtypeStruct(conv0_t.shape, F32),
                   jax.ShapeDtypeStruct(h0.shape, F32)],
        compiler_params=_cparams("arbitrary"),
        name="odd_sample",
    )(p_od, p_od, conv0_t, h0, cw, cb, wa, wx, ba, bx, lam)


def _pad_last(a, n):
    return jnp.pad(a, [(0, 0)] * (a.ndim - 1) + [(0, n - a.shape[-1])])


def _round_up(n, m):
    return (n + m - 1) // m * m


class _RwLayout:
    def __init__(self, rw, dw, da, dg):
        self.rw, self.dw, self.da, self.dg = rw, dw, da, dg
        self.dwp, self.dap, self.dgp = (_round_up(d, LANES) for d in (dw, da, dg))
        self.width = 3 * rw + dw + da + dg
        self.padded = 3 * rw + self.dwp + self.dap + self.dgp

    def pad(self, a):
        o = 3 * self.rw
        parts = [a[..., :o], _pad_last(a[..., o:o + self.dw], self.dwp),
                 _pad_last(a[..., o + self.dw:o + self.dw + self.da], self.dap),
                 _pad_last(a[..., o + self.dw + self.da:self.width], self.dgp)]
        return jnp.concatenate(parts, axis=-1)

    def unpad(self, a):
        o = 3 * self.rw
        parts = [a[..., :o], a[..., o:o + self.dw], a[..., o + self.dwp:o + self.dwp + self.da],
                 a[..., o + self.dwp + self.dap:o + self.dwp + self.dap + self.dg]]
        return jnp.concatenate(parts, axis=-1)


def kernel(x_prompt, x_sample, state_rwkv_shift, state_rwkv, state_hgrn, state_conv, state_lru, ffn1_norm, ffn1_w_gu, ffn1_w_down, mix_norm, ffn2_norm, ffn2_w_gu, ffn2_w_down, ev_w_in, rw_mu, rw_w0, rw_w2, rw_a0, rw_a2, rw_g2, rw_k_k, rw_k_a, rw_r_k, rw_ln_w, rw_ln_b, hg_lb, hg_norm, ev_w_out, od_w_in, conv_w, conv_b, lru_wa, lru_ba, lru_wx, lru_bx, lru_lambda, od_w_out, final_norm):
    bp, tp, d = x_prompt.shape
    bs, ts, _ = x_sample.shape
    depth = ffn1_norm.shape[0]
    n_even = ev_w_in.shape[0]
    mp, ms = bp * tp, bs * ts
    rw = rw_w0.shape[1]
    nh_rw = rw // RW_HEAD
    hgw = hg_norm.shape[1]
    nh_hg = hgw // HG_DIM
    lay = _RwLayout(rw, rw_w2.shape[1], rw_a2.shape[1], rw_g2.shape[1])
    assert bp * nh_rw * 2 == LANES and (bs * nh_rw) % LANES == 0
    dt = x_prompt.dtype

    ffn_src = {1: (ffn1_norm, ffn1_w_gu, ffn1_w_down), 2: (ffn2_norm, ffn2_w_gu, ffn2_w_down)}
    ffn_w = (ffn1_w_gu[0].astype(BF16), ffn1_w_down[0].astype(BF16))

    def ffn_step(x, l, which, ffn_w):
        norm_g = ffn_src[which][0]
        nxt = (l, 2) if which == 1 else (l + 1, 1)
        if nxt[0] >= depth:
            return _ffn(x, norm_g, l, *ffn_w)[0], None
        x, c_gu, c_dn = _ffn(x, norm_g, l, *ffn_w, next_src=(ffn_src[nxt[1]][1], ffn_src[nxt[1]][2], nxt[0]))
        return x, (c_gu, c_dn)

    ev_out = ev_w_out.astype(BF16)
    od_in, od_out = od_w_in.astype(BF16), od_w_out.astype(BF16)
    wa_bf, wx_bf = lru_wa.astype(BF16), lru_wx.astype(BF16)

    x = jnp.concatenate([x_prompt.reshape(mp, d), x_sample.transpose(1, 0, 2).reshape(ms, d)], axis=0)

    lbs = jnp.cumsum(jax.nn.softmax(hg_lb.astype(F32), axis=0), axis=0)
    lb_all = lbs - lbs[0]

    def k_lanes_p(a):
        return a.reshape(tp, bp * nh_rw, RW_HEAD).swapaxes(1, 2)

    def v_lanes_p(a):
        return a.reshape(tp, bp, nh_rw, 2, RW_HEAD // 2).transpose(0, 4, 3, 1, 2).reshape(tp, RW_HEAD // 2, LANES)

    def v_lanes_p_inv(a):
        return a.reshape(tp, RW_HEAD // 2, 2, bp, nh_rw).transpose(3, 0, 4, 2, 1).reshape(mp, rw)

    def k_lanes_s(a):
        return a.reshape(ts, bs, nh_rw, RW_HEAD).transpose(0, 3, 1, 2).reshape(ts, RW_HEAD, bs * nh_rw)

    def k_lanes_s_inv(a):
        return a.reshape(ts, RW_HEAD, bs, nh_rw).transpose(0, 2, 3, 1).reshape(ms, rw)

    def kparam(p):
        return jnp.tile(p.reshape(nh_rw, RW_HEAD).T, (1, LANES // nh_rw))

    def vparam_p(p):
        a = p.reshape(nh_rw, 2, RW_HEAD // 2).transpose(2, 1, 0)[:, :, None, :]
        return jnp.broadcast_to(a, (RW_HEAD // 2, 2, bp, nh_rw)).reshape(RW_HEAD // 2, LANES)

    shifts_p, shifts_s, rws_p, hgs_p, convs_p, convs_s, lrus_p, lrus_s = ([] for _ in range(8))
    rw_rows = state_rwkv.astype(F32).reshape(bs, n_even, nh_rw, RW_HEAD * RW_HEAD)
    hg_new = None

    for l in range(depth):
        x, ffn_w = ffn_step(x, l, 1, ffn_w)
        if l % 2 == 0:
            e = l // 2
            w_in = ev_w_in[e]
            w_rw = lay.pad(w_in[:, :lay.width]).astype(BF16)
            w_hg = w_in[:, lay.width:].astype(BF16)
            p_rw = _norm_proj(x, mix_norm, l, w_rw, _pick_block(lay.padded, 1792, LANES))
            p_hg = _norm_proj(x, mix_norm, l, w_hg, _pick_block(4 * hgw, 1024, LANES))
            shift_p = jnp.zeros((bp, lay.padded), F32)
            shift_s = lay.pad(state_rwkv_shift[:, e].astype(F32))
            rkvwa_p, rkvwa_s, g_ = _rwkv_prep(
                p_rw, shift_p, shift_s, lay.pad(rw_mu[e])[None], rw_w0[e][None],
                _pad_last(rw_w2[e].T, lay.dwp).T.astype(BF16), rw_a0[e][None],
                _pad_last(rw_a2[e].T, lay.dap).T.astype(BF16),
                _pad_last(rw_g2[e].T, lay.dgp).T.astype(BF16), rw=rw, mp=mp, tp=tp, bs=bs)
            shifts_p.append(lay.unpad(p_rw[tp - 1:mp:tp]))
            shifts_s.append(lay.unpad(p_rw[mp + (ts - 1) * bs:]))
            kk_p, ka_p = kparam(rw_k_k[e]), kparam(rw_k_a[e])
            rk_p = kparam(rw_r_k[e].reshape(rw))
            lb = lb_all[e][None]
            loglb, log1m = jnp.log(lb), jnp.log1p(-lb)
            nw = hg_norm[e][None]
            chunk = _pick_block(tp, HG_CHUNK, 1)
            oh_p, hs_p = _hgrn_prompt(p_hg, jnp.zeros((bp, nh_hg, HG_DIM, HG_DIM), F32), loglb, log1m, nw,
                                      bp=bp, tp=tp, chunk=chunk)
            r_p, k_p, v_p, w_p, a_p = rkvwa_p
            o_p, s_p = _rwkv_scan(
                k_lanes_p(r_p), k_lanes_p(w_p), k_lanes_p(k_p), k_lanes_p(a_p), v_lanes_p(v_p),
                kk_p, ka_p, rk_p, vparam_p(rw_ln_w[e]), vparam_p(rw_ln_b[e]),
                jnp.zeros((RW_HEAD // 2, RW_HEAD, LANES), F32), tc=_pick_block(tp, 64, 1), vsplit=True,
                after=oh_p)
            rws_p.append(s_p.reshape(RW_HEAD // 2, RW_HEAD, 2, bp, nh_rw).transpose(3, 4, 2, 0, 1)
                         .reshape(bp, nh_rw, RW_HEAD, RW_HEAD))
            r_s, k_s, v_s, w_s, a_s = rkvwa_s
            o_s, rw_rows = _rwkv_scan(
                k_lanes_s(r_s), k_lanes_s(w_s), k_lanes_s(k_s), k_lanes_s(a_s), k_lanes_s(v_s),
                kk_p, ka_p, rk_p, kparam(rw_ln_w[e]), kparam(rw_ln_b[e]),
                rw_rows, tc=ts, vsplit=False, layer=e, after=o_p)
            o_rw = jnp.concatenate([v_lanes_p_inv(o_p), k_lanes_s_inv(o_s)], axis=0)
            p_hg_bm = p_hg[mp:].reshape(ts, bs, 4 * hgw).transpose(1, 0, 2).reshape(ms, 4 * hgw)
            oh_s, hg_new = _hgrn_sample(p_hg_bm, state_hgrn.astype(F32), e, loglb, log1m, nw, bs=bs, ts=ts,
                                        s_prev=hg_new, after=o_p)
            oh_s = oh_s.reshape(bs, ts, hgw).transpose(1, 0, 2).reshape(ms, hgw)
            hgs_p.append(hs_p)
            o_hg = jnp.concatenate([oh_p, oh_s], axis=0)
            x = _proj_res_even(x, o_rw, g_, o_hg, ev_out[e])
        else:
            o_i = l // 2
            p_od = _norm_proj(x, mix_norm, l, od_in[o_i], _pick_block(od_in.shape[-1], 1024, LANES))
            lsz = p_od.shape[1] // 2
            vec = lambda a: a[:, None, :]
            args = (conv_w, vec(conv_b), wa_bf, wx_bf, vec(lru_ba), vec(lru_bx), vec(lru_lambda))
            conv0_p = jnp.zeros((bp, SUBLANES, lsz), F32)
            y_p, c_p, h_p = _odd_prompt(p_od, conv0_p, jnp.zeros((bp, 1, lsz), F32), o_i, *args, bp=bp, tp=tp)
            conv0_s = state_conv[:, o_i].astype(F32).transpose(1, 0, 2)
            y_s, c_s, h_s = _odd_sample(p_od, conv0_s, state_lru[:, o_i].astype(F32), o_i, *args,
                                        mp=mp, bs=bs, ts=ts)
            convs_p.append(c_p[:, SUBLANES - (CONV_W - 1):])
            convs_s.append(c_s.transpose(1, 0, 2))
            lrus_p.append(h_p[:, 0])
            lrus_s.append(h_s)
            x = _proj_res(x, jnp.concatenate([y_p, y_s], axis=0), od_out[o_i])
        x, ffn_w = ffn_step(x, l, 2, ffn_w)

    y_p, y_s = _final_norm(x, final_norm, mp)
    y_prompt = y_p.reshape(bp, tp, d)
    y_sample = y_s.reshape(ts, bs, d).transpose(1, 0, 2)
    st = lambda xs: jnp.stack(xs, axis=1).astype(dt)
    sample_rwkv = rw_rows.reshape(state_rwkv.shape).astype(dt)
    return (y_prompt, y_sample, st(shifts_p), st(rws_p), st(hgs_p), st(convs_p), st(lrus_p),
            st(shifts_s), sample_rwkv, hg_new.astype(dt), st(convs_s), st(lrus_s))
```

```python
import functools
import math

import jax
import jax.numpy as jnp
from jax import lax
from jax.experimental import pallas as pl
from jax.experimental.pallas import tpu as pltpu

F32 = jnp.float32
BF16 = jnp.bfloat16

NORM_EPS = 1e-6
RW_GN_EPS = 64e-5
HG_NORM_EPS = 1e-5
LRU_C = 8.0
RW_HEAD = 64
HG_DIM = 128
LRU_BLOCKS = 8
CONV_W = 4
LANES = 128
SUBLANES = 8
VMEM_LIMIT = 62 * 1024 * 1024
MAX_BM = 1088
HG_CHUNK = 64


def _cparams(*sem):
    return pltpu.CompilerParams(dimension_semantics=sem, vmem_limit_bytes=VMEM_LIMIT)


def _pick_block(n, cap, mult):
    best = None
    for d in range(mult, min(n, cap) + 1, mult):
        if n % d == 0:
            best = d
    assert best is not None, (n, cap, mult)
    return best


def _gcd(a, b):
    return math.gcd(a, b)


def _dot(a, b):
    return jnp.dot(a, b, preferred_element_type=F32)


def _dot_nt(a, b):
    return lax.dot_general(a, b, (((1,), (1,)), ((), ())), preferred_element_type=F32)


def _dot_tn(a, b):
    return lax.dot_general(a, b, (((0,), (0,)), ((), ())), preferred_element_type=F32)


def _softplus(x):
    return jnp.maximum(x, 0.0) + jnp.log1p(jnp.exp(-jnp.abs(x)))


def _neg_expm1(y):
    return 1.0 - jnp.exp(y)


def _rms_rows(x, g):
    ms = jnp.mean(x * x, axis=-1, keepdims=True)
    return x * lax.rsqrt(ms + NORM_EPS) * g


def _ffn_body(x_ref, g_ref, wg_ref, wu_ref, wd_ref, *rest, cast_next):
    if cast_next:
        sgu_ref, sdn_ref, o_ref, cgu_ref, cdn_ref, xn_ref = rest
        cgu_ref[...] = sgu_ref[...].astype(BF16)
        cdn_ref[...] = sdn_ref[...].astype(BF16)
    else:
        o_ref, xn_ref = rest

    @pl.when(pl.program_id(1) == 0)
    def _():
        x = x_ref[...]
        xn_ref[...] = _rms_rows(x, g_ref[...]).astype(BF16)
        o_ref[...] = x

    xn = xn_ref[...]
    gate = _dot(xn, wg_ref[...])
    up = _dot(xn, wu_ref[...])
    act = (0.5 * (gate * jax.nn.sigmoid(gate)) * up).astype(BF16)
    o_ref[...] += _dot(act, wd_ref[...])


def _ffn(x, norm_g, layer, w_gu, w_down, next_src=None):
    m, d = x.shape
    f = w_down.shape[0]
    bm = _pick_block(m, MAX_BM, 16)
    bf = _pick_block(f, 512, LANES)
    ni, nf = m // bm, f // bf
    in_specs = [
        pl.BlockSpec((bm, d), lambda i, j: (i, 0)),
        pl.BlockSpec((None, 1, d), lambda i, j: (layer, 0, 0)),
        pl.BlockSpec((d, bf), lambda i, j: (0, j)),
        pl.BlockSpec((d, bf), lambda i, j: (0, j + nf)),
        pl.BlockSpec((bf, d), lambda i, j: (j, 0)),
    ]
    operands = [x, norm_g[:, None, :], w_gu, w_gu, w_down]
    out_specs = [pl.BlockSpec((bm, d), lambda i, j: (i, 0))]
    out_shape = [jax.ShapeDtypeStruct((m, d), F32)]
    if next_src is not None:
        s_gu, s_dn, nl = next_src
        assert d % ni == 0 and (2 * f) % nf == 0 and (d // ni) % 16 == 0 and (d // ni) % LANES == 0
        gu_blk, dn_blk = (d // ni, 2 * f // nf), (f // nf, d // ni)
        in_specs += [pl.BlockSpec((None,) + gu_blk, lambda i, j: (nl, i, j)),
                     pl.BlockSpec((None,) + dn_blk, lambda i, j: (nl, j, i))]
        operands += [s_gu, s_dn]
        out_specs += [pl.BlockSpec(gu_blk, lambda i, j: (i, j)), pl.BlockSpec(dn_blk, lambda i, j: (j, i))]
        out_shape += [jax.ShapeDtypeStruct((d, 2 * f), BF16), jax.ShapeDtypeStruct((f, d), BF16)]
    return pl.pallas_call(
        functools.partial(_ffn_body, cast_next=next_src is not None),
        grid=(ni, nf),
        in_specs=in_specs,
        out_specs=out_specs,
        out_shape=out_shape,
        scratch_shapes=[pltpu.VMEM((bm, d), BF16)],
        compiler_params=_cparams("parallel", "arbitrary"),
        name="ffn",
    )(*operands)


def _norm_proj_body(x_ref, g_ref, w_ref, o_ref, xn_ref):
    @pl.when(pl.program_id(1) == 0)
    def _():
        xn_ref[...] = _rms_rows(x_ref[...], g_ref[...]).astype(BF16)

    o_ref[...] = _dot(xn_ref[...], w_ref[...].astype(BF16))


def _w_spec(w, blk, idx, w_layer):
    if w.ndim == 2:
        return pl.BlockSpec(blk, idx)
    return pl.BlockSpec((None,) + blk, lambda i, j: (w_layer,) + idx(i, j))


def _norm_proj(x, norm_g, layer, w, bn, w_layer=None):
    m, d = x.shape
    n = w.shape[-1]
    bm = _pick_block(m, MAX_BM, 16)
    return pl.pallas_call(
        _norm_proj_body,
        grid=(m // bm, n // bn),
        in_specs=[
            pl.BlockSpec((bm, d), lambda i, j: (i, 0)),
            pl.BlockSpec((None, 1, d), lambda i, j: (layer, 0, 0)),
            _w_spec(w, (d, bn), lambda i, j: (0, j), w_layer),
        ],
        out_specs=pl.BlockSpec((bm, bn), lambda i, j: (i, j)),
        out_shape=jax.ShapeDtypeStruct((m, n), F32),
        scratch_shapes=[pltpu.VMEM((bm, d), BF16)],
        compiler_params=_cparams("parallel", "arbitrary"),
        name="norm_proj",
    )(x, norm_g[:, None, :], w)


def _proj_res_body(x_ref, y_ref, w_ref, o_ref):
    o_ref[...] = x_ref[...] + _dot(y_ref[...], w_ref[...].astype(BF16))


def _proj_res(x, y, w, w_layer):
    m, d = x.shape
    k = y.shape[1]
    bm = _pick_block(m, MAX_BM, 16)
    bn = _pick_block(d, 1024, LANES)
    return pl.pallas_call(
        _proj_res_body,
        grid=(m // bm, d // bn),
        in_specs=[
            pl.BlockSpec((bm, bn), lambda i, j: (i, j)),
            pl.BlockSpec((bm, k), lambda i, j: (i, 0)),
            _w_spec(w, (k, bn), lambda i, j: (0, j), w_layer),
        ],
        out_specs=pl.BlockSpec((bm, bn), lambda i, j: (i, j)),
        out_shape=jax.ShapeDtypeStruct((m, d), F32),
        compiler_params=_cparams("parallel", "arbitrary"),
        name="proj_res",
    )(x, y, w)


def _proj_res_even_body(x_ref, orw_ref, g_ref, ohg_ref, w1_ref, w2_ref, o_ref):
    y1 = (orw_ref[...] * g_ref[...]).astype(BF16)
    o_ref[...] = (x_ref[...] + _dot(y1, w1_ref[...].astype(BF16))
                  + _dot(ohg_ref[...], w2_ref[...].astype(BF16)))


def _proj_res_even(x, o_rw, g, o_hg, w, w_layer):
    m, d = x.shape
    rw = o_rw.shape[1]
    hg = o_hg.shape[1]
    assert rw == hg
    bm = _pick_block(m, MAX_BM, 16)
    bn = _pick_block(d, 1024, LANES)
    return pl.pallas_call(
        _proj_res_even_body,
        grid=(m // bm, d // bn),
        in_specs=[
            pl.BlockSpec((bm, bn), lambda i, j: (i, j)),
            pl.BlockSpec((bm, rw), lambda i, j: (i, 0)),
            pl.BlockSpec((bm, rw), lambda i, j: (i, 0)),
            pl.BlockSpec((bm, hg), lambda i, j: (i, 0)),
            _w_spec(w, (rw, bn), lambda i, j: (0, j), w_layer),
            _w_spec(w, (hg, bn), lambda i, j: (1, j), w_layer),
        ],
        out_specs=pl.BlockSpec((bm, bn), lambda i, j: (i, j)),
        out_shape=jax.ShapeDtypeStruct((m, d), F32),
        compiler_params=_cparams("parallel", "arbitrary"),
        name="proj_res_even",
    )(x, o_rw, g, o_hg, w, w)


def _final_norm_body(x_ref, g_ref, op_ref, os_ref, *, n_prompt_blk):
    y = _rms_rows(x_ref[...], g_ref[...])
    i = pl.program_id(0)

    @pl.when(i < n_prompt_blk)
    def _():
        op_ref[...] = y

    @pl.when(i >= n_prompt_blk)
    def _():
        os_ref[...] = y


def _final_norm(x, g, mp):
    m, d = x.shape
    ms = m - mp
    bm = _pick_block(_gcd(mp, ms), MAX_BM, SUBLANES)
    npb = mp // bm
    return pl.pallas_call(
        functools.partial(_final_norm_body, n_prompt_blk=npb),
        grid=(m // bm,),
        in_specs=[pl.BlockSpec((bm, d), lambda i: (i, 0)), pl.BlockSpec((1, d), lambda i: (0, 0))],
        out_specs=[pl.BlockSpec((bm, d), lambda i: (jnp.minimum(i, npb - 1), 0)),
                   pl.BlockSpec((bm, d), lambda i: (jnp.maximum(i - npb, 0), 0))],
        out_shape=[jax.ShapeDtypeStruct((mp, d), F32), jax.ShapeDtypeStruct((ms, d), F32)],
        compiler_params=_cparams("arbitrary"),
        name="final_norm",
    )(x, g[None, :])


def _rwkv_prep_body(cur_ref, prevblk_ref, shiftp_ref, shifts_ref, mu_ref, w0_ref, w2_ref, a0_ref, a2_ref,
                    g2_ref, rp_ref, kp_ref, vp_ref, wp_ref, ap_ref, rs_ref, ks_ref, vs_ref, ws_ref, as_ref,
                    g_ref, *, rw, dwp, dap, n_prompt_blk, blk_per_seq):
    i = pl.program_id(0)
    cur = cur_ref[...]
    prevblk = prevblk_ref[...]
    rows = cur.shape[0]
    seq = jnp.minimum(i // blk_per_seq, shiftp_ref.shape[0] - 1)
    first = (i % blk_per_seq) == 0
    row0 = jnp.where(first, shiftp_ref[pl.ds(seq, 1), :], prevblk[rows - 1:rows, :])
    rid = lax.broadcasted_iota(jnp.int32, (rows, 1), 0)
    prev_p = jnp.where(rid == 0, row0, pltpu.roll(cur, 1, 0))
    prev_s = jnp.where(i == n_prompt_blk, shifts_ref[...], prevblk)
    prev = jnp.where(i >= n_prompt_blk, prev_s, prev_p)

    z = cur + (prev - cur) * mu_ref[...]
    o = 3 * rw
    zw = z[:, o:o + dwp]
    za = z[:, o + dwp:o + dwp + dap]
    zg = z[:, o + dwp + dap:]
    lw = w0_ref[...] + _dot(jnp.tanh(zw).astype(BF16), w2_ref[...])
    w_log = -_softplus(-lw) - 0.5
    decay = jnp.exp(-jnp.exp(w_log))
    a = jax.nn.sigmoid(a0_ref[...] + _dot(za.astype(BF16), a2_ref[...]))
    g_ref[...] = _dot(jax.nn.sigmoid(zg).astype(BF16), g2_ref[...])

    def write(r_ref, k_ref, v_ref, w_ref, a_ref):
        r_ref[...] = z[:, :rw]
        k_ref[...] = z[:, rw:2 * rw]
        v_ref[...] = z[:, 2 * rw:3 * rw]
        w_ref[...] = decay
        a_ref[...] = a

    @pl.when(i < n_prompt_blk)
    def _():
        write(rp_ref, kp_ref, vp_ref, wp_ref, ap_ref)

    @pl.when(i >= n_prompt_blk)
    def _():
        write(rs_ref, ks_ref, vs_ref, ws_ref, as_ref)


def _rwkv_prep(p_rw, shift_p, shift_s, mu, w0, w2, a0, a2, g2, *, rw, mp, tp, bs):
    m, prw = p_rw.shape
    ms = m - mp
    assert mp % bs == 0 and tp % bs == 0 and ms % bs == 0
    npb = mp // bs
    dwp, dap = w2.shape[0], a2.shape[0]
    full = lambda a: pl.BlockSpec(a.shape, lambda i: (0,) * a.ndim)
    bps = tp // bs
    bp = mp // tp

    def prompt_idx(i):
        ic = jnp.minimum(i, npb - 1)
        return ic % bps, ic // bps

    spec_p = pl.BlockSpec((bs, rw), prompt_idx)
    spec_s = pl.BlockSpec((bs, rw), lambda i: (jnp.maximum(i - npb, 0), 0))
    body = functools.partial(_rwkv_prep_body, rw=rw, dwp=dwp, dap=dap, n_prompt_blk=npb, blk_per_seq=tp // bs)
    outs = pl.pallas_call(
        body,
        grid=(m // bs,),
        in_specs=[
            pl.BlockSpec((bs, prw), lambda i: (i, 0)),
            pl.BlockSpec((bs, prw), lambda i: (jnp.maximum(i - 1, 0), 0)),
            full(shift_p), full(shift_s), full(mu), full(w0), full(w2), full(a0), full(a2), full(g2),
        ],
        out_specs=[spec_p] * 5 + [spec_s] * 5 + [pl.BlockSpec((bs, rw), lambda i: (i, 0))],
        out_shape=[jax.ShapeDtypeStruct((tp, bp * rw), F32)] * 5 + [jax.ShapeDtypeStruct((ms, rw), F32)] * 5
        + [jax.ShapeDtypeStruct((m, rw), F32)],
        compiler_params=_cparams("arbitrary"),
        name="rwkv_prep",
    )(p_rw, p_rw, shift_p, shift_s, mu, w0, w2, a0, a2, g2)
    return outs[:5], outs[5:10], outs[10]


def _rwkv_scan_body(r_ref, w_ref, k_ref, a_ref, v_ref, kk_p_ref, ka_p_ref, rk_p_ref, lnw_ref, lnb_ref, s0_ref,
                    o_ref, sout_ref, s_ref, r_s, w_s, nkk_s, b_s, km_s, *, vsplit, state_rows):
    tc_len, nv, lanes = v_ref.shape
    kc = s_ref.shape[1]
    ti = pl.program_id(1)

    @pl.when(ti == 0)
    def _():
        if state_rows:
            s_ref[...] = s0_ref[...].reshape(lanes, nv * kc).T.reshape(nv, kc, lanes)
        else:
            s_ref[...] = s0_ref[...]

    def widen(x):
        return jnp.concatenate([x, x], axis=-1) if vsplit else x

    kvec = widen(k_ref[...])
    a_ = widen(a_ref[...])
    r_s[...] = widen(r_ref[...])
    w_s[...] = widen(w_ref[...])
    kk = kvec * kk_p_ref[...][None]
    nrm = jnp.sqrt(jnp.sum(kk * kk, axis=1, keepdims=True))
    kk = kk / jnp.maximum(nrm, 1e-12)
    nkk_s[...] = -kk
    b_s[...] = kk * a_
    km_s[...] = kvec * (1.0 + (a_ - 1.0) * ka_p_ref[...][None])

    def step(t, carry):
        nkk = nkk_s[t]
        b = b_s[t]
        km = km_s[t]
        w = w_s[t]
        r = r_s[t]
        vt = v_ref[t]
        for vi in range(nv):
            sv = s_ref[vi]
            sa = jnp.sum(sv * nkk, axis=0, keepdims=True)
            sv = sv * w + sa * b + vt[vi:vi + 1, :] * km
            s_ref[vi] = sv
            o_ref[t, pl.ds(vi, 1), :] = jnp.sum(sv * r, axis=0, keepdims=True)
        return carry

    lax.fori_loop(0, tc_len, step, 0)

    o = o_ref[...]
    n_val = nv * (2 if vsplit else 1)

    def head_sum(x):
        if vsplit:
            x2 = x.reshape(tc_len * nv, lanes)
            x = (x2 + pltpu.roll(x2, lanes // 2, 1)).reshape(tc_len, nv, lanes)
        return jnp.sum(x, axis=1, keepdims=True)

    mu = head_sum(o) / n_val
    d = o - mu
    var = head_sum(d * d) / n_val
    o = d * lax.rsqrt(var + RW_GN_EPS) * lnw_ref[...][None] + lnb_ref[...][None]
    bonus = jnp.sum(r_s[...] * km_s[...] * rk_p_ref[...][None], axis=1, keepdims=True)
    o_ref[...] = o + bonus * v_ref[...]

    @pl.when(ti == pl.num_programs(1) - 1)
    def _():
        if state_rows:
            sout_ref[...] = s_ref[...].reshape(nv * kc, lanes).T.reshape(sout_ref.shape)
        else:
            sout_ref[...] = s_ref[...]


def _rwkv_scan(r, w, k, a, v, kk_p, ka_p, rk_p, lnw, lnb, s0, *, tc, vsplit, layer=None, after=None):
    t, kc, kl = r.shape
    nv = v.shape[1]
    ln = v.shape[2]
    g = ln // LANES
    kspec = pl.BlockSpec((tc, kc, kl // g), lambda gi, ti: (ti, 0, gi))
    vspec = pl.BlockSpec((tc, nv, LANES), lambda gi, ti: (ti, 0, gi))
    pk = pl.BlockSpec((kc, LANES), lambda gi, ti: (0, 0))
    pv = pl.BlockSpec((nv, LANES), lambda gi, ti: (0, 0))
    state_rows = not vsplit
    operands = [r, w, k, a, v, kk_p, ka_p, rk_p, lnw, lnb, s0]
    in_specs = [kspec, kspec, kspec, kspec, vspec, pk, pk, pk, pv, pv]
    aliases = {}
    if state_rows:
        nb, _, nh, sz = s0.shape
        assert sz == nv * kc and nb * nh == ln
        sspec = pl.BlockSpec((LANES // nh, None, nh, sz), lambda gi, ti: (gi, layer, 0, 0))
        in_specs.append(sspec)
        aliases = {10: 1}
        s_shape = s0.shape
    else:
        sspec = pl.BlockSpec((nv, kc, LANES), lambda gi, ti: (0, 0, gi))
        in_specs.append(sspec)
        s_shape = (nv, kc, ln)
    if after is not None:
        operands.append(after)
        in_specs.append(pl.BlockSpec(memory_space=pl.ANY))

    def body(*refs):
        n_in = len(operands)
        ins = refs[:11]
        _rwkv_scan_body(*ins, *refs[n_in:], vsplit=vsplit, state_rows=state_rows)

    return pl.pallas_call(
        body,
        grid=(g, t // tc),
        in_specs=in_specs,
        out_specs=[vspec, sspec],
        out_shape=[jax.ShapeDtypeStruct((t, nv, ln), F32), jax.ShapeDtypeStruct(s_shape, F32)],
        scratch_shapes=[pltpu.VMEM((nv, kc, LANES), F32)] + [pltpu.VMEM((tc, kc, LANES), F32)] * 5,
        input_output_aliases=aliases,
        compiler_params=_cparams("parallel", "arbitrary"),
        name="rwkv_scan",
    )(*operands)


def _ref_rows(gc, j):
    c, w = gc.shape
    b = 1 << j
    nt = c // SUBLANES
    gv = gc.reshape(nt, SUBLANES, w)
    if 2 * b <= SUBLANES:
        sub = lax.broadcasted_iota(jnp.int32, (1, SUBLANES, 1), 1)
        out = None
        for g0 in range(0, SUBLANES, 2 * b):
            piece = jnp.broadcast_to(gv[:, g0 + b - 1:g0 + b, :], gv.shape)
            out = piece if out is None else jnp.where(sub >= g0, piece, out)
        return out.reshape(c, w)
    tiles_per_group = 2 * b // SUBLANES
    pieces = []
    for ti in range(nt):
        src = (ti // tiles_per_group) * tiles_per_group + b // SUBLANES - 1
        pieces.append(jnp.broadcast_to(gv[src, SUBLANES - 1:SUBLANES, :], (SUBLANES, w)))
    return jnp.concatenate(pieces, axis=0)


def _hgrn_chunk(q, f, val, og, get_st, set_st, loglb, log1mlb, nrm_w, seq_len):
    c, width = q.shape
    nh = width // HG_DIM
    nseq = c // seq_len
    lg = seq_len.bit_length() - 1
    assert (1 << lg) == seq_len and c % SUBLANES == 0
    ls = jnp.minimum(f, 0.0) - jnp.log1p(jnp.exp(-jnp.abs(f)))
    b_ = log1mlb + ls
    g = jnp.maximum(loglb, b_) + jnp.log1p(jnp.exp(-jnp.abs(loglb - b_)))
    kk = _neg_expm1(g)
    qs = q * jax.nn.sigmoid(q)
    row = lax.broadcasted_iota(jnp.int32, (c, 1), 0)
    col = lax.broadcasted_iota(jnp.int32, (1, c), 1)
    tpos = row & (seq_len - 1)
    gc = g
    for j in range(lg):
        s = 1 << j
        gc = gc + jnp.where(tpos >= s, pltpu.roll(gc, s, 0), 0.0)
    qbs, kbs, sames = [qs.astype(BF16)], [kk.astype(BF16)], [row == col]
    for j in range(lg):
        e = jnp.exp(-jnp.abs(gc - _ref_rows(gc, j)))
        is_q = ((row >> j) & 1) == 1
        qbs.append(jnp.where(is_q, qs * e, 0.0).astype(BF16))
        kbs.append(jnp.where(is_q, 0.0, kk * e).astype(BF16))
        sames.append((row >> (j + 1)) == (col >> (j + 1)))
    vb = val.astype(BF16)
    qg = (qs * jnp.exp(gc)).astype(BF16)
    k2s, eglast, in_seq = [], [], []
    for s_i in range(nseq):
        glast = gc[s_i * seq_len + seq_len - 1:s_i * seq_len + seq_len, :]
        eglast.append(jnp.exp(glast))
        if nseq == 1:
            in_seq.append(None)
            k2s.append((kk * jnp.exp(glast - gc)).astype(BF16))
        else:
            in_s = (row >> lg) == s_i
            in_seq.append(in_s)
            k2s.append(jnp.where(in_s, kk * jnp.exp(jnp.minimum(glast - gc, 0.0)), 0.0).astype(BF16))
    outs = []
    for h in range(nh):
        cs = slice(h * HG_DIM, (h + 1) * HG_DIM)
        amat = None
        for qb, kb, same in zip(qbs, kbs, sames):
            term = jnp.where(same, _dot_nt(qb[:, cs], kb[:, cs]), 0.0)
            amat = term if amat is None else amat + term
        o = _dot(amat.astype(BF16), vb[:, cs])
        for s_i in range(nseq):
            st = get_st(s_i, h)
            o_int = _dot_nt(qg[:, cs], st.astype(BF16))
            o = o + (o_int if nseq == 1 else jnp.where(in_seq[s_i], o_int, 0.0))
            set_st(s_i, h, st * eglast[s_i][:, cs] + _dot_tn(vb[:, cs], k2s[s_i][:, cs]))
        outs.append(o * lax.rsqrt(jnp.mean(o * o, axis=-1, keepdims=True) + HG_NORM_EPS))
    on = jnp.concatenate(outs, axis=1) * nrm_w
    return on * (og * jax.nn.sigmoid(og))


def _hgrn_prompt_body(q_ref, f_ref, i_ref, og_ref, s0_ref, loglb_ref, log1m_ref, nw_ref, o_ref, sout_ref, st_ref,
                      *, chunk):
    nh = st_ref.shape[0]
    tb = pl.program_id(1)

    @pl.when(tb == 0)
    def _():
        for h in range(nh):
            st_ref[h] = s0_ref[h].T

    def set_st(s_i, h, v):
        st_ref[h] = v

    def body(ci, carry):
        rows = pl.ds(pl.multiple_of(ci * chunk, chunk), chunk)
        y = _hgrn_chunk(q_ref[rows, :], f_ref[rows, :], i_ref[rows, :], og_ref[rows, :],
                        lambda s_i, h: st_ref[h], set_st, loglb_ref[...], log1m_ref[...], nw_ref[...], chunk)
        o_ref[rows, :] = y.astype(BF16)
        return carry

    lax.fori_loop(0, q_ref.shape[0] // chunk, body, 0)

    @pl.when(tb == pl.num_programs(1) - 1)
    def _():
        for h in range(nh):
            sout_ref[h] = st_ref[h].T


def _hgrn_prompt(p_hg, s0, loglb, log1m, nw, *, bp, tp, chunk):
    hgw = p_hg.shape[1] // 4
    nh = hgw // HG_DIM
    tb = _pick_block(tp, 512, chunk)
    nt = tp // tb
    cspec = lambda c: pl.BlockSpec((tb, hgw), lambda b, t: (b * nt + t, c))
    pspec = pl.BlockSpec((1, hgw), lambda b, t: (0, 0))
    sspec = pl.BlockSpec((None, nh, HG_DIM, HG_DIM), lambda b, t: (b, 0, 0, 0))
    return pl.pallas_call(
        functools.partial(_hgrn_prompt_body, chunk=chunk),
        grid=(bp, nt),
        in_specs=[cspec(0), cspec(1), cspec(2), cspec(3), sspec, pspec, pspec, pspec],
        out_specs=[pl.BlockSpec((tb, hgw), lambda b, t: (b * nt + t, 0)), sspec],
        out_shape=[jax.ShapeDtypeStruct((bp * tp, hgw), BF16), jax.ShapeDtypeStruct(s0.shape, F32)],
        scratch_shapes=[pltpu.VMEM((nh, HG_DIM, HG_DIM), F32)],
        compiler_params=_cparams("parallel", "arbitrary"),
        name="hgrn_prompt",
    )(p_hg, p_hg, p_hg, p_hg, s0, loglb, log1m, nw)


def _hgrn_sample_body(q_ref, f_ref, i_ref, og_ref, s0_ref, loglb_ref, log1m_ref, nw_ref, *rest, seq_len, nseq):
    o_ref, sout_ref = rest[-2:]
    c = seq_len * nseq

    def body(gi, carry):
        rows = pl.ds(pl.multiple_of(gi * c, c), c)

        def set_st(s_i, h, v):
            sout_ref[gi * nseq + s_i, h] = v.T

        y = _hgrn_chunk(q_ref[rows, :], f_ref[rows, :], i_ref[rows, :], og_ref[rows, :],
                        lambda s_i, h: s0_ref[gi * nseq + s_i, h].T, set_st,
                        loglb_ref[...], log1m_ref[...], nw_ref[...], seq_len)
        o_ref[rows, :] = y.astype(BF16)
        return carry

    lax.fori_loop(0, q_ref.shape[0] // c, body, 0)


def _hgrn_sample(p_hg_bm, s0, layer, loglb, log1m, nw, *, bs, ts, s_prev=None, after=None):
    hgw = p_hg_bm.shape[1] // 4
    nh = hgw // HG_DIM
    nseq = max(1, 16 // ts)
    assert bs % nseq == 0
    bb = _pick_block(bs, 8, nseq)
    rows = bb * ts
    cspec = lambda c: pl.BlockSpec((rows, hgw), lambda i: (i, c))
    pspec = pl.BlockSpec((1, hgw), lambda i: (0, 0))
    sspec = pl.BlockSpec((bb, None, nh, HG_DIM, HG_DIM), lambda i: (i, layer, 0, 0, 0))
    operands = [p_hg_bm, p_hg_bm, p_hg_bm, p_hg_bm, s0, loglb, log1m, nw]
    in_specs = [cspec(0), cspec(1), cspec(2), cspec(3), sspec, pspec, pspec, pspec]
    aliases = {}
    if s_prev is not None:
        operands.append(s_prev)
        in_specs.append(pl.BlockSpec(memory_space=pl.ANY))
        aliases = {len(operands) - 1: 1}
    if after is not None:
        operands.append(after)
        in_specs.append(pl.BlockSpec(memory_space=pl.ANY))
    return pl.pallas_call(
        functools.partial(_hgrn_sample_body, seq_len=ts, nseq=nseq),
        grid=(bs // bb,),
        in_specs=in_specs,
        out_specs=[pl.BlockSpec((rows, hgw), lambda i: (i, 0)), sspec],
        out_shape=[jax.ShapeDtypeStruct((bs * ts, hgw), BF16), jax.ShapeDtypeStruct(s0.shape, F32)],
        input_output_aliases=aliases,
        compiler_params=_cparams("parallel"),
        name="hgrn_sample",
    )(*operands)


def _lru_gates(xconv, wa_ref, wx_ref, ba, bx, lam):
    l = xconv.shape[1]
    blk = l // LRU_BLOCKS
    ga, gx = [], []
    for n in range(LRU_BLOCKS):
        xb = xconv[:, n * blk:(n + 1) * blk].astype(BF16)
        ga.append(_dot(xb, wa_ref[n]))
        gx.append(_dot(xb, wx_ref[n]))
    ga = jnp.concatenate(ga, axis=1) + ba
    gx = jnp.concatenate(gx, axis=1) + bx
    log_a = -LRU_C * jax.nn.sigmoid(ga) * _softplus(-lam)
    a = jnp.exp(log_a)
    mult = jnp.sqrt(1.0 - a * a)
    return a, mult * jax.nn.sigmoid(gx) * xconv


def _gelu_tanh(x):
    c = 0.7978845608028654
    return 0.5 * x * (1.0 + jnp.tanh(c * (x + 0.044715 * (x * x * x))))


def _odd_prompt_body(gate_ref, xb_ref, conv0_ref, h0_ref, cw_ref, cb_ref, wa_ref, wx_ref, ba_ref, bx_ref, lam_ref,
                     y_ref, convo_ref, ho_ref, ext_ref, hc_ref, a_s, b_s, h_s):
    rows = xb_ref.shape[0]
    hist = SUBLANES

    @pl.when(pl.program_id(1) == 0)
    def _():
        ext_ref[0:hist, :] = conv0_ref[...]
        hc_ref[...] = h0_ref[...]

    ext_ref[hist:hist + rows, :] = xb_ref[...]
    cw = cw_ref[...]
    xconv = cw[0:1, :] * ext_ref[pl.ds(hist - (CONV_W - 1), rows), :]
    for j in range(1, CONV_W):
        xconv = xconv + cw[j:j + 1, :] * ext_ref[pl.ds(hist - (CONV_W - 1) + j, rows), :]
    xconv = cb_ref[...] + xconv
    a, b = _lru_gates(xconv, wa_ref, wx_ref, ba_ref[...], bx_ref[...], lam_ref[...])
    width = a.shape[1]
    a = a.reshape(rows // SUBLANES, SUBLANES, width)
    b = b.reshape(rows // SUBLANES, SUBLANES, width)
    r8 = lax.broadcasted_iota(jnp.int32, (1, SUBLANES, 1), 1)
    for s in (1, 2, 4):
        m = r8 >= s
        a_sh = pltpu.roll(a, s, 1)
        b_sh = pltpu.roll(b, s, 1)
        b = jnp.where(m, a * b_sh + b, b)
        a = jnp.where(m, a * a_sh, a)
    a_s[...] = a.reshape(rows, width)
    b_s[...] = b.reshape(rows, width)

    def tile(j, hc):
        r0 = pl.multiple_of(j * SUBLANES, SUBLANES)
        h = a_s[pl.ds(r0, SUBLANES), :] * hc + b_s[pl.ds(r0, SUBLANES), :]
        h_s[pl.ds(r0, SUBLANES), :] = h
        return h[SUBLANES - 1:SUBLANES, :]

    hc = lax.fori_loop(0, rows // SUBLANES, tile, hc_ref[...])
    hc_ref[...] = hc
    ho_ref[...] = hc
    y_ref[...] = (h_s[...] * _gelu_tanh(gate_ref[...])).astype(BF16)
    tail = ext_ref[rows:rows + hist, :]
    ext_ref[0:hist, :] = tail
    convo_ref[...] = tail


def _odd_prompt(p_od, conv0, h0, layer, cw, cb, wa, wx, ba, bx, lam, *, bp, tp):
    l = p_od.shape[1] // 2
    rows = _pick_block(tp, 256, SUBLANES)
    nt = tp // rows
    lsel = lambda a: pl.BlockSpec((None,) + a.shape[1:], lambda b, t: (layer,) + (0,) * (a.ndim - 1))
    return pl.pallas_call(
        _odd_prompt_body,
        grid=(bp, nt),
        in_specs=[
            pl.BlockSpec((rows, l), lambda b, t: (b * nt + t, 0)),
            pl.BlockSpec((rows, l), lambda b, t: (b * nt + t, 1)),
            pl.BlockSpec((None, SUBLANES, l), lambda b, t: (b, 0, 0)),
            pl.BlockSpec((None, 1, l), lambda b, t: (b, 0, 0)),
            lsel(cw), lsel(cb), lsel(wa), lsel(wx), lsel(ba), lsel(bx), lsel(lam),
        ],
        out_specs=[
            pl.BlockSpec((rows, l), lambda b, t: (b * nt + t, 0)),
            pl.BlockSpec((None, SUBLANES, l), lambda b, t: (b, 0, 0)),
            pl.BlockSpec((None, 1, l), lambda b, t: (b, 0, 0)),
        ],
        out_shape=[jax.ShapeDtypeStruct((p_od.shape[0], l), BF16), jax.ShapeDtypeStruct((bp, SUBLANES, l), F32),
                   jax.ShapeDtypeStruct((bp, 1, l), F32)],
        scratch_shapes=[pltpu.VMEM((rows + SUBLANES, l), F32), pltpu.VMEM((1, l), F32),
                        pltpu.VMEM((rows, l), F32), pltpu.VMEM((rows, l), F32), pltpu.VMEM((rows, l), F32)],
        compiler_params=_cparams("parallel", "arbitrary"),
        name="odd_prompt",
    )(p_od, p_od, conv0, h0, cw, cb, wa, wx, ba, bx, lam)


def _odd_sample_body(gate_ref, xb_ref, conv0_ref, h0_ref, cw_ref, cb_ref, wa_ref, wx_ref, ba_ref, bx_ref, lam_ref,
                     yall_ref, y_ref, convo_ref, ho_ref):
    @pl.when(pl.program_id(0) == 0)
    def _():
        convo_ref[...] = conv0_ref[...]
        ho_ref[...] = h0_ref[...]

    xb = xb_ref[...]
    cw = cw_ref[...]
    xconv = cw[CONV_W - 1:CONV_W, :] * xb
    for j in range(CONV_W - 1):
        xconv = xconv + cw[j:j + 1, :] * convo_ref[j]
    xconv = cb_ref[...] + xconv
    a, b = _lru_gates(xconv, wa_ref, wx_ref, ba_ref[...], bx_ref[...], lam_ref[...])
    h = a * ho_ref[...] + b
    ho_ref[...] = h
    y_ref[...] = (h * _gelu_tanh(gate_ref[...])).astype(BF16)
    for j in range(CONV_W - 2):
        convo_ref[j] = convo_ref[j + 1]
    convo_ref[CONV_W - 2] = xb


def _odd_sample(p_od, conv0_t, h0, layer, cw, cb, wa, wx, ba, bx, lam, y_all, *, mp, bs, ts):
    l = p_od.shape[1] // 2
    assert mp % bs == 0
    off = mp // bs
    lsel = lambda a: pl.BlockSpec((None,) + a.shape[1:], lambda t: (layer,) + (0,) * (a.ndim - 1))
    full = lambda a: pl.BlockSpec(a.shape, lambda t: (0,) * a.ndim)
    return pl.pallas_call(
        _odd_sample_body,
        grid=(ts,),
        in_specs=[
            pl.BlockSpec((bs, l), lambda t: (off + t, 0)),
            pl.BlockSpec((bs, l), lambda t: (off + t, 1)),
            full(conv0_t), full(h0),
            lsel(cw), lsel(cb), lsel(wa), lsel(wx), lsel(ba), lsel(bx), lsel(lam),
            pl.BlockSpec(memory_space=pl.ANY),
        ],
        out_specs=[pl.BlockSpec((bs, l), lambda t: (off + t, 0)), full(conv0_t), full(h0)],
        out_shape=[jax.ShapeDtypeStruct(y_all.shape, BF16), jax.ShapeDtypeStruct(conv0_t.shape, F32),
                   jax.ShapeDtypeStruct(h0.shape, F32)],
        input_output_aliases={11: 0},
        compiler_params=_cparams("arbitrary"),
        name="odd_sample",
    )(p_od, p_od, conv0_t, h0, cw, cb, wa, wx, ba, bx, lam, y_all)


def _pad_last(a, n):
    return jnp.pad(a, [(0, 0)] * (a.ndim - 1) + [(0, n - a.shape[-1])])


def _round_up(n, m):
    return (n + m - 1) // m * m


class _RwLayout:
    def __init__(self, rw, dw, da, dg):
        self.rw, self.dw, self.da, self.dg = rw, dw, da, dg
        self.dwp, self.dap, self.dgp = (_round_up(d, LANES) for d in (dw, da, dg))
        self.width = 3 * rw + dw + da + dg
        self.padded = 3 * rw + self.dwp + self.dap + self.dgp

    def pad(self, a):
        o = 3 * self.rw
        parts = [a[..., :o], _pad_last(a[..., o:o + self.dw], self.dwp),
                 _pad_last(a[..., o + self.dw:o + self.dw + self.da], self.dap),
                 _pad_last(a[..., o + self.dw + self.da:self.width], self.dgp)]
        return jnp.concatenate(parts, axis=-1)

    def unpad(self, a):
        o = 3 * self.rw
        parts = [a[..., :o], a[..., o:o + self.dw], a[..., o + self.dwp:o + self.dwp + self.da],
                 a[..., o + self.dwp + self.dap:o + self.dwp + self.dap + self.dg]]
        return jnp.concatenate(parts, axis=-1)


def kernel(x_prompt, x_sample, state_rwkv_shift, state_rwkv, state_hgrn, state_conv, state_lru, ffn1_norm, ffn1_w_gu, ffn1_w_down, mix_norm, ffn2_norm, ffn2_w_gu, ffn2_w_down, ev_w_in, rw_mu, rw_w0, rw_w2, rw_a0, rw_a2, rw_g2, rw_k_k, rw_k_a, rw_r_k, rw_ln_w, rw_ln_b, hg_lb, hg_norm, ev_w_out, od_w_in, conv_w, conv_b, lru_wa, lru_ba, lru_wx, lru_bx, lru_lambda, od_w_out, final_norm):
    bp, tp, d = x_prompt.shape
    bs, ts, _ = x_sample.shape
    depth = ffn1_norm.shape[0]
    n_even = ev_w_in.shape[0]
    mp, ms = bp * tp, bs * ts
    rw = rw_w0.shape[1]
    nh_rw = rw // RW_HEAD
    hgw = hg_norm.shape[1]
    nh_hg = hgw // HG_DIM
    lay = _RwLayout(rw, rw_w2.shape[1], rw_a2.shape[1], rw_g2.shape[1])
    assert bp * nh_rw * 2 == LANES and (bs * nh_rw) % LANES == 0
    dt = x_prompt.dtype

    ffn_src = {1: (ffn1_norm, ffn1_w_gu, ffn1_w_down), 2: (ffn2_norm, ffn2_w_gu, ffn2_w_down)}
    ffn_w = (ffn1_w_gu[0].astype(BF16), ffn1_w_down[0].astype(BF16))

    def ffn_step(x, l, which, ffn_w):
        norm_g = ffn_src[which][0]
        nxt = (l, 2) if which == 1 else (l + 1, 1)
        if nxt[0] >= depth:
            return _ffn(x, norm_g, l, *ffn_w)[0], None
        x, c_gu, c_dn = _ffn(x, norm_g, l, *ffn_w, next_src=(ffn_src[nxt[1]][1], ffn_src[nxt[1]][2], nxt[0]))
        return x, (c_gu, c_dn)

    wa_bf, wx_bf = lru_wa.astype(BF16), lru_wx.astype(BF16)

    x = jnp.concatenate([x_prompt.reshape(mp, d), x_sample.transpose(1, 0, 2).reshape(ms, d)], axis=0)

    lbs = jnp.cumsum(jax.nn.softmax(hg_lb.astype(F32), axis=0), axis=0)
    lb_all = lbs - lbs[0]

    def k_lanes_p(a):
        return a.reshape(tp, bp * nh_rw, RW_HEAD).swapaxes(1, 2)

    def v_lanes_p(a):
        return a.reshape(tp, bp, nh_rw, 2, RW_HEAD // 2).transpose(0, 4, 3, 1, 2).reshape(tp, RW_HEAD // 2, LANES)

    def v_lanes_p_inv(a):
        return a.reshape(tp, RW_HEAD // 2, 2, bp, nh_rw).transpose(3, 0, 4, 2, 1).reshape(mp, rw)

    def k_lanes_s(a):
        return a.reshape(ts, bs, nh_rw, RW_HEAD).transpose(0, 3, 1, 2).reshape(ts, RW_HEAD, bs * nh_rw)

    def k_lanes_s_inv(a):
        return a.reshape(ts, RW_HEAD, bs, nh_rw).transpose(0, 2, 3, 1).reshape(ms, rw)

    def kparam(p):
        return jnp.tile(p.reshape(nh_rw, RW_HEAD).T, (1, LANES // nh_rw))

    def vparam_p(p):
        a = p.reshape(nh_rw, 2, RW_HEAD // 2).transpose(2, 1, 0)[:, :, None, :]
        return jnp.broadcast_to(a, (RW_HEAD // 2, 2, bp, nh_rw)).reshape(RW_HEAD // 2, LANES)

    shifts_p, shifts_s, rws_p, hgs_p, convs_p, convs_s, lrus_p, lrus_s = ([] for _ in range(8))
    rw_rows = state_rwkv.astype(F32).reshape(bs, n_even, nh_rw, RW_HEAD * RW_HEAD)
    hg_new = None

    for l in range(depth):
        x, ffn_w = ffn_step(x, l, 1, ffn_w)
        if l % 2 == 0:
            e = l // 2
            w_in = ev_w_in[e]
            w_rw = lay.pad(w_in[:, :lay.width]).astype(BF16)
            w_hg = w_in[:, lay.width:].astype(BF16)
            p_rw = _norm_proj(x, mix_norm, l, w_rw, _pick_block(lay.padded, 1792, LANES))
            p_hg = _norm_proj(x, mix_norm, l, w_hg, _pick_block(4 * hgw, 1024, LANES))
            shift_p = jnp.zeros((bp, lay.padded), F32)
            shift_s = lay.pad(state_rwkv_shift[:, e].astype(F32))
            rkvwa_p, rkvwa_s, g_ = _rwkv_prep(
                p_rw, shift_p, shift_s, lay.pad(rw_mu[e])[None], rw_w0[e][None],
                _pad_last(rw_w2[e].T, lay.dwp).T.astype(BF16), rw_a0[e][None],
                _pad_last(rw_a2[e].T, lay.dap).T.astype(BF16),
                _pad_last(rw_g2[e].T, lay.dgp).T.astype(BF16), rw=rw, mp=mp, tp=tp, bs=bs)
            shifts_p.append(lay.unpad(jnp.concatenate([p_rw[b * tp + tp - 1:(b + 1) * tp] for b in range(bp)])))
            shifts_s.append(lay.unpad(p_rw[mp + (ts - 1) * bs:]))
            kk_p, ka_p = kparam(rw_k_k[e]), kparam(rw_k_a[e])
            rk_p = kparam(rw_r_k[e].reshape(rw))
            lb = lb_all[e][None]
            loglb, log1m = jnp.log(lb), jnp.log1p(-lb)
            nw = hg_norm[e][None]
            chunk = _pick_block(tp, HG_CHUNK, 1)
            oh_p, hs_p = _hgrn_prompt(p_hg, jnp.zeros((bp, nh_hg, HG_DIM, HG_DIM), F32), loglb, log1m, nw,
                                      bp=bp, tp=tp, chunk=chunk)
            r_p, k_p, v_p, w_p, a_p = rkvwa_p
            o_p, s_p = _rwkv_scan(
                k_lanes_p(r_p), k_lanes_p(w_p), k_lanes_p(k_p), k_lanes_p(a_p), v_lanes_p(v_p),
                kk_p, ka_p, rk_p, vparam_p(rw_ln_w[e]), vparam_p(rw_ln_b[e]),
                jnp.zeros((RW_HEAD // 2, RW_HEAD, LANES), F32), tc=_pick_block(tp, 64, 1), vsplit=True,
                after=oh_p)
            rws_p.append(s_p.reshape(RW_HEAD // 2, RW_HEAD, 2, bp, nh_rw).transpose(3, 4, 2, 0, 1)
                         .reshape(bp, nh_rw, RW_HEAD, RW_HEAD))
            r_s, k_s, v_s, w_s, a_s = rkvwa_s
            o_s, rw_rows = _rwkv_scan(
                k_lanes_s(r_s), k_lanes_s(w_s), k_lanes_s(k_s), k_lanes_s(a_s), k_lanes_s(v_s),
                kk_p, ka_p, rk_p, kparam(rw_ln_w[e]), kparam(rw_ln_b[e]),
                rw_rows, tc=ts, vsplit=False, layer=e, after=o_p)
            o_rw = jnp.concatenate([v_lanes_p_inv(o_p), k_lanes_s_inv(o_s)], axis=0)
            p_hg_bm = p_hg[mp:].reshape(ts, bs, 4 * hgw).transpose(1, 0, 2).reshape(ms, 4 * hgw)
            oh_s, hg_new = _hgrn_sample(p_hg_bm, state_hgrn.astype(F32), e, loglb, log1m, nw, bs=bs, ts=ts,
                                        s_prev=hg_new, after=o_p)
            oh_s = oh_s.reshape(bs, ts, hgw).transpose(1, 0, 2).reshape(ms, hgw)
            hgs_p.append(hs_p)
            o_hg = jnp.concatenate([oh_p, oh_s], axis=0)
            x = _proj_res_even(x, o_rw, g_, o_hg, ev_w_out, e)
        else:
            o_i = l // 2
            p_od = _norm_proj(x, mix_norm, l, od_w_in, _pick_block(od_w_in.shape[-1], 1024, LANES), w_layer=o_i)
            lsz = p_od.shape[1] // 2
            vec = lambda a: a[:, None, :]
            args = (conv_w, vec(conv_b), wa_bf, wx_bf, vec(lru_ba), vec(lru_bx), vec(lru_lambda))
            conv0_p = jnp.zeros((bp, SUBLANES, lsz), F32)
            y_p, c_p, h_p = _odd_prompt(p_od, conv0_p, jnp.zeros((bp, 1, lsz), F32), o_i, *args, bp=bp, tp=tp)
            conv0_s = state_conv[:, o_i].astype(F32).transpose(1, 0, 2)
            y_all, c_s, h_s = _odd_sample(p_od, conv0_s, state_lru[:, o_i].astype(F32), o_i, *args, y_p,
                                          mp=mp, bs=bs, ts=ts)
            convs_p.append(c_p[:, SUBLANES - (CONV_W - 1):])
            convs_s.append(c_s.transpose(1, 0, 2))
            lrus_p.append(h_p[:, 0])
            lrus_s.append(h_s)
            x = _proj_res(x, y_all, od_w_out, o_i)
        x, ffn_w = ffn_step(x, l, 2, ffn_w)

    y_p, y_s = _final_norm(x, final_norm, mp)
    y_prompt = y_p.reshape(bp, tp, d)
    y_sample = y_s.reshape(ts, bs, d).transpose(1, 0, 2)
    st = lambda xs: jnp.stack(xs, axis=1).astype(dt)
    sample_rwkv = rw_rows.reshape(state_rwkv.shape).astype(dt)
    return (y_prompt, y_sample, st(shifts_p), st(rws_p), st(hgs_p), st(convs_p), st(lrus_p),
            st(shifts_s), sample_rwkv, hg_new.astype(dt), st(convs_s), st(lrus_s))
```

```python
import functools
import math

import jax
import jax.numpy as jnp
from jax import lax
from jax.experimental import pallas as pl
from jax.experimental.pallas import tpu as pltpu

F32 = jnp.float32
BF16 = jnp.bfloat16

NORM_EPS = 1e-6
RW_GN_EPS = 64e-5
HG_NORM_EPS = 1e-5
LRU_C = 8.0
RW_HEAD = 64
HG_DIM = 128
LRU_BLOCKS = 8
CONV_W = 4
LANES = 128
SUBLANES = 8
VMEM_LIMIT = 62 * 1024 * 1024
MAX_BM = 1088
HG_CHUNK = 64


def _cparams(*sem):
    return pltpu.CompilerParams(dimension_semantics=sem, vmem_limit_bytes=VMEM_LIMIT)


def _pick_block(n, cap, mult):
    best = None
    for d in range(mult, min(n, cap) + 1, mult):
        if n % d == 0:
            best = d
    assert best is not None, (n, cap, mult)
    return best


def _gcd(a, b):
    return math.gcd(a, b)


def _dot(a, b):
    return jnp.dot(a, b, preferred_element_type=F32)


def _dot_nt(a, b):
    return lax.dot_general(a, b, (((1,), (1,)), ((), ())), preferred_element_type=F32)


def _dot_tn(a, b):
    return lax.dot_general(a, b, (((0,), (0,)), ((), ())), preferred_element_type=F32)


def _softplus(x):
    return jnp.maximum(x, 0.0) + jnp.log1p(jnp.exp(-jnp.abs(x)))


def _neg_expm1(y):
    return 1.0 - jnp.exp(y)


def _rms_rows(x, g):
    ms = jnp.mean(x * x, axis=-1, keepdims=True)
    return x * lax.rsqrt(ms + NORM_EPS) * g


def _ffn_body(x_ref, g_ref, wg_ref, wu_ref, wd_ref, *rest, cast_next):
    if cast_next:
        sgu_ref, sdn_ref, o_ref, cgu_ref, cdn_ref, xn_ref = rest
        cgu_ref[...] = sgu_ref[...].astype(BF16)
        cdn_ref[...] = sdn_ref[...].astype(BF16)
    else:
        o_ref, xn_ref = rest

    @pl.when(pl.program_id(1) == 0)
    def _():
        x = x_ref[...]
        xn_ref[...] = _rms_rows(x, g_ref[...]).astype(BF16)
        o_ref[...] = x

    xn = xn_ref[...]
    gate = _dot(xn, wg_ref[...])
    up = _dot(xn, wu_ref[...])
    act = (0.5 * (gate * jax.nn.sigmoid(gate)) * up).astype(BF16)
    o_ref[...] += _dot(act, wd_ref[...])


def _ffn(x, norm_g, layer, w_gu, w_down, next_src=None):
    m, d = x.shape
    f = w_down.shape[0]
    bm = _pick_block(m, MAX_BM, 16)
    bf = _pick_block(f, 512, LANES)
    ni, nf = m // bm, f // bf
    in_specs = [
        pl.BlockSpec((bm, d), lambda i, j: (i, 0)),
        pl.BlockSpec((None, 1, d), lambda i, j: (layer, 0, 0)),
        pl.BlockSpec((d, bf), lambda i, j: (0, j)),
        pl.BlockSpec((d, bf), lambda i, j: (0, j + nf)),
        pl.BlockSpec((bf, d), lambda i, j: (j, 0)),
    ]
    operands = [x, norm_g[:, None, :], w_gu, w_gu, w_down]
    out_specs = [pl.BlockSpec((bm, d), lambda i, j: (i, 0))]
    out_shape = [jax.ShapeDtypeStruct((m, d), F32)]
    if next_src is not None:
        s_gu, s_dn, nl = next_src
        assert d % ni == 0 and (2 * f) % nf == 0 and (d // ni) % 16 == 0 and (d // ni) % LANES == 0
        gu_blk, dn_blk = (d // ni, 2 * f // nf), (f // nf, d // ni)
        in_specs += [pl.BlockSpec((None,) + gu_blk, lambda i, j: (nl, i, j)),
                     pl.BlockSpec((None,) + dn_blk, lambda i, j: (nl, j, i))]
        operands += [s_gu, s_dn]
        out_specs += [pl.BlockSpec(gu_blk, lambda i, j: (i, j)), pl.BlockSpec(dn_blk, lambda i, j: (j, i))]
        out_shape += [jax.ShapeDtypeStruct((d, 2 * f), BF16), jax.ShapeDtypeStruct((f, d), BF16)]
    return pl.pallas_call(
        functools.partial(_ffn_body, cast_next=next_src is not None),
        grid=(ni, nf),
        in_specs=in_specs,
        out_specs=out_specs,
        out_shape=out_shape,
        scratch_shapes=[pltpu.VMEM((bm, d), BF16)],
        compiler_params=_cparams("parallel", "arbitrary"),
        name="ffn",
    )(*operands)


def _norm_proj_body(x_ref, g_ref, w_ref, o_ref, xn_ref):
    @pl.when(pl.program_id(1) == 0)
    def _():
        xn_ref[...] = _rms_rows(x_ref[...], g_ref[...]).astype(BF16)

    o_ref[...] = _dot(xn_ref[...], w_ref[...])


def _w_spec(w, blk, idx, w_layer):
    if w.ndim == 2:
        return pl.BlockSpec(blk, idx)
    return pl.BlockSpec((None,) + blk, lambda i, j: (w_layer,) + idx(i, j))


def _norm_proj(x, norm_g, layer, w, bn, w_layer=None):
    m, d = x.shape
    n = w.shape[-1]
    bm = _pick_block(m, MAX_BM, 16)
    return pl.pallas_call(
        _norm_proj_body,
        grid=(m // bm, n // bn),
        in_specs=[
            pl.BlockSpec((bm, d), lambda i, j: (i, 0)),
            pl.BlockSpec((None, 1, d), lambda i, j: (layer, 0, 0)),
            _w_spec(w, (d, bn), lambda i, j: (0, j), w_layer),
        ],
        out_specs=pl.BlockSpec((bm, bn), lambda i, j: (i, j)),
        out_shape=jax.ShapeDtypeStruct((m, n), F32),
        scratch_shapes=[pltpu.VMEM((bm, d), BF16)],
        compiler_params=_cparams("parallel", "arbitrary"),
        name="norm_proj",
    )(x, norm_g[:, None, :], w)


def _proj_res_body(x_ref, y_ref, w_ref, o_ref):
    o_ref[...] = x_ref[...] + _dot(y_ref[...], w_ref[...])


def _proj_res(x, y, w, w_layer):
    m, d = x.shape
    k = y.shape[1]
    bm = _pick_block(m, MAX_BM, 16)
    bn = _pick_block(d, 1024, LANES)
    return pl.pallas_call(
        _proj_res_body,
        grid=(m // bm, d // bn),
        in_specs=[
            pl.BlockSpec((bm, bn), lambda i, j: (i, j)),
            pl.BlockSpec((bm, k), lambda i, j: (i, 0)),
            _w_spec(w, (k, bn), lambda i, j: (0, j), w_layer),
        ],
        out_specs=pl.BlockSpec((bm, bn), lambda i, j: (i, j)),
        out_shape=jax.ShapeDtypeStruct((m, d), F32),
        compiler_params=_cparams("parallel", "arbitrary"),
        name="proj_res",
    )(x, y, w)


def _proj_res_even_body(x_ref, orw_ref, g_ref, ohg_ref, w1_ref, w2_ref, o_ref):
    y1 = (orw_ref[...] * g_ref[...]).astype(BF16)
    o_ref[...] = x_ref[...] + _dot(y1, w1_ref[...]) + _dot(ohg_ref[...], w2_ref[...])


def _proj_res_even(x, o_rw, g, o_hg, w, w_layer):
    m, d = x.shape
    rw = o_rw.shape[1]
    hg = o_hg.shape[1]
    assert rw == hg
    bm = _pick_block(m, MAX_BM, 16)
    bn = _pick_block(d, 1024, LANES)
    return pl.pallas_call(
        _proj_res_even_body,
        grid=(m // bm, d // bn),
        in_specs=[
            pl.BlockSpec((bm, bn), lambda i, j: (i, j)),
            pl.BlockSpec((bm, rw), lambda i, j: (i, 0)),
            pl.BlockSpec((bm, rw), lambda i, j: (i, 0)),
            pl.BlockSpec((bm, hg), lambda i, j: (i, 0)),
            _w_spec(w, (rw, bn), lambda i, j: (0, j), w_layer),
            _w_spec(w, (hg, bn), lambda i, j: (1, j), w_layer),
        ],
        out_specs=pl.BlockSpec((bm, bn), lambda i, j: (i, j)),
        out_shape=jax.ShapeDtypeStruct((m, d), F32),
        compiler_params=_cparams("parallel", "arbitrary"),
        name="proj_res_even",
    )(x, o_rw, g, o_hg, w, w)


def _final_norm_body(x_ref, g_ref, op_ref, os_ref, *, n_prompt_blk):
    y = _rms_rows(x_ref[...], g_ref[...])
    i = pl.program_id(0)

    @pl.when(i < n_prompt_blk)
    def _():
        op_ref[...] = y

    @pl.when(i >= n_prompt_blk)
    def _():
        os_ref[...] = y


def _final_norm(x, g, mp):
    m, d = x.shape
    ms = m - mp
    bm = _pick_block(_gcd(mp, ms), MAX_BM, SUBLANES)
    npb = mp // bm
    return pl.pallas_call(
        functools.partial(_final_norm_body, n_prompt_blk=npb),
        grid=(m // bm,),
        in_specs=[pl.BlockSpec((bm, d), lambda i: (i, 0)), pl.BlockSpec((1, d), lambda i: (0, 0))],
        out_specs=[pl.BlockSpec((bm, d), lambda i: (jnp.minimum(i, npb - 1), 0)),
                   pl.BlockSpec((bm, d), lambda i: (jnp.maximum(i - npb, 0), 0))],
        out_shape=[jax.ShapeDtypeStruct((mp, d), F32), jax.ShapeDtypeStruct((ms, d), F32)],
        compiler_params=_cparams("arbitrary"),
        name="final_norm",
    )(x, g[None, :])


def _rwkv_prep_body(cur_ref, prevblk_ref, shiftp_ref, shifts_ref, mu_ref, w0_ref, w2_ref, a0_ref, a2_ref,
                    g2_ref, rp_ref, kp_ref, vp_ref, wp_ref, ap_ref, rs_ref, ks_ref, vs_ref, ws_ref, as_ref,
                    g_ref, *, rw, dwp, dap, n_prompt_blk, blk_per_seq):
    i = pl.program_id(0)
    cur = cur_ref[...]
    prevblk = prevblk_ref[...]
    rows = cur.shape[0]
    seq = jnp.minimum(i // blk_per_seq, shiftp_ref.shape[0] - 1)
    first = (i % blk_per_seq) == 0
    row0 = jnp.where(first, shiftp_ref[pl.ds(seq, 1), :], prevblk[rows - 1:rows, :])
    rid = lax.broadcasted_iota(jnp.int32, (rows, 1), 0)
    prev_p = jnp.where(rid == 0, row0, pltpu.roll(cur, 1, 0))
    prev_s = jnp.where(i == n_prompt_blk, shifts_ref[...], prevblk)
    prev = jnp.where(i >= n_prompt_blk, prev_s, prev_p)

    z = cur + (prev - cur) * mu_ref[...]
    o = 3 * rw
    zw = z[:, o:o + dwp]
    za = z[:, o + dwp:o + dwp + dap]
    zg = z[:, o + dwp + dap:]
    lw = w0_ref[...] + _dot(jnp.tanh(zw).astype(BF16), w2_ref[...])
    w_log = -_softplus(-lw) - 0.5
    decay = jnp.exp(-jnp.exp(w_log))
    a = jax.nn.sigmoid(a0_ref[...] + _dot(za.astype(BF16), a2_ref[...]))
    g_ref[...] = _dot(jax.nn.sigmoid(zg).astype(BF16), g2_ref[...])

    def write(r_ref, k_ref, v_ref, w_ref, a_ref):
        r_ref[...] = z[:, :rw]
        k_ref[...] = z[:, rw:2 * rw]
        v_ref[...] = z[:, 2 * rw:3 * rw]
        w_ref[...] = decay
        a_ref[...] = a

    @pl.when(i < n_prompt_blk)
    def _():
        write(rp_ref, kp_ref, vp_ref, wp_ref, ap_ref)

    @pl.when(i >= n_prompt_blk)
    def _():
        write(rs_ref, ks_ref, vs_ref, ws_ref, as_ref)


def _rwkv_prep(p_rw, shift_p, shift_s, mu, w0, w2, a0, a2, g2, *, rw, mp, tp, bs):
    m, prw = p_rw.shape
    ms = m - mp
    assert mp % bs == 0 and tp % bs == 0 and ms % bs == 0
    npb = mp // bs
    dwp, dap = w2.shape[0], a2.shape[0]
    full = lambda a: pl.BlockSpec(a.shape, lambda i: (0,) * a.ndim)
    bps = tp // bs
    bp = mp // tp

    def prompt_idx(i):
        ic = jnp.minimum(i, npb - 1)
        return ic % bps, ic // bps

    spec_p = pl.BlockSpec((bs, rw), prompt_idx)
    spec_s = pl.BlockSpec((bs, rw), lambda i: (jnp.maximum(i - npb, 0), 0))
    body = functools.partial(_rwkv_prep_body, rw=rw, dwp=dwp, dap=dap, n_prompt_blk=npb, blk_per_seq=tp // bs)
    outs = pl.pallas_call(
        body,
        grid=(m // bs,),
        in_specs=[
            pl.BlockSpec((bs, prw), lambda i: (i, 0)),
            pl.BlockSpec((bs, prw), lambda i: (jnp.maximum(i - 1, 0), 0)),
            full(shift_p), full(shift_s), full(mu), full(w0), full(w2), full(a0), full(a2), full(g2),
        ],
        out_specs=[spec_p] * 5 + [spec_s] * 5 + [pl.BlockSpec((bs, rw), lambda i: (i, 0))],
        out_shape=[jax.ShapeDtypeStruct((tp, bp * rw), F32)] * 5 + [jax.ShapeDtypeStruct((ms, rw), F32)] * 5
        + [jax.ShapeDtypeStruct((m, rw), F32)],
        compiler_params=_cparams("arbitrary"),
        name="rwkv_prep",
    )(p_rw, p_rw, shift_p, shift_s, mu, w0, w2, a0, a2, g2)
    return outs[:5], outs[5:10], outs[10]


def _rwkv_scan_body(r_ref, w_ref, k_ref, a_ref, v_ref, kk_p_ref, ka_p_ref, rk_p_ref, lnw_ref, lnb_ref, s0_ref,
                    o_ref, sout_ref, s_ref, r_s, w_s, nkk_s, b_s, km_s, *, vsplit, state_rows):
    tc_len, nv, lanes = v_ref.shape
    kc = s_ref.shape[1]
    ti = pl.program_id(1)

    @pl.when(ti == 0)
    def _():
        if state_rows:
            s_ref[...] = s0_ref[...].reshape(lanes, nv * kc).T.reshape(nv, kc, lanes)
        else:
            s_ref[...] = s0_ref[...]

    def widen(x):
        return jnp.concatenate([x, x], axis=-1) if vsplit else x

    kvec = widen(k_ref[...])
    a_ = widen(a_ref[...])
    r_s[...] = widen(r_ref[...])
    w_s[...] = widen(w_ref[...])
    kk = kvec * kk_p_ref[...][None]
    nrm = jnp.sqrt(jnp.sum(kk * kk, axis=1, keepdims=True))
    kk = kk / jnp.maximum(nrm, 1e-12)
    nkk_s[...] = -kk
    b_s[...] = kk * a_
    km_s[...] = kvec * (1.0 + (a_ - 1.0) * ka_p_ref[...][None])

    def step(t, carry):
        nkk = nkk_s[t]
        b = b_s[t]
        km = km_s[t]
        w = w_s[t]
        r = r_s[t]
        vt = v_ref[t]
        for vi in range(nv):
            sv = s_ref[vi]
            sa = jnp.sum(sv * nkk, axis=0, keepdims=True)
            sv = sv * w + sa * b + vt[vi:vi + 1, :] * km
            s_ref[vi] = sv
            o_ref[t, pl.ds(vi, 1), :] = jnp.sum(sv * r, axis=0, keepdims=True)
        return carry

    lax.fori_loop(0, tc_len, step, 0)

    o = o_ref[...]
    n_val = nv * (2 if vsplit else 1)

    def head_sum(x):
        if vsplit:
            x2 = x.reshape(tc_len * nv, lanes)
            x = (x2 + pltpu.roll(x2, lanes // 2, 1)).reshape(tc_len, nv, lanes)
        return jnp.sum(x, axis=1, keepdims=True)

    mu = head_sum(o) / n_val
    d = o - mu
    var = head_sum(d * d) / n_val
    o = d * lax.rsqrt(var + RW_GN_EPS) * lnw_ref[...][None] + lnb_ref[...][None]
    bonus = jnp.sum(r_s[...] * km_s[...] * rk_p_ref[...][None], axis=1, keepdims=True)
    o_ref[...] = o + bonus * v_ref[...]

    @pl.when(ti == pl.num_programs(1) - 1)
    def _():
        if state_rows:
            sout_ref[...] = s_ref[...].reshape(nv * kc, lanes).T.reshape(sout_ref.shape)
        else:
            sout_ref[...] = s_ref[...]


def _rwkv_scan(r, w, k, a, v, kk_p, ka_p, rk_p, lnw, lnb, s0, *, tc, vsplit, layer=None, after=None):
    t, kc, kl = r.shape
    nv = v.shape[1]
    ln = v.shape[2]
    g = ln // LANES
    kspec = pl.BlockSpec((tc, kc, kl // g), lambda gi, ti: (ti, 0, gi))
    vspec = pl.BlockSpec((tc, nv, LANES), lambda gi, ti: (ti, 0, gi))
    pk = pl.BlockSpec((kc, LANES), lambda gi, ti: (0, 0))
    pv = pl.BlockSpec((nv, LANES), lambda gi, ti: (0, 0))
    state_rows = not vsplit
    operands = [r, w, k, a, v, kk_p, ka_p, rk_p, lnw, lnb, s0]
    in_specs = [kspec, kspec, kspec, kspec, vspec, pk, pk, pk, pv, pv]
    aliases = {}
    if state_rows:
        nb, _, nh, sz = s0.shape
        assert sz == nv * kc and nb * nh == ln
        sspec = pl.BlockSpec((LANES // nh, None, nh, sz), lambda gi, ti: (gi, layer, 0, 0))
        in_specs.append(sspec)
        aliases = {10: 1}
        s_shape = s0.shape
    else:
        sspec = pl.BlockSpec((nv, kc, LANES), lambda gi, ti: (0, 0, gi))
        in_specs.append(sspec)
        s_shape = (nv, kc, ln)
    if after is not None:
        operands.append(after)
        in_specs.append(pl.BlockSpec(memory_space=pl.ANY))

    def body(*refs):
        n_in = len(operands)
        ins = refs[:11]
        _rwkv_scan_body(*ins, *refs[n_in:], vsplit=vsplit, state_rows=state_rows)

    return pl.pallas_call(
        body,
        grid=(g, t // tc),
        in_specs=in_specs,
        out_specs=[vspec, sspec],
        out_shape=[jax.ShapeDtypeStruct((t, nv, ln), F32), jax.ShapeDtypeStruct(s_shape, F32)],
        scratch_shapes=[pltpu.VMEM((nv, kc, LANES), F32)] + [pltpu.VMEM((tc, kc, LANES), F32)] * 5,
        input_output_aliases=aliases,
        compiler_params=_cparams("parallel", "arbitrary"),
        name="rwkv_scan",
    )(*operands)


def _ref_rows(gc, j):
    c, w = gc.shape
    b = 1 << j
    nt = c // SUBLANES
    gv = gc.reshape(nt, SUBLANES, w)
    if 2 * b <= SUBLANES:
        sub = lax.broadcasted_iota(jnp.int32, (1, SUBLANES, 1), 1)
        out = None
        for g0 in range(0, SUBLANES, 2 * b):
            piece = jnp.broadcast_to(gv[:, g0 + b - 1:g0 + b, :], gv.shape)
            out = piece if out is None else jnp.where(sub >= g0, piece, out)
        return out.reshape(c, w)
    tiles_per_group = 2 * b // SUBLANES
    pieces = []
    for ti in range(nt):
        src = (ti // tiles_per_group) * tiles_per_group + b // SUBLANES - 1
        pieces.append(jnp.broadcast_to(gv[src, SUBLANES - 1:SUBLANES, :], (SUBLANES, w)))
    return jnp.concatenate(pieces, axis=0)


def _hgrn_chunk(q, f, val, og, get_st, set_st, loglb, log1mlb, nrm_w, seq_len):
    c, width = q.shape
    nh = width // HG_DIM
    nseq = c // seq_len
    lg = seq_len.bit_length() - 1
    assert (1 << lg) == seq_len and c % SUBLANES == 0
    ls = jnp.minimum(f, 0.0) - jnp.log1p(jnp.exp(-jnp.abs(f)))
    b_ = log1mlb + ls
    g = jnp.maximum(loglb, b_) + jnp.log1p(jnp.exp(-jnp.abs(loglb - b_)))
    kk = _neg_expm1(g)
    qs = q * jax.nn.sigmoid(q)
    row = lax.broadcasted_iota(jnp.int32, (c, 1), 0)
    col = lax.broadcasted_iota(jnp.int32, (1, c), 1)
    tpos = row & (seq_len - 1)
    gc = g
    for j in range(lg):
        s = 1 << j
        gc = gc + jnp.where(tpos >= s, pltpu.roll(gc, s, 0), 0.0)
    qbs, kbs, sames = [qs.astype(BF16)], [kk.astype(BF16)], [row == col]
    for j in range(lg):
        e = jnp.exp(-jnp.abs(gc - _ref_rows(gc, j)))
        is_q = ((row >> j) & 1) == 1
        pk = (jnp.where(is_q, qs, kk) * e).astype(BF16)
        qbs.append(pk)
        kbs.append(pk)
        sames.append(((row >> (j + 1)) == (col >> (j + 1))) & is_q & (((col >> j) & 1) == 0))
    vb = val.astype(BF16)
    qg = (qs * jnp.exp(gc)).astype(BF16)
    k2s, eglast, in_seq = [], [], []
    for s_i in range(nseq):
        glast = gc[s_i * seq_len + seq_len - 1:s_i * seq_len + seq_len, :]
        eglast.append(jnp.exp(glast))
        if nseq == 1:
            in_seq.append(None)
            k2s.append((kk * jnp.exp(glast - gc)).astype(BF16))
        else:
            in_s = (row >> lg) == s_i
            in_seq.append(in_s)
            k2s.append(jnp.where(in_s, kk * jnp.exp(jnp.minimum(glast - gc, 0.0)), 0.0).astype(BF16))
    outs = []
    for h in range(nh):
        cs = slice(h * HG_DIM, (h + 1) * HG_DIM)
        amat = None
        for qb, kb, same in zip(qbs, kbs, sames):
            term = jnp.where(same, _dot_nt(qb[:, cs], kb[:, cs]), 0.0)
            amat = term if amat is None else amat + term
        o = _dot(amat.astype(BF16), vb[:, cs])
        for s_i in range(nseq):
            st = get_st(s_i, h)
            o_int = _dot_nt(qg[:, cs], st.astype(BF16))
            o = o + (o_int if nseq == 1 else jnp.where(in_seq[s_i], o_int, 0.0))
            set_st(s_i, h, st * eglast[s_i][:, cs] + _dot_tn(vb[:, cs], k2s[s_i][:, cs]))
        outs.append(o * lax.rsqrt(jnp.mean(o * o, axis=-1, keepdims=True) + HG_NORM_EPS))
    on = jnp.concatenate(outs, axis=1) * nrm_w
    return on * (og * jax.nn.sigmoid(og))


def _hgrn_prompt_body(q_ref, f_ref, i_ref, og_ref, s0_ref, loglb_ref, log1m_ref, nw_ref, o_ref, sout_ref, st_ref,
                      *, chunk):
    nh = st_ref.shape[0]
    tb = pl.program_id(1)

    @pl.when(tb == 0)
    def _():
        for h in range(nh):
            st_ref[h] = s0_ref[h].T

    def set_st(s_i, h, v):
        st_ref[h] = v

    def body(ci, carry):
        rows = pl.ds(pl.multiple_of(ci * chunk, chunk), chunk)
        y = _hgrn_chunk(q_ref[rows, :], f_ref[rows, :], i_ref[rows, :], og_ref[rows, :],
                        lambda s_i, h: st_ref[h], set_st, loglb_ref[...], log1m_ref[...], nw_ref[...], chunk)
        o_ref[rows, :] = y.astype(BF16)
        return carry

    lax.fori_loop(0, q_ref.shape[0] // chunk, body, 0)

    @pl.when(tb == pl.num_programs(1) - 1)
    def _():
        for h in range(nh):
            sout_ref[h] = st_ref[h].T


def _hgrn_prompt(p_hg, s0, loglb, log1m, nw, *, bp, tp, chunk):
    hgw = p_hg.shape[1] // 4
    nh = hgw // HG_DIM
    tb = _pick_block(tp, 512, chunk)
    nt = tp // tb
    cspec = lambda c: pl.BlockSpec((tb, hgw), lambda b, t: (b * nt + t, c))
    pspec = pl.BlockSpec((1, hgw), lambda b, t: (0, 0))
    sspec = pl.BlockSpec((None, nh, HG_DIM, HG_DIM), lambda b, t: (b, 0, 0, 0))
    return pl.pallas_call(
        functools.partial(_hgrn_prompt_body, chunk=chunk),
        grid=(bp, nt),
        in_specs=[cspec(0), cspec(1), cspec(2), cspec(3), sspec, pspec, pspec, pspec],
        out_specs=[pl.BlockSpec((tb, hgw), lambda b, t: (b * nt + t, 0)), sspec],
        out_shape=[jax.ShapeDtypeStruct((p_hg.shape[0], hgw), BF16), jax.ShapeDtypeStruct(s0.shape, F32)],
        scratch_shapes=[pltpu.VMEM((nh, HG_DIM, HG_DIM), F32)],
        compiler_params=_cparams("parallel", "arbitrary"),
        name="hgrn_prompt",
    )(p_hg, p_hg, p_hg, p_hg, s0, loglb, log1m, nw)


def _hgrn_sample_body(q_ref, f_ref, i_ref, og_ref, s0_ref, loglb_ref, log1m_ref, nw_ref, *rest, seq_len, nseq):
    o_ref, sout_ref = rest[-2:]
    c = seq_len * nseq

    def body(gi, carry):
        rows = pl.ds(pl.multiple_of(gi * c, c), c)

        def set_st(s_i, h, v):
            sout_ref[gi * nseq + s_i, h] = v.T

        y = _hgrn_chunk(q_ref[rows, :], f_ref[rows, :], i_ref[rows, :], og_ref[rows, :],
                        lambda s_i, h: s0_ref[gi * nseq + s_i, h].T, set_st,
                        loglb_ref[...], log1m_ref[...], nw_ref[...], seq_len)
        o_ref[rows, :] = y.astype(BF16)
        return carry

    lax.fori_loop(0, q_ref.shape[0] // c, body, 0)


def _hgrn_sample(p_hg_bm, s0, layer, loglb, log1m, nw, *, bs, ts, s_prev=None, after=None):
    hgw = p_hg_bm.shape[1] // 4
    nh = hgw // HG_DIM
    nseq = max(1, 16 // ts)
    assert bs % nseq == 0
    bb = _pick_block(bs, 8, nseq)
    rows = bb * ts
    cspec = lambda c: pl.BlockSpec((rows, hgw), lambda i: (i, c))
    pspec = pl.BlockSpec((1, hgw), lambda i: (0, 0))
    sspec = pl.BlockSpec((bb, None, nh, HG_DIM, HG_DIM), lambda i: (i, layer, 0, 0, 0))
    operands = [p_hg_bm, p_hg_bm, p_hg_bm, p_hg_bm, s0, loglb, log1m, nw]
    in_specs = [cspec(0), cspec(1), cspec(2), cspec(3), sspec, pspec, pspec, pspec]
    aliases = {}
    if s_prev is not None:
        operands.append(s_prev)
        in_specs.append(pl.BlockSpec(memory_space=pl.ANY))
        aliases = {len(operands) - 1: 1}
    if after is not None:
        operands.append(after)
        in_specs.append(pl.BlockSpec(memory_space=pl.ANY))
    return pl.pallas_call(
        functools.partial(_hgrn_sample_body, seq_len=ts, nseq=nseq),
        grid=(bs // bb,),
        in_specs=in_specs,
        out_specs=[pl.BlockSpec((rows, hgw), lambda i: (i, 0)), sspec],
        out_shape=[jax.ShapeDtypeStruct((bs * ts, hgw), BF16), jax.ShapeDtypeStruct(s0.shape, F32)],
        input_output_aliases=aliases,
        compiler_params=_cparams("parallel"),
        name="hgrn_sample",
    )(*operands)


def _lru_gates(xconv, wa_ref, wx_ref, ba, bx, lam):
    l = xconv.shape[1]
    blk = l // LRU_BLOCKS
    ga, gx = [], []
    for n in range(LRU_BLOCKS):
        xb = xconv[:, n * blk:(n + 1) * blk].astype(BF16)
        ga.append(_dot(xb, wa_ref[n]))
        gx.append(_dot(xb, wx_ref[n]))
    ga = jnp.concatenate(ga, axis=1) + ba
    gx = jnp.concatenate(gx, axis=1) + bx
    log_a = -LRU_C * jax.nn.sigmoid(ga) * _softplus(-lam)
    a = jnp.exp(log_a)
    mult = jnp.sqrt(1.0 - a * a)
    return a, mult * jax.nn.sigmoid(gx) * xconv


def _gelu_tanh(x):
    c = 0.7978845608028654
    return 0.5 * x * (1.0 + jnp.tanh(c * (x + 0.044715 * (x * x * x))))


def _odd_prompt_body(gate_ref, xb_ref, conv0_ref, h0_ref, cw_ref, cb_ref, wa_ref, wx_ref, ba_ref, bx_ref, lam_ref,
                     y_ref, convo_ref, ho_ref, ext_ref, hc_ref, a_s, b_s, h_s):
    rows = xb_ref.shape[0]
    hist = SUBLANES

    @pl.when(pl.program_id(1) == 0)
    def _():
        ext_ref[0:hist, :] = conv0_ref[...]
        hc_ref[...] = h0_ref[...]

    ext_ref[hist:hist + rows, :] = xb_ref[...]
    cw = cw_ref[...]
    xconv = cw[0:1, :] * ext_ref[pl.ds(hist - (CONV_W - 1), rows), :]
    for j in range(1, CONV_W):
        xconv = xconv + cw[j:j + 1, :] * ext_ref[pl.ds(hist - (CONV_W - 1) + j, rows), :]
    xconv = cb_ref[...] + xconv
    a, b = _lru_gates(xconv, wa_ref, wx_ref, ba_ref[...], bx_ref[...], lam_ref[...])
    width = a.shape[1]
    a = a.reshape(rows // SUBLANES, SUBLANES, width)
    b = b.reshape(rows // SUBLANES, SUBLANES, width)
    r8 = lax.broadcasted_iota(jnp.int32, (1, SUBLANES, 1), 1)
    for s in (1, 2, 4):
        m = r8 >= s
        a_sh = pltpu.roll(a, s, 1)
        b_sh = pltpu.roll(b, s, 1)
        b = jnp.where(m, a * b_sh + b, b)
        a = jnp.where(m, a * a_sh, a)
    a_s[...] = a.reshape(rows, width)
    b_s[...] = b.reshape(rows, width)

    def tile(j, hc):
        r0 = pl.multiple_of(j * SUBLANES, SUBLANES)
        h = a_s[pl.ds(r0, SUBLANES), :] * hc + b_s[pl.ds(r0, SUBLANES), :]
        h_s[pl.ds(r0, SUBLANES), :] = h
        return h[SUBLANES - 1:SUBLANES, :]

    hc = lax.fori_loop(0, rows // SUBLANES, tile, hc_ref[...])
    hc_ref[...] = hc
    ho_ref[...] = hc
    y_ref[...] = (h_s[...] * _gelu_tanh(gate_ref[...])).astype(BF16)
    tail = ext_ref[rows:rows + hist, :]
    ext_ref[0:hist, :] = tail
    convo_ref[...] = tail


def _odd_prompt(p_od, conv0, h0, layer, cw, cb, wa, wx, ba, bx, lam, *, bp, tp):
    l = p_od.shape[1] // 2
    rows = _pick_block(tp, 256, SUBLANES)
    nt = tp // rows
    lsel = lambda a: pl.BlockSpec((None,) + a.shape[1:], lambda b, t: (layer,) + (0,) * (a.ndim - 1))
    return pl.pallas_call(
        _odd_prompt_body,
        grid=(bp, nt),
        in_specs=[
            pl.BlockSpec((rows, l), lambda b, t: (b * nt + t, 0)),
            pl.BlockSpec((rows, l), lambda b, t: (b * nt + t, 1)),
            pl.BlockSpec((None, SUBLANES, l), lambda b, t: (b, 0, 0)),
            pl.BlockSpec((None, 1, l), lambda b, t: (b, 0, 0)),
            lsel(cw), lsel(cb), lsel(wa), lsel(wx), lsel(ba), lsel(bx), lsel(lam),
        ],
        out_specs=[
            pl.BlockSpec((rows, l), lambda b, t: (b * nt + t, 0)),
            pl.BlockSpec((None, SUBLANES, l), lambda b, t: (b, 0, 0)),
            pl.BlockSpec((None, 1, l), lambda b, t: (b, 0, 0)),
        ],
        out_shape=[jax.ShapeDtypeStruct((p_od.shape[0], l), BF16), jax.ShapeDtypeStruct((bp, SUBLANES, l), F32),
                   jax.ShapeDtypeStruct((bp, 1, l), F32)],
        scratch_shapes=[pltpu.VMEM((rows + SUBLANES, l), F32), pltpu.VMEM((1, l), F32),
                        pltpu.VMEM((rows, l), F32), pltpu.VMEM((rows, l), F32), pltpu.VMEM((rows, l), F32)],
        compiler_params=_cparams("parallel", "arbitrary"),
        name="odd_prompt",
    )(p_od, p_od, conv0, h0, cw, cb, wa, wx, ba, bx, lam)


def _odd_sample_body(gate_ref, xb_ref, conv0_ref, h0_ref, cw_ref, cb_ref, wa_ref, wx_ref, ba_ref, bx_ref, lam_ref,
                     yall_ref, y_ref, convo_ref, ho_ref):
    @pl.when(pl.program_id(0) == 0)
    def _():
        convo_ref[...] = conv0_ref[...]
        ho_ref[...] = h0_ref[...]

    xb = xb_ref[...]
    cw = cw_ref[...]
    xconv = cw[CONV_W - 1:CONV_W, :] * xb
    for j in range(CONV_W - 1):
        xconv = xconv + cw[j:j + 1, :] * convo_ref[j]
    xconv = cb_ref[...] + xconv
    a, b = _lru_gates(xconv, wa_ref, wx_ref, ba_ref[...], bx_ref[...], lam_ref[...])
    h = a * ho_ref[...] + b
    ho_ref[...] = h
    y_ref[...] = (h * _gelu_tanh(gate_ref[...])).astype(BF16)
    for j in range(CONV_W - 2):
        convo_ref[j] = convo_ref[j + 1]
    convo_ref[CONV_W - 2] = xb


def _odd_sample(p_od, conv0_t, h0, layer, cw, cb, wa, wx, ba, bx, lam, y_all, *, mp, bs, ts):
    l = p_od.shape[1] // 2
    assert mp % bs == 0
    off = mp // bs
    lsel = lambda a: pl.BlockSpec((None,) + a.shape[1:], lambda t: (layer,) + (0,) * (a.ndim - 1))
    full = lambda a: pl.BlockSpec(a.shape, lambda t: (0,) * a.ndim)
    return pl.pallas_call(
        _odd_sample_body,
        grid=(ts,),
        in_specs=[
            pl.BlockSpec((bs, l), lambda t: (off + t, 0)),
            pl.BlockSpec((bs, l), lambda t: (off + t, 1)),
            full(conv0_t), full(h0),
            lsel(cw), lsel(cb), lsel(wa), lsel(wx), lsel(ba), lsel(bx), lsel(lam),
            pl.BlockSpec(memory_space=pl.ANY),
        ],
        out_specs=[pl.BlockSpec((bs, l), lambda t: (off + t, 0)), full(conv0_t), full(h0)],
        out_shape=[jax.ShapeDtypeStruct(y_all.shape, BF16), jax.ShapeDtypeStruct(conv0_t.shape, F32),
                   jax.ShapeDtypeStruct(h0.shape, F32)],
        input_output_aliases={11: 0},
        compiler_params=_cparams("arbitrary"),
        name="odd_sample",
    )(p_od, p_od, conv0_t, h0, cw, cb, wa, wx, ba, bx, lam, y_all)


def _pad_last(a, n):
    return jnp.pad(a, [(0, 0)] * (a.ndim - 1) + [(0, n - a.shape[-1])])


def _round_up(n, m):
    return (n + m - 1) // m * m


class _RwLayout:
    def __init__(self, rw, dw, da, dg):
        self.rw, self.dw, self.da, self.dg = rw, dw, da, dg
        self.dwp, self.dap, self.dgp = (_round_up(d, LANES) for d in (dw, da, dg))
        self.width = 3 * rw + dw + da + dg
        self.padded = 3 * rw + self.dwp + self.dap + self.dgp

    def pad(self, a):
        o = 3 * self.rw
        parts = [a[..., :o], _pad_last(a[..., o:o + self.dw], self.dwp),
                 _pad_last(a[..., o + self.dw:o + self.dw + self.da], self.dap),
                 _pad_last(a[..., o + self.dw + self.da:self.width], self.dgp)]
        return jnp.concatenate(parts, axis=-1)

    def unpad(self, a):
        o = 3 * self.rw
        parts = [a[..., :o], a[..., o:o + self.dw], a[..., o + self.dwp:o + self.dwp + self.da],
                 a[..., o + self.dwp + self.dap:o + self.dwp + self.dap + self.dg]]
        return jnp.concatenate(parts, axis=-1)


def kernel(x_prompt, x_sample, state_rwkv_shift, state_rwkv, state_hgrn, state_conv, state_lru, ffn1_norm, ffn1_w_gu, ffn1_w_down, mix_norm, ffn2_norm, ffn2_w_gu, ffn2_w_down, ev_w_in, rw_mu, rw_w0, rw_w2, rw_a0, rw_a2, rw_g2, rw_k_k, rw_k_a, rw_r_k, rw_ln_w, rw_ln_b, hg_lb, hg_norm, ev_w_out, od_w_in, conv_w, conv_b, lru_wa, lru_ba, lru_wx, lru_bx, lru_lambda, od_w_out, final_norm):
    bp, tp, d = x_prompt.shape
    bs, ts, _ = x_sample.shape
    depth = ffn1_norm.shape[0]
    n_even = ev_w_in.shape[0]
    mp, ms = bp * tp, bs * ts
    rw = rw_w0.shape[1]
    nh_rw = rw // RW_HEAD
    hgw = hg_norm.shape[1]
    nh_hg = hgw // HG_DIM
    lay = _RwLayout(rw, rw_w2.shape[1], rw_a2.shape[1], rw_g2.shape[1])
    assert bp * nh_rw * 2 == LANES and (bs * nh_rw) % LANES == 0
    dt = x_prompt.dtype

    ffn_src = {1: (ffn1_norm, ffn1_w_gu, ffn1_w_down), 2: (ffn2_norm, ffn2_w_gu, ffn2_w_down)}
    ffn_w = (ffn1_w_gu[0].astype(BF16), ffn1_w_down[0].astype(BF16))

    def ffn_step(x, l, which, ffn_w):
        norm_g = ffn_src[which][0]
        nxt = (l, 2) if which == 1 else (l + 1, 1)
        if nxt[0] >= depth:
            return _ffn(x, norm_g, l, *ffn_w)[0], None
        x, c_gu, c_dn = _ffn(x, norm_g, l, *ffn_w, next_src=(ffn_src[nxt[1]][1], ffn_src[nxt[1]][2], nxt[0]))
        return x, (c_gu, c_dn)

    ev_out = ev_w_out.astype(BF16)
    od_in, od_out = od_w_in.astype(BF16), od_w_out.astype(BF16)

    wa_bf, wx_bf = lru_wa.astype(BF16), lru_wx.astype(BF16)

    x = jnp.concatenate([x_prompt.reshape(mp, d), x_sample.transpose(1, 0, 2).reshape(ms, d)], axis=0)

    lbs = jnp.cumsum(jax.nn.softmax(hg_lb.astype(F32), axis=0), axis=0)
    lb_all = lbs - lbs[0]

    def k_lanes_p(a):
        return a.reshape(tp, bp * nh_rw, RW_HEAD).swapaxes(1, 2)

    def v_lanes_p(a):
        return a.reshape(tp, bp, nh_rw, 2, RW_HEAD // 2).transpose(0, 4, 3, 1, 2).reshape(tp, RW_HEAD // 2, LANES)

    def v_lanes_p_inv(a):
        return a.reshape(tp, RW_HEAD // 2, 2, bp, nh_rw).transpose(3, 0, 4, 2, 1).reshape(mp, rw)

    def k_lanes_s(a):
        return a.reshape(ts, bs, nh_rw, RW_HEAD).transpose(0, 3, 1, 2).reshape(ts, RW_HEAD, bs * nh_rw)

    def k_lanes_s_inv(a):
        return a.reshape(ts, RW_HEAD, bs, nh_rw).transpose(0, 2, 3, 1).reshape(ms, rw)

    def kparam(p):
        return jnp.tile(p.reshape(nh_rw, RW_HEAD).T, (1, LANES // nh_rw))

    def vparam_p(p):
        a = p.reshape(nh_rw, 2, RW_HEAD // 2).transpose(2, 1, 0)[:, :, None, :]
        return jnp.broadcast_to(a, (RW_HEAD // 2, 2, bp, nh_rw)).reshape(RW_HEAD // 2, LANES)

    shifts_p, shifts_s, rws_p, hgs_p, convs_p, convs_s, lrus_p, lrus_s = ([] for _ in range(8))
    rw_rows = state_rwkv.astype(F32).reshape(bs, n_even, nh_rw, RW_HEAD * RW_HEAD)
    hg_new = None

    for l in range(depth):
        x, ffn_w = ffn_step(x, l, 1, ffn_w)
        if l % 2 == 0:
            e = l // 2
            w_in = ev_w_in[e]
            w_rw = lay.pad(w_in[:, :lay.width]).astype(BF16)
            w_hg = w_in[:, lay.width:].astype(BF16)
            p_rw = _norm_proj(x, mix_norm, l, w_rw, _pick_block(lay.padded, 1792, LANES))
            p_hg = _norm_proj(x, mix_norm, l, w_hg, _pick_block(4 * hgw, 1024, LANES))
            shift_p = jnp.zeros((bp, lay.padded), F32)
            shift_s = lay.pad(state_rwkv_shift[:, e].astype(F32))
            rkvwa_p, rkvwa_s, g_ = _rwkv_prep(
                p_rw, shift_p, shift_s, lay.pad(rw_mu[e])[None], rw_w0[e][None],
                _pad_last(rw_w2[e].T, lay.dwp).T.astype(BF16), rw_a0[e][None],
                _pad_last(rw_a2[e].T, lay.dap).T.astype(BF16),
                _pad_last(rw_g2[e].T, lay.dgp).T.astype(BF16), rw=rw, mp=mp, tp=tp, bs=bs)
            shifts_p.append(lay.unpad(jnp.concatenate([p_rw[b * tp + tp - 1:(b + 1) * tp] for b in range(bp)])))
            shifts_s.append(lay.unpad(p_rw[mp + (ts - 1) * bs:]))
            kk_p, ka_p = kparam(rw_k_k[e]), kparam(rw_k_a[e])
            rk_p = kparam(rw_r_k[e].reshape(rw))
            lb = lb_all[e][None]
            loglb, log1m = jnp.log(lb), jnp.log1p(-lb)
            nw = hg_norm[e][None]
            chunk = _pick_block(tp, HG_CHUNK, 1)
            oh_p, hs_p = _hgrn_prompt(p_hg, jnp.zeros((bp, nh_hg, HG_DIM, HG_DIM), F32), loglb, log1m, nw,
                                      bp=bp, tp=tp, chunk=chunk)
            r_p, k_p, v_p, w_p, a_p = rkvwa_p
            o_p, s_p = _rwkv_scan(
                k_lanes_p(r_p), k_lanes_p(w_p), k_lanes_p(k_p), k_lanes_p(a_p), v_lanes_p(v_p),
                kk_p, ka_p, rk_p, vparam_p(rw_ln_w[e]), vparam_p(rw_ln_b[e]),
                jnp.zeros((RW_HEAD // 2, RW_HEAD, LANES), F32), tc=_pick_block(tp, 64, 1), vsplit=True,
                after=oh_p)
            rws_p.append(s_p.reshape(RW_HEAD // 2, RW_HEAD, 2, bp, nh_rw).transpose(3, 4, 2, 0, 1)
                         .reshape(bp, nh_rw, RW_HEAD, RW_HEAD))
            r_s, k_s, v_s, w_s, a_s = rkvwa_s
            o_s, rw_rows = _rwkv_scan(
                k_lanes_s(r_s), k_lanes_s(w_s), k_lanes_s(k_s), k_lanes_s(a_s), k_lanes_s(v_s),
                kk_p, ka_p, rk_p, kparam(rw_ln_w[e]), kparam(rw_ln_b[e]),
                rw_rows, tc=ts, vsplit=False, layer=e, after=o_p)
            o_rw = jnp.concatenate([v_lanes_p_inv(o_p), k_lanes_s_inv(o_s)], axis=0)
            p_hg_bm = p_hg[mp:].reshape(ts, bs, 4 * hgw).transpose(1, 0, 2).reshape(ms, 4 * hgw)
            oh_s, hg_new = _hgrn_sample(p_hg_bm, state_hgrn.astype(F32), e, loglb, log1m, nw, bs=bs, ts=ts,
                                        s_prev=hg_new, after=o_p)
            oh_s = oh_s.reshape(bs, ts, hgw).transpose(1, 0, 2).reshape(ms, hgw)
            hgs_p.append(hs_p)
            o_hg = oh_p.at[mp:].set(oh_s)
            x = _proj_res_even(x, o_rw, g_, o_hg, ev_out, e)
        else:
            o_i = l // 2
            p_od = _norm_proj(x, mix_norm, l, od_in, _pick_block(od_in.shape[-1], 1024, LANES), w_layer=o_i)
            lsz = p_od.shape[1] // 2
            vec = lambda a: a[:, None, :]
            args = (conv_w, vec(conv_b), wa_bf, wx_bf, vec(lru_ba), vec(lru_bx), vec(lru_lambda))
            conv0_p = jnp.zeros((bp, SUBLANES, lsz), F32)
            y_p, c_p, h_p = _odd_prompt(p_od, conv0_p, jnp.zeros((bp, 1, lsz), F32), o_i, *args, bp=bp, tp=tp)
            conv0_s = state_conv[:, o_i].astype(F32).transpose(1, 0, 2)
            y_all, c_s, h_s = _odd_sample(p_od, conv0_s, state_lru[:, o_i].astype(F32), o_i, *args, y_p,
                                          mp=mp, bs=bs, ts=ts)
            convs_p.append(c_p[:, SUBLANES - (CONV_W - 1):])
            convs_s.append(c_s.transpose(1, 0, 2))
            lrus_p.append(h_p[:, 0])
            lrus_s.append(h_s)
            x = _proj_res(x, y_all, od_out, o_i)
        x, ffn_w = ffn_step(x, l, 2, ffn_w)

    y_p, y_s = _final_norm(x, final_norm, mp)
    y_prompt = y_p.reshape(bp, tp, d)
    y_sample = y_s.reshape(ts, bs, d).transpose(1, 0, 2)
    st = lambda xs: jnp.stack(xs, axis=1).astype(dt)
    sample_rwkv = rw_rows.reshape(state_rwkv.shape).astype(dt)
    return (y_prompt, y_sample, st(shifts_p), st(rws_p), st(hgs_p), st(convs_p), st(lrus_p),
            st(shifts_s), sample_rwkv, hg_new.astype(dt), st(convs_s), st(lrus_s))
```

```python
import functools
import math

import jax
import jax.numpy as jnp
from jax import lax
from jax.experimental import pallas as pl
from jax.experimental.pallas import tpu as pltpu

F32 = jnp.float32
BF16 = jnp.bfloat16

NORM_EPS = 1e-6
RW_GN_EPS = 64e-5
HG_NORM_EPS = 1e-5
LRU_C = 8.0
RW_HEAD = 64
HG_DIM = 128
LRU_BLOCKS = 8
CONV_W = 4
LANES = 128
SUBLANES = 8
VMEM_LIMIT = 62 * 1024 * 1024
MAX_BM = 1088
HG_CHUNK = 64


def _cparams(*sem):
    return pltpu.CompilerParams(dimension_semantics=sem, vmem_limit_bytes=VMEM_LIMIT)


def _pick_block(n, cap, mult):
    best = None
    for d in range(mult, min(n, cap) + 1, mult):
        if n % d == 0:
            best = d
    assert best is not None, (n, cap, mult)
    return best


def _gcd(a, b):
    return math.gcd(a, b)


def _dot(a, b):
    return jnp.dot(a, b, preferred_element_type=F32)


def _dot_nt(a, b):
    return lax.dot_general(a, b, (((1,), (1,)), ((), ())), preferred_element_type=F32)


def _dot_tn(a, b):
    return lax.dot_general(a, b, (((0,), (0,)), ((), ())), preferred_element_type=F32)


def _softplus(x):
    return jnp.maximum(x, 0.0) + jnp.log1p(jnp.exp(-jnp.abs(x)))


def _neg_expm1(y):
    return 1.0 - jnp.exp(y)


def _rms_rows(x, g):
    ms = jnp.mean(x * x, axis=-1, keepdims=True)
    return x * lax.rsqrt(ms + NORM_EPS) * g


def _ffn_body(x_ref, g_ref, wg_ref, wu_ref, wd_ref, *rest, cast_next):
    if cast_next:
        sgu_ref, sdn_ref, o_ref, cgu_ref, cdn_ref, xn_ref = rest
        cgu_ref[...] = sgu_ref[...].astype(BF16)
        cdn_ref[...] = sdn_ref[...].astype(BF16)
    else:
        o_ref, xn_ref = rest

    @pl.when(pl.program_id(1) == 0)
    def _():
        x = x_ref[...]
        xn_ref[...] = _rms_rows(x, g_ref[...]).astype(BF16)
        o_ref[...] = x

    xn = xn_ref[...]
    gate = _dot(xn, wg_ref[...])
    up = _dot(xn, wu_ref[...])
    act = (0.5 * (gate * jax.nn.sigmoid(gate)) * up).astype(BF16)
    o_ref[...] += _dot(act, wd_ref[...])


def _ffn(x, norm_g, layer, w_gu, w_down, next_src=None):
    m, d = x.shape
    f = w_down.shape[0]
    bm = _pick_block(m, MAX_BM, 16)
    bf = _pick_block(f, 512, LANES)
    ni, nf = m // bm, f // bf
    in_specs = [
        pl.BlockSpec((bm, d), lambda i, j: (i, 0)),
        pl.BlockSpec((None, 1, d), lambda i, j: (layer, 0, 0)),
        pl.BlockSpec((d, bf), lambda i, j: (0, j)),
        pl.BlockSpec((d, bf), lambda i, j: (0, j + nf)),
        pl.BlockSpec((bf, d), lambda i, j: (j, 0)),
    ]
    operands = [x, norm_g[:, None, :], w_gu, w_gu, w_down]
    out_specs = [pl.BlockSpec((bm, d), lambda i, j: (i, 0))]
    out_shape = [jax.ShapeDtypeStruct((m, d), F32)]
    if next_src is not None:
        s_gu, s_dn, nl = next_src
        assert d % ni == 0 and (2 * f) % nf == 0 and (d // ni) % 16 == 0 and (d // ni) % LANES == 0
        gu_blk, dn_blk = (d // ni, 2 * f // nf), (f // nf, d // ni)
        in_specs += [pl.BlockSpec((None,) + gu_blk, lambda i, j: (nl, i, j)),
                     pl.BlockSpec((None,) + dn_blk, lambda i, j: (nl, j, i))]
        operands += [s_gu, s_dn]
        out_specs += [pl.BlockSpec(gu_blk, lambda i, j: (i, j)), pl.BlockSpec(dn_blk, lambda i, j: (j, i))]
        out_shape += [jax.ShapeDtypeStruct((d, 2 * f), BF16), jax.ShapeDtypeStruct((f, d), BF16)]
    return pl.pallas_call(
        functools.partial(_ffn_body, cast_next=next_src is not None),
        grid=(ni, nf),
        in_specs=in_specs,
        out_specs=out_specs,
        out_shape=out_shape,
        scratch_shapes=[pltpu.VMEM((bm, d), BF16)],
        compiler_params=_cparams("parallel", "arbitrary"),
        name="ffn",
    )(*operands)


def _norm_proj_body(x_ref, g_ref, w_ref, o_ref, xn_ref):
    @pl.when(pl.program_id(1) == 0)
    def _():
        xn_ref[...] = _rms_rows(x_ref[...], g_ref[...]).astype(BF16)

    o_ref[...] = _dot(xn_ref[...], w_ref[...])


def _w_spec(w, blk, idx, w_layer):
    if w.ndim == 2:
        return pl.BlockSpec(blk, idx)
    return pl.BlockSpec((None,) + blk, lambda i, j: (w_layer,) + idx(i, j))


def _norm_proj(x, norm_g, layer, w, bn, w_layer=None):
    m, d = x.shape
    n = w.shape[-1]
    bm = _pick_block(m, MAX_BM, 16)
    return pl.pallas_call(
        _norm_proj_body,
        grid=(m // bm, n // bn),
        in_specs=[
            pl.BlockSpec((bm, d), lambda i, j: (i, 0)),
            pl.BlockSpec((None, 1, d), lambda i, j: (layer, 0, 0)),
            _w_spec(w, (d, bn), lambda i, j: (0, j), w_layer),
        ],
        out_specs=pl.BlockSpec((bm, bn), lambda i, j: (i, j)),
        out_shape=jax.ShapeDtypeStruct((m, n), F32),
        scratch_shapes=[pltpu.VMEM((bm, d), BF16)],
        compiler_params=_cparams("parallel", "arbitrary"),
        name="norm_proj",
    )(x, norm_g[:, None, :], w)


def _proj_res_body(x_ref, y_ref, w_ref, o_ref):
    o_ref[...] = x_ref[...] + _dot(y_ref[...], w_ref[...])


def _proj_res(x, y, w, w_layer):
    m, d = x.shape
    k = y.shape[1]
    bm = _pick_block(m, MAX_BM, 16)
    bn = _pick_block(d, 1024, LANES)
    return pl.pallas_call(
        _proj_res_body,
        grid=(m // bm, d // bn),
        in_specs=[
            pl.BlockSpec((bm, bn), lambda i, j: (i, j)),
            pl.BlockSpec((bm, k), lambda i, j: (i, 0)),
            _w_spec(w, (k, bn), lambda i, j: (0, j), w_layer),
        ],
        out_specs=pl.BlockSpec((bm, bn), lambda i, j: (i, j)),
        out_shape=jax.ShapeDtypeStruct((m, d), F32),
        compiler_params=_cparams("parallel", "arbitrary"),
        name="proj_res",
    )(x, y, w)


def _proj_res_even_body(x_ref, orw_ref, g_ref, ohg_ref, w1_ref, w2_ref, o_ref):
    y1 = (orw_ref[...] * g_ref[...]).astype(BF16)
    o_ref[...] = x_ref[...] + _dot(y1, w1_ref[...]) + _dot(ohg_ref[...], w2_ref[...])


def _proj_res_even(x, o_rw, g, o_hg, w, w_layer):
    m, d = x.shape
    rw = o_rw.shape[1]
    hg = o_hg.shape[1]
    assert rw == hg
    bm = _pick_block(m, MAX_BM, 16)
    bn = _pick_block(d, 1024, LANES)
    return pl.pallas_call(
        _proj_res_even_body,
        grid=(m // bm, d // bn),
        in_specs=[
            pl.BlockSpec((bm, bn), lambda i, j: (i, j)),
            pl.BlockSpec((bm, rw), lambda i, j: (i, 0)),
            pl.BlockSpec((bm, rw), lambda i, j: (i, 0)),
            pl.BlockSpec((bm, hg), lambda i, j: (i, 0)),
            _w_spec(w, (rw, bn), lambda i, j: (0, j), w_layer),
            _w_spec(w, (hg, bn), lambda i, j: (1, j), w_layer),
        ],
        out_specs=pl.BlockSpec((bm, bn), lambda i, j: (i, j)),
        out_shape=jax.ShapeDtypeStruct((m, d), F32),
        compiler_params=_cparams("parallel", "arbitrary"),
        name="proj_res_even",
    )(x, o_rw, g, o_hg, w, w)


def _final_norm_body(x_ref, g_ref, op_ref, os_ref, *, n_prompt_blk):
    y = _rms_rows(x_ref[...], g_ref[...])
    i = pl.program_id(0)

    @pl.when(i < n_prompt_blk)
    def _():
        op_ref[...] = y

    @pl.when(i >= n_prompt_blk)
    def _():
        os_ref[...] = y


def _final_norm(x, g, mp):
    m, d = x.shape
    ms = m - mp
    bm = _pick_block(_gcd(mp, ms), MAX_BM, SUBLANES)
    npb = mp // bm
    return pl.pallas_call(
        functools.partial(_final_norm_body, n_prompt_blk=npb),
        grid=(m // bm,),
        in_specs=[pl.BlockSpec((bm, d), lambda i: (i, 0)), pl.BlockSpec((1, d), lambda i: (0, 0))],
        out_specs=[pl.BlockSpec((bm, d), lambda i: (jnp.minimum(i, npb - 1), 0)),
                   pl.BlockSpec((bm, d), lambda i: (jnp.maximum(i - npb, 0), 0))],
        out_shape=[jax.ShapeDtypeStruct((mp, d), F32), jax.ShapeDtypeStruct((ms, d), F32)],
        compiler_params=_cparams("arbitrary"),
        name="final_norm",
    )(x, g[None, :])


def _rwkv_prep_body(cur_ref, prevblk_ref, shiftp_ref, shifts_ref, mu_ref, w0_ref, w2_ref, a0_ref, a2_ref,
                    g2_ref, r_ref, k_ref, v_ref, w_ref, a_ref, g_ref, *, rw, dwp, dap, n_prompt_blk, blk_per_seq):
    i = pl.program_id(0)
    rows = cur_ref.shape[0]
    prw = cur_ref.shape[1]
    seq = jnp.minimum(i // blk_per_seq, shiftp_ref.shape[0] - 1)
    first = (i % blk_per_seq) == 0
    is_sample = i >= n_prompt_blk
    rid = lax.broadcasted_iota(jnp.int32, (rows, 1), 0)
    shift_row = shiftp_ref[pl.ds(seq, 1), :]

    def shifted(c0, c1):
        cs = slice(c0, c1)
        cur = cur_ref[:, cs]
        prevblk = prevblk_ref[:, cs]
        row0 = jnp.where(first, shift_row[:, cs], prevblk[rows - 1:rows, :])
        prev_p = jnp.where(rid == 0, row0, pltpu.roll(cur, 1, 0))
        prev_s = jnp.where(i == n_prompt_blk, shifts_ref[:, cs], prevblk)
        prev = jnp.where(is_sample, prev_s, prev_p)
        return cur + (prev - cur) * mu_ref[:, cs]

    r_ref[...] = shifted(0, rw)
    k_ref[...] = shifted(rw, 2 * rw)
    v_ref[...] = shifted(2 * rw, 3 * rw)
    o = 3 * rw
    lw = w0_ref[...] + _dot(jnp.tanh(shifted(o, o + dwp)).astype(BF16), w2_ref[...])
    w_log = -_softplus(-lw) - 0.5
    w_ref[...] = jnp.exp(-jnp.exp(w_log))
    a_ref[...] = jax.nn.sigmoid(a0_ref[...] + _dot(shifted(o + dwp, o + dwp + dap).astype(BF16), a2_ref[...]))
    g_ref[...] = _dot(jax.nn.sigmoid(shifted(o + dwp + dap, prw)).astype(BF16), g2_ref[...])


def _rwkv_prep(p_rw, shift_p, shift_s, mu, w0, w2, a0, a2, g2, *, rw, mp, tp, bs):
    m, prw = p_rw.shape
    ms = m - mp
    assert mp % bs == 0 and tp % bs == 0 and ms % bs == 0
    npb = mp // bs
    dwp, dap = w2.shape[0], a2.shape[0]
    full = lambda a: pl.BlockSpec(a.shape, lambda i: (0,) * a.ndim)
    bps = tp // bs
    bp = mp // tp

    ts = ms // bs
    assert ts % bp == 0

    def out_idx(i):
        j = i - npb
        return jnp.where(i < npb, i % bps, bps + j // bp), jnp.where(i < npb, i // bps, j % bp)

    spec_o = pl.BlockSpec((bs, rw), out_idx)
    body = functools.partial(_rwkv_prep_body, rw=rw, dwp=dwp, dap=dap, n_prompt_blk=npb, blk_per_seq=tp // bs)
    outs = pl.pallas_call(
        body,
        grid=(m // bs,),
        in_specs=[
            pl.BlockSpec((bs, prw), lambda i: (i, 0)),
            pl.BlockSpec((bs, prw), lambda i: (jnp.maximum(i - 1, 0), 0)),
            full(shift_p), full(shift_s), full(mu), full(w0), full(w2), full(a0), full(a2), full(g2),
        ],
        out_specs=[spec_o] * 5 + [pl.BlockSpec((bs, rw), lambda i: (i, 0))],
        out_shape=[jax.ShapeDtypeStruct((tp + ms // bp, bp * rw), F32)] * 5 + [jax.ShapeDtypeStruct((m, rw), F32)],
        compiler_params=_cparams("arbitrary"),
        name="rwkv_prep",
    )(p_rw, p_rw, shift_p, shift_s, mu, w0, w2, a0, a2, g2)
    return outs[:5], outs[5]


def _rwkv_scan_body(r_ref, w_ref, k_ref, a_ref, v_ref, kk_p_ref, ka_p_ref, rk_p_ref, lnw_ref, lnb_ref, s0_ref,
                    o_ref, sout_ref, s_ref, r_s, w_s, nkk_s, b_s, km_s, *, vsplit, state_rows):
    tc_len, nv, lanes = v_ref.shape
    kc = s_ref.shape[1]
    ti = pl.program_id(1)

    @pl.when(ti == 0)
    def _():
        if state_rows:
            s_ref[...] = s0_ref[...].reshape(lanes, nv * kc).T.reshape(nv, kc, lanes)
        else:
            s_ref[...] = s0_ref[...]

    def widen(x):
        return jnp.concatenate([x, x], axis=-1) if vsplit else x

    kvec = widen(k_ref[...])
    a_ = widen(a_ref[...])
    r_s[...] = widen(r_ref[...])
    w_s[...] = widen(w_ref[...])
    kk = kvec * kk_p_ref[...][None]
    nrm = jnp.sqrt(jnp.sum(kk * kk, axis=1, keepdims=True))
    kk = kk / jnp.maximum(nrm, 1e-12)
    nkk_s[...] = -kk
    b_s[...] = kk * a_
    km_s[...] = kvec * (1.0 + (a_ - 1.0) * ka_p_ref[...][None])

    def step(t, carry):
        nkk = nkk_s[t]
        b = b_s[t]
        km = km_s[t]
        w = w_s[t]
        r = r_s[t]
        vt = v_ref[t]
        for vi in range(nv):
            sv = s_ref[vi]
            sa = jnp.sum(sv * nkk, axis=0, keepdims=True)
            sv = sv * w + sa * b + vt[vi:vi + 1, :] * km
            s_ref[vi] = sv
            o_ref[t, pl.ds(vi, 1), :] = jnp.sum(sv * r, axis=0, keepdims=True)
        return carry

    lax.fori_loop(0, tc_len, step, 0)

    o = o_ref[...]
    n_val = nv * (2 if vsplit else 1)

    def head_sum(x):
        if vsplit:
            x2 = x.reshape(tc_len * nv, lanes)
            x = (x2 + pltpu.roll(x2, lanes // 2, 1)).reshape(tc_len, nv, lanes)
        return jnp.sum(x, axis=1, keepdims=True)

    mu = head_sum(o) / n_val
    d = o - mu
    var = head_sum(d * d) / n_val
    o = d * lax.rsqrt(var + RW_GN_EPS) * lnw_ref[...][None] + lnb_ref[...][None]
    bonus = jnp.sum(r_s[...] * km_s[...] * rk_p_ref[...][None], axis=1, keepdims=True)
    o_ref[...] = o + bonus * v_ref[...]

    @pl.when(ti == pl.num_programs(1) - 1)
    def _():
        if state_rows:
            sout_ref[...] = s_ref[...].reshape(nv * kc, lanes).T.reshape(sout_ref.shape)
        else:
            sout_ref[...] = s_ref[...]


def _rwkv_scan(r, w, k, a, v, kk_p, ka_p, rk_p, lnw, lnb, s0, *, t, tc, vsplit, layer=None, after=None):
    _, kc, kl = r.shape
    nv = v.shape[1]
    ln = v.shape[2]
    g = ln // LANES
    kspec = pl.BlockSpec((tc, kc, kl // g), lambda gi, ti: (ti, 0, gi))
    vspec = pl.BlockSpec((tc, nv, LANES), lambda gi, ti: (ti, 0, gi))
    pk = pl.BlockSpec((kc, LANES), lambda gi, ti: (0, 0))
    pv = pl.BlockSpec((nv, LANES), lambda gi, ti: (0, 0))
    state_rows = not vsplit
    operands = [r, w, k, a, v, kk_p, ka_p, rk_p, lnw, lnb, s0]
    in_specs = [kspec, kspec, kspec, kspec, vspec, pk, pk, pk, pv, pv]
    aliases = {}
    if state_rows:
        nb, _, nh, sz = s0.shape
        assert sz == nv * kc and nb * nh == ln
        sspec = pl.BlockSpec((LANES // nh, None, nh, sz), lambda gi, ti: (gi, layer, 0, 0))
        in_specs.append(sspec)
        aliases = {10: 1}
        s_shape = s0.shape
    else:
        sspec = pl.BlockSpec((nv, kc, LANES), lambda gi, ti: (0, 0, gi))
        in_specs.append(sspec)
        s_shape = (nv, kc, ln)
    if after is not None:
        operands.append(after)
        in_specs.append(pl.BlockSpec(memory_space=pl.ANY))

    def body(*refs):
        n_in = len(operands)
        ins = refs[:11]
        _rwkv_scan_body(*ins, *refs[n_in:], vsplit=vsplit, state_rows=state_rows)

    return pl.pallas_call(
        body,
        grid=(g, t // tc),
        in_specs=in_specs,
        out_specs=[vspec, sspec],
        out_shape=[jax.ShapeDtypeStruct((t, nv, ln), F32), jax.ShapeDtypeStruct(s_shape, F32)],
        scratch_shapes=[pltpu.VMEM((nv, kc, LANES), F32)] + [pltpu.VMEM((tc, kc, LANES), F32)] * 5,
        input_output_aliases=aliases,
        compiler_params=_cparams("parallel", "arbitrary"),
        name="rwkv_scan",
    )(*operands)


def _ref_rows(gc, j):
    c, w = gc.shape
    b = 1 << j
    nt = c // SUBLANES
    gv = gc.reshape(nt, SUBLANES, w)
    if 2 * b <= SUBLANES:
        sub = lax.broadcasted_iota(jnp.int32, (1, SUBLANES, 1), 1)
        out = None
        for g0 in range(0, SUBLANES, 2 * b):
            piece = jnp.broadcast_to(gv[:, g0 + b - 1:g0 + b, :], gv.shape)
            out = piece if out is None else jnp.where(sub >= g0, piece, out)
        return out.reshape(c, w)
    tiles_per_group = 2 * b // SUBLANES
    pieces = []
    for ti in range(nt):
        src = (ti // tiles_per_group) * tiles_per_group + b // SUBLANES - 1
        pieces.append(jnp.broadcast_to(gv[src, SUBLANES - 1:SUBLANES, :], (SUBLANES, w)))
    return jnp.concatenate(pieces, axis=0)


def _hgrn_chunk(q, f, val, og, get_st, set_st, loglb, log1mlb, nrm_w, seq_len):
    c, width = q.shape
    nh = width // HG_DIM
    nseq = c // seq_len
    lg = seq_len.bit_length() - 1
    assert (1 << lg) == seq_len and c % SUBLANES == 0
    ls = jnp.minimum(f, 0.0) - jnp.log1p(jnp.exp(-jnp.abs(f)))
    b_ = log1mlb + ls
    g = jnp.maximum(loglb, b_) + jnp.log1p(jnp.exp(-jnp.abs(loglb - b_)))
    kk = _neg_expm1(g)
    qs = q * jax.nn.sigmoid(q)
    row = lax.broadcasted_iota(jnp.int32, (c, 1), 0)
    col = lax.broadcasted_iota(jnp.int32, (1, c), 1)
    tpos = row & (seq_len - 1)
    gc = g
    for j in range(lg):
        s = 1 << j
        gc = gc + jnp.where(tpos >= s, pltpu.roll(gc, s, 0), 0.0)
    qbs, kbs, sames = [qs.astype(BF16)], [kk.astype(BF16)], [row == col]
    for j in range(lg):
        e = jnp.exp(-jnp.abs(gc - _ref_rows(gc, j)))
        is_q = ((row >> j) & 1) == 1
        pk = (jnp.where(is_q, qs, kk) * e).astype(BF16)
        qbs.append(pk)
        kbs.append(pk)
        sames.append(((row >> (j + 1)) == (col >> (j + 1))) & is_q & (((col >> j) & 1) == 0))
    vb = val.astype(BF16)
    qg = (qs * jnp.exp(gc)).astype(BF16)
    k2s, eglast, in_seq = [], [], []
    for s_i in range(nseq):
        glast = gc[s_i * seq_len + seq_len - 1:s_i * seq_len + seq_len, :]
        eglast.append(jnp.exp(glast))
        if nseq == 1:
            in_seq.append(None)
            k2s.append((kk * jnp.exp(glast - gc)).astype(BF16))
        else:
            in_s = (row >> lg) == s_i
            in_seq.append(in_s)
            k2s.append(jnp.where(in_s, kk * jnp.exp(jnp.minimum(glast - gc, 0.0)), 0.0).astype(BF16))
    outs = []
    for h in range(nh):
        cs = slice(h * HG_DIM, (h + 1) * HG_DIM)
        amat = None
        for qb, kb, same in zip(qbs, kbs, sames):
            term = jnp.where(same, _dot_nt(qb[:, cs], kb[:, cs]), 0.0)
            amat = term if amat is None else amat + term
        o = _dot(amat.astype(BF16), vb[:, cs])
        for s_i in range(nseq):
            st = get_st(s_i, h)
            o_int = _dot_nt(qg[:, cs], st.astype(BF16))
            o = o + (o_int if nseq == 1 else jnp.where(in_seq[s_i], o_int, 0.0))
            set_st(s_i, h, st * eglast[s_i][:, cs] + _dot_tn(vb[:, cs], k2s[s_i][:, cs]))
        outs.append(o * lax.rsqrt(jnp.mean(o * o, axis=-1, keepdims=True) + HG_NORM_EPS))
    on = jnp.concatenate(outs, axis=1) * nrm_w
    return on * (og * jax.nn.sigmoid(og))


def _hgrn_prompt_body(q_ref, f_ref, i_ref, og_ref, s0_ref, loglb_ref, log1m_ref, nw_ref, o_ref, sout_ref, st_ref,
                      *, chunk):
    nh = st_ref.shape[0]
    tb = pl.program_id(1)

    @pl.when(tb == 0)
    def _():
        for h in range(nh):
            st_ref[h] = s0_ref[h].T

    def set_st(s_i, h, v):
        st_ref[h] = v

    def body(ci, carry):
        rows = pl.ds(pl.multiple_of(ci * chunk, chunk), chunk)
        y = _hgrn_chunk(q_ref[rows, :], f_ref[rows, :], i_ref[rows, :], og_ref[rows, :],
                        lambda s_i, h: st_ref[h], set_st, loglb_ref[...], log1m_ref[...], nw_ref[...], chunk)
        o_ref[rows, :] = y.astype(BF16)
        return carry

    lax.fori_loop(0, q_ref.shape[0] // chunk, body, 0)

    @pl.when(tb == pl.num_programs(1) - 1)
    def _():
        for h in range(nh):
            sout_ref[h] = st_ref[h].T


def _hgrn_prompt(p_hg, s0, loglb, log1m, nw, *, bp, tp, chunk):
    hgw = p_hg.shape[1] // 4
    nh = hgw // HG_DIM
    tb = _pick_block(tp, 512, chunk)
    nt = tp // tb
    cspec = lambda c: pl.BlockSpec((tb, hgw), lambda b, t: (b * nt + t, c))
    pspec = pl.BlockSpec((1, hgw), lambda b, t: (0, 0))
    sspec = pl.BlockSpec((None, nh, HG_DIM, HG_DIM), lambda b, t: (b, 0, 0, 0))
    return pl.pallas_call(
        functools.partial(_hgrn_prompt_body, chunk=chunk),
        grid=(bp, nt),
        in_specs=[cspec(0), cspec(1), cspec(2), cspec(3), sspec, pspec, pspec, pspec],
        out_specs=[pl.BlockSpec((tb, hgw), lambda b, t: (b * nt + t, 0)), sspec],
        out_shape=[jax.ShapeDtypeStruct((p_hg.shape[0], hgw), BF16), jax.ShapeDtypeStruct(s0.shape, F32)],
        scratch_shapes=[pltpu.VMEM((nh, HG_DIM, HG_DIM), F32)],
        compiler_params=_cparams("parallel", "arbitrary"),
        name="hgrn_prompt",
    )(p_hg, p_hg, p_hg, p_hg, s0, loglb, log1m, nw)


def _hgrn_sample_body(q_ref, f_ref, i_ref, og_ref, s0_ref, loglb_ref, log1m_ref, nw_ref, *rest, seq_len, nseq):
    o_ref, sout_ref = rest[-2:]
    c = seq_len * nseq

    def body(gi, carry):
        rows = pl.ds(pl.multiple_of(gi * c, c), c)

        def set_st(s_i, h, v):
            sout_ref[gi * nseq + s_i, h] = v.T

        y = _hgrn_chunk(q_ref[rows, :], f_ref[rows, :], i_ref[rows, :], og_ref[rows, :],
                        lambda s_i, h: s0_ref[gi * nseq + s_i, h].T, set_st,
                        loglb_ref[...], log1m_ref[...], nw_ref[...], seq_len)
        o_ref[rows, :] = y.astype(BF16)
        return carry

    lax.fori_loop(0, q_ref.shape[0] // c, body, 0)


def _hgrn_sample(p_hg_bm, s0, layer, loglb, log1m, nw, *, bs, ts, s_prev=None, after=None):
    hgw = p_hg_bm.shape[1] // 4
    nh = hgw // HG_DIM
    nseq = max(1, 16 // ts)
    assert bs % nseq == 0
    bb = _pick_block(bs, 8, nseq)
    rows = bb * ts
    cspec = lambda c: pl.BlockSpec((rows, hgw), lambda i: (i, c))
    pspec = pl.BlockSpec((1, hgw), lambda i: (0, 0))
    sspec = pl.BlockSpec((bb, None, nh, HG_DIM, HG_DIM), lambda i: (i, layer, 0, 0, 0))
    operands = [p_hg_bm, p_hg_bm, p_hg_bm, p_hg_bm, s0, loglb, log1m, nw]
    in_specs = [cspec(0), cspec(1), cspec(2), cspec(3), sspec, pspec, pspec, pspec]
    aliases = {}
    if s_prev is not None:
        operands.append(s_prev)
        in_specs.append(pl.BlockSpec(memory_space=pl.ANY))
        aliases = {len(operands) - 1: 1}
    if after is not None:
        operands.append(after)
        in_specs.append(pl.BlockSpec(memory_space=pl.ANY))
    return pl.pallas_call(
        functools.partial(_hgrn_sample_body, seq_len=ts, nseq=nseq),
        grid=(bs // bb,),
        in_specs=in_specs,
        out_specs=[pl.BlockSpec((rows, hgw), lambda i: (i, 0)), sspec],
        out_shape=[jax.ShapeDtypeStruct((bs * ts, hgw), BF16), jax.ShapeDtypeStruct(s0.shape, F32)],
        input_output_aliases=aliases,
        compiler_params=_cparams("parallel"),
        name="hgrn_sample",
    )(*operands)


def _lru_gates(xconv, wa_ref, wx_ref, ba, bx, lam):
    l = xconv.shape[1]
    blk = l // LRU_BLOCKS
    ga, gx = [], []
    for n in range(LRU_BLOCKS):
        xb = xconv[:, n * blk:(n + 1) * blk].astype(BF16)
        ga.append(_dot(xb, wa_ref[n]))
        gx.append(_dot(xb, wx_ref[n]))
    ga = jnp.concatenate(ga, axis=1) + ba
    gx = jnp.concatenate(gx, axis=1) + bx
    log_a = -LRU_C * jax.nn.sigmoid(ga) * _softplus(-lam)
    a = jnp.exp(log_a)
    mult = jnp.sqrt(1.0 - a * a)
    return a, mult * jax.nn.sigmoid(gx) * xconv


def _gelu_tanh(x):
    c = 0.7978845608028654
    return 0.5 * x * (1.0 + jnp.tanh(c * (x + 0.044715 * (x * x * x))))


def _odd_prompt_body(gate_ref, xb_ref, conv0_ref, h0_ref, cw_ref, cb_ref, wa_ref, wx_ref, ba_ref, bx_ref, lam_ref,
                     y_ref, convo_ref, ho_ref, ext_ref, hc_ref, a_s, b_s, h_s):
    rows = xb_ref.shape[0]
    hist = SUBLANES

    @pl.when(pl.program_id(1) == 0)
    def _():
        ext_ref[0:hist, :] = conv0_ref[...]
        hc_ref[...] = h0_ref[...]

    ext_ref[hist:hist + rows, :] = xb_ref[...]
    cw = cw_ref[...]
    xconv = cw[0:1, :] * ext_ref[pl.ds(hist - (CONV_W - 1), rows), :]
    for j in range(1, CONV_W):
        xconv = xconv + cw[j:j + 1, :] * ext_ref[pl.ds(hist - (CONV_W - 1) + j, rows), :]
    xconv = cb_ref[...] + xconv
    a, b = _lru_gates(xconv, wa_ref, wx_ref, ba_ref[...], bx_ref[...], lam_ref[...])
    width = a.shape[1]
    a = a.reshape(rows // SUBLANES, SUBLANES, width)
    b = b.reshape(rows // SUBLANES, SUBLANES, width)
    r8 = lax.broadcasted_iota(jnp.int32, (1, SUBLANES, 1), 1)
    for s in (1, 2, 4):
        m = r8 >= s
        a_sh = pltpu.roll(a, s, 1)
        b_sh = pltpu.roll(b, s, 1)
        b = jnp.where(m, a * b_sh + b, b)
        a = jnp.where(m, a * a_sh, a)
    a_s[...] = a.reshape(rows, width)
    b_s[...] = b.reshape(rows, width)

    def tile(j, hc):
        r0 = pl.multiple_of(j * SUBLANES, SUBLANES)
        h = a_s[pl.ds(r0, SUBLANES), :] * hc + b_s[pl.ds(r0, SUBLANES), :]
        h_s[pl.ds(r0, SUBLANES), :] = h
        return h[SUBLANES - 1:SUBLANES, :]

    hc = lax.fori_loop(0, rows // SUBLANES, tile, hc_ref[...])
    hc_ref[...] = hc
    ho_ref[...] = hc
    y_ref[...] = (h_s[...] * _gelu_tanh(gate_ref[...])).astype(BF16)
    tail = ext_ref[rows:rows + hist, :]
    ext_ref[0:hist, :] = tail
    convo_ref[...] = tail


def _odd_prompt(p_od, conv0, h0, layer, cw, cb, wa, wx, ba, bx, lam, *, bp, tp):
    l = p_od.shape[1] // 2
    rows = _pick_block(tp, 256, SUBLANES)
    nt = tp // rows
    lsel = lambda a: pl.BlockSpec((None,) + a.shape[1:], lambda b, t: (layer,) + (0,) * (a.ndim - 1))
    return pl.pallas_call(
        _odd_prompt_body,
        grid=(bp, nt),
        in_specs=[
            pl.BlockSpec((rows, l), lambda b, t: (b * nt + t, 0)),
            pl.BlockSpec((rows, l), lambda b, t: (b * nt + t, 1)),
            pl.BlockSpec((None, SUBLANES, l), lambda b, t: (b, 0, 0)),
            pl.BlockSpec((None, 1, l), lambda b, t: (b, 0, 0)),
            lsel(cw), lsel(cb), lsel(wa), lsel(wx), lsel(ba), lsel(bx), lsel(lam),
        ],
        out_specs=[
            pl.BlockSpec((rows, l), lambda b, t: (b * nt + t, 0)),
            pl.BlockSpec((None, SUBLANES, l), lambda b, t: (b, 0, 0)),
            pl.BlockSpec((None, 1, l), lambda b, t: (b, 0, 0)),
        ],
        out_shape=[jax.ShapeDtypeStruct((p_od.shape[0], l), BF16), jax.ShapeDtypeStruct((bp, SUBLANES, l), F32),
                   jax.ShapeDtypeStruct((bp, 1, l), F32)],
        scratch_shapes=[pltpu.VMEM((rows + SUBLANES, l), F32), pltpu.VMEM((1, l), F32),
                        pltpu.VMEM((rows, l), F32), pltpu.VMEM((rows, l), F32), pltpu.VMEM((rows, l), F32)],
        compiler_params=_cparams("parallel", "arbitrary"),
        name="odd_prompt",
    )(p_od, p_od, conv0, h0, cw, cb, wa, wx, ba, bx, lam)


def _odd_sample_body(gate_ref, xb_ref, conv0_ref, h0_ref, cw_ref, cb_ref, wa_ref, wx_ref, ba_ref, bx_ref, lam_ref,
                     yall_ref, y_ref, convo_ref, ho_ref):
    @pl.when(pl.program_id(0) == 0)
    def _():
        convo_ref[...] = conv0_ref[...]
        ho_ref[...] = h0_ref[...]

    xb = xb_ref[...]
    cw = cw_ref[...]
    xconv = cw[CONV_W - 1:CONV_W, :] * xb
    for j in range(CONV_W - 1):
        xconv = xconv + cw[j:j + 1, :] * convo_ref[j]
    xconv = cb_ref[...] + xconv
    a, b = _lru_gates(xconv, wa_ref, wx_ref, ba_ref[...], bx_ref[...], lam_ref[...])
    h = a * ho_ref[...] + b
    ho_ref[...] = h
    y_ref[...] = (h * _gelu_tanh(gate_ref[...])).astype(BF16)
    for j in range(CONV_W - 2):
        convo_ref[j] = convo_ref[j + 1]
    convo_ref[CONV_W - 2] = xb


def _odd_sample(p_od, conv0_t, h0, layer, cw, cb, wa, wx, ba, bx, lam, y_all, *, mp, bs, ts):
    l = p_od.shape[1] // 2
    assert mp % bs == 0
    off = mp // bs
    lsel = lambda a: pl.BlockSpec((None,) + a.shape[1:], lambda t: (layer,) + (0,) * (a.ndim - 1))
    full = lambda a: pl.BlockSpec(a.shape, lambda t: (0,) * a.ndim)
    return pl.pallas_call(
        _odd_sample_body,
        grid=(ts,),
        in_specs=[
            pl.BlockSpec((bs, l), lambda t: (off + t, 0)),
            pl.BlockSpec((bs, l), lambda t: (off + t, 1)),
            full(conv0_t), full(h0),
            lsel(cw), lsel(cb), lsel(wa), lsel(wx), lsel(ba), lsel(bx), lsel(lam),
            pl.BlockSpec(memory_space=pl.ANY),
        ],
        out_specs=[pl.BlockSpec((bs, l), lambda t: (off + t, 0)), full(conv0_t), full(h0)],
        out_shape=[jax.ShapeDtypeStruct(y_all.shape, BF16), jax.ShapeDtypeStruct(conv0_t.shape, F32),
                   jax.ShapeDtypeStruct(h0.shape, F32)],
        input_output_aliases={11: 0},
        compiler_params=_cparams("arbitrary"),
        name="odd_sample",
    )(p_od, p_od, conv0_t, h0, cw, cb, wa, wx, ba, bx, lam, y_all)


def _pad_last(a, n):
    return jnp.pad(a, [(0, 0)] * (a.ndim - 1) + [(0, n - a.shape[-1])])


def _round_up(n, m):
    return (n + m - 1) // m * m


class _RwLayout:
    def __init__(self, rw, dw, da, dg):
        self.rw, self.dw, self.da, self.dg = rw, dw, da, dg
        self.dwp, self.dap, self.dgp = (_round_up(d, LANES) for d in (dw, da, dg))
        self.width = 3 * rw + dw + da + dg
        self.padded = 3 * rw + self.dwp + self.dap + self.dgp

    def pad(self, a):
        o = 3 * self.rw
        parts = [a[..., :o], _pad_last(a[..., o:o + self.dw], self.dwp),
                 _pad_last(a[..., o + self.dw:o + self.dw + self.da], self.dap),
                 _pad_last(a[..., o + self.dw + self.da:self.width], self.dgp)]
        return jnp.concatenate(parts, axis=-1)

    def unpad(self, a):
        o = 3 * self.rw
        parts = [a[..., :o], a[..., o:o + self.dw], a[..., o + self.dwp:o + self.dwp + self.da],
                 a[..., o + self.dwp + self.dap:o + self.dwp + self.dap + self.dg]]
        return jnp.concatenate(parts, axis=-1)


def kernel(x_prompt, x_sample, state_rwkv_shift, state_rwkv, state_hgrn, state_conv, state_lru, ffn1_norm, ffn1_w_gu, ffn1_w_down, mix_norm, ffn2_norm, ffn2_w_gu, ffn2_w_down, ev_w_in, rw_mu, rw_w0, rw_w2, rw_a0, rw_a2, rw_g2, rw_k_k, rw_k_a, rw_r_k, rw_ln_w, rw_ln_b, hg_lb, hg_norm, ev_w_out, od_w_in, conv_w, conv_b, lru_wa, lru_ba, lru_wx, lru_bx, lru_lambda, od_w_out, final_norm):
    bp, tp, d = x_prompt.shape
    bs, ts, _ = x_sample.shape
    depth = ffn1_norm.shape[0]
    n_even = ev_w_in.shape[0]
    mp, ms = bp * tp, bs * ts
    rw = rw_w0.shape[1]
    nh_rw = rw // RW_HEAD
    hgw = hg_norm.shape[1]
    nh_hg = hgw // HG_DIM
    lay = _RwLayout(rw, rw_w2.shape[1], rw_a2.shape[1], rw_g2.shape[1])
    assert bp * nh_rw * 2 == LANES and (bs * nh_rw) % LANES == 0
    dt = x_prompt.dtype

    ffn_src = {1: (ffn1_norm, ffn1_w_gu, ffn1_w_down), 2: (ffn2_norm, ffn2_w_gu, ffn2_w_down)}
    ffn_w = (ffn1_w_gu[0].astype(BF16), ffn1_w_down[0].astype(BF16))

    def ffn_step(x, l, which, ffn_w):
        norm_g = ffn_src[which][0]
        nxt = (l, 2) if which == 1 else (l + 1, 1)
        if nxt[0] >= depth:
            return _ffn(x, norm_g, l, *ffn_w)[0], None
        x, c_gu, c_dn = _ffn(x, norm_g, l, *ffn_w, next_src=(ffn_src[nxt[1]][1], ffn_src[nxt[1]][2], nxt[0]))
        return x, (c_gu, c_dn)

    ev_out = ev_w_out.astype(BF16)
    od_in, od_out = od_w_in.astype(BF16), od_w_out.astype(BF16)

    wa_bf, wx_bf = lru_wa.astype(BF16), lru_wx.astype(BF16)

    x = jnp.concatenate([x_prompt.reshape(mp, d), x_sample.transpose(1, 0, 2).reshape(ms, d)], axis=0)

    lbs = jnp.cumsum(jax.nn.softmax(hg_lb.astype(F32), axis=0), axis=0)
    lb_all = lbs - lbs[0]

    def k_lanes_p(a):
        return a.reshape(a.shape[0], bp * nh_rw, RW_HEAD).swapaxes(1, 2)

    def v_lanes_p(a):
        n = a.shape[0]
        return a.reshape(n, bp, nh_rw, 2, RW_HEAD // 2).transpose(0, 4, 3, 1, 2).reshape(n, RW_HEAD // 2, LANES)

    def v_lanes_p_inv(a):
        return a.reshape(tp, RW_HEAD // 2, 2, bp, nh_rw).transpose(3, 0, 4, 2, 1).reshape(mp, rw)

    def k_lanes_s(a):
        a = a.reshape(ts // bp, bs, bp, nh_rw, RW_HEAD).transpose(0, 2, 4, 1, 3)
        return a.reshape(ts, RW_HEAD, bs * nh_rw)

    def k_lanes_s_inv(a):
        return a.reshape(ts, RW_HEAD, bs, nh_rw).transpose(0, 2, 3, 1).reshape(ms, rw)

    def kparam(p):
        return jnp.tile(p.reshape(nh_rw, RW_HEAD).T, (1, LANES // nh_rw))

    def vparam_p(p):
        a = p.reshape(nh_rw, 2, RW_HEAD // 2).transpose(2, 1, 0)[:, :, None, :]
        return jnp.broadcast_to(a, (RW_HEAD // 2, 2, bp, nh_rw)).reshape(RW_HEAD // 2, LANES)

    shifts_p, shifts_s, rws_p, hgs_p, convs_p, convs_s, lrus_p, lrus_s = ([] for _ in range(8))
    rw_rows = state_rwkv.astype(F32).reshape(bs, n_even, nh_rw, RW_HEAD * RW_HEAD)
    hg_new = None

    for l in range(depth):
        x, ffn_w = ffn_step(x, l, 1, ffn_w)
        if l % 2 == 0:
            e = l // 2
            w_in = ev_w_in[e]
            w_rw = lay.pad(w_in[:, :lay.width]).astype(BF16)
            w_hg = w_in[:, lay.width:].astype(BF16)
            p_rw = _norm_proj(x, mix_norm, l, w_rw, _pick_block(lay.padded, 1792, LANES))
            p_hg = _norm_proj(x, mix_norm, l, w_hg, _pick_block(4 * hgw, 1024, LANES))
            shift_p = jnp.zeros((bp, lay.padded), F32)
            shift_s = lay.pad(state_rwkv_shift[:, e].astype(F32))
            rkvwa, g_ = _rwkv_prep(
                p_rw, shift_p, shift_s, lay.pad(rw_mu[e])[None], rw_w0[e][None],
                _pad_last(rw_w2[e].T, lay.dwp).T.astype(BF16), rw_a0[e][None],
                _pad_last(rw_a2[e].T, lay.dap).T.astype(BF16),
                _pad_last(rw_g2[e].T, lay.dgp).T.astype(BF16), rw=rw, mp=mp, tp=tp, bs=bs)
            shifts_p.append(lay.unpad(jnp.concatenate([p_rw[b * tp + tp - 1:(b + 1) * tp] for b in range(bp)])))
            shifts_s.append(lay.unpad(p_rw[mp + (ts - 1) * bs:]))
            kk_p, ka_p = kparam(rw_k_k[e]), kparam(rw_k_a[e])
            rk_p = kparam(rw_r_k[e].reshape(rw))
            lb = lb_all[e][None]
            loglb, log1m = jnp.log(lb), jnp.log1p(-lb)
            nw = hg_norm[e][None]
            chunk = _pick_block(tp, HG_CHUNK, 1)
            oh_p, hs_p = _hgrn_prompt(p_hg, jnp.zeros((bp, nh_hg, HG_DIM, HG_DIM), F32), loglb, log1m, nw,
                                      bp=bp, tp=tp, chunk=chunk)
            r_p, k_p, v_p, w_p, a_p = rkvwa
            o_p, s_p = _rwkv_scan(
                k_lanes_p(r_p), k_lanes_p(w_p), k_lanes_p(k_p), k_lanes_p(a_p), v_lanes_p(v_p),
                kk_p, ka_p, rk_p, vparam_p(rw_ln_w[e]), vparam_p(rw_ln_b[e]),
                jnp.zeros((RW_HEAD // 2, RW_HEAD, LANES), F32), t=tp, tc=_pick_block(tp, 64, 1), vsplit=True,
                after=oh_p)
            rws_p.append(s_p.reshape(RW_HEAD // 2, RW_HEAD, 2, bp, nh_rw).transpose(3, 4, 2, 0, 1)
                         .reshape(bp, nh_rw, RW_HEAD, RW_HEAD))
            r_s, k_s, v_s, w_s, a_s = (a[tp:] for a in rkvwa)
            o_s, rw_rows = _rwkv_scan(
                k_lanes_s(r_s), k_lanes_s(w_s), k_lanes_s(k_s), k_lanes_s(a_s), k_lanes_s(v_s),
                kk_p, ka_p, rk_p, kparam(rw_ln_w[e]), kparam(rw_ln_b[e]),
                rw_rows, t=ts, tc=ts, vsplit=False, layer=e, after=o_p)
            o_rw = jnp.concatenate([v_lanes_p_inv(o_p), k_lanes_s_inv(o_s)], axis=0)
            p_hg_bm = p_hg[mp:].reshape(ts, bs, 4 * hgw).transpose(1, 0, 2).reshape(ms, 4 * hgw)
            oh_s, hg_new = _hgrn_sample(p_hg_bm, state_hgrn.astype(F32), e, loglb, log1m, nw, bs=bs, ts=ts,
                                        s_prev=hg_new, after=o_p)
            oh_s = oh_s.reshape(bs, ts, hgw).transpose(1, 0, 2).reshape(ms, hgw)
            hgs_p.append(hs_p)
            o_hg = oh_p.at[mp:].set(oh_s)
            x = _proj_res_even(x, o_rw, g_, o_hg, ev_out, e)
        else:
            o_i = l // 2
            p_od = _norm_proj(x, mix_norm, l, od_in, _pick_block(od_in.shape[-1], 1024, LANES), w_layer=o_i)
            lsz = p_od.shape[1] // 2
            vec = lambda a: a[:, None, :]
            args = (conv_w, vec(conv_b), wa_bf, wx_bf, vec(lru_ba), vec(lru_bx), vec(lru_lambda))
            conv0_p = jnp.zeros((bp, SUBLANES, lsz), F32)
            y_p, c_p, h_p = _odd_prompt(p_od, conv0_p, jnp.zeros((bp, 1, lsz), F32), o_i, *args, bp=bp, tp=tp)
            conv0_s = state_conv[:, o_i].astype(F32).transpose(1, 0, 2)
            y_all, c_s, h_s = _odd_sample(p_od, conv0_s, state_lru[:, o_i].astype(F32), o_i, *args, y_p,
                                          mp=mp, bs=bs, ts=ts)
            convs_p.append(c_p[:, SUBLANES - (CONV_W - 1):])
            convs_s.append(c_s.transpose(1, 0, 2))
            lrus_p.append(h_p[:, 0])
            lrus_s.append(h_s)
            x = _proj_res(x, y_all, od_out, o_i)
        x, ffn_w = ffn_step(x, l, 2, ffn_w)

    y_p, y_s = _final_norm(x, final_norm, mp)
    y_prompt = y_p.reshape(bp, tp, d)
    y_sample = y_s.reshape(ts, bs, d).transpose(1, 0, 2)
    st = lambda xs: jnp.stack(xs, axis=1).astype(dt)
    sample_rwkv = rw_rows.reshape(state_rwkv.shape).astype(dt)
    return (y_prompt, y_sample, st(shifts_p), st(rws_p), st(hgs_p), st(convs_p), st(lrus_p),
            st(shifts_s), sample_rwkv, hg_new.astype(dt), st(convs_s), st(lrus_s))
```

```python
import functools
import math

import jax
import jax.numpy as jnp
from jax import lax
from jax.experimental import pallas as pl
from jax.experimental.pallas import tpu as pltpu

F32 = jnp.float32
BF16 = jnp.bfloat16

NORM_EPS = 1e-6
RW_GN_EPS = 64e-5
HG_NORM_EPS = 1e-5
LRU_C = 8.0
RW_HEAD = 64
HG_DIM = 128
LRU_BLOCKS = 8
CONV_W = 4
LANES = 128
SUBLANES = 8
VMEM_LIMIT = 62 * 1024 * 1024
MAX_BM = 1088
HG_CHUNK = 64


def _cparams(*sem):
    return pltpu.CompilerParams(dimension_semantics=sem, vmem_limit_bytes=VMEM_LIMIT)


def _pick_block(n, cap, mult):
    best = None
    for d in range(mult, min(n, cap) + 1, mult):
        if n % d == 0:
            best = d
    assert best is not None, (n, cap, mult)
    return best


def _gcd(a, b):
    return math.gcd(a, b)


def _dot(a, b):
    return jnp.dot(a, b, preferred_element_type=F32)


def _dot_nt(a, b):
    return lax.dot_general(a, b, (((1,), (1,)), ((), ())), preferred_element_type=F32)


def _dot_tn(a, b):
    return lax.dot_general(a, b, (((0,), (0,)), ((), ())), preferred_element_type=F32)


def _softplus(x):
    return jnp.maximum(x, 0.0) + jnp.log1p(jnp.exp(-jnp.abs(x)))


def _neg_expm1(y):
    return 1.0 - jnp.exp(y)


def _rms_rows(x, g):
    ms = jnp.mean(x * x, axis=-1, keepdims=True)
    return x * lax.rsqrt(ms + NORM_EPS) * g


def _ffn_body(x_ref, g_ref, wg_ref, wu_ref, wd_ref, *rest, cast_next):
    if cast_next:
        sgu_ref, sdn_ref, o_ref, cgu_ref, cdn_ref, xn_ref = rest
        cgu_ref[...] = sgu_ref[...].astype(BF16)
        cdn_ref[...] = sdn_ref[...].astype(BF16)
    else:
        o_ref, xn_ref = rest

    @pl.when(pl.program_id(1) == 0)
    def _():
        x = x_ref[...]
        xn_ref[...] = _rms_rows(x, g_ref[...]).astype(BF16)
        o_ref[...] = x

    xn = xn_ref[...]
    gate = _dot(xn, wg_ref[...])
    up = _dot(xn, wu_ref[...])
    act = (0.5 * (gate * jax.nn.sigmoid(gate)) * up).astype(BF16)
    o_ref[...] += _dot(act, wd_ref[...])


def _ffn(x, norm_g, layer, w_gu, w_down, next_src=None):
    m, d = x.shape
    f = w_down.shape[0]
    bm = _pick_block(m, MAX_BM, 16)
    bf = _pick_block(f, 512, LANES)
    ni, nf = m // bm, f // bf
    in_specs = [
        pl.BlockSpec((bm, d), lambda i, j: (i, 0)),
        pl.BlockSpec((None, 1, d), lambda i, j: (layer, 0, 0)),
        pl.BlockSpec((d, bf), lambda i, j: (0, j)),
        pl.BlockSpec((d, bf), lambda i, j: (0, j + nf)),
        pl.BlockSpec((bf, d), lambda i, j: (j, 0)),
    ]
    operands = [x, norm_g[:, None, :], w_gu, w_gu, w_down]
    out_specs = [pl.BlockSpec((bm, d), lambda i, j: (i, 0))]
    out_shape = [jax.ShapeDtypeStruct((m, d), F32)]
    if next_src is not None:
        s_gu, s_dn, nl = next_src
        assert d % ni == 0 and (2 * f) % nf == 0 and (d // ni) % 16 == 0 and (d // ni) % LANES == 0
        gu_blk, dn_blk = (d // ni, 2 * f // nf), (f // nf, d // ni)
        in_specs += [pl.BlockSpec((None,) + gu_blk, lambda i, j: (nl, i, j)),
                     pl.BlockSpec((None,) + dn_blk, lambda i, j: (nl, j, i))]
        operands += [s_gu, s_dn]
        out_specs += [pl.BlockSpec(gu_blk, lambda i, j: (i, j)), pl.BlockSpec(dn_blk, lambda i, j: (j, i))]
        out_shape += [jax.ShapeDtypeStruct((d, 2 * f), BF16), jax.ShapeDtypeStruct((f, d), BF16)]
    return pl.pallas_call(
        functools.partial(_ffn_body, cast_next=next_src is not None),
        grid=(ni, nf),
        in_specs=in_specs,
        out_specs=out_specs,
        out_shape=out_shape,
        scratch_shapes=[pltpu.VMEM((bm, d), BF16)],
        compiler_params=_cparams("parallel", "arbitrary"),
        name="ffn",
    )(*operands)


def _norm_proj_body(x_ref, g_ref, w_ref, o_ref, xn_ref):
    @pl.when(pl.program_id(1) == 0)
    def _():
        xn_ref[...] = _rms_rows(x_ref[...], g_ref[...]).astype(BF16)

    o_ref[...] = _dot(xn_ref[...], w_ref[...])


def _w_spec(w, blk, idx, w_layer):
    if w.ndim == 2:
        return pl.BlockSpec(blk, idx)
    return pl.BlockSpec((None,) + blk, lambda i, j: (w_layer,) + idx(i, j))


def _norm_proj(x, norm_g, layer, w, bn, w_layer=None):
    m, d = x.shape
    n = w.shape[-1]
    bm = _pick_block(m, MAX_BM, 16)
    return pl.pallas_call(
        _norm_proj_body,
        grid=(m // bm, n // bn),
        in_specs=[
            pl.BlockSpec((bm, d), lambda i, j: (i, 0)),
            pl.BlockSpec((None, 1, d), lambda i, j: (layer, 0, 0)),
            _w_spec(w, (d, bn), lambda i, j: (0, j), w_layer),
        ],
        out_specs=pl.BlockSpec((bm, bn), lambda i, j: (i, j)),
        out_shape=jax.ShapeDtypeStruct((m, n), F32),
        scratch_shapes=[pltpu.VMEM((bm, d), BF16)],
        compiler_params=_cparams("parallel", "arbitrary"),
        name="norm_proj",
    )(x, norm_g[:, None, :], w)


def _proj_res_body(x_ref, y_ref, w_ref, o_ref):
    o_ref[...] = x_ref[...] + _dot(y_ref[...], w_ref[...])


def _proj_res(x, y, w, w_layer):
    m, d = x.shape
    k = y.shape[1]
    bm = _pick_block(m, MAX_BM, 16)
    bn = _pick_block(d, 1024, LANES)
    return pl.pallas_call(
        _proj_res_body,
        grid=(m // bm, d // bn),
        in_specs=[
            pl.BlockSpec((bm, bn), lambda i, j: (i, j)),
            pl.BlockSpec((bm, k), lambda i, j: (i, 0)),
            _w_spec(w, (k, bn), lambda i, j: (0, j), w_layer),
        ],
        out_specs=pl.BlockSpec((bm, bn), lambda i, j: (i, j)),
        out_shape=jax.ShapeDtypeStruct((m, d), F32),
        compiler_params=_cparams("parallel", "arbitrary"),
        name="proj_res",
    )(x, y, w)


def _proj_res_even_body(x_ref, orw_ref, g_ref, ohg_ref, w1_ref, w2_ref, o_ref):
    y1 = (orw_ref[...] * g_ref[...]).astype(BF16)
    o_ref[...] = x_ref[...] + _dot(y1, w1_ref[...]) + _dot(ohg_ref[...], w2_ref[...])


def _proj_res_even(x, o_rw, g, o_hg, w, w_layer):
    m, d = x.shape
    rw = o_rw.shape[1]
    hg = o_hg.shape[1]
    assert rw == hg
    bm = _pick_block(m, MAX_BM, 16)
    bn = _pick_block(d, 1024, LANES)
    return pl.pallas_call(
        _proj_res_even_body,
        grid=(m // bm, d // bn),
        in_specs=[
            pl.BlockSpec((bm, bn), lambda i, j: (i, j)),
            pl.BlockSpec((bm, rw), lambda i, j: (i, 0)),
            pl.BlockSpec((bm, rw), lambda i, j: (i, 0)),
            pl.BlockSpec((bm, hg), lambda i, j: (i, 0)),
            _w_spec(w, (rw, bn), lambda i, j: (0, j), w_layer),
            _w_spec(w, (hg, bn), lambda i, j: (1, j), w_layer),
        ],
        out_specs=pl.BlockSpec((bm, bn), lambda i, j: (i, j)),
        out_shape=jax.ShapeDtypeStruct((m, d), F32),
        compiler_params=_cparams("parallel", "arbitrary"),
        name="proj_res_even",
    )(x, o_rw, g, o_hg, w, w)


def _final_norm_body(x_ref, g_ref, op_ref, os_ref, *, n_prompt_blk):
    y = _rms_rows(x_ref[...], g_ref[...])
    i = pl.program_id(0)

    @pl.when(i < n_prompt_blk)
    def _():
        op_ref[...] = y

    @pl.when(i >= n_prompt_blk)
    def _():
        os_ref[...] = y


def _final_norm(x, g, mp):
    m, d = x.shape
    ms = m - mp
    bm = _pick_block(_gcd(mp, ms), MAX_BM, SUBLANES)
    npb = mp // bm
    return pl.pallas_call(
        functools.partial(_final_norm_body, n_prompt_blk=npb),
        grid=(m // bm,),
        in_specs=[pl.BlockSpec((bm, d), lambda i: (i, 0)), pl.BlockSpec((1, d), lambda i: (0, 0))],
        out_specs=[pl.BlockSpec((bm, d), lambda i: (jnp.minimum(i, npb - 1), 0)),
                   pl.BlockSpec((bm, d), lambda i: (jnp.maximum(i - npb, 0), 0))],
        out_shape=[jax.ShapeDtypeStruct((mp, d), F32), jax.ShapeDtypeStruct((ms, d), F32)],
        compiler_params=_cparams("arbitrary"),
        name="final_norm",
    )(x, g[None, :])


def _rwkv_prep_body(cur_ref, prevblk_ref, shiftp_ref, shifts_ref, mu_ref, w0_ref, w2_ref, a0_ref, a2_ref,
                    g2_ref, r_ref, k_ref, v_ref, w_ref, a_ref, g_ref, *, rw, dwp, dap, n_prompt_blk, blk_per_seq):
    i = pl.program_id(0)
    rows = cur_ref.shape[0]
    prw = cur_ref.shape[1]
    seq = jnp.minimum(i // blk_per_seq, shiftp_ref.shape[0] - 1)
    first = (i % blk_per_seq) == 0
    is_sample = i >= n_prompt_blk
    rid = lax.broadcasted_iota(jnp.int32, (rows, 1), 0)
    shift_row = shiftp_ref[pl.ds(seq, 1), :]

    def shifted(c0, c1):
        cs = slice(c0, c1)
        cur = cur_ref[:, cs]
        prevblk = prevblk_ref[:, cs]
        row0 = jnp.where(first, shift_row[:, cs], prevblk[rows - 1:rows, :])
        prev_p = jnp.where(rid == 0, row0, pltpu.roll(cur, 1, 0))
        prev_s = jnp.where(i == n_prompt_blk, shifts_ref[:, cs], prevblk)
        prev = jnp.where(is_sample, prev_s, prev_p)
        return cur + (prev - cur) * mu_ref[:, cs]

    r_ref[...] = shifted(0, rw).T
    k_ref[...] = shifted(rw, 2 * rw).T
    v_ref[...] = shifted(2 * rw, 3 * rw).T
    o = 3 * rw
    lw = w0_ref[...] + _dot(jnp.tanh(shifted(o, o + dwp)).astype(BF16), w2_ref[...])
    w_log = -_softplus(-lw) - 0.5
    w_ref[...] = jnp.exp(-jnp.exp(w_log)).T
    a_ref[...] = jax.nn.sigmoid(a0_ref[...] + _dot(shifted(o + dwp, o + dwp + dap).astype(BF16), a2_ref[...])).T
    g_ref[...] = _dot(jax.nn.sigmoid(shifted(o + dwp + dap, prw)).astype(BF16), g2_ref[...])


def _rwkv_prep(p_rw, shift_p, shift_s, mu, w0, w2, a0, a2, g2, *, rw, mp, tp, bs):
    m, prw = p_rw.shape
    ms = m - mp
    assert mp % bs == 0 and tp % bs == 0 and ms % bs == 0
    npb = mp // bs
    dwp, dap = w2.shape[0], a2.shape[0]
    full = lambda a: pl.BlockSpec(a.shape, lambda i: (0,) * a.ndim)
    bps = tp // bs
    bp = mp // tp

    ts = ms // bs
    assert ts % bp == 0

    def out_idx(i):
        j = i - npb
        return jnp.where(i < npb, i // bps, j % bp), jnp.where(i < npb, i % bps, bps + j // bp)

    spec_o = pl.BlockSpec((rw, bs), out_idx)
    body = functools.partial(_rwkv_prep_body, rw=rw, dwp=dwp, dap=dap, n_prompt_blk=npb, blk_per_seq=tp // bs)
    outs = pl.pallas_call(
        body,
        grid=(m // bs,),
        in_specs=[
            pl.BlockSpec((bs, prw), lambda i: (i, 0)),
            pl.BlockSpec((bs, prw), lambda i: (jnp.maximum(i - 1, 0), 0)),
            full(shift_p), full(shift_s), full(mu), full(w0), full(w2), full(a0), full(a2), full(g2),
        ],
        out_specs=[spec_o] * 5 + [pl.BlockSpec((bs, rw), lambda i: (i, 0))],
        out_shape=[jax.ShapeDtypeStruct((bp * rw, tp + ms // bp), F32)] * 5 + [jax.ShapeDtypeStruct((m, rw), F32)],
        compiler_params=_cparams("arbitrary"),
        name="rwkv_prep",
    )(p_rw, p_rw, shift_p, shift_s, mu, w0, w2, a0, a2, g2)
    return outs[:5], outs[5]


def _rwkv_scan_body(r_ref, w_ref, k_ref, a_ref, v_ref, kk_p_ref, ka_p_ref, rk_p_ref, lnw_ref, lnb_ref, s0_ref,
                    o_ref, sout_ref, s_ref, r_s, w_s, nkk_s, b_s, km_s, *, vsplit, state_rows):
    tc_len, nv, lanes = v_ref.shape
    kc = s_ref.shape[1]
    ti = pl.program_id(1)

    @pl.when(ti == 0)
    def _():
        if state_rows:
            s_ref[...] = s0_ref[...].reshape(lanes, nv * kc).T.reshape(nv, kc, lanes)
        else:
            s_ref[...] = s0_ref[...]

    def widen(x):
        return jnp.concatenate([x, x], axis=-1) if vsplit else x

    kvec = widen(k_ref[...])
    a_ = widen(a_ref[...])
    r_s[...] = widen(r_ref[...])
    w_s[...] = widen(w_ref[...])
    kk = kvec * kk_p_ref[...][None]
    nrm = jnp.sqrt(jnp.sum(kk * kk, axis=1, keepdims=True))
    kk = kk / jnp.maximum(nrm, 1e-12)
    nkk_s[...] = -kk
    b_s[...] = kk * a_
    km_s[...] = kvec * (1.0 + (a_ - 1.0) * ka_p_ref[...][None])

    def step(t, carry):
        nkk = nkk_s[t]
        b = b_s[t]
        km = km_s[t]
        w = w_s[t]
        r = r_s[t]
        vt = v_ref[t]
        for vi in range(nv):
            sv = s_ref[vi]
            sa = jnp.sum(sv * nkk, axis=0, keepdims=True)
            sv = sv * w + sa * b + vt[vi:vi + 1, :] * km
            s_ref[vi] = sv
            o_ref[t, pl.ds(vi, 1), :] = jnp.sum(sv * r, axis=0, keepdims=True)
        return carry

    lax.fori_loop(0, tc_len, step, 0)

    o = o_ref[...]
    n_val = nv * (2 if vsplit else 1)

    def head_sum(x):
        if vsplit:
            x2 = x.reshape(tc_len * nv, lanes)
            x = (x2 + pltpu.roll(x2, lanes // 2, 1)).reshape(tc_len, nv, lanes)
        return jnp.sum(x, axis=1, keepdims=True)

    mu = head_sum(o) / n_val
    d = o - mu
    var = head_sum(d * d) / n_val
    o = d * lax.rsqrt(var + RW_GN_EPS) * lnw_ref[...][None] + lnb_ref[...][None]
    bonus = jnp.sum(r_s[...] * km_s[...] * rk_p_ref[...][None], axis=1, keepdims=True)
    o_ref[...] = o + bonus * v_ref[...]

    @pl.when(ti == pl.num_programs(1) - 1)
    def _():
        if state_rows:
            sout_ref[...] = s_ref[...].reshape(nv * kc, lanes).T.reshape(sout_ref.shape)
        else:
            sout_ref[...] = s_ref[...]


def _rwkv_scan(r, w, k, a, v, kk_p, ka_p, rk_p, lnw, lnb, s0, *, t, tc, vsplit, layer=None, after=None):
    _, kc, kl = r.shape
    nv = v.shape[1]
    ln = v.shape[2]
    g = ln // LANES
    kspec = pl.BlockSpec((tc, kc, kl // g), lambda gi, ti: (ti, 0, gi))
    vspec = pl.BlockSpec((tc, nv, LANES), lambda gi, ti: (ti, 0, gi))
    pk = pl.BlockSpec((kc, LANES), lambda gi, ti: (0, 0))
    pv = pl.BlockSpec((nv, LANES), lambda gi, ti: (0, 0))
    state_rows = not vsplit
    operands = [r, w, k, a, v, kk_p, ka_p, rk_p, lnw, lnb, s0]
    in_specs = [kspec, kspec, kspec, kspec, vspec, pk, pk, pk, pv, pv]
    aliases = {}
    if state_rows:
        nb, _, nh, sz = s0.shape
        assert sz == nv * kc and nb * nh == ln
        sspec = pl.BlockSpec((LANES // nh, None, nh, sz), lambda gi, ti: (gi, layer, 0, 0))
        in_specs.append(sspec)
        aliases = {10: 1}
        s_shape = s0.shape
    else:
        sspec = pl.BlockSpec((nv, kc, LANES), lambda gi, ti: (0, 0, gi))
        in_specs.append(sspec)
        s_shape = (nv, kc, ln)
    if after is not None:
        operands.append(after)
        in_specs.append(pl.BlockSpec(memory_space=pl.ANY))

    def body(*refs):
        n_in = len(operands)
        ins = refs[:11]
        _rwkv_scan_body(*ins, *refs[n_in:], vsplit=vsplit, state_rows=state_rows)

    return pl.pallas_call(
        body,
        grid=(g, t // tc),
        in_specs=in_specs,
        out_specs=[vspec, sspec],
        out_shape=[jax.ShapeDtypeStruct((t, nv, ln), F32), jax.ShapeDtypeStruct(s_shape, F32)],
        scratch_shapes=[pltpu.VMEM((nv, kc, LANES), F32)] + [pltpu.VMEM((tc, kc, LANES), F32)] * 5,
        input_output_aliases=aliases,
        compiler_params=_cparams("parallel", "arbitrary"),
        name="rwkv_scan",
    )(*operands)


def _ref_rows(gc, j):
    c, w = gc.shape
    b = 1 << j
    nt = c // SUBLANES
    gv = gc.reshape(nt, SUBLANES, w)
    if 2 * b <= SUBLANES:
        sub = lax.broadcasted_iota(jnp.int32, (1, SUBLANES, 1), 1)
        out = None
        for g0 in range(0, SUBLANES, 2 * b):
            piece = jnp.broadcast_to(gv[:, g0 + b - 1:g0 + b, :], gv.shape)
            out = piece if out is None else jnp.where(sub >= g0, piece, out)
        return out.reshape(c, w)
    tiles_per_group = 2 * b // SUBLANES
    pieces = []
    for ti in range(nt):
        src = (ti // tiles_per_group) * tiles_per_group + b // SUBLANES - 1
        pieces.append(jnp.broadcast_to(gv[src, SUBLANES - 1:SUBLANES, :], (SUBLANES, w)))
    return jnp.concatenate(pieces, axis=0)


def _hgrn_chunk(q, f, val, og, get_st, set_st, loglb, log1mlb, nrm_w, seq_len):
    c, width = q.shape
    nh = width // HG_DIM
    nseq = c // seq_len
    lg = seq_len.bit_length() - 1
    assert (1 << lg) == seq_len and c % SUBLANES == 0
    ls = jnp.minimum(f, 0.0) - jnp.log1p(jnp.exp(-jnp.abs(f)))
    b_ = log1mlb + ls
    g = jnp.maximum(loglb, b_) + jnp.log1p(jnp.exp(-jnp.abs(loglb - b_)))
    kk = _neg_expm1(g)
    qs = q * jax.nn.sigmoid(q)
    row = lax.broadcasted_iota(jnp.int32, (c, 1), 0)
    col = lax.broadcasted_iota(jnp.int32, (1, c), 1)
    tpos = row & (seq_len - 1)
    gc = g
    for j in range(lg):
        s = 1 << j
        gc = gc + jnp.where(tpos >= s, pltpu.roll(gc, s, 0), 0.0)
    qbs, kbs, sames = [qs.astype(BF16)], [kk.astype(BF16)], [row == col]
    for j in range(lg):
        e = jnp.exp(-jnp.abs(gc - _ref_rows(gc, j)))
        is_q = ((row >> j) & 1) == 1
        pk = (jnp.where(is_q, qs, kk) * e).astype(BF16)
        qbs.append(pk)
        kbs.append(pk)
        sames.append(((row >> (j + 1)) == (col >> (j + 1))) & is_q & (((col >> j) & 1) == 0))
    vb = val.astype(BF16)
    qg = (qs * jnp.exp(gc)).astype(BF16)
    k2s, eglast, in_seq = [], [], []
    for s_i in range(nseq):
        glast = gc[s_i * seq_len + seq_len - 1:s_i * seq_len + seq_len, :]
        eglast.append(jnp.exp(glast))
        if nseq == 1:
            in_seq.append(None)
            k2s.append((kk * jnp.exp(glast - gc)).astype(BF16))
        else:
            in_s = (row >> lg) == s_i
            in_seq.append(in_s)
            k2s.append(jnp.where(in_s, kk * jnp.exp(jnp.minimum(glast - gc, 0.0)), 0.0).astype(BF16))
    outs = []
    for h in range(nh):
        cs = slice(h * HG_DIM, (h + 1) * HG_DIM)
        amat = None
        for qb, kb, same in zip(qbs, kbs, sames):
            term = jnp.where(same, _dot_nt(qb[:, cs], kb[:, cs]), 0.0)
            amat = term if amat is None else amat + term
        o = _dot(amat.astype(BF16), vb[:, cs])
        for s_i in range(nseq):
            st = get_st(s_i, h)
            o_int = _dot_nt(qg[:, cs], st.astype(BF16))
            o = o + (o_int if nseq == 1 else jnp.where(in_seq[s_i], o_int, 0.0))
            set_st(s_i, h, st * eglast[s_i][:, cs] + _dot_tn(vb[:, cs], k2s[s_i][:, cs]))
        outs.append(o * lax.rsqrt(jnp.mean(o * o, axis=-1, keepdims=True) + HG_NORM_EPS))
    on = jnp.concatenate(outs, axis=1) * nrm_w
    return on * (og * jax.nn.sigmoid(og))


def _hgrn_prompt_body(q_ref, f_ref, i_ref, og_ref, s0_ref, loglb_ref, log1m_ref, nw_ref, o_ref, sout_ref, st_ref,
                      *, chunk):
    nh = st_ref.shape[0]
    tb = pl.program_id(1)

    @pl.when(tb == 0)
    def _():
        for h in range(nh):
            st_ref[h] = s0_ref[h].T

    def set_st(s_i, h, v):
        st_ref[h] = v

    def body(ci, carry):
        rows = pl.ds(pl.multiple_of(ci * chunk, chunk), chunk)
        y = _hgrn_chunk(q_ref[rows, :], f_ref[rows, :], i_ref[rows, :], og_ref[rows, :],
                        lambda s_i, h: st_ref[h], set_st, loglb_ref[...], log1m_ref[...], nw_ref[...], chunk)
        o_ref[rows, :] = y.astype(BF16)
        return carry

    lax.fori_loop(0, q_ref.shape[0] // chunk, body, 0)

    @pl.when(tb == pl.num_programs(1) - 1)
    def _():
        for h in range(nh):
            sout_ref[h] = st_ref[h].T


def _hgrn_prompt(p_hg, s0, loglb, log1m, nw, *, bp, tp, chunk):
    hgw = p_hg.shape[1] // 4
    nh = hgw // HG_DIM
    tb = _pick_block(tp, 512, chunk)
    nt = tp // tb
    cspec = lambda c: pl.BlockSpec((tb, hgw), lambda b, t: (b * nt + t, c))
    pspec = pl.BlockSpec((1, hgw), lambda b, t: (0, 0))
    sspec = pl.BlockSpec((None, nh, HG_DIM, HG_DIM), lambda b, t: (b, 0, 0, 0))
    return pl.pallas_call(
        functools.partial(_hgrn_prompt_body, chunk=chunk),
        grid=(bp, nt),
        in_specs=[cspec(0), cspec(1), cspec(2), cspec(3), sspec, pspec, pspec, pspec],
        out_specs=[pl.BlockSpec((tb, hgw), lambda b, t: (b * nt + t, 0)), sspec],
        out_shape=[jax.ShapeDtypeStruct((p_hg.shape[0], hgw), BF16), jax.ShapeDtypeStruct(s0.shape, F32)],
        scratch_shapes=[pltpu.VMEM((nh, HG_DIM, HG_DIM), F32)],
        compiler_params=_cparams("parallel", "arbitrary"),
        name="hgrn_prompt",
    )(p_hg, p_hg, p_hg, p_hg, s0, loglb, log1m, nw)


def _hgrn_sample_body(q_ref, f_ref, i_ref, og_ref, s0_ref, loglb_ref, log1m_ref, nw_ref, *rest, seq_len, nseq):
    o_ref, sout_ref = rest[-2:]
    c = seq_len * nseq

    def body(gi, carry):
        rows = pl.ds(pl.multiple_of(gi * c, c), c)

        def set_st(s_i, h, v):
            sout_ref[gi * nseq + s_i, h] = v.T

        y = _hgrn_chunk(q_ref[rows, :], f_ref[rows, :], i_ref[rows, :], og_ref[rows, :],
                        lambda s_i, h: s0_ref[gi * nseq + s_i, h].T, set_st,
                        loglb_ref[...], log1m_ref[...], nw_ref[...], seq_len)
        o_ref[rows, :] = y.astype(BF16)
        return carry

    lax.fori_loop(0, q_ref.shape[0] // c, body, 0)


def _hgrn_sample(p_hg_bm, s0, layer, loglb, log1m, nw, *, bs, ts, s_prev=None, after=None):
    hgw = p_hg_bm.shape[1] // 4
    nh = hgw // HG_DIM
    nseq = max(1, 16 // ts)
    assert bs % nseq == 0
    bb = _pick_block(bs, 8, nseq)
    rows = bb * ts
    cspec = lambda c: pl.BlockSpec((rows, hgw), lambda i: (i, c))
    pspec = pl.BlockSpec((1, hgw), lambda i: (0, 0))
    sspec = pl.BlockSpec((bb, None, nh, HG_DIM, HG_DIM), lambda i: (i, layer, 0, 0, 0))
    operands = [p_hg_bm, p_hg_bm, p_hg_bm, p_hg_bm, s0, loglb, log1m, nw]
    in_specs = [cspec(0), cspec(1), cspec(2), cspec(3), sspec, pspec, pspec, pspec]
    aliases = {}
    if s_prev is not None:
        operands.append(s_prev)
        in_specs.append(pl.BlockSpec(memory_space=pl.ANY))
        aliases = {len(operands) - 1: 1}
    if after is not None:
        operands.append(after)
        in_specs.append(pl.BlockSpec(memory_space=pl.ANY))
    return pl.pallas_call(
        functools.partial(_hgrn_sample_body, seq_len=ts, nseq=nseq),
        grid=(bs // bb,),
        in_specs=in_specs,
        out_specs=[pl.BlockSpec((rows, hgw), lambda i: (i, 0)), sspec],
        out_shape=[jax.ShapeDtypeStruct((bs * ts, hgw), BF16), jax.ShapeDtypeStruct(s0.shape, F32)],
        input_output_aliases=aliases,
        compiler_params=_cparams("parallel"),
        name="hgrn_sample",
    )(*operands)


def _lru_gates(xconv, wa_ref, wx_ref, ba, bx, lam):
    l = xconv.shape[1]
    blk = l // LRU_BLOCKS
    ga, gx = [], []
    for n in range(LRU_BLOCKS):
        xb = xconv[:, n * blk:(n + 1) * blk].astype(BF16)
        ga.append(_dot(xb, wa_ref[n]))
        gx.append(_dot(xb, wx_ref[n]))
    ga = jnp.concatenate(ga, axis=1) + ba
    gx = jnp.concatenate(gx, axis=1) + bx
    log_a = -LRU_C * jax.nn.sigmoid(ga) * _softplus(-lam)
    a = jnp.exp(log_a)
    mult = jnp.sqrt(1.0 - a * a)
    return a, mult * jax.nn.sigmoid(gx) * xconv


def _gelu_tanh(x):
    c = 0.7978845608028654
    return 0.5 * x * (1.0 + jnp.tanh(c * (x + 0.044715 * (x * x * x))))


def _odd_prompt_body(gate_ref, xb_ref, conv0_ref, h0_ref, cw_ref, cb_ref, wa_ref, wx_ref, ba_ref, bx_ref, lam_ref,
                     y_ref, convo_ref, ho_ref, ext_ref, hc_ref, a_s, b_s, h_s):
    rows = xb_ref.shape[0]
    hist = SUBLANES

    @pl.when(pl.program_id(1) == 0)
    def _():
        ext_ref[0:hist, :] = conv0_ref[...]
        hc_ref[...] = h0_ref[...]

    ext_ref[hist:hist + rows, :] = xb_ref[...]
    cw = cw_ref[...]
    xconv = cw[0:1, :] * ext_ref[pl.ds(hist - (CONV_W - 1), rows), :]
    for j in range(1, CONV_W):
        xconv = xconv + cw[j:j + 1, :] * ext_ref[pl.ds(hist - (CONV_W - 1) + j, rows), :]
    xconv = cb_ref[...] + xconv
    a, b = _lru_gates(xconv, wa_ref, wx_ref, ba_ref[...], bx_ref[...], lam_ref[...])
    width = a.shape[1]
    a = a.reshape(rows // SUBLANES, SUBLANES, width)
    b = b.reshape(rows // SUBLANES, SUBLANES, width)
    r8 = lax.broadcasted_iota(jnp.int32, (1, SUBLANES, 1), 1)
    for s in (1, 2, 4):
        m = r8 >= s
        a_sh = pltpu.roll(a, s, 1)
        b_sh = pltpu.roll(b, s, 1)
        b = jnp.where(m, a * b_sh + b, b)
        a = jnp.where(m, a * a_sh, a)
    a_s[...] = a.reshape(rows, width)
    b_s[...] = b.reshape(rows, width)

    def tile(j, hc):
        r0 = pl.multiple_of(j * SUBLANES, SUBLANES)
        h = a_s[pl.ds(r0, SUBLANES), :] * hc + b_s[pl.ds(r0, SUBLANES), :]
        h_s[pl.ds(r0, SUBLANES), :] = h
        return h[SUBLANES - 1:SUBLANES, :]

    hc = lax.fori_loop(0, rows // SUBLANES, tile, hc_ref[...])
    hc_ref[...] = hc
    ho_ref[...] = hc
    y_ref[...] = (h_s[...] * _gelu_tanh(gate_ref[...])).astype(BF16)
    tail = ext_ref[rows:rows + hist, :]
    ext_ref[0:hist, :] = tail
    convo_ref[...] = tail


def _odd_prompt(p_od, conv0, h0, layer, cw, cb, wa, wx, ba, bx, lam, *, bp, tp):
    l = p_od.shape[1] // 2
    rows = _pick_block(tp, 256, SUBLANES)
    nt = tp // rows
    lsel = lambda a: pl.BlockSpec((None,) + a.shape[1:], lambda b, t: (layer,) + (0,) * (a.ndim - 1))
    return pl.pallas_call(
        _odd_prompt_body,
        grid=(bp, nt),
        in_specs=[
            pl.BlockSpec((rows, l), lambda b, t: (b * nt + t, 0)),
            pl.BlockSpec((rows, l), lambda b, t: (b * nt + t, 1)),
            pl.BlockSpec((None, SUBLANES, l), lambda b, t: (b, 0, 0)),
            pl.BlockSpec((None, 1, l), lambda b, t: (b, 0, 0)),
            lsel(cw), lsel(cb), lsel(wa), lsel(wx), lsel(ba), lsel(bx), lsel(lam),
        ],
        out_specs=[
            pl.BlockSpec((rows, l), lambda b, t: (b * nt + t, 0)),
            pl.BlockSpec((None, SUBLANES, l), lambda b, t: (b, 0, 0)),
            pl.BlockSpec((None, 1, l), lambda b, t: (b, 0, 0)),
        ],
        out_shape=[jax.ShapeDtypeStruct((p_od.shape[0], l), BF16), jax.ShapeDtypeStruct((bp, SUBLANES, l), F32),
                   jax.ShapeDtypeStruct((bp, 1, l), F32)],
        scratch_shapes=[pltpu.VMEM((rows + SUBLANES, l), F32), pltpu.VMEM((1, l), F32),
                        pltpu.VMEM((rows, l), F32), pltpu.VMEM((rows, l), F32), pltpu.VMEM((rows, l), F32)],
        compiler_params=_cparams("parallel", "arbitrary"),
        name="odd_prompt",
    )(p_od, p_od, conv0, h0, cw, cb, wa, wx, ba, bx, lam)


def _odd_sample_body(gate_ref, xb_ref, conv0_ref, h0_ref, cw_ref, cb_ref, wa_ref, wx_ref, ba_ref, bx_ref, lam_ref,
                     yall_ref, y_ref, convo_ref, ho_ref):
    @pl.when(pl.program_id(0) == 0)
    def _():
        convo_ref[...] = conv0_ref[...]
        ho_ref[...] = h0_ref[...]

    xb = xb_ref[...]
    cw = cw_ref[...]
    xconv = cw[CONV_W - 1:CONV_W, :] * xb
    for j in range(CONV_W - 1):
        xconv = xconv + cw[j:j + 1, :] * convo_ref[j]
    xconv = cb_ref[...] + xconv
    a, b = _lru_gates(xconv, wa_ref, wx_ref, ba_ref[...], bx_ref[...], lam_ref[...])
    h = a * ho_ref[...] + b
    ho_ref[...] = h
    y_ref[...] = (h * _gelu_tanh(gate_ref[...])).astype(BF16)
    for j in range(CONV_W - 2):
        convo_ref[j] = convo_ref[j + 1]
    convo_ref[CONV_W - 2] = xb


def _odd_sample(p_od, conv0_t, h0, layer, cw, cb, wa, wx, ba, bx, lam, y_all, *, mp, bs, ts):
    l = p_od.shape[1] // 2
    assert mp % bs == 0
    off = mp // bs
    lsel = lambda a: pl.BlockSpec((None,) + a.shape[1:], lambda t: (layer,) + (0,) * (a.ndim - 1))
    full = lambda a: pl.BlockSpec(a.shape, lambda t: (0,) * a.ndim)
    return pl.pallas_call(
        _odd_sample_body,
        grid=(ts,),
        in_specs=[
            pl.BlockSpec((bs, l), lambda t: (off + t, 0)),
            pl.BlockSpec((bs, l), lambda t: (off + t, 1)),
            full(conv0_t), full(h0),
            lsel(cw), lsel(cb), lsel(wa), lsel(wx), lsel(ba), lsel(bx), lsel(lam),
            pl.BlockSpec(memory_space=pl.ANY),
        ],
        out_specs=[pl.BlockSpec((bs, l), lambda t: (off + t, 0)), full(conv0_t), full(h0)],
        out_shape=[jax.ShapeDtypeStruct(y_all.shape, BF16), jax.ShapeDtypeStruct(conv0_t.shape, F32),
                   jax.ShapeDtypeStruct(h0.shape, F32)],
        input_output_aliases={11: 0},
        compiler_params=_cparams("arbitrary"),
        name="odd_sample",
    )(p_od, p_od, conv0_t, h0, cw, cb, wa, wx, ba, bx, lam, y_all)


def _pad_last(a, n):
    return jnp.pad(a, [(0, 0)] * (a.ndim - 1) + [(0, n - a.shape[-1])])


def _round_up(n, m):
    return (n + m - 1) // m * m


class _RwLayout:
    def __init__(self, rw, dw, da, dg):
        self.rw, self.dw, self.da, self.dg = rw, dw, da, dg
        self.dwp, self.dap, self.dgp = (_round_up(d, LANES) for d in (dw, da, dg))
        self.width = 3 * rw + dw + da + dg
        self.padded = 3 * rw + self.dwp + self.dap + self.dgp

    def pad(self, a):
        o = 3 * self.rw
        parts = [a[..., :o], _pad_last(a[..., o:o + self.dw], self.dwp),
                 _pad_last(a[..., o + self.dw:o + self.dw + self.da], self.dap),
                 _pad_last(a[..., o + self.dw + self.da:self.width], self.dgp)]
        return jnp.concatenate(parts, axis=-1)

    def unpad(self, a):
        o = 3 * self.rw
        parts = [a[..., :o], a[..., o:o + self.dw], a[..., o + self.dwp:o + self.dwp + self.da],
                 a[..., o + self.dwp + self.dap:o + self.dwp + self.dap + self.dg]]
        return jnp.concatenate(parts, axis=-1)


def kernel(x_prompt, x_sample, state_rwkv_shift, state_rwkv, state_hgrn, state_conv, state_lru, ffn1_norm, ffn1_w_gu, ffn1_w_down, mix_norm, ffn2_norm, ffn2_w_gu, ffn2_w_down, ev_w_in, rw_mu, rw_w0, rw_w2, rw_a0, rw_a2, rw_g2, rw_k_k, rw_k_a, rw_r_k, rw_ln_w, rw_ln_b, hg_lb, hg_norm, ev_w_out, od_w_in, conv_w, conv_b, lru_wa, lru_ba, lru_wx, lru_bx, lru_lambda, od_w_out, final_norm):
    bp, tp, d = x_prompt.shape
    bs, ts, _ = x_sample.shape
    depth = ffn1_norm.shape[0]
    n_even = ev_w_in.shape[0]
    mp, ms = bp * tp, bs * ts
    rw = rw_w0.shape[1]
    nh_rw = rw // RW_HEAD
    hgw = hg_norm.shape[1]
    nh_hg = hgw // HG_DIM
    lay = _RwLayout(rw, rw_w2.shape[1], rw_a2.shape[1], rw_g2.shape[1])
    assert bp * nh_rw * 2 == LANES and (bs * nh_rw) % LANES == 0
    dt = x_prompt.dtype

    ffn_src = {1: (ffn1_norm, ffn1_w_gu, ffn1_w_down), 2: (ffn2_norm, ffn2_w_gu, ffn2_w_down)}
    ffn_w = (ffn1_w_gu[0].astype(BF16), ffn1_w_down[0].astype(BF16))

    def ffn_step(x, l, which, ffn_w):
        norm_g = ffn_src[which][0]
        nxt = (l, 2) if which == 1 else (l + 1, 1)
        if nxt[0] >= depth:
            return _ffn(x, norm_g, l, *ffn_w)[0], None
        x, c_gu, c_dn = _ffn(x, norm_g, l, *ffn_w, next_src=(ffn_src[nxt[1]][1], ffn_src[nxt[1]][2], nxt[0]))
        return x, (c_gu, c_dn)

    ev_out = ev_w_out.astype(BF16)
    od_in, od_out = od_w_in.astype(BF16), od_w_out.astype(BF16)

    wa_bf, wx_bf = lru_wa.astype(BF16), lru_wx.astype(BF16)

    x = jnp.concatenate([x_prompt.reshape(mp, d), x_sample.transpose(1, 0, 2).reshape(ms, d)], axis=0)

    lbs = jnp.cumsum(jax.nn.softmax(hg_lb.astype(F32), axis=0), axis=0)
    lb_all = lbs - lbs[0]

    def k_lanes_p(a):
        return a.reshape(bp * nh_rw, RW_HEAD, a.shape[1]).transpose(2, 1, 0)

    def v_lanes_p(a):
        n = a.shape[1]
        return a.reshape(bp, nh_rw, 2, RW_HEAD // 2, n).transpose(4, 3, 2, 0, 1).reshape(n, RW_HEAD // 2, LANES)

    def v_lanes_p_inv(a):
        return a.reshape(tp, RW_HEAD // 2, 2, bp, nh_rw).transpose(3, 0, 4, 2, 1).reshape(mp, rw)

    def k_lanes_s(a):
        a = a.reshape(bp, nh_rw, RW_HEAD, ts // bp, bs).transpose(3, 0, 2, 4, 1)
        return a.reshape(ts, RW_HEAD, bs * nh_rw)

    def k_lanes_s_inv(a):
        return a.reshape(ts, RW_HEAD, bs, nh_rw).transpose(0, 2, 3, 1).reshape(ms, rw)

    def kparam(p):
        return jnp.tile(p.reshape(nh_rw, RW_HEAD).T, (1, LANES // nh_rw))

    def vparam_p(p):
        a = p.reshape(nh_rw, 2, RW_HEAD // 2).transpose(2, 1, 0)[:, :, None, :]
        return jnp.broadcast_to(a, (RW_HEAD // 2, 2, bp, nh_rw)).reshape(RW_HEAD // 2, LANES)

    shifts_p, shifts_s, rws_p, hgs_p, convs_p, convs_s, lrus_p, lrus_s = ([] for _ in range(8))
    rw_rows = state_rwkv.astype(F32).reshape(bs, n_even, nh_rw, RW_HEAD * RW_HEAD)
    hg_new = None

    for l in range(depth):
        x, ffn_w = ffn_step(x, l, 1, ffn_w)
        if l % 2 == 0:
            e = l // 2
            w_in = ev_w_in[e]
            w_rw = lay.pad(w_in[:, :lay.width]).astype(BF16)
            w_hg = w_in[:, lay.width:].astype(BF16)
            p_rw = _norm_proj(x, mix_norm, l, w_rw, _pick_block(lay.padded, 1792, LANES))
            p_hg = _norm_proj(x, mix_norm, l, w_hg, _pick_block(4 * hgw, 1024, LANES))
            shift_p = jnp.zeros((bp, lay.padded), F32)
            shift_s = lay.pad(state_rwkv_shift[:, e].astype(F32))
            rkvwa, g_ = _rwkv_prep(
                p_rw, shift_p, shift_s, lay.pad(rw_mu[e])[None], rw_w0[e][None],
                _pad_last(rw_w2[e].T, lay.dwp).T.astype(BF16), rw_a0[e][None],
                _pad_last(rw_a2[e].T, lay.dap).T.astype(BF16),
                _pad_last(rw_g2[e].T, lay.dgp).T.astype(BF16), rw=rw, mp=mp, tp=tp, bs=bs)
            shifts_p.append(lay.unpad(jnp.concatenate([p_rw[b * tp + tp - 1:(b + 1) * tp] for b in range(bp)])))
            shifts_s.append(lay.unpad(p_rw[mp + (ts - 1) * bs:]))
            kk_p, ka_p = kparam(rw_k_k[e]), kparam(rw_k_a[e])
            rk_p = kparam(rw_r_k[e].reshape(rw))
            lb = lb_all[e][None]
            loglb, log1m = jnp.log(lb), jnp.log1p(-lb)
            nw = hg_norm[e][None]
            chunk = _pick_block(tp, HG_CHUNK, 1)
            oh_p, hs_p = _hgrn_prompt(p_hg, jnp.zeros((bp, nh_hg, HG_DIM, HG_DIM), F32), loglb, log1m, nw,
                                      bp=bp, tp=tp, chunk=chunk)
            r_p, k_p, v_p, w_p, a_p = rkvwa
            o_p, s_p = _rwkv_scan(
                k_lanes_p(r_p), k_lanes_p(w_p), k_lanes_p(k_p), k_lanes_p(a_p), v_lanes_p(v_p),
                kk_p, ka_p, rk_p, vparam_p(rw_ln_w[e]), vparam_p(rw_ln_b[e]),
                jnp.zeros((RW_HEAD // 2, RW_HEAD, LANES), F32), t=tp, tc=_pick_block(tp, 64, 1), vsplit=True,
                after=oh_p)
            rws_p.append(s_p.reshape(RW_HEAD // 2, RW_HEAD, 2, bp, nh_rw).transpose(3, 4, 2, 0, 1)
                         .reshape(bp, nh_rw, RW_HEAD, RW_HEAD))
            r_s, k_s, v_s, w_s, a_s = (a[:, tp:] for a in rkvwa)
            o_s, rw_rows = _rwkv_scan(
                k_lanes_s(r_s), k_lanes_s(w_s), k_lanes_s(k_s), k_lanes_s(a_s), k_lanes_s(v_s),
                kk_p, ka_p, rk_p, kparam(rw_ln_w[e]), kparam(rw_ln_b[e]),
                rw_rows, t=ts, tc=ts, vsplit=False, layer=e, after=o_p)
            o_rw = jnp.concatenate([v_lanes_p_inv(o_p), k_lanes_s_inv(o_s)], axis=0)
            p_hg_bm = p_hg[mp:].reshape(ts, bs, 4 * hgw).transpose(1, 0, 2).reshape(ms, 4 * hgw)
            oh_s, hg_new = _hgrn_sample(p_hg_bm, state_hgrn.astype(F32), e, loglb, log1m, nw, bs=bs, ts=ts,
                                        s_prev=hg_new, after=o_p)
            oh_s = oh_s.reshape(bs, ts, hgw).transpose(1, 0, 2).reshape(ms, hgw)
            hgs_p.append(hs_p)
            o_hg = oh_p.at[mp:].set(oh_s)
            x = _proj_res_even(x, o_rw, g_, o_hg, ev_out, e)
        else:
            o_i = l // 2
            p_od = _norm_proj(x, mix_norm, l, od_in, _pick_block(od_in.shape[-1], 1024, LANES), w_layer=o_i)
            lsz = p_od.shape[1] // 2
            vec = lambda a: a[:, None, :]
            args = (conv_w, vec(conv_b), wa_bf, wx_bf, vec(lru_ba), vec(lru_bx), vec(lru_lambda))
            conv0_p = jnp.zeros((bp, SUBLANES, lsz), F32)
            y_p, c_p, h_p = _odd_prompt(p_od, conv0_p, jnp.zeros((bp, 1, lsz), F32), o_i, *args, bp=bp, tp=tp)
            conv0_s = state_conv[:, o_i].astype(F32).transpose(1, 0, 2)
            y_all, c_s, h_s = _odd_sample(p_od, conv0_s, state_lru[:, o_i].astype(F32), o_i, *args, y_p,
                                          mp=mp, bs=bs, ts=ts)
            convs_p.append(c_p[:, SUBLANES - (CONV_W - 1):])
            convs_s.append(c_s.transpose(1, 0, 2))
            lrus_p.append(h_p[:, 0])
            lrus_s.append(h_s)
            x = _proj_res(x, y_all, od_out, o_i)
        x, ffn_w = ffn_step(x, l, 2, ffn_w)

    y_p, y_s = _final_norm(x, final_norm, mp)
    y_prompt = y_p.reshape(bp, tp, d)
    y_sample = y_s.reshape(ts, bs, d).transpose(1, 0, 2)
    st = lambda xs: jnp.stack(xs, axis=1).astype(dt)
    sample_rwkv = rw_rows.reshape(state_rwkv.shape).astype(dt)
    return (y_prompt, y_sample, st(shifts_p), st(rws_p), st(hgs_p), st(convs_p), st(lrus_p),
            st(shifts_s), sample_rwkv, hg_new.astype(dt), st(convs_s), st(lrus_s))
```

```python
import functools
import math

import jax
import jax.numpy as jnp
from jax import lax
from jax.experimental import pallas as pl
from jax.experimental.pallas import tpu as pltpu

F32 = jnp.float32
BF16 = jnp.bfloat16

NORM_EPS = 1e-6
RW_GN_EPS = 64e-5
HG_NORM_EPS = 1e-5
LRU_C = 8.0
RW_HEAD = 64
HG_DIM = 128
LRU_BLOCKS = 8
CONV_W = 4
LANES = 128
SUBLANES = 8
VMEM_LIMIT = 62 * 1024 * 1024
MAX_BM = 1088
HG_CHUNK = 64


def _cparams(*sem):
    return pltpu.CompilerParams(dimension_semantics=sem, vmem_limit_bytes=VMEM_LIMIT)


def _pick_block(n, cap, mult):
    best = None
    for d in range(mult, min(n, cap) + 1, mult):
        if n % d == 0:
            best = d
    assert best is not None, (n, cap, mult)
    return best


def _gcd(a, b):
    return math.gcd(a, b)


def _dot(a, b):
    return jnp.dot(a, b, preferred_element_type=F32)


def _dot_nt(a, b):
    return lax.dot_general(a, b, (((1,), (1,)), ((), ())), preferred_element_type=F32)


def _dot_tn(a, b):
    return lax.dot_general(a, b, (((0,), (0,)), ((), ())), preferred_element_type=F32)


def _softplus(x):
    return jnp.maximum(x, 0.0) + jnp.log1p(jnp.exp(-jnp.abs(x)))


def _neg_expm1(y):
    return 1.0 - jnp.exp(y)


def _rms_rows(x, g):
    ms = jnp.mean(x * x, axis=-1, keepdims=True)
    return x * lax.rsqrt(ms + NORM_EPS) * g


def _ffn_body(x_ref, g_ref, wg_ref, wu_ref, wd_ref, *rest, cast_next):
    if cast_next:
        sgu_ref, sdn_ref, o_ref, cgu_ref, cdn_ref, xn_ref = rest
        cgu_ref[...] = sgu_ref[...].astype(BF16)
        cdn_ref[...] = sdn_ref[...].astype(BF16)
    else:
        o_ref, xn_ref = rest

    @pl.when(pl.program_id(1) == 0)
    def _():
        x = x_ref[...]
        xn_ref[...] = _rms_rows(x, g_ref[...]).astype(BF16)
        o_ref[...] = x

    xn = xn_ref[...]
    gate = _dot(xn, wg_ref[...])
    up = _dot(xn, wu_ref[...])
    act = (0.5 * (gate * jax.nn.sigmoid(gate)) * up).astype(BF16)
    o_ref[...] += _dot(act, wd_ref[...])


def _ffn(x, norm_g, layer, w_gu, w_down, next_src=None):
    m, d = x.shape
    f = w_down.shape[0]
    bm = _pick_block(m, MAX_BM, 16)
    bf = _pick_block(f, 512, LANES)
    ni, nf = m // bm, f // bf
    in_specs = [
        pl.BlockSpec((bm, d), lambda i, j: (i, 0)),
        pl.BlockSpec((None, 1, d), lambda i, j: (layer, 0, 0)),
        pl.BlockSpec((d, bf), lambda i, j: (0, j)),
        pl.BlockSpec((d, bf), lambda i, j: (0, j + nf)),
        pl.BlockSpec((bf, d), lambda i, j: (j, 0)),
    ]
    operands = [x, norm_g[:, None, :], w_gu, w_gu, w_down]
    out_specs = [pl.BlockSpec((bm, d), lambda i, j: (i, 0))]
    out_shape = [jax.ShapeDtypeStruct((m, d), F32)]
    if next_src is not None:
        s_gu, s_dn, nl = next_src
        assert d % ni == 0 and (2 * f) % nf == 0 and (d // ni) % 16 == 0 and (d // ni) % LANES == 0
        gu_blk, dn_blk = (d // ni, 2 * f // nf), (f // nf, d // ni)
        in_specs += [pl.BlockSpec((None,) + gu_blk, lambda i, j: (nl, i, j)),
                     pl.BlockSpec((None,) + dn_blk, lambda i, j: (nl, j, i))]
        operands += [s_gu, s_dn]
        out_specs += [pl.BlockSpec(gu_blk, lambda i, j: (i, j)), pl.BlockSpec(dn_blk, lambda i, j: (j, i))]
        out_shape += [jax.ShapeDtypeStruct((d, 2 * f), BF16), jax.ShapeDtypeStruct((f, d), BF16)]
    return pl.pallas_call(
        functools.partial(_ffn_body, cast_next=next_src is not None),
        grid=(ni, nf),
        in_specs=in_specs,
        out_specs=out_specs,
        out_shape=out_shape,
        scratch_shapes=[pltpu.VMEM((bm, d), BF16)],
        compiler_params=_cparams("parallel", "arbitrary"),
        name="ffn",
    )(*operands)


def _norm_proj_body(x_ref, g_ref, w_ref, o_ref, xn_ref):
    @pl.when(pl.program_id(1) == 0)
    def _():
        xn_ref[...] = _rms_rows(x_ref[...], g_ref[...]).astype(BF16)

    o_ref[...] = _dot(xn_ref[...], w_ref[...])


def _w_spec(w, blk, idx, w_layer):
    if w.ndim == 2:
        return pl.BlockSpec(blk, idx)
    return pl.BlockSpec((None,) + blk, lambda i, j: (w_layer,) + idx(i, j))


def _norm_proj(x, norm_g, layer, w, bn, w_layer=None):
    m, d = x.shape
    n = w.shape[-1]
    bm = _pick_block(m, MAX_BM, 16)
    return pl.pallas_call(
        _norm_proj_body,
        grid=(m // bm, n // bn),
        in_specs=[
            pl.BlockSpec((bm, d), lambda i, j: (i, 0)),
            pl.BlockSpec((None, 1, d), lambda i, j: (layer, 0, 0)),
            _w_spec(w, (d, bn), lambda i, j: (0, j), w_layer),
        ],
        out_specs=pl.BlockSpec((bm, bn), lambda i, j: (i, j)),
        out_shape=jax.ShapeDtypeStruct((m, n), F32),
        scratch_shapes=[pltpu.VMEM((bm, d), BF16)],
        compiler_params=_cparams("parallel", "arbitrary"),
        name="norm_proj",
    )(x, norm_g[:, None, :], w)


def _proj_res_body(x_ref, y_ref, w_ref, o_ref):
    o_ref[...] = x_ref[...] + _dot(y_ref[...], w_ref[...])


def _proj_res(x, y, w, w_layer):
    m, d = x.shape
    k = y.shape[1]
    bm = _pick_block(m, MAX_BM, 16)
    bn = _pick_block(d, 1024, LANES)
    return pl.pallas_call(
        _proj_res_body,
        grid=(m // bm, d // bn),
        in_specs=[
            pl.BlockSpec((bm, bn), lambda i, j: (i, j)),
            pl.BlockSpec((bm, k), lambda i, j: (i, 0)),
            _w_spec(w, (k, bn), lambda i, j: (0, j), w_layer),
        ],
        out_specs=pl.BlockSpec((bm, bn), lambda i, j: (i, j)),
        out_shape=jax.ShapeDtypeStruct((m, d), F32),
        compiler_params=_cparams("parallel", "arbitrary"),
        name="proj_res",
    )(x, y, w)


def _proj_res_even_body(x_ref, orw_ref, g_ref, ohg_ref, w1_ref, w2_ref, o_ref):
    y1 = (orw_ref[...] * g_ref[...]).astype(BF16)
    o_ref[...] = x_ref[...] + _dot(y1, w1_ref[...]) + _dot(ohg_ref[...], w2_ref[...])


def _proj_res_even(x, o_rw, g, o_hg, w, w_layer):
    m, d = x.shape
    rw = o_rw.shape[1]
    hg = o_hg.shape[1]
    assert rw == hg
    bm = _pick_block(m, MAX_BM, 16)
    bn = _pick_block(d, 1024, LANES)
    return pl.pallas_call(
        _proj_res_even_body,
        grid=(m // bm, d // bn),
        in_specs=[
            pl.BlockSpec((bm, bn), lambda i, j: (i, j)),
            pl.BlockSpec((bm, rw), lambda i, j: (i, 0)),
            pl.BlockSpec((bm, rw), lambda i, j: (i, 0)),
            pl.BlockSpec((bm, hg), lambda i, j: (i, 0)),
            _w_spec(w, (rw, bn), lambda i, j: (0, j), w_layer),
            _w_spec(w, (hg, bn), lambda i, j: (1, j), w_layer),
        ],
        out_specs=pl.BlockSpec((bm, bn), lambda i, j: (i, j)),
        out_shape=jax.ShapeDtypeStruct((m, d), F32),
        compiler_params=_cparams("parallel", "arbitrary"),
        name="proj_res_even",
    )(x, o_rw, g, o_hg, w, w)


def _final_norm_body(x_ref, g_ref, op_ref, os_ref, *, n_prompt_blk):
    y = _rms_rows(x_ref[...], g_ref[...])
    i = pl.program_id(0)

    @pl.when(i < n_prompt_blk)
    def _():
        op_ref[...] = y

    @pl.when(i >= n_prompt_blk)
    def _():
        os_ref[...] = y


def _final_norm(x, g, mp):
    m, d = x.shape
    ms = m - mp
    bm = _pick_block(_gcd(mp, ms), MAX_BM, SUBLANES)
    npb = mp // bm
    return pl.pallas_call(
        functools.partial(_final_norm_body, n_prompt_blk=npb),
        grid=(m // bm,),
        in_specs=[pl.BlockSpec((bm, d), lambda i: (i, 0)), pl.BlockSpec((1, d), lambda i: (0, 0))],
        out_specs=[pl.BlockSpec((bm, d), lambda i: (jnp.minimum(i, npb - 1), 0)),
                   pl.BlockSpec((bm, d), lambda i: (jnp.maximum(i - npb, 0), 0))],
        out_shape=[jax.ShapeDtypeStruct((mp, d), F32), jax.ShapeDtypeStruct((ms, d), F32)],
        compiler_params=_cparams("arbitrary"),
        name="final_norm",
    )(x, g[None, :])


def _rwkv_prep_body(cur_ref, tail_ref, prevs_ref, shiftp_ref, shifts_ref, mu_ref, w0_ref, w2_ref, a0_ref, a2_ref,
                    g2_ref, r_ref, k_ref, v_ref, w_ref, a_ref, g_ref, *, rw, dwp, dap, n_prompt_blk, blk_per_seq):
    i = pl.program_id(0)
    rows = cur_ref.shape[0]
    prw = cur_ref.shape[1]
    seq = jnp.minimum(i // blk_per_seq, shiftp_ref.shape[0] - 1)
    first = (i % blk_per_seq) == 0
    is_sample = i >= n_prompt_blk
    rid = lax.broadcasted_iota(jnp.int32, (rows, 1), 0)
    shift_row = shiftp_ref[pl.ds(seq, 1), :]

    def shifted(c0, c1):
        cs = slice(c0, c1)
        cur = cur_ref[:, cs]
        row0 = jnp.where(first, shift_row[:, cs], tail_ref[SUBLANES - 1:SUBLANES, cs])
        prev_p = jnp.where(rid == 0, row0, pltpu.roll(cur, 1, 0))
        prev_s = jnp.where(i == n_prompt_blk, shifts_ref[:, cs], prevs_ref[:, cs])
        prev = jnp.where(is_sample, prev_s, prev_p)
        return cur + (prev - cur) * mu_ref[:, cs]

    r_ref[...] = shifted(0, rw).T
    k_ref[...] = shifted(rw, 2 * rw).T
    v_ref[...] = shifted(2 * rw, 3 * rw).T
    o = 3 * rw
    lw = w0_ref[...] + _dot(jnp.tanh(shifted(o, o + dwp)).astype(BF16), w2_ref[...])
    w_log = -_softplus(-lw) - 0.5
    w_ref[...] = jnp.exp(-jnp.exp(w_log)).T
    a_ref[...] = jax.nn.sigmoid(a0_ref[...] + _dot(shifted(o + dwp, o + dwp + dap).astype(BF16), a2_ref[...])).T
    g_ref[...] = _dot(jax.nn.sigmoid(shifted(o + dwp + dap, prw)).astype(BF16), g2_ref[...])


def _rwkv_prep(p_rw, shift_p, shift_s, mu, w0, w2, a0, a2, g2, *, rw, mp, tp, bs):
    m, prw = p_rw.shape
    ms = m - mp
    assert mp % bs == 0 and tp % bs == 0 and ms % bs == 0
    npb = mp // bs
    dwp, dap = w2.shape[0], a2.shape[0]
    full = lambda a: pl.BlockSpec(a.shape, lambda i: (0,) * a.ndim)
    bps = tp // bs
    bp = mp // tp

    ts = ms // bs
    assert ts % bp == 0

    def out_idx(i):
        j = i - npb
        return jnp.where(i < npb, i // bps, j % bp), jnp.where(i < npb, i % bps, bps + j // bp)

    spec_o = pl.BlockSpec((rw, bs), out_idx)
    body = functools.partial(_rwkv_prep_body, rw=rw, dwp=dwp, dap=dap, n_prompt_blk=npb, blk_per_seq=tp // bs)
    outs = pl.pallas_call(
        body,
        grid=(m // bs,),
        in_specs=[
            pl.BlockSpec((bs, prw), lambda i: (i, 0)),
            pl.BlockSpec((SUBLANES, prw), lambda i: (jnp.maximum(i * (bs // SUBLANES) - 1, 0), 0)),
            pl.BlockSpec((bs, prw), lambda i: (jnp.maximum(i - 1, npb), 0)),
            full(shift_p), full(shift_s), full(mu), full(w0), full(w2), full(a0), full(a2), full(g2),
        ],
        out_specs=[spec_o] * 5 + [pl.BlockSpec((bs, rw), lambda i: (i, 0))],
        out_shape=[jax.ShapeDtypeStruct((bp * rw, tp + ms // bp), F32)] * 5 + [jax.ShapeDtypeStruct((m, rw), F32)],
        compiler_params=_cparams("arbitrary"),
        name="rwkv_prep",
    )(p_rw, p_rw, p_rw, shift_p, shift_s, mu, w0, w2, a0, a2, g2)
    return outs[:5], outs[5]


def _rwkv_scan_body(r_ref, w_ref, k_ref, a_ref, v_ref, kk_p_ref, ka_p_ref, rk_p_ref, lnw_ref, lnb_ref, s0_ref,
                    o_ref, sout_ref, s_ref, r_s, w_s, nkk_s, b_s, km_s, v_s, *, vsplit, state_rows):
    tc_len, nv, lanes = o_ref.shape
    kc = s_ref.shape[1]
    ti = pl.program_id(1)

    @pl.when(ti == 0)
    def _():
        if state_rows:
            s_ref[...] = s0_ref[...].reshape(lanes, nv * kc).T.reshape(nv, kc, lanes)
        else:
            s_ref[...] = s0_ref[...]

    def widen(x):
        return jnp.concatenate([x, x], axis=-1) if vsplit else x

    if vsplit:
        v_in = v_ref[...]
        v_s[...] = jnp.concatenate([v_in[:, :nv, :], v_in[:, nv:, :]], axis=-1)
    else:
        v_s[...] = v_ref[...]
    kvec = widen(k_ref[...])
    a_ = widen(a_ref[...])
    r_s[...] = widen(r_ref[...])
    w_s[...] = widen(w_ref[...])
    kk = kvec * kk_p_ref[...][None]
    nrm = jnp.sqrt(jnp.sum(kk * kk, axis=1, keepdims=True))
    kk = kk / jnp.maximum(nrm, 1e-12)
    nkk_s[...] = -kk
    b_s[...] = kk * a_
    km_s[...] = kvec * (1.0 + (a_ - 1.0) * ka_p_ref[...][None])

    def step(t, carry):
        nkk = nkk_s[t]
        b = b_s[t]
        km = km_s[t]
        w = w_s[t]
        r = r_s[t]
        vt = v_s[t]
        for vi in range(nv):
            sv = s_ref[vi]
            sa = jnp.sum(sv * nkk, axis=0, keepdims=True)
            sv = sv * w + sa * b + vt[vi:vi + 1, :] * km
            s_ref[vi] = sv
            o_ref[t, pl.ds(vi, 1), :] = jnp.sum(sv * r, axis=0, keepdims=True)
        return carry

    lax.fori_loop(0, tc_len, step, 0)

    o = o_ref[...]
    n_val = nv * (2 if vsplit else 1)

    def head_sum(x):
        if vsplit:
            x2 = x.reshape(tc_len * nv, lanes)
            x = (x2 + pltpu.roll(x2, lanes // 2, 1)).reshape(tc_len, nv, lanes)
        return jnp.sum(x, axis=1, keepdims=True)

    mu = head_sum(o) / n_val
    d = o - mu
    var = head_sum(d * d) / n_val
    o = d * lax.rsqrt(var + RW_GN_EPS) * lnw_ref[...][None] + lnb_ref[...][None]
    bonus = jnp.sum(r_s[...] * km_s[...] * rk_p_ref[...][None], axis=1, keepdims=True)
    o_ref[...] = o + bonus * v_s[...]

    @pl.when(ti == pl.num_programs(1) - 1)
    def _():
        if state_rows:
            sout_ref[...] = s_ref[...].reshape(nv * kc, lanes).T.reshape(sout_ref.shape)
        else:
            sout_ref[...] = s_ref[...]


def _rwkv_scan(r, w, k, a, v, kk_p, ka_p, rk_p, lnw, lnb, s0, *, t, tc, vsplit, layer=None, after=None):
    _, kc, kl = r.shape
    nv = v.shape[1] // 2 if vsplit else v.shape[1]
    ln = v.shape[2] * 2 if vsplit else v.shape[2]
    g = ln // LANES
    kspec = pl.BlockSpec((tc, kc, kl // g), lambda gi, ti: (ti, 0, gi))
    vspec = pl.BlockSpec((tc, nv, LANES), lambda gi, ti: (ti, 0, gi))
    pk = pl.BlockSpec((kc, LANES), lambda gi, ti: (0, 0))
    pv = pl.BlockSpec((nv, LANES), lambda gi, ti: (0, 0))
    state_rows = not vsplit
    operands = [r, w, k, a, v, kk_p, ka_p, rk_p, lnw, lnb, s0]
    in_specs = [kspec, kspec, kspec, kspec, kspec if vsplit else vspec, pk, pk, pk, pv, pv]
    aliases = {}
    if state_rows:
        nb, _, nh, sz = s0.shape
        assert sz == nv * kc and nb * nh == ln
        sspec = pl.BlockSpec((LANES // nh, None, nh, sz), lambda gi, ti: (gi, layer, 0, 0))
        in_specs.append(sspec)
        aliases = {10: 1}
        s_shape = s0.shape
    else:
        sspec = pl.BlockSpec((nv, kc, LANES), lambda gi, ti: (0, 0, gi))
        in_specs.append(sspec)
        s_shape = (nv, kc, ln)
    if after is not None:
        operands.append(after)
        in_specs.append(pl.BlockSpec(memory_space=pl.ANY))

    def body(*refs):
        n_in = len(operands)
        ins = refs[:11]
        _rwkv_scan_body(*ins, *refs[n_in:], vsplit=vsplit, state_rows=state_rows)

    return pl.pallas_call(
        body,
        grid=(g, t // tc),
        in_specs=in_specs,
        out_specs=[vspec, sspec],
        out_shape=[jax.ShapeDtypeStruct((t, nv, ln), F32), jax.ShapeDtypeStruct(s_shape, F32)],
        scratch_shapes=[pltpu.VMEM((nv, kc, LANES), F32)] + [pltpu.VMEM((tc, kc, LANES), F32)] * 5
        + [pltpu.VMEM((tc, nv, LANES), F32)],
        input_output_aliases=aliases,
        compiler_params=_cparams("parallel", "arbitrary"),
        name="rwkv_scan",
    )(*operands)


def _ref_rows(gc, j):
    c, w = gc.shape
    b = 1 << j
    nt = c // SUBLANES
    gv = gc.reshape(nt, SUBLANES, w)
    if 2 * b <= SUBLANES:
        sub = lax.broadcasted_iota(jnp.int32, (1, SUBLANES, 1), 1)
        out = None
        for g0 in range(0, SUBLANES, 2 * b):
            piece = jnp.broadcast_to(gv[:, g0 + b - 1:g0 + b, :], gv.shape)
            out = piece if out is None else jnp.where(sub >= g0, piece, out)
        return out.reshape(c, w)
    tiles_per_group = 2 * b // SUBLANES
    pieces = []
    for ti in range(nt):
        src = (ti // tiles_per_group) * tiles_per_group + b // SUBLANES - 1
        pieces.append(jnp.broadcast_to(gv[src, SUBLANES - 1:SUBLANES, :], (SUBLANES, w)))
    return jnp.concatenate(pieces, axis=0)


def _hgrn_chunk(q, f, val, og, get_st, set_st, loglb, log1mlb, nrm_w, seq_len):
    c, width = q.shape
    nh = width // HG_DIM
    nseq = c // seq_len
    lg = seq_len.bit_length() - 1
    assert (1 << lg) == seq_len and c % SUBLANES == 0
    ls = jnp.minimum(f, 0.0) - jnp.log1p(jnp.exp(-jnp.abs(f)))
    b_ = log1mlb + ls
    g = jnp.maximum(loglb, b_) + jnp.log1p(jnp.exp(-jnp.abs(loglb - b_)))
    kk = _neg_expm1(g)
    qs = q * jax.nn.sigmoid(q)
    row = lax.broadcasted_iota(jnp.int32, (c, 1), 0)
    col = lax.broadcasted_iota(jnp.int32, (1, c), 1)
    tpos = row & (seq_len - 1)
    gc = g
    for j in range(lg):
        s = 1 << j
        gc = gc + jnp.where(tpos >= s, pltpu.roll(gc, s, 0), 0.0)
    qbs, kbs, sames = [qs.astype(BF16)], [kk.astype(BF16)], [row == col]
    for j in range(lg):
        e = jnp.exp(-jnp.abs(gc - _ref_rows(gc, j)))
        is_q = ((row >> j) & 1) == 1
        pk = (jnp.where(is_q, qs, kk) * e).astype(BF16)
        qbs.append(pk)
        kbs.append(pk)
        sames.append(((row >> (j + 1)) == (col >> (j + 1))) & is_q & (((col >> j) & 1) == 0))
    vb = val.astype(BF16)
    qg = (qs * jnp.exp(gc)).astype(BF16)
    k2s, eglast, in_seq = [], [], []
    for s_i in range(nseq):
        glast = gc[s_i * seq_len + seq_len - 1:s_i * seq_len + seq_len, :]
        eglast.append(jnp.exp(glast))
        if nseq == 1:
            in_seq.append(None)
            k2s.append((kk * jnp.exp(glast - gc)).astype(BF16))
        else:
            in_s = (row >> lg) == s_i
            in_seq.append(in_s)
            k2s.append(jnp.where(in_s, kk * jnp.exp(jnp.minimum(glast - gc, 0.0)), 0.0).astype(BF16))
    outs = []
    for h in range(nh):
        cs = slice(h * HG_DIM, (h + 1) * HG_DIM)
        amat = None
        for qb, kb, same in zip(qbs, kbs, sames):
            term = jnp.where(same, _dot_nt(qb[:, cs], kb[:, cs]), 0.0)
            amat = term if amat is None else amat + term
        o = _dot(amat.astype(BF16), vb[:, cs])
        for s_i in range(nseq):
            st = get_st(s_i, h)
            o_int = _dot_nt(qg[:, cs], st.astype(BF16))
            o = o + (o_int if nseq == 1 else jnp.where(in_seq[s_i], o_int, 0.0))
            set_st(s_i, h, st * eglast[s_i][:, cs] + _dot_tn(vb[:, cs], k2s[s_i][:, cs]))
        outs.append(o * lax.rsqrt(jnp.mean(o * o, axis=-1, keepdims=True) + HG_NORM_EPS))
    on = jnp.concatenate(outs, axis=1) * nrm_w
    return on * (og * jax.nn.sigmoid(og))


def _hgrn_prompt_body(q_ref, f_ref, i_ref, og_ref, s0_ref, loglb_ref, log1m_ref, nw_ref, o_ref, sout_ref, st_ref,
                      *, chunk):
    nh = st_ref.shape[0]
    tb = pl.program_id(1)

    @pl.when(tb == 0)
    def _():
        for h in range(nh):
            st_ref[h] = s0_ref[h].T

    def set_st(s_i, h, v):
        st_ref[h] = v

    def body(ci, carry):
        rows = pl.ds(pl.multiple_of(ci * chunk, chunk), chunk)
        y = _hgrn_chunk(q_ref[rows, :], f_ref[rows, :], i_ref[rows, :], og_ref[rows, :],
                        lambda s_i, h: st_ref[h], set_st, loglb_ref[...], log1m_ref[...], nw_ref[...], chunk)
        o_ref[rows, :] = y.astype(BF16)
        return carry

    lax.fori_loop(0, q_ref.shape[0] // chunk, body, 0)

    @pl.when(tb == pl.num_programs(1) - 1)
    def _():
        for h in range(nh):
            sout_ref[h] = st_ref[h].T


def _hgrn_prompt(p_hg, s0, loglb, log1m, nw, *, bp, tp, chunk):
    hgw = p_hg.shape[1] // 4
    nh = hgw // HG_DIM
    tb = _pick_block(tp, 512, chunk)
    nt = tp // tb
    cspec = lambda c: pl.BlockSpec((tb, hgw), lambda b, t: (b * nt + t, c))
    pspec = pl.BlockSpec((1, hgw), lambda b, t: (0, 0))
    sspec = pl.BlockSpec((None, nh, HG_DIM, HG_DIM), lambda b, t: (b, 0, 0, 0))
    return pl.pallas_call(
        functools.partial(_hgrn_prompt_body, chunk=chunk),
        grid=(bp, nt),
        in_specs=[cspec(0), cspec(1), cspec(2), cspec(3), sspec, pspec, pspec, pspec],
        out_specs=[pl.BlockSpec((tb, hgw), lambda b, t: (b * nt + t, 0)), sspec],
        out_shape=[jax.ShapeDtypeStruct((p_hg.shape[0], hgw), BF16), jax.ShapeDtypeStruct(s0.shape, F32)],
        scratch_shapes=[pltpu.VMEM((nh, HG_DIM, HG_DIM), F32)],
        compiler_params=_cparams("parallel", "arbitrary"),
        name="hgrn_prompt",
    )(p_hg, p_hg, p_hg, p_hg, s0, loglb, log1m, nw)


def _hgrn_sample_body(q_ref, f_ref, i_ref, og_ref, s0_ref, loglb_ref, log1m_ref, nw_ref, *rest, seq_len, nseq):
    o_ref, sout_ref = rest[-2:]
    c = seq_len * nseq

    def body(gi, carry):
        rows = pl.ds(pl.multiple_of(gi * c, c), c)

        def set_st(s_i, h, v):
            sout_ref[gi * nseq + s_i, h] = v.T

        y = _hgrn_chunk(q_ref[rows, :], f_ref[rows, :], i_ref[rows, :], og_ref[rows, :],
                        lambda s_i, h: s0_ref[gi * nseq + s_i, h].T, set_st,
                        loglb_ref[...], log1m_ref[...], nw_ref[...], seq_len)
        o_ref[rows, :] = y.astype(BF16)
        return carry

    lax.fori_loop(0, q_ref.shape[0] // c, body, 0)


def _hgrn_sample(p_hg_bm, s0, layer, loglb, log1m, nw, *, bs, ts, s_prev=None, after=None):
    hgw = p_hg_bm.shape[1] // 4
    nh = hgw // HG_DIM
    nseq = max(1, 16 // ts)
    assert bs % nseq == 0
    bb = _pick_block(bs, 8, nseq)
    rows = bb * ts
    cspec = lambda c: pl.BlockSpec((rows, hgw), lambda i: (i, c))
    pspec = pl.BlockSpec((1, hgw), lambda i: (0, 0))
    sspec = pl.BlockSpec((bb, None, nh, HG_DIM, HG_DIM), lambda i: (i, layer, 0, 0, 0))
    operands = [p_hg_bm, p_hg_bm, p_hg_bm, p_hg_bm, s0, loglb, log1m, nw]
    in_specs = [cspec(0), cspec(1), cspec(2), cspec(3), sspec, pspec, pspec, pspec]
    aliases = {}
    if s_prev is not None:
        operands.append(s_prev)
        in_specs.append(pl.BlockSpec(memory_space=pl.ANY))
        aliases = {len(operands) - 1: 1}
    if after is not None:
        operands.append(after)
        in_specs.append(pl.BlockSpec(memory_space=pl.ANY))
    return pl.pallas_call(
        functools.partial(_hgrn_sample_body, seq_len=ts, nseq=nseq),
        grid=(bs // bb,),
        in_specs=in_specs,
        out_specs=[pl.BlockSpec((rows, hgw), lambda i: (i, 0)), sspec],
        out_shape=[jax.ShapeDtypeStruct((bs * ts, hgw), BF16), jax.ShapeDtypeStruct(s0.shape, F32)],
        input_output_aliases=aliases,
        compiler_params=_cparams("parallel"),
        name="hgrn_sample",
    )(*operands)


def _lru_gates(xconv, wa_ref, wx_ref, ba, bx, lam):
    l = xconv.shape[1]
    blk = l // LRU_BLOCKS
    ga, gx = [], []
    for n in range(LRU_BLOCKS):
        xb = xconv[:, n * blk:(n + 1) * blk].astype(BF16)
        ga.append(_dot(xb, wa_ref[n]))
        gx.append(_dot(xb, wx_ref[n]))
    ga = jnp.concatenate(ga, axis=1) + ba
    gx = jnp.concatenate(gx, axis=1) + bx
    log_a = -LRU_C * jax.nn.sigmoid(ga) * _softplus(-lam)
    a = jnp.exp(log_a)
    mult = jnp.sqrt(1.0 - a * a)
    return a, mult * jax.nn.sigmoid(gx) * xconv


def _gelu_tanh(x):
    c = 0.7978845608028654
    return 0.5 * x * (1.0 + jnp.tanh(c * (x + 0.044715 * (x * x * x))))


def _odd_prompt_body(gate_ref, xb_ref, conv0_ref, h0_ref, cw_ref, cb_ref, wa_ref, wx_ref, ba_ref, bx_ref, lam_ref,
                     y_ref, convo_ref, ho_ref, ext_ref, hc_ref, a_s, b_s, h_s):
    rows = xb_ref.shape[0]
    hist = SUBLANES

    @pl.when(pl.program_id(1) == 0)
    def _():
        ext_ref[0:hist, :] = conv0_ref[...]
        hc_ref[...] = h0_ref[...]

    ext_ref[hist:hist + rows, :] = xb_ref[...]
    cw = cw_ref[...]
    xconv = cw[0:1, :] * ext_ref[pl.ds(hist - (CONV_W - 1), rows), :]
    for j in range(1, CONV_W):
        xconv = xconv + cw[j:j + 1, :] * ext_ref[pl.ds(hist - (CONV_W - 1) + j, rows), :]
    xconv = cb_ref[...] + xconv
    a, b = _lru_gates(xconv, wa_ref, wx_ref, ba_ref[...], bx_ref[...], lam_ref[...])
    width = a.shape[1]
    a = a.reshape(rows // SUBLANES, SUBLANES, width)
    b = b.reshape(rows // SUBLANES, SUBLANES, width)
    r8 = lax.broadcasted_iota(jnp.int32, (1, SUBLANES, 1), 1)
    for s in (1, 2, 4):
        m = r8 >= s
        a_sh = pltpu.roll(a, s, 1)
        b_sh = pltpu.roll(b, s, 1)
        b = jnp.where(m, a * b_sh + b, b)
        a = jnp.where(m, a * a_sh, a)
    a_s[...] = a.reshape(rows, width)
    b_s[...] = b.reshape(rows, width)

    def tile(j, hc):
        r0 = pl.multiple_of(j * SUBLANES, SUBLANES)
        h = a_s[pl.ds(r0, SUBLANES), :] * hc + b_s[pl.ds(r0, SUBLANES), :]
        h_s[pl.ds(r0, SUBLANES), :] = h
        return h[SUBLANES - 1:SUBLANES, :]

    hc = lax.fori_loop(0, rows // SUBLANES, tile, hc_ref[...])
    hc_ref[...] = hc
    ho_ref[...] = hc
    y_ref[...] = (h_s[...] * _gelu_tanh(gate_ref[...])).astype(BF16)
    tail = ext_ref[rows:rows + hist, :]
    ext_ref[0:hist, :] = tail
    convo_ref[...] = tail


def _odd_prompt(p_od, conv0, h0, layer, cw, cb, wa, wx, ba, bx, lam, *, bp, tp):
    l = p_od.shape[1] // 2
    rows = _pick_block(tp, 256, SUBLANES)
    nt = tp // rows
    lsel = lambda a: pl.BlockSpec((None,) + a.shape[1:], lambda b, t: (layer,) + (0,) * (a.ndim - 1))
    return pl.pallas_call(
        _odd_prompt_body,
        grid=(bp, nt),
        in_specs=[
            pl.BlockSpec((rows, l), lambda b, t: (b * nt + t, 0)),
            pl.BlockSpec((rows, l), lambda b, t: (b * nt + t, 1)),
            pl.BlockSpec((None, SUBLANES, l), lambda b, t: (b, 0, 0)),
            pl.BlockSpec((None, 1, l), lambda b, t: (b, 0, 0)),
            lsel(cw), lsel(cb), lsel(wa), lsel(wx), lsel(ba), lsel(bx), lsel(lam),
        ],
        out_specs=[
            pl.BlockSpec((rows, l), lambda b, t: (b * nt + t, 0)),
            pl.BlockSpec((None, SUBLANES, l), lambda b, t: (b, 0, 0)),
            pl.BlockSpec((None, 1, l), lambda b, t: (b, 0, 0)),
        ],
        out_shape=[jax.ShapeDtypeStruct((p_od.shape[0], l), BF16), jax.ShapeDtypeStruct((bp, SUBLANES, l), F32),
                   jax.ShapeDtypeStruct((bp, 1, l), F32)],
        scratch_shapes=[pltpu.VMEM((rows + SUBLANES, l), F32), pltpu.VMEM((1, l), F32),
                        pltpu.VMEM((rows, l), F32), pltpu.VMEM((rows, l), F32), pltpu.VMEM((rows, l), F32)],
        compiler_params=_cparams("parallel", "arbitrary"),
        name="odd_prompt",
    )(p_od, p_od, conv0, h0, cw, cb, wa, wx, ba, bx, lam)


def _odd_sample_body(gate_ref, xb_ref, conv0_ref, h0_ref, cw_ref, cb_ref, wa_ref, wx_ref, ba_ref, bx_ref, lam_ref,
                     yall_ref, y_ref, convo_ref, ho_ref):
    @pl.when(pl.program_id(0) == 0)
    def _():
        convo_ref[...] = conv0_ref[...]
        ho_ref[...] = h0_ref[...]

    xb = xb_ref[...]
    cw = cw_ref[...]
    xconv = cw[CONV_W - 1:CONV_W, :] * xb
    for j in range(CONV_W - 1):
        xconv = xconv + cw[j:j + 1, :] * convo_ref[j]
    xconv = cb_ref[...] + xconv
    a, b = _lru_gates(xconv, wa_ref, wx_ref, ba_ref[...], bx_ref[...], lam_ref[...])
    h = a * ho_ref[...] + b
    ho_ref[...] = h
    y_ref[...] = (h * _gelu_tanh(gate_ref[...])).astype(BF16)
    for j in range(CONV_W - 2):
        convo_ref[j] = convo_ref[j + 1]
    convo_ref[CONV_W - 2] = xb


def _odd_sample(p_od, conv0_t, h0, layer, cw, cb, wa, wx, ba, bx, lam, y_all, *, mp, bs, ts):
    l = p_od.shape[1] // 2
    assert mp % bs == 0
    off = mp // bs
    lsel = lambda a: pl.BlockSpec((None,) + a.shape[1:], lambda t: (layer,) + (0,) * (a.ndim - 1))
    full = lambda a: pl.BlockSpec(a.shape, lambda t: (0,) * a.ndim)
    return pl.pallas_call(
        _odd_sample_body,
        grid=(ts,),
        in_specs=[
            pl.BlockSpec((bs, l), lambda t: (off + t, 0)),
            pl.BlockSpec((bs, l), lambda t: (off + t, 1)),
            full(conv0_t), full(h0),
            lsel(cw), lsel(cb), lsel(wa), lsel(wx), lsel(ba), lsel(bx), lsel(lam),
            pl.BlockSpec(memory_space=pl.ANY),
        ],
        out_specs=[pl.BlockSpec((bs, l), lambda t: (off + t, 0)), full(conv0_t), full(h0)],
        out_shape=[jax.ShapeDtypeStruct(y_all.shape, BF16), jax.ShapeDtypeStruct(conv0_t.shape, F32),
                   jax.ShapeDtypeStruct(h0.shape, F32)],
        input_output_aliases={11: 0},
        compiler_params=_cparams("arbitrary"),
        name="odd_sample",
    )(p_od, p_od, conv0_t, h0, cw, cb, wa, wx, ba, bx, lam, y_all)


def _pad_last(a, n):
    return jnp.pad(a, [(0, 0)] * (a.ndim - 1) + [(0, n - a.shape[-1])])


def _round_up(n, m):
    return (n + m - 1) // m * m


class _RwLayout:
    def __init__(self, rw, dw, da, dg):
        self.rw, self.dw, self.da, self.dg = rw, dw, da, dg
        self.dwp, self.dap, self.dgp = (_round_up(d, LANES) for d in (dw, da, dg))
        self.width = 3 * rw + dw + da + dg
        self.padded = 3 * rw + self.dwp + self.dap + self.dgp

    def pad(self, a):
        o = 3 * self.rw
        parts = [a[..., :o], _pad_last(a[..., o:o + self.dw], self.dwp),
                 _pad_last(a[..., o + self.dw:o + self.dw + self.da], self.dap),
                 _pad_last(a[..., o + self.dw + self.da:self.width], self.dgp)]
        return jnp.concatenate(parts, axis=-1)

    def unpad(self, a):
        o = 3 * self.rw
        parts = [a[..., :o], a[..., o:o + self.dw], a[..., o + self.dwp:o + self.dwp + self.da],
                 a[..., o + self.dwp + self.dap:o + self.dwp + self.dap + self.dg]]
        return jnp.concatenate(parts, axis=-1)


def kernel(x_prompt, x_sample, state_rwkv_shift, state_rwkv, state_hgrn, state_conv, state_lru, ffn1_norm, ffn1_w_gu, ffn1_w_down, mix_norm, ffn2_norm, ffn2_w_gu, ffn2_w_down, ev_w_in, rw_mu, rw_w0, rw_w2, rw_a0, rw_a2, rw_g2, rw_k_k, rw_k_a, rw_r_k, rw_ln_w, rw_ln_b, hg_lb, hg_norm, ev_w_out, od_w_in, conv_w, conv_b, lru_wa, lru_ba, lru_wx, lru_bx, lru_lambda, od_w_out, final_norm):
    bp, tp, d = x_prompt.shape
    bs, ts, _ = x_sample.shape
    depth = ffn1_norm.shape[0]
    n_even = ev_w_in.shape[0]
    mp, ms = bp * tp, bs * ts
    rw = rw_w0.shape[1]
    nh_rw = rw // RW_HEAD
    hgw = hg_norm.shape[1]
    nh_hg = hgw // HG_DIM
    lay = _RwLayout(rw, rw_w2.shape[1], rw_a2.shape[1], rw_g2.shape[1])
    assert bp * nh_rw * 2 == LANES and (bs * nh_rw) % LANES == 0
    dt = x_prompt.dtype

    ffn_src = {1: (ffn1_norm, ffn1_w_gu, ffn1_w_down), 2: (ffn2_norm, ffn2_w_gu, ffn2_w_down)}
    ffn_w = (ffn1_w_gu[0].astype(BF16), ffn1_w_down[0].astype(BF16))

    def ffn_step(x, l, which, ffn_w):
        norm_g = ffn_src[which][0]
        nxt = (l, 2) if which == 1 else (l + 1, 1)
        if nxt[0] >= depth:
            return _ffn(x, norm_g, l, *ffn_w)[0], None
        x, c_gu, c_dn = _ffn(x, norm_g, l, *ffn_w, next_src=(ffn_src[nxt[1]][1], ffn_src[nxt[1]][2], nxt[0]))
        return x, (c_gu, c_dn)

    ev_out = ev_w_out.astype(BF16)
    od_in, od_out = od_w_in.astype(BF16), od_w_out.astype(BF16)

    wa_bf, wx_bf = lru_wa.astype(BF16), lru_wx.astype(BF16)

    x = jnp.concatenate([x_prompt.reshape(mp, d), x_sample.transpose(1, 0, 2).reshape(ms, d)], axis=0)

    lbs = jnp.cumsum(jax.nn.softmax(hg_lb.astype(F32), axis=0), axis=0)
    lb_all = lbs - lbs[0]

    def k_lanes_p(a):
        return a.reshape(bp * nh_rw, RW_HEAD, a.shape[1]).transpose(2, 1, 0)

    def v_lanes_p_inv(a):
        return a.reshape(tp, RW_HEAD // 2, 2, bp, nh_rw).transpose(3, 0, 4, 2, 1).reshape(mp, rw)

    def k_lanes_s(a):
        a = a.reshape(bp, nh_rw, RW_HEAD, ts // bp, bs).transpose(3, 0, 2, 4, 1)
        return a.reshape(ts, RW_HEAD, bs * nh_rw)

    def k_lanes_s_inv(a):
        return a.reshape(ts, RW_HEAD, bs, nh_rw).transpose(0, 2, 3, 1).reshape(ms, rw)

    def kparam(p):
        return jnp.tile(p.reshape(nh_rw, RW_HEAD).T, (1, LANES // nh_rw))

    def vparam_p(p):
        a = p.reshape(nh_rw, 2, RW_HEAD // 2).transpose(2, 1, 0)[:, :, None, :]
        return jnp.broadcast_to(a, (RW_HEAD // 2, 2, bp, nh_rw)).reshape(RW_HEAD // 2, LANES)

    shifts_p, shifts_s, rws_p, hgs_p, convs_p, convs_s, lrus_p, lrus_s = ([] for _ in range(8))
    rw_rows = state_rwkv.astype(F32).reshape(bs, n_even, nh_rw, RW_HEAD * RW_HEAD)
    hg_new = None

    for l in range(depth):
        x, ffn_w = ffn_step(x, l, 1, ffn_w)
        if l % 2 == 0:
            e = l // 2
            w_in = ev_w_in[e]
            w_rw = lay.pad(w_in[:, :lay.width]).astype(BF16)
            w_hg = w_in[:, lay.width:].astype(BF16)
            p_rw = _norm_proj(x, mix_norm, l, w_rw, _pick_block(lay.padded, 1792, LANES))
            p_hg = _norm_proj(x, mix_norm, l, w_hg, _pick_block(4 * hgw, 1024, LANES))
            shift_p = jnp.zeros((bp, lay.padded), F32)
            shift_s = lay.pad(state_rwkv_shift[:, e].astype(F32))
            rkvwa, g_ = _rwkv_prep(
                p_rw, shift_p, shift_s, lay.pad(rw_mu[e])[None], rw_w0[e][None],
                _pad_last(rw_w2[e].T, lay.dwp).T.astype(BF16), rw_a0[e][None],
                _pad_last(rw_a2[e].T, lay.dap).T.astype(BF16),
                _pad_last(rw_g2[e].T, lay.dgp).T.astype(BF16), rw=rw, mp=mp, tp=tp, bs=bs)
            shifts_p.append(lay.unpad(jnp.concatenate([p_rw[b * tp + tp - 1:(b + 1) * tp] for b in range(bp)])))
            shifts_s.append(lay.unpad(p_rw[mp + (ts - 1) * bs:]))
            kk_p, ka_p = kparam(rw_k_k[e]), kparam(rw_k_a[e])
            rk_p = kparam(rw_r_k[e].reshape(rw))
            lb = lb_all[e][None]
            loglb, log1m = jnp.log(lb), jnp.log1p(-lb)
            nw = hg_norm[e][None]
            chunk = _pick_block(tp, HG_CHUNK, 1)
            oh_p, hs_p = _hgrn_prompt(p_hg, jnp.zeros((bp, nh_hg, HG_DIM, HG_DIM), F32), loglb, log1m, nw,
                                      bp=bp, tp=tp, chunk=chunk)
            r_p, k_p, v_p, w_p, a_p = rkvwa
            o_p, s_p = _rwkv_scan(
                k_lanes_p(r_p), k_lanes_p(w_p), k_lanes_p(k_p), k_lanes_p(a_p), k_lanes_p(v_p),
                kk_p, ka_p, rk_p, vparam_p(rw_ln_w[e]), vparam_p(rw_ln_b[e]),
                jnp.zeros((RW_HEAD // 2, RW_HEAD, LANES), F32), t=tp, tc=_pick_block(tp, 64, 1), vsplit=True,
                after=oh_p)
            rws_p.append(s_p.reshape(RW_HEAD // 2, RW_HEAD, 2, bp, nh_rw).transpose(3, 4, 2, 0, 1)
                         .reshape(bp, nh_rw, RW_HEAD, RW_HEAD))
            r_s, k_s, v_s, w_s, a_s = (a[:, tp:] for a in rkvwa)
            o_s, rw_rows = _rwkv_scan(
                k_lanes_s(r_s), k_lanes_s(w_s), k_lanes_s(k_s), k_lanes_s(a_s), k_lanes_s(v_s),
                kk_p, ka_p, rk_p, kparam(rw_ln_w[e]), kparam(rw_ln_b[e]),
                rw_rows, t=ts, tc=ts, vsplit=False, layer=e, after=o_p)
            o_rw = jnp.concatenate([v_lanes_p_inv(o_p), k_lanes_s_inv(o_s)], axis=0)
            p_hg_bm = p_hg[mp:].reshape(ts, bs, 4 * hgw).transpose(1, 0, 2).reshape(ms, 4 * hgw)
            oh_s, hg_new = _hgrn_sample(p_hg_bm, state_hgrn.astype(F32), e, loglb, log1m, nw, bs=bs, ts=ts,
                                        s_prev=hg_new, after=o_p)
            oh_s = oh_s.reshape(bs, ts, hgw).transpose(1, 0, 2).reshape(ms, hgw)
            hgs_p.append(hs_p)
            o_hg = oh_p.at[mp:].set(oh_s)
            x = _proj_res_even(x, o_rw, g_, o_hg, ev_out, e)
        else:
            o_i = l // 2
            p_od = _norm_proj(x, mix_norm, l, od_in, _pick_block(od_in.shape[-1], 1024, LANES), w_layer=o_i)
            lsz = p_od.shape[1] // 2
            vec = lambda a: a[:, None, :]
            args = (conv_w, vec(conv_b), wa_bf, wx_bf, vec(lru_ba), vec(lru_bx), vec(lru_lambda))
            conv0_p = jnp.zeros((bp, SUBLANES, lsz), F32)
            y_p, c_p, h_p = _odd_prompt(p_od, conv0_p, jnp.zeros((bp, 1, lsz), F32), o_i, *args, bp=bp, tp=tp)
            conv0_s = state_conv[:, o_i].astype(F32).transpose(1, 0, 2)
            y_all, c_s, h_s = _odd_sample(p_od, conv0_s, state_lru[:, o_i].astype(F32), o_i, *args, y_p,
                                          mp=mp, bs=bs, ts=ts)
            convs_p.append(c_p[:, SUBLANES - (CONV_W - 1):])
            convs_s.append(c_s.transpose(1, 0, 2))
            lrus_p.append(h_p[:, 0])
            lrus_s.append(h_s)
            x = _proj_res(x, y_all, od_out, o_i)
        x, ffn_w = ffn_step(x, l, 2, ffn_w)

    y_p, y_s = _final_norm(x, final_norm, mp)
    y_prompt = y_p.reshape(bp, tp, d)
    y_sample = y_s.reshape(ts, bs, d).transpose(1, 0, 2)
    st = lambda xs: jnp.stack(xs, axis=1).astype(dt)
    sample_rwkv = rw_rows.reshape(state_rwkv.shape).astype(dt)
    return (y_prompt, y_sample, st(shifts_p), st(rws_p), st(hgs_p), st(convs_p), st(lrus_p),
            st(shifts_s), sample_rwkv, hg_new.astype(dt), st(convs_s), st(lrus_s))
```

```python
import functools
import math

import jax
import jax.numpy as jnp
from jax import lax
from jax.experimental import pallas as pl
from jax.experimental.pallas import tpu as pltpu

F32 = jnp.float32
BF16 = jnp.bfloat16

NORM_EPS = 1e-6
RW_GN_EPS = 64e-5
HG_NORM_EPS = 1e-5
LRU_C = 8.0
RW_HEAD = 64
HG_DIM = 128
LRU_BLOCKS = 8
CONV_W = 4
LANES = 128
SUBLANES = 8
VMEM_LIMIT = 62 * 1024 * 1024
MAX_BM = 1088
HG_CHUNK = 64


def _cparams(*sem):
    return pltpu.CompilerParams(dimension_semantics=sem, vmem_limit_bytes=VMEM_LIMIT)


def _pick_block(n, cap, mult):
    best = None
    for d in range(mult, min(n, cap) + 1, mult):
        if n % d == 0:
            best = d
    assert best is not None, (n, cap, mult)
    return best


def _gcd(a, b):
    return math.gcd(a, b)


def _dot(a, b):
    return jnp.dot(a, b, preferred_element_type=F32)


def _dot_nt(a, b):
    return lax.dot_general(a, b, (((1,), (1,)), ((), ())), preferred_element_type=F32)


def _dot_tn(a, b):
    return lax.dot_general(a, b, (((0,), (0,)), ((), ())), preferred_element_type=F32)


def _softplus(x):
    return jnp.maximum(x, 0.0) + jnp.log1p(jnp.exp(-jnp.abs(x)))


def _neg_expm1(y):
    return 1.0 - jnp.exp(y)


def _rms_rows(x, g):
    ms = jnp.mean(x * x, axis=-1, keepdims=True)
    return x * lax.rsqrt(ms + NORM_EPS) * g


def _ffn_body(x_ref, g_ref, wg_ref, wu_ref, wd_ref, *rest, cast_next):
    if cast_next:
        sgu_ref, sdn_ref, o_ref, cgu_ref, cdn_ref, xn_ref = rest
        cgu_ref[...] = sgu_ref[...].astype(BF16)
        cdn_ref[...] = sdn_ref[...].astype(BF16)
    else:
        o_ref, xn_ref = rest

    @pl.when(pl.program_id(1) == 0)
    def _():
        x = x_ref[...]
        xn_ref[...] = _rms_rows(x, g_ref[...]).astype(BF16)
        o_ref[...] = x

    xn = xn_ref[...]
    gate = _dot(xn, wg_ref[...])
    up = _dot(xn, wu_ref[...])
    act = (0.5 * (gate * jax.nn.sigmoid(gate)) * up).astype(BF16)
    o_ref[...] += _dot(act, wd_ref[...])


def _ffn(x, norm_g, layer, w_gu, w_down, next_src=None):
    m, d = x.shape
    f = w_down.shape[0]
    bm = _pick_block(m, MAX_BM, 16)
    bf = _pick_block(f, 512, LANES)
    ni, nf = m // bm, f // bf
    in_specs = [
        pl.BlockSpec((bm, d), lambda i, j: (i, 0)),
        pl.BlockSpec((None, 1, d), lambda i, j: (layer, 0, 0)),
        pl.BlockSpec((d, bf), lambda i, j: (0, j)),
        pl.BlockSpec((d, bf), lambda i, j: (0, j + nf)),
        pl.BlockSpec((bf, d), lambda i, j: (j, 0)),
    ]
    operands = [x, norm_g[:, None, :], w_gu, w_gu, w_down]
    out_specs = [pl.BlockSpec((bm, d), lambda i, j: (i, 0))]
    out_shape = [jax.ShapeDtypeStruct((m, d), F32)]
    if next_src is not None:
        s_gu, s_dn, nl = next_src
        assert d % ni == 0 and (2 * f) % nf == 0 and (d // ni) % 16 == 0 and (d // ni) % LANES == 0
        gu_blk, dn_blk = (d // ni, 2 * f // nf), (f // nf, d // ni)
        in_specs += [pl.BlockSpec((None,) + gu_blk, lambda i, j: (nl, i, j)),
                     pl.BlockSpec((None,) + dn_blk, lambda i, j: (nl, j, i))]
        operands += [s_gu, s_dn]
        out_specs += [pl.BlockSpec(gu_blk, lambda i, j: (i, j)), pl.BlockSpec(dn_blk, lambda i, j: (j, i))]
        out_shape += [jax.ShapeDtypeStruct((d, 2 * f), BF16), jax.ShapeDtypeStruct((f, d), BF16)]
    return pl.pallas_call(
        functools.partial(_ffn_body, cast_next=next_src is not None),
        grid=(ni, nf),
        in_specs=in_specs,
        out_specs=out_specs,
        out_shape=out_shape,
        scratch_shapes=[pltpu.VMEM((bm, d), BF16)],
        compiler_params=_cparams("parallel", "arbitrary"),
        name="ffn",
    )(*operands)


def _norm_proj_body(x_ref, g_ref, w_ref, o_ref, xn_ref):
    @pl.when(pl.program_id(1) == 0)
    def _():
        xn_ref[...] = _rms_rows(x_ref[...], g_ref[...]).astype(BF16)

    o_ref[...] = _dot(xn_ref[...], w_ref[...])


def _w_spec(w, blk, idx, w_layer):
    if w.ndim == 2:
        return pl.BlockSpec(blk, idx)
    return pl.BlockSpec((None,) + blk, lambda i, j: (w_layer,) + idx(i, j))


def _norm_proj(x, norm_g, layer, w, bn, w_layer=None):
    m, d = x.shape
    n = w.shape[-1]
    bm = _pick_block(m, MAX_BM, 16)
    return pl.pallas_call(
        _norm_proj_body,
        grid=(m // bm, n // bn),
        in_specs=[
            pl.BlockSpec((bm, d), lambda i, j: (i, 0)),
            pl.BlockSpec((None, 1, d), lambda i, j: (layer, 0, 0)),
            _w_spec(w, (d, bn), lambda i, j: (0, j), w_layer),
        ],
        out_specs=pl.BlockSpec((bm, bn), lambda i, j: (i, j)),
        out_shape=jax.ShapeDtypeStruct((m, n), F32),
        scratch_shapes=[pltpu.VMEM((bm, d), BF16)],
        compiler_params=_cparams("parallel", "arbitrary"),
        name="norm_proj",
    )(x, norm_g[:, None, :], w)


def _proj_res_body(x_ref, y_ref, w_ref, o_ref):
    o_ref[...] = x_ref[...] + _dot(y_ref[...], w_ref[...])


def _proj_res(x, y, w, w_layer):
    m, d = x.shape
    k = y.shape[1]
    bm = _pick_block(m, MAX_BM, 16)
    bn = _pick_block(d, 1024, LANES)
    return pl.pallas_call(
        _proj_res_body,
        grid=(m // bm, d // bn),
        in_specs=[
            pl.BlockSpec((bm, bn), lambda i, j: (i, j)),
            pl.BlockSpec((bm, k), lambda i, j: (i, 0)),
            _w_spec(w, (k, bn), lambda i, j: (0, j), w_layer),
        ],
        out_specs=pl.BlockSpec((bm, bn), lambda i, j: (i, j)),
        out_shape=jax.ShapeDtypeStruct((m, d), F32),
        compiler_params=_cparams("parallel", "arbitrary"),
        name="proj_res",
    )(x, y, w)


def _proj_res_even_body(x_ref, opt_ref, os_ref, g_ref, ohg_ref, w1_ref, w2_ref, o_ref, *, n_prompt_blk):
    o_rw = jnp.where(pl.program_id(0) < n_prompt_blk, opt_ref[...].T, os_ref[...])
    y1 = (o_rw * g_ref[...]).astype(BF16)
    o_ref[...] = x_ref[...] + _dot(y1, w1_ref[...]) + _dot(ohg_ref[...], w2_ref[...])


def _proj_res_even(x, o_pt, o_s, g, o_hg, w, w_layer):
    m, d = x.shape
    bp, rw, tp = o_pt.shape
    ms = o_s.shape[0]
    hg = o_hg.shape[1]
    assert rw == hg and bp * tp + ms == m
    bm = _pick_block(_gcd(tp, ms), 512, LANES)
    bn = _pick_block(d, 1024, LANES)
    npb, tpb = bp * tp // bm, tp // bm

    def pt_idx(i, j):
        ic = jnp.minimum(i, npb - 1)
        return ic // tpb, 0, ic % tpb

    return pl.pallas_call(
        functools.partial(_proj_res_even_body, n_prompt_blk=npb),
        grid=(m // bm, d // bn),
        in_specs=[
            pl.BlockSpec((bm, bn), lambda i, j: (i, j)),
            pl.BlockSpec((None, rw, bm), pt_idx),
            pl.BlockSpec((bm, rw), lambda i, j: (jnp.maximum(i - npb, 0), 0)),
            pl.BlockSpec((bm, rw), lambda i, j: (i, 0)),
            pl.BlockSpec((bm, hg), lambda i, j: (i, 0)),
            _w_spec(w, (rw, bn), lambda i, j: (0, j), w_layer),
            _w_spec(w, (hg, bn), lambda i, j: (1, j), w_layer),
        ],
        out_specs=pl.BlockSpec((bm, bn), lambda i, j: (i, j)),
        out_shape=jax.ShapeDtypeStruct((m, d), F32),
        compiler_params=_cparams("parallel", "arbitrary"),
        name="proj_res_even",
    )(x, o_pt, o_s, g, o_hg, w, w)


def _final_norm_body(x_ref, g_ref, op_ref, os_ref, *, n_prompt_blk):
    y = _rms_rows(x_ref[...], g_ref[...])
    i = pl.program_id(0)

    @pl.when(i < n_prompt_blk)
    def _():
        op_ref[...] = y

    @pl.when(i >= n_prompt_blk)
    def _():
        os_ref[...] = y


def _final_norm(x, g, mp):
    m, d = x.shape
    ms = m - mp
    bm = _pick_block(_gcd(mp, ms), MAX_BM, SUBLANES)
    npb = mp // bm
    return pl.pallas_call(
        functools.partial(_final_norm_body, n_prompt_blk=npb),
        grid=(m // bm,),
        in_specs=[pl.BlockSpec((bm, d), lambda i: (i, 0)), pl.BlockSpec((1, d), lambda i: (0, 0))],
        out_specs=[pl.BlockSpec((bm, d), lambda i: (jnp.minimum(i, npb - 1), 0)),
                   pl.BlockSpec((bm, d), lambda i: (jnp.maximum(i - npb, 0), 0))],
        out_shape=[jax.ShapeDtypeStruct((mp, d), F32), jax.ShapeDtypeStruct((ms, d), F32)],
        compiler_params=_cparams("arbitrary"),
        name="final_norm",
    )(x, g[None, :])


def _rwkv_prep_body(cur_ref, tail_ref, prevs_ref, shiftp_ref, shifts_ref, mu_ref, w0_ref, w2_ref, a0_ref, a2_ref,
                    g2_ref, r_ref, k_ref, v_ref, w_ref, a_ref, g_ref, *, rw, dwp, dap, n_prompt_blk, blk_per_seq):
    i = pl.program_id(0)
    rows = cur_ref.shape[0]
    prw = cur_ref.shape[1]
    seq = jnp.minimum(i // blk_per_seq, shiftp_ref.shape[0] - 1)
    first = (i % blk_per_seq) == 0
    is_sample = i >= n_prompt_blk
    rid = lax.broadcasted_iota(jnp.int32, (rows, 1), 0)
    shift_row = shiftp_ref[pl.ds(seq, 1), :]

    def shifted(c0, c1):
        cs = slice(c0, c1)
        cur = cur_ref[:, cs]
        row0 = jnp.where(first, shift_row[:, cs], tail_ref[SUBLANES - 1:SUBLANES, cs])
        prev_p = jnp.where(rid == 0, row0, pltpu.roll(cur, 1, 0))
        prev_s = jnp.where(i == n_prompt_blk, shifts_ref[:, cs], prevs_ref[:, cs])
        prev = jnp.where(is_sample, prev_s, prev_p)
        return cur + (prev - cur) * mu_ref[:, cs]

    r_ref[...] = shifted(0, rw).T
    k_ref[...] = shifted(rw, 2 * rw).T
    v_ref[...] = shifted(2 * rw, 3 * rw).T
    o = 3 * rw
    lw = w0_ref[...] + _dot(jnp.tanh(shifted(o, o + dwp)).astype(BF16), w2_ref[...])
    w_log = -_softplus(-lw) - 0.5
    w_ref[...] = jnp.exp(-jnp.exp(w_log)).T
    a_ref[...] = jax.nn.sigmoid(a0_ref[...] + _dot(shifted(o + dwp, o + dwp + dap).astype(BF16), a2_ref[...])).T
    g_ref[...] = _dot(jax.nn.sigmoid(shifted(o + dwp + dap, prw)).astype(BF16), g2_ref[...])


def _rwkv_prep(p_rw, shift_p, shift_s, mu, w0, w2, a0, a2, g2, *, rw, mp, tp, bs):
    m, prw = p_rw.shape
    ms = m - mp
    assert mp % bs == 0 and tp % bs == 0 and ms % bs == 0
    npb = mp // bs
    dwp, dap = w2.shape[0], a2.shape[0]
    full = lambda a: pl.BlockSpec(a.shape, lambda i: (0,) * a.ndim)
    bps = tp // bs
    bp = mp // tp

    ts = ms // bs
    assert ts % bp == 0

    def out_idx(i):
        j = i - npb
        return jnp.where(i < npb, i // bps, j % bp), jnp.where(i < npb, i % bps, bps + j // bp)

    spec_o = pl.BlockSpec((rw, bs), out_idx)
    body = functools.partial(_rwkv_prep_body, rw=rw, dwp=dwp, dap=dap, n_prompt_blk=npb, blk_per_seq=tp // bs)
    outs = pl.pallas_call(
        body,
        grid=(m // bs,),
        in_specs=[
            pl.BlockSpec((bs, prw), lambda i: (i, 0)),
            pl.BlockSpec((SUBLANES, prw), lambda i: (jnp.maximum(i * (bs // SUBLANES) - 1, 0), 0)),
            pl.BlockSpec((bs, prw), lambda i: (jnp.maximum(i - 1, npb), 0)),
            full(shift_p), full(shift_s), full(mu), full(w0), full(w2), full(a0), full(a2), full(g2),
        ],
        out_specs=[spec_o] * 5 + [pl.BlockSpec((bs, rw), lambda i: (i, 0))],
        out_shape=[jax.ShapeDtypeStruct((bp * rw, tp + ms // bp), F32)] * 5 + [jax.ShapeDtypeStruct((m, rw), F32)],
        compiler_params=_cparams("arbitrary"),
        name="rwkv_prep",
    )(p_rw, p_rw, p_rw, shift_p, shift_s, mu, w0, w2, a0, a2, g2)
    return outs[:5], outs[5]


def _rwkv_scan_body(r_ref, w_ref, k_ref, a_ref, v_ref, kk_p_ref, ka_p_ref, rk_p_ref, lnw_ref, lnb_ref, s0_ref,
                    o_ref, sout_ref, s_ref, r_s, w_s, nkk_s, b_s, km_s, v_s, o_s, *, vsplit, state_rows):
    tc_len, nv, lanes = v_s.shape
    kc = s_ref.shape[1]
    ti = pl.program_id(1)

    @pl.when(ti == 0)
    def _():
        if state_rows:
            s_ref[...] = s0_ref[...].reshape(lanes, nv * kc).T.reshape(nv, kc, lanes)
        else:
            s_ref[...] = s0_ref[...]

    def widen(x):
        return jnp.concatenate([x, x], axis=-1) if vsplit else x

    if vsplit:
        v_in = v_ref[...]
        v_s[...] = jnp.concatenate([v_in[:, :nv, :], v_in[:, nv:, :]], axis=-1)
    else:
        v_s[...] = v_ref[...]
    kvec = widen(k_ref[...])
    a_ = widen(a_ref[...])
    r_s[...] = widen(r_ref[...])
    w_s[...] = widen(w_ref[...])
    kk = kvec * kk_p_ref[...][None]
    nrm = jnp.sqrt(jnp.sum(kk * kk, axis=1, keepdims=True))
    kk = kk / jnp.maximum(nrm, 1e-12)
    nkk_s[...] = -kk
    b_s[...] = kk * a_
    km_s[...] = kvec * (1.0 + (a_ - 1.0) * ka_p_ref[...][None])

    def step(t, carry):
        nkk = nkk_s[t]
        b = b_s[t]
        km = km_s[t]
        w = w_s[t]
        r = r_s[t]
        vt = v_s[t]
        for vi in range(nv):
            sv = s_ref[vi]
            sa = jnp.sum(sv * nkk, axis=0, keepdims=True)
            sv = sv * w + sa * b + vt[vi:vi + 1, :] * km
            s_ref[vi] = sv
            o_s[t, pl.ds(vi, 1), :] = jnp.sum(sv * r, axis=0, keepdims=True)
        return carry

    lax.fori_loop(0, tc_len, step, 0)

    o = o_s[...]
    n_val = nv * (2 if vsplit else 1)

    def head_sum(x):
        if vsplit:
            x2 = x.reshape(tc_len * nv, lanes)
            x = (x2 + pltpu.roll(x2, lanes // 2, 1)).reshape(tc_len, nv, lanes)
        return jnp.sum(x, axis=1, keepdims=True)

    mu = head_sum(o) / n_val
    d = o - mu
    var = head_sum(d * d) / n_val
    o = d * lax.rsqrt(var + RW_GN_EPS) * lnw_ref[...][None] + lnb_ref[...][None]
    bonus = jnp.sum(r_s[...] * km_s[...] * rk_p_ref[...][None], axis=1, keepdims=True)
    o = o + bonus * v_s[...]
    if vsplit:
        o_ref[...] = jnp.concatenate([o[:, :, :lanes // 2], o[:, :, lanes // 2:]], axis=1)
    else:
        o_ref[...] = o

    @pl.when(ti == pl.num_programs(1) - 1)
    def _():
        if state_rows:
            sout_ref[...] = s_ref[...].reshape(nv * kc, lanes).T.reshape(sout_ref.shape)
        else:
            sout_ref[...] = s_ref[...]


def _rwkv_scan(r, w, k, a, v, kk_p, ka_p, rk_p, lnw, lnb, s0, *, t, tc, vsplit, layer=None, after=None):
    _, kc, kl = r.shape
    nv = v.shape[1] // 2 if vsplit else v.shape[1]
    ln = v.shape[2] * 2 if vsplit else v.shape[2]
    g = ln // LANES
    kspec = pl.BlockSpec((tc, kc, kl // g), lambda gi, ti: (ti, 0, gi))
    vspec = pl.BlockSpec((tc, nv, LANES), lambda gi, ti: (ti, 0, gi))
    pk = pl.BlockSpec((kc, LANES), lambda gi, ti: (0, 0))
    pv = pl.BlockSpec((nv, LANES), lambda gi, ti: (0, 0))
    state_rows = not vsplit
    operands = [r, w, k, a, v, kk_p, ka_p, rk_p, lnw, lnb, s0]
    in_specs = [kspec, kspec, kspec, kspec, kspec if vsplit else vspec, pk, pk, pk, pv, pv]
    aliases = {}
    if state_rows:
        nb, _, nh, sz = s0.shape
        assert sz == nv * kc and nb * nh == ln
        sspec = pl.BlockSpec((LANES // nh, None, nh, sz), lambda gi, ti: (gi, layer, 0, 0))
        in_specs.append(sspec)
        aliases = {10: 1}
        s_shape = s0.shape
    else:
        sspec = pl.BlockSpec((nv, kc, LANES), lambda gi, ti: (0, 0, gi))
        in_specs.append(sspec)
        s_shape = (nv, kc, ln)
    if after is not None:
        operands.append(after)
        in_specs.append(pl.BlockSpec(memory_space=pl.ANY))

    def body(*refs):
        n_in = len(operands)
        ins = refs[:11]
        _rwkv_scan_body(*ins, *refs[n_in:], vsplit=vsplit, state_rows=state_rows)

    return pl.pallas_call(
        body,
        grid=(g, t // tc),
        in_specs=in_specs,
        out_specs=[kspec if vsplit else vspec, sspec],
        out_shape=[jax.ShapeDtypeStruct((t, kc, kl) if vsplit else (t, nv, ln), F32),
                   jax.ShapeDtypeStruct(s_shape, F32)],
        scratch_shapes=[pltpu.VMEM((nv, kc, LANES), F32)] + [pltpu.VMEM((tc, kc, LANES), F32)] * 5
        + [pltpu.VMEM((tc, nv, LANES), F32)] * 2,
        input_output_aliases=aliases,
        compiler_params=_cparams("parallel", "arbitrary"),
        name="rwkv_scan",
    )(*operands)


def _ref_rows(gc, j):
    c, w = gc.shape
    b = 1 << j
    nt = c // SUBLANES
    gv = gc.reshape(nt, SUBLANES, w)
    if 2 * b <= SUBLANES:
        sub = lax.broadcasted_iota(jnp.int32, (1, SUBLANES, 1), 1)
        out = None
        for g0 in range(0, SUBLANES, 2 * b):
            piece = jnp.broadcast_to(gv[:, g0 + b - 1:g0 + b, :], gv.shape)
            out = piece if out is None else jnp.where(sub >= g0, piece, out)
        return out.reshape(c, w)
    tiles_per_group = 2 * b // SUBLANES
    pieces = []
    for ti in range(nt):
        src = (ti // tiles_per_group) * tiles_per_group + b // SUBLANES - 1
        pieces.append(jnp.broadcast_to(gv[src, SUBLANES - 1:SUBLANES, :], (SUBLANES, w)))
    return jnp.concatenate(pieces, axis=0)


def _hgrn_chunk(q, f, val, og, get_st, set_st, loglb, log1mlb, nrm_w, seq_len):
    c, width = q.shape
    nh = width // HG_DIM
    nseq = c // seq_len
    lg = seq_len.bit_length() - 1
    assert (1 << lg) == seq_len and c % SUBLANES == 0
    ls = jnp.minimum(f, 0.0) - jnp.log1p(jnp.exp(-jnp.abs(f)))
    b_ = log1mlb + ls
    g = jnp.maximum(loglb, b_) + jnp.log1p(jnp.exp(-jnp.abs(loglb - b_)))
    kk = _neg_expm1(g)
    qs = q * jax.nn.sigmoid(q)
    row = lax.broadcasted_iota(jnp.int32, (c, 1), 0)
    col = lax.broadcasted_iota(jnp.int32, (1, c), 1)
    tpos = row & (seq_len - 1)
    gc = g
    for j in range(lg):
        s = 1 << j
        gc = gc + jnp.where(tpos >= s, pltpu.roll(gc, s, 0), 0.0)
    qbs, kbs, sames = [qs.astype(BF16)], [kk.astype(BF16)], [row == col]
    for j in range(lg):
        e = jnp.exp(-jnp.abs(gc - _ref_rows(gc, j)))
        is_q = ((row >> j) & 1) == 1
        pk = (jnp.where(is_q, qs, kk) * e).astype(BF16)
        qbs.append(pk)
        kbs.append(pk)
        sames.append(((row >> (j + 1)) == (col >> (j + 1))) & is_q & (((col >> j) & 1) == 0))
    vb = val.astype(BF16)
    qg = (qs * jnp.exp(gc)).astype(BF16)
    k2s, eglast, in_seq = [], [], []
    for s_i in range(nseq):
        glast = gc[s_i * seq_len + seq_len - 1:s_i * seq_len + seq_len, :]
        eglast.append(jnp.exp(glast))
        if nseq == 1:
            in_seq.append(None)
            k2s.append((kk * jnp.exp(glast - gc)).astype(BF16))
        else:
            in_s = (row >> lg) == s_i
            in_seq.append(in_s)
            k2s.append(jnp.where(in_s, kk * jnp.exp(jnp.minimum(glast - gc, 0.0)), 0.0).astype(BF16))
    outs = []
    for h in range(nh):
        cs = slice(h * HG_DIM, (h + 1) * HG_DIM)
        amat = None
        for qb, kb, same in zip(qbs, kbs, sames):
            term = jnp.where(same, _dot_nt(qb[:, cs], kb[:, cs]), 0.0)
            amat = term if amat is None else amat + term
        o = _dot(amat.astype(BF16), vb[:, cs])
        for s_i in range(nseq):
            st = get_st(s_i, h)
            o_int = _dot_nt(qg[:, cs], st.astype(BF16))
            o = o + (o_int if nseq == 1 else jnp.where(in_seq[s_i], o_int, 0.0))
            set_st(s_i, h, st * eglast[s_i][:, cs] + _dot_tn(vb[:, cs], k2s[s_i][:, cs]))
        outs.append(o * lax.rsqrt(jnp.mean(o * o, axis=-1, keepdims=True) + HG_NORM_EPS))
    on = jnp.concatenate(outs, axis=1) * nrm_w
    return on * (og * jax.nn.sigmoid(og))


def _hgrn_prompt_body(q_ref, f_ref, i_ref, og_ref, s0_ref, loglb_ref, log1m_ref, nw_ref, o_ref, sout_ref, st_ref,
                      *, chunk):
    nh = st_ref.shape[0]
    tb = pl.program_id(1)

    @pl.when(tb == 0)
    def _():
        for h in range(nh):
            st_ref[h] = s0_ref[h].T

    def set_st(s_i, h, v):
        st_ref[h] = v

    def body(ci, carry):
        rows = pl.ds(pl.multiple_of(ci * chunk, chunk), chunk)
        y = _hgrn_chunk(q_ref[rows, :], f_ref[rows, :], i_ref[rows, :], og_ref[rows, :],
                        lambda s_i, h: st_ref[h], set_st, loglb_ref[...], log1m_ref[...], nw_ref[...], chunk)
        o_ref[rows, :] = y.astype(BF16)
        return carry

    lax.fori_loop(0, q_ref.shape[0] // chunk, body, 0)

    @pl.when(tb == pl.num_programs(1) - 1)
    def _():
        for h in range(nh):
            sout_ref[h] = st_ref[h].T


def _hgrn_prompt(p_hg, s0, loglb, log1m, nw, *, bp, tp, chunk):
    hgw = p_hg.shape[1] // 4
    nh = hgw // HG_DIM
    tb = _pick_block(tp, 512, chunk)
    nt = tp // tb
    cspec = lambda c: pl.BlockSpec((tb, hgw), lambda b, t: (b * nt + t, c))
    pspec = pl.BlockSpec((1, hgw), lambda b, t: (0, 0))
    sspec = pl.BlockSpec((None, nh, HG_DIM, HG_DIM), lambda b, t: (b, 0, 0, 0))
    return pl.pallas_call(
        functools.partial(_hgrn_prompt_body, chunk=chunk),
        grid=(bp, nt),
        in_specs=[cspec(0), cspec(1), cspec(2), cspec(3), sspec, pspec, pspec, pspec],
        out_specs=[pl.BlockSpec((tb, hgw), lambda b, t: (b * nt + t, 0)), sspec],
        out_shape=[jax.ShapeDtypeStruct((p_hg.shape[0], hgw), BF16), jax.ShapeDtypeStruct(s0.shape, F32)],
        scratch_shapes=[pltpu.VMEM((nh, HG_DIM, HG_DIM), F32)],
        compiler_params=_cparams("parallel", "arbitrary"),
        name="hgrn_prompt",
    )(p_hg, p_hg, p_hg, p_hg, s0, loglb, log1m, nw)


def _hgrn_sample_body(q_ref, f_ref, i_ref, og_ref, s0_ref, loglb_ref, log1m_ref, nw_ref, *rest, seq_len, nseq):
    o_ref, sout_ref = rest[-2:]
    c = seq_len * nseq

    def body(gi, carry):
        rows = pl.ds(pl.multiple_of(gi * c, c), c)

        def set_st(s_i, h, v):
            sout_ref[gi * nseq + s_i, h] = v.T

        y = _hgrn_chunk(q_ref[rows, :], f_ref[rows, :], i_ref[rows, :], og_ref[rows, :],
                        lambda s_i, h: s0_ref[gi * nseq + s_i, h].T, set_st,
                        loglb_ref[...], log1m_ref[...], nw_ref[...], seq_len)
        o_ref[rows, :] = y.astype(BF16)
        return carry

    lax.fori_loop(0, q_ref.shape[0] // c, body, 0)


def _hgrn_sample(p_hg_bm, s0, layer, loglb, log1m, nw, *, bs, ts, s_prev=None, after=None):
    hgw = p_hg_bm.shape[1] // 4
    nh = hgw // HG_DIM
    nseq = max(1, 16 // ts)
    assert bs % nseq == 0
    bb = _pick_block(bs, 8, nseq)
    rows = bb * ts
    cspec = lambda c: pl.BlockSpec((rows, hgw), lambda i: (i, c))
    pspec = pl.BlockSpec((1, hgw), lambda i: (0, 0))
    sspec = pl.BlockSpec((bb, None, nh, HG_DIM, HG_DIM), lambda i: (i, layer, 0, 0, 0))
    operands = [p_hg_bm, p_hg_bm, p_hg_bm, p_hg_bm, s0, loglb, log1m, nw]
    in_specs = [cspec(0), cspec(1), cspec(2), cspec(3), sspec, pspec, pspec, pspec]
    aliases = {}
    if s_prev is not None:
        operands.append(s_prev)
        in_specs.append(pl.BlockSpec(memory_space=pl.ANY))
        aliases = {len(operands) - 1: 1}
    if after is not None:
        operands.append(after)
        in_specs.append(pl.BlockSpec(memory_space=pl.ANY))
    return pl.pallas_call(
        functools.partial(_hgrn_sample_body, seq_len=ts, nseq=nseq),
        grid=(bs // bb,),
        in_specs=in_specs,
        out_specs=[pl.BlockSpec((rows, hgw), lambda i: (i, 0)), sspec],
        out_shape=[jax.ShapeDtypeStruct((bs * ts, hgw), BF16), jax.ShapeDtypeStruct(s0.shape, F32)],
        input_output_aliases=aliases,
        compiler_params=_cparams("parallel"),
        name="hgrn_sample",
    )(*operands)


def _lru_gates(xconv, wa_ref, wx_ref, ba, bx, lam):
    l = xconv.shape[1]
    blk = l // LRU_BLOCKS
    ga, gx = [], []
    for n in range(LRU_BLOCKS):
        xb = xconv[:, n * blk:(n + 1) * blk].astype(BF16)
        ga.append(_dot(xb, wa_ref[n]))
        gx.append(_dot(xb, wx_ref[n]))
    ga = jnp.concatenate(ga, axis=1) + ba
    gx = jnp.concatenate(gx, axis=1) + bx
    log_a = -LRU_C * jax.nn.sigmoid(ga) * _softplus(-lam)
    a = jnp.exp(log_a)
    mult = jnp.sqrt(1.0 - a * a)
    return a, mult * jax.nn.sigmoid(gx) * xconv


def _gelu_tanh(x):
    c = 0.7978845608028654
    return 0.5 * x * (1.0 + jnp.tanh(c * (x + 0.044715 * (x * x * x))))


def _odd_prompt_body(gate_ref, xb_ref, conv0_ref, h0_ref, cw_ref, cb_ref, wa_ref, wx_ref, ba_ref, bx_ref, lam_ref,
                     y_ref, convo_ref, ho_ref, ext_ref, hc_ref, a_s, b_s, h_s):
    rows = xb_ref.shape[0]
    hist = SUBLANES

    @pl.when(pl.program_id(1) == 0)
    def _():
        ext_ref[0:hist, :] = conv0_ref[...]
        hc_ref[...] = h0_ref[...]

    ext_ref[hist:hist + rows, :] = xb_ref[...]
    cw = cw_ref[...]
    xconv = cw[0:1, :] * ext_ref[pl.ds(hist - (CONV_W - 1), rows), :]
    for j in range(1, CONV_W):
        xconv = xconv + cw[j:j + 1, :] * ext_ref[pl.ds(hist - (CONV_W - 1) + j, rows), :]
    xconv = cb_ref[...] + xconv
    a, b = _lru_gates(xconv, wa_ref, wx_ref, ba_ref[...], bx_ref[...], lam_ref[...])
    width = a.shape[1]
    a = a.reshape(rows // SUBLANES, SUBLANES, width)
    b = b.reshape(rows // SUBLANES, SUBLANES, width)
    r8 = lax.broadcasted_iota(jnp.int32, (1, SUBLANES, 1), 1)
    for s in (1, 2, 4):
        m = r8 >= s
        a_sh = pltpu.roll(a, s, 1)
        b_sh = pltpu.roll(b, s, 1)
        b = jnp.where(m, a * b_sh + b, b)
        a = jnp.where(m, a * a_sh, a)
    a_s[...] = a.reshape(rows, width)
    b_s[...] = b.reshape(rows, width)

    def tile(j, hc):
        r0 = pl.multiple_of(j * SUBLANES, SUBLANES)
        h = a_s[pl.ds(r0, SUBLANES), :] * hc + b_s[pl.ds(r0, SUBLANES), :]
        h_s[pl.ds(r0, SUBLANES), :] = h
        return h[SUBLANES - 1:SUBLANES, :]

    hc = lax.fori_loop(0, rows // SUBLANES, tile, hc_ref[...])
    hc_ref[...] = hc
    ho_ref[...] = hc
    y_ref[...] = (h_s[...] * _gelu_tanh(gate_ref[...])).astype(BF16)
    tail = ext_ref[rows:rows + hist, :]
    ext_ref[0:hist, :] = tail
    convo_ref[...] = tail


def _odd_prompt(p_od, conv0, h0, layer, cw, cb, wa, wx, ba, bx, lam, *, bp, tp):
    l = p_od.shape[1] // 2
    rows = _pick_block(tp, 256, SUBLANES)
    nt = tp // rows
    lsel = lambda a: pl.BlockSpec((None,) + a.shape[1:], lambda b, t: (layer,) + (0,) * (a.ndim - 1))
    return pl.pallas_call(
        _odd_prompt_body,
        grid=(bp, nt),
        in_specs=[
            pl.BlockSpec((rows, l), lambda b, t: (b * nt + t, 0)),
            pl.BlockSpec((rows, l), lambda b, t: (b * nt + t, 1)),
            pl.BlockSpec((None, SUBLANES, l), lambda b, t: (b, 0, 0)),
            pl.BlockSpec((None, 1, l), lambda b, t: (b, 0, 0)),
            lsel(cw), lsel(cb), lsel(wa), lsel(wx), lsel(ba), lsel(bx), lsel(lam),
        ],
        out_specs=[
            pl.BlockSpec((rows, l), lambda b, t: (b * nt + t, 0)),
            pl.BlockSpec((None, SUBLANES, l), lambda b, t: (b, 0, 0)),
            pl.BlockSpec((None, 1, l), lambda b, t: (b, 0, 0)),
        ],
        out_shape=[jax.ShapeDtypeStruct((p_od.shape[0], l), BF16), jax.ShapeDtypeStruct((bp, SUBLANES, l), F32),
                   jax.ShapeDtypeStruct((bp, 1, l), F32)],
        scratch_shapes=[pltpu.VMEM((rows + SUBLANES, l), F32), pltpu.VMEM((1, l), F32),
                        pltpu.VMEM((rows, l), F32), pltpu.VMEM((rows, l), F32), pltpu.VMEM((rows, l), F32)],
        compiler_params=_cparams("parallel", "arbitrary"),
        name="odd_prompt",
    )(p_od, p_od, conv0, h0, cw, cb, wa, wx, ba, bx, lam)


def _odd_sample_body(gate_ref, xb_ref, conv0_ref, h0_ref, cw_ref, cb_ref, wa_ref, wx_ref, ba_ref, bx_ref, lam_ref,
                     yall_ref, y_ref, convo_ref, ho_ref):
    @pl.when(pl.program_id(0) == 0)
    def _():
        convo_ref[...] = conv0_ref[...]
        ho_ref[...] = h0_ref[...]

    xb = xb_ref[...]
    cw = cw_ref[...]
    xconv = cw[CONV_W - 1:CONV_W, :] * xb
    for j in range(CONV_W - 1):
        xconv = xconv + cw[j:j + 1, :] * convo_ref[j]
    xconv = cb_ref[...] + xconv
    a, b = _lru_gates(xconv, wa_ref, wx_ref, ba_ref[...], bx_ref[...], lam_ref[...])
    h = a * ho_ref[...] + b
    ho_ref[...] = h
    y_ref[...] = (h * _gelu_tanh(gate_ref[...])).astype(BF16)
    for j in range(CONV_W - 2):
        convo_ref[j] = convo_ref[j + 1]
    convo_ref[CONV_W - 2] = xb


def _odd_sample(p_od, conv0_t, h0, layer, cw, cb, wa, wx, ba, bx, lam, y_all, *, mp, bs, ts):
    l = p_od.shape[1] // 2
    assert mp % bs == 0
    off = mp // bs
    lsel = lambda a: pl.BlockSpec((None,) + a.shape[1:], lambda t: (layer,) + (0,) * (a.ndim - 1))
    full = lambda a: pl.BlockSpec(a.shape, lambda t: (0,) * a.ndim)
    return pl.pallas_call(
        _odd_sample_body,
        grid=(ts,),
        in_specs=[
            pl.BlockSpec((bs, l), lambda t: (off + t, 0)),
            pl.BlockSpec((bs, l), lambda t: (off + t, 1)),
            full(conv0_t), full(h0),
            lsel(cw), lsel(cb), lsel(wa), lsel(wx), lsel(ba), lsel(bx), lsel(lam),
            pl.BlockSpec(memory_space=pl.ANY),
        ],
        out_specs=[pl.BlockSpec((bs, l), lambda t: (off + t, 0)), full(conv0_t), full(h0)],
        out_shape=[jax.ShapeDtypeStruct(y_all.shape, BF16), jax.ShapeDtypeStruct(conv0_t.shape, F32),
                   jax.ShapeDtypeStruct(h0.shape, F32)],
        input_output_aliases={11: 0},
        compiler_params=_cparams("arbitrary"),
        name="odd_sample",
    )(p_od, p_od, conv0_t, h0, cw, cb, wa, wx, ba, bx, lam, y_all)


def _pad_last(a, n):
    return jnp.pad(a, [(0, 0)] * (a.ndim - 1) + [(0, n - a.shape[-1])])


def _round_up(n, m):
    return (n + m - 1) // m * m


class _RwLayout:
    def __init__(self, rw, dw, da, dg):
        self.rw, self.dw, self.da, self.dg = rw, dw, da, dg
        self.dwp, self.dap, self.dgp = (_round_up(d, LANES) for d in (dw, da, dg))
        self.width = 3 * rw + dw + da + dg
        self.padded = 3 * rw + self.dwp + self.dap + self.dgp

    def pad(self, a):
        o = 3 * self.rw
        parts = [a[..., :o], _pad_last(a[..., o:o + self.dw], self.dwp),
                 _pad_last(a[..., o + self.dw:o + self.dw + self.da], self.dap),
                 _pad_last(a[..., o + self.dw + self.da:self.width], self.dgp)]
        return jnp.concatenate(parts, axis=-1)

    def unpad(self, a):
        o = 3 * self.rw
        parts = [a[..., :o], a[..., o:o + self.dw], a[..., o + self.dwp:o + self.dwp + self.da],
                 a[..., o + self.dwp + self.dap:o + self.dwp + self.dap + self.dg]]
        return jnp.concatenate(parts, axis=-1)


def kernel(x_prompt, x_sample, state_rwkv_shift, state_rwkv, state_hgrn, state_conv, state_lru, ffn1_norm, ffn1_w_gu, ffn1_w_down, mix_norm, ffn2_norm, ffn2_w_gu, ffn2_w_down, ev_w_in, rw_mu, rw_w0, rw_w2, rw_a0, rw_a2, rw_g2, rw_k_k, rw_k_a, rw_r_k, rw_ln_w, rw_ln_b, hg_lb, hg_norm, ev_w_out, od_w_in, conv_w, conv_b, lru_wa, lru_ba, lru_wx, lru_bx, lru_lambda, od_w_out, final_norm):
    bp, tp, d = x_prompt.shape
    bs, ts, _ = x_sample.shape
    depth = ffn1_norm.shape[0]
    n_even = ev_w_in.shape[0]
    mp, ms = bp * tp, bs * ts
    rw = rw_w0.shape[1]
    nh_rw = rw // RW_HEAD
    hgw = hg_norm.shape[1]
    nh_hg = hgw // HG_DIM
    lay = _RwLayout(rw, rw_w2.shape[1], rw_a2.shape[1], rw_g2.shape[1])
    assert bp * nh_rw * 2 == LANES and (bs * nh_rw) % LANES == 0
    dt = x_prompt.dtype

    ffn_src = {1: (ffn1_norm, ffn1_w_gu, ffn1_w_down), 2: (ffn2_norm, ffn2_w_gu, ffn2_w_down)}
    ffn_w = (ffn1_w_gu[0].astype(BF16), ffn1_w_down[0].astype(BF16))

    def ffn_step(x, l, which, ffn_w):
        norm_g = ffn_src[which][0]
        nxt = (l, 2) if which == 1 else (l + 1, 1)
        if nxt[0] >= depth:
            return _ffn(x, norm_g, l, *ffn_w)[0], None
        x, c_gu, c_dn = _ffn(x, norm_g, l, *ffn_w, next_src=(ffn_src[nxt[1]][1], ffn_src[nxt[1]][2], nxt[0]))
        return x, (c_gu, c_dn)

    ev_out = ev_w_out.astype(BF16)
    od_in, od_out = od_w_in.astype(BF16), od_w_out.astype(BF16)

    wa_bf, wx_bf = lru_wa.astype(BF16), lru_wx.astype(BF16)

    x = jnp.concatenate([x_prompt.reshape(mp, d), x_sample.transpose(1, 0, 2).reshape(ms, d)], axis=0)

    lbs = jnp.cumsum(jax.nn.softmax(hg_lb.astype(F32), axis=0), axis=0)
    lb_all = lbs - lbs[0]

    def k_lanes_p(a):
        return a.reshape(bp * nh_rw, RW_HEAD, a.shape[1]).transpose(2, 1, 0)

    def k_lanes_p_inv(a):
        return a.transpose(2, 1, 0).reshape(bp, rw, tp)

    def k_lanes_s(a):
        a = a.reshape(bp, nh_rw, RW_HEAD, ts // bp, bs).transpose(3, 0, 2, 4, 1)
        return a.reshape(ts, RW_HEAD, bs * nh_rw)

    def k_lanes_s_inv(a):
        return a.reshape(ts, RW_HEAD, bs, nh_rw).transpose(0, 2, 3, 1).reshape(ms, rw)

    def kparam(p):
        return jnp.tile(p.reshape(nh_rw, RW_HEAD).T, (1, LANES // nh_rw))

    def vparam_p(p):
        a = p.reshape(nh_rw, 2, RW_HEAD // 2).transpose(2, 1, 0)[:, :, None, :]
        return jnp.broadcast_to(a, (RW_HEAD // 2, 2, bp, nh_rw)).reshape(RW_HEAD // 2, LANES)

    shifts_p, shifts_s, rws_p, hgs_p, convs_p, convs_s, lrus_p, lrus_s = ([] for _ in range(8))
    rw_rows = state_rwkv.astype(F32).reshape(bs, n_even, nh_rw, RW_HEAD * RW_HEAD)
    hg_new = None

    for l in range(depth):
        x, ffn_w = ffn_step(x, l, 1, ffn_w)
        if l % 2 == 0:
            e = l // 2
            w_in = ev_w_in[e]
            w_rw = lay.pad(w_in[:, :lay.width]).astype(BF16)
            w_hg = w_in[:, lay.width:].astype(BF16)
            p_rw = _norm_proj(x, mix_norm, l, w_rw, _pick_block(lay.padded, 1792, LANES))
            p_hg = _norm_proj(x, mix_norm, l, w_hg, _pick_block(4 * hgw, 1024, LANES))
            shift_p = jnp.zeros((bp, lay.padded), F32)
            shift_s = lay.pad(state_rwkv_shift[:, e].astype(F32))
            rkvwa, g_ = _rwkv_prep(
                p_rw, shift_p, shift_s, lay.pad(rw_mu[e])[None], rw_w0[e][None],
                _pad_last(rw_w2[e].T, lay.dwp).T.astype(BF16), rw_a0[e][None],
                _pad_last(rw_a2[e].T, lay.dap).T.astype(BF16),
                _pad_last(rw_g2[e].T, lay.dgp).T.astype(BF16), rw=rw, mp=mp, tp=tp, bs=bs)
            shifts_p.append(lay.unpad(jnp.concatenate([p_rw[b * tp + tp - 1:(b + 1) * tp] for b in range(bp)])))
            shifts_s.append(lay.unpad(p_rw[mp + (ts - 1) * bs:]))
            kk_p, ka_p = kparam(rw_k_k[e]), kparam(rw_k_a[e])
            rk_p = kparam(rw_r_k[e].reshape(rw))
            lb = lb_all[e][None]
            loglb, log1m = jnp.log(lb), jnp.log1p(-lb)
            nw = hg_norm[e][None]
            chunk = _pick_block(tp, HG_CHUNK, 1)
            oh_p, hs_p = _hgrn_prompt(p_hg, jnp.zeros((bp, nh_hg, HG_DIM, HG_DIM), F32), loglb, log1m, nw,
                                      bp=bp, tp=tp, chunk=chunk)
            r_p, k_p, v_p, w_p, a_p = rkvwa
            o_p, s_p = _rwkv_scan(
                k_lanes_p(r_p), k_lanes_p(w_p), k_lanes_p(k_p), k_lanes_p(a_p), k_lanes_p(v_p),
                kk_p, ka_p, rk_p, vparam_p(rw_ln_w[e]), vparam_p(rw_ln_b[e]),
                jnp.zeros((RW_HEAD // 2, RW_HEAD, LANES), F32), t=tp, tc=_pick_block(tp, 64, 1), vsplit=True,
                after=oh_p)
            rws_p.append(s_p.reshape(RW_HEAD // 2, RW_HEAD, 2, bp, nh_rw).transpose(3, 4, 2, 0, 1)
                         .reshape(bp, nh_rw, RW_HEAD, RW_HEAD))
            r_s, k_s, v_s, w_s, a_s = (a[:, tp:] for a in rkvwa)
            o_s, rw_rows = _rwkv_scan(
                k_lanes_s(r_s), k_lanes_s(w_s), k_lanes_s(k_s), k_lanes_s(a_s), k_lanes_s(v_s),
                kk_p, ka_p, rk_p, kparam(rw_ln_w[e]), kparam(rw_ln_b[e]),
                rw_rows, t=ts, tc=ts, vsplit=False, layer=e, after=o_p)
            o_pt, o_st = k_lanes_p_inv(o_p), k_lanes_s_inv(o_s)
            p_hg_bm = p_hg[mp:].reshape(ts, bs, 4 * hgw).transpose(1, 0, 2).reshape(ms, 4 * hgw)
            oh_s, hg_new = _hgrn_sample(p_hg_bm, state_hgrn.astype(F32), e, loglb, log1m, nw, bs=bs, ts=ts,
                                        s_prev=hg_new, after=o_p)
            oh_s = oh_s.reshape(bs, ts, hgw).transpose(1, 0, 2).reshape(ms, hgw)
            hgs_p.append(hs_p)
            o_hg = oh_p.at[mp:].set(oh_s)
            x = _proj_res_even(x, o_pt, o_st, g_, o_hg, ev_out, e)
        else:
            o_i = l // 2
            p_od = _norm_proj(x, mix_norm, l, od_in, _pick_block(od_in.shape[-1], 1024, LANES), w_layer=o_i)
            lsz = p_od.shape[1] // 2
            vec = lambda a: a[:, None, :]
            args = (conv_w, vec(conv_b), wa_bf, wx_bf, vec(lru_ba), vec(lru_bx), vec(lru_lambda))
            conv0_p = jnp.zeros((bp, SUBLANES, lsz), F32)
            y_p, c_p, h_p = _odd_prompt(p_od, conv0_p, jnp.zeros((bp, 1, lsz), F32), o_i, *args, bp=bp, tp=tp)
            conv0_s = state_conv[:, o_i].astype(F32).transpose(1, 0, 2)
            y_all, c_s, h_s = _odd_sample(p_od, conv0_s, state_lru[:, o_i].astype(F32), o_i, *args, y_p,
                                          mp=mp, bs=bs, ts=ts)
            convs_p.append(c_p[:, SUBLANES - (CONV_W - 1):])
            convs_s.append(c_s.transpose(1, 0, 2))
            lrus_p.append(h_p[:, 0])
            lrus_s.append(h_s)
            x = _proj_res(x, y_all, od_out, o_i)
        x, ffn_w = ffn_step(x, l, 2, ffn_w)

    y_p, y_s = _final_norm(x, final_norm, mp)
    y_prompt = y_p.reshape(bp, tp, d)
    y_sample = y_s.reshape(ts, bs, d).transpose(1, 0, 2)
    st = lambda xs: jnp.stack(xs, axis=1).astype(dt)
    sample_rwkv = rw_rows.reshape(state_rwkv.shape).astype(dt)
    return (y_prompt, y_sample, st(shifts_p), st(rws_p), st(hgs_p), st(convs_p), st(lrus_p),
            st(shifts_s), sample_rwkv, hg_new.astype(dt), st(convs_s), st(lrus_s))
```

```python
import functools
import math

import jax
import jax.numpy as jnp
from jax import lax
from jax.experimental import pallas as pl
from jax.experimental.pallas import tpu as pltpu

F32 = jnp.float32
BF16 = jnp.bfloat16

NORM_EPS = 1e-6
RW_GN_EPS = 64e-5
HG_NORM_EPS = 1e-5
LRU_C = 8.0
RW_HEAD = 64
HG_DIM = 128
LRU_BLOCKS = 8
CONV_W = 4
LANES = 128
SUBLANES = 8
VMEM_LIMIT = 62 * 1024 * 1024
MAX_BM = 1088
HG_CHUNK = 64


def _cparams(*sem):
    return pltpu.CompilerParams(dimension_semantics=sem, vmem_limit_bytes=VMEM_LIMIT)


def _pick_block(n, cap, mult):
    best = None
    for d in range(mult, min(n, cap) + 1, mult):
        if n % d == 0:
            best = d
    assert best is not None, (n, cap, mult)
    return best


def _gcd(a, b):
    return math.gcd(a, b)


def _dot(a, b):
    return jnp.dot(a, b, preferred_element_type=F32)


def _dot_nt(a, b):
    return lax.dot_general(a, b, (((1,), (1,)), ((), ())), preferred_element_type=F32)


def _dot_tn(a, b):
    return lax.dot_general(a, b, (((0,), (0,)), ((), ())), preferred_element_type=F32)


def _softplus(x):
    return jnp.maximum(x, 0.0) + jnp.log1p(jnp.exp(-jnp.abs(x)))


def _neg_expm1(y):
    return 1.0 - jnp.exp(y)


def _rms_rows(x, g):
    ms = jnp.mean(x * x, axis=-1, keepdims=True)
    return x * lax.rsqrt(ms + NORM_EPS) * g


def _ffn_body(x_ref, g_ref, wg_ref, wu_ref, wd_ref, *rest, cast_next):
    if cast_next:
        sgu_ref, sdn_ref, o_ref, cgu_ref, cdn_ref, xn_ref = rest
        cgu_ref[...] = sgu_ref[...].astype(BF16)
        cdn_ref[...] = sdn_ref[...].astype(BF16)
    else:
        o_ref, xn_ref = rest

    @pl.when(pl.program_id(1) == 0)
    def _():
        x = x_ref[...]
        xn_ref[...] = _rms_rows(x, g_ref[...]).astype(BF16)
        o_ref[...] = x

    xn = xn_ref[...]
    gate = _dot(xn, wg_ref[...])
    up = _dot(xn, wu_ref[...])
    act = (0.5 * (gate * jax.nn.sigmoid(gate)) * up).astype(BF16)
    o_ref[...] += _dot(act, wd_ref[...])


def _ffn(x, norm_g, layer, w_gu, w_down, next_src=None):
    m, d = x.shape
    f = w_down.shape[0]
    bm = _pick_block(m, MAX_BM, 16)
    bf = _pick_block(f, 512, LANES)
    ni, nf = m // bm, f // bf
    in_specs = [
        pl.BlockSpec((bm, d), lambda i, j: (i, 0)),
        pl.BlockSpec((None, 1, d), lambda i, j: (layer, 0, 0)),
        pl.BlockSpec((d, bf), lambda i, j: (0, j)),
        pl.BlockSpec((d, bf), lambda i, j: (0, j + nf)),
        pl.BlockSpec((bf, d), lambda i, j: (j, 0)),
    ]
    operands = [x, norm_g[:, None, :], w_gu, w_gu, w_down]
    out_specs = [pl.BlockSpec((bm, d), lambda i, j: (i, 0))]
    out_shape = [jax.ShapeDtypeStruct((m, d), F32)]
    if next_src is not None:
        s_gu, s_dn, nl = next_src
        assert d % ni == 0 and (2 * f) % nf == 0 and (d // ni) % 16 == 0 and (d // ni) % LANES == 0
        gu_blk, dn_blk = (d // ni, 2 * f // nf), (f // nf, d // ni)
        in_specs += [pl.BlockSpec((None,) + gu_blk, lambda i, j: (nl, i, j)),
                     pl.BlockSpec((None,) + dn_blk, lambda i, j: (nl, j, i))]
        operands += [s_gu, s_dn]
        out_specs += [pl.BlockSpec(gu_blk, lambda i, j: (i, j)), pl.BlockSpec(dn_blk, lambda i, j: (j, i))]
        out_shape += [jax.ShapeDtypeStruct((d, 2 * f), BF16), jax.ShapeDtypeStruct((f, d), BF16)]
    return pl.pallas_call(
        functools.partial(_ffn_body, cast_next=next_src is not None),
        grid=(ni, nf),
        in_specs=in_specs,
        out_specs=out_specs,
        out_shape=out_shape,
        scratch_shapes=[pltpu.VMEM((bm, d), BF16)],
        compiler_params=_cparams("parallel", "arbitrary"),
        name="ffn",
    )(*operands)


def _norm_proj_body(x_ref, g_ref, w_ref, o_ref, xn_ref):
    @pl.when(pl.program_id(1) == 0)
    def _():
        xn_ref[...] = _rms_rows(x_ref[...], g_ref[...]).astype(BF16)

    o_ref[...] = _dot(xn_ref[...], w_ref[...])


def _w_spec(w, blk, idx, w_layer):
    if w.ndim == 2:
        return pl.BlockSpec(blk, idx)
    return pl.BlockSpec((None,) + blk, lambda i, j: (w_layer,) + idx(i, j))


def _norm_proj(x, norm_g, layer, w, bn, w_layer=None):
    m, d = x.shape
    n = w.shape[-1]
    bm = _pick_block(m, MAX_BM, 16)
    return pl.pallas_call(
        _norm_proj_body,
        grid=(m // bm, n // bn),
        in_specs=[
            pl.BlockSpec((bm, d), lambda i, j: (i, 0)),
            pl.BlockSpec((None, 1, d), lambda i, j: (layer, 0, 0)),
            _w_spec(w, (d, bn), lambda i, j: (0, j), w_layer),
        ],
        out_specs=pl.BlockSpec((bm, bn), lambda i, j: (i, j)),
        out_shape=jax.ShapeDtypeStruct((m, n), F32),
        scratch_shapes=[pltpu.VMEM((bm, d), BF16)],
        compiler_params=_cparams("parallel", "arbitrary"),
        name="norm_proj",
    )(x, norm_g[:, None, :], w)


def _proj_res_body(x_ref, y_ref, w_ref, o_ref):
    o_ref[...] = x_ref[...] + _dot(y_ref[...], w_ref[...])


def _proj_res(x, y, w, w_layer):
    m, d = x.shape
    k = y.shape[1]
    bm = _pick_block(m, MAX_BM, 16)
    bn = _pick_block(d, 1024, LANES)
    return pl.pallas_call(
        _proj_res_body,
        grid=(m // bm, d // bn),
        in_specs=[
            pl.BlockSpec((bm, bn), lambda i, j: (i, j)),
            pl.BlockSpec((bm, k), lambda i, j: (i, 0)),
            _w_spec(w, (k, bn), lambda i, j: (0, j), w_layer),
        ],
        out_specs=pl.BlockSpec((bm, bn), lambda i, j: (i, j)),
        out_shape=jax.ShapeDtypeStruct((m, d), F32),
        compiler_params=_cparams("parallel", "arbitrary"),
        name="proj_res",
    )(x, y, w)


def _proj_res_even_body(x_ref, opt_ref, os_ref, g_ref, ohg_ref, w1_ref, w2_ref, o_ref, *, n_prompt_blk):
    o_rw = jnp.where(pl.program_id(0) < n_prompt_blk, opt_ref[...].T, os_ref[...])
    y1 = (o_rw * g_ref[...]).astype(BF16)
    o_ref[...] = x_ref[...] + _dot(y1, w1_ref[...]) + _dot(ohg_ref[...], w2_ref[...])


def _proj_res_even(x, o_pt, o_s, g, o_hg, w, w_layer):
    m, d = x.shape
    bp, rw, tp = o_pt.shape
    ms = o_s.shape[0]
    hg = o_hg.shape[1]
    assert rw == hg and bp * tp + ms == m
    bm = _pick_block(_gcd(tp, ms), 512, LANES)
    bn = _pick_block(d, 2048, LANES)
    npb, tpb = bp * tp // bm, tp // bm

    def pt_idx(i, j):
        ic = jnp.minimum(i, npb - 1)
        return ic // tpb, 0, ic % tpb

    return pl.pallas_call(
        functools.partial(_proj_res_even_body, n_prompt_blk=npb),
        grid=(m // bm, d // bn),
        in_specs=[
            pl.BlockSpec((bm, bn), lambda i, j: (i, j)),
            pl.BlockSpec((None, rw, bm), pt_idx),
            pl.BlockSpec((bm, rw), lambda i, j: (jnp.maximum(i - npb, 0), 0)),
            pl.BlockSpec((bm, rw), lambda i, j: (i, 0)),
            pl.BlockSpec((bm, hg), lambda i, j: (i, 0)),
            _w_spec(w, (rw, bn), lambda i, j: (0, j), w_layer),
            _w_spec(w, (hg, bn), lambda i, j: (1, j), w_layer),
        ],
        out_specs=pl.BlockSpec((bm, bn), lambda i, j: (i, j)),
        out_shape=jax.ShapeDtypeStruct((m, d), F32),
        compiler_params=_cparams("parallel", "arbitrary"),
        name="proj_res_even",
    )(x, o_pt, o_s, g, o_hg, w, w)


def _final_norm_body(x_ref, g_ref, op_ref, os_ref, *, n_prompt_blk):
    y = _rms_rows(x_ref[...], g_ref[...])
    i = pl.program_id(0)

    @pl.when(i < n_prompt_blk)
    def _():
        op_ref[...] = y

    @pl.when(i >= n_prompt_blk)
    def _():
        os_ref[...] = y


def _final_norm(x, g, mp):
    m, d = x.shape
    ms = m - mp
    bm = _pick_block(_gcd(mp, ms), MAX_BM, SUBLANES)
    npb = mp // bm
    return pl.pallas_call(
        functools.partial(_final_norm_body, n_prompt_blk=npb),
        grid=(m // bm,),
        in_specs=[pl.BlockSpec((bm, d), lambda i: (i, 0)), pl.BlockSpec((1, d), lambda i: (0, 0))],
        out_specs=[pl.BlockSpec((bm, d), lambda i: (jnp.minimum(i, npb - 1), 0)),
                   pl.BlockSpec((bm, d), lambda i: (jnp.maximum(i - npb, 0), 0))],
        out_shape=[jax.ShapeDtypeStruct((mp, d), F32), jax.ShapeDtypeStruct((ms, d), F32)],
        compiler_params=_cparams("arbitrary"),
        name="final_norm",
    )(x, g[None, :])


def _rwkv_prep_body(cur_ref, tail_ref, prevs_ref, shiftp_ref, shifts_ref, mu_ref, w0_ref, w2_ref, a0_ref, a2_ref,
                    g2_ref, r_ref, k_ref, v_ref, w_ref, a_ref, g_ref, *, rw, dwp, dap, n_prompt_blk, blk_per_seq):
    i = pl.program_id(0)
    rows = cur_ref.shape[0]
    prw = cur_ref.shape[1]
    seq = jnp.minimum(i // blk_per_seq, shiftp_ref.shape[0] - 1)
    first = (i % blk_per_seq) == 0
    is_sample = i >= n_prompt_blk
    rid = lax.broadcasted_iota(jnp.int32, (rows, 1), 0)
    shift_row = shiftp_ref[pl.ds(seq, 1), :]

    def shifted(c0, c1):
        cs = slice(c0, c1)
        cur = cur_ref[:, cs]
        row0 = jnp.where(first, shift_row[:, cs], tail_ref[SUBLANES - 1:SUBLANES, cs])
        prev_p = jnp.where(rid == 0, row0, pltpu.roll(cur, 1, 0))
        prev_s = jnp.where(i == n_prompt_blk, shifts_ref[:, cs], prevs_ref[:, cs])
        prev = jnp.where(is_sample, prev_s, prev_p)
        return cur + (prev - cur) * mu_ref[:, cs]

    r_ref[...] = shifted(0, rw).T
    k_ref[...] = shifted(rw, 2 * rw).T
    v_ref[...] = shifted(2 * rw, 3 * rw).T
    o = 3 * rw
    lw = w0_ref[...] + _dot(jnp.tanh(shifted(o, o + dwp)).astype(BF16), w2_ref[...])
    w_log = -_softplus(-lw) - 0.5
    w_ref[...] = jnp.exp(-jnp.exp(w_log)).T
    a_ref[...] = jax.nn.sigmoid(a0_ref[...] + _dot(shifted(o + dwp, o + dwp + dap).astype(BF16), a2_ref[...])).T
    g_ref[...] = _dot(jax.nn.sigmoid(shifted(o + dwp + dap, prw)).astype(BF16), g2_ref[...])


def _rwkv_prep(p_rw, shift_p, shift_s, mu, w0, w2, a0, a2, g2, *, rw, mp, tp, bs):
    m, prw = p_rw.shape
    ms = m - mp
    assert mp % bs == 0 and tp % bs == 0 and ms % bs == 0
    npb = mp // bs
    dwp, dap = w2.shape[0], a2.shape[0]
    full = lambda a: pl.BlockSpec(a.shape, lambda i: (0,) * a.ndim)
    bps = tp // bs
    bp = mp // tp

    ts = ms // bs
    assert ts % bp == 0

    def out_idx(i):
        j = i - npb
        return jnp.where(i < npb, i // bps, j % bp), jnp.where(i < npb, i % bps, bps + j // bp)

    spec_o = pl.BlockSpec((rw, bs), out_idx)
    body = functools.partial(_rwkv_prep_body, rw=rw, dwp=dwp, dap=dap, n_prompt_blk=npb, blk_per_seq=tp // bs)
    outs = pl.pallas_call(
        body,
        grid=(m // bs,),
        in_specs=[
            pl.BlockSpec((bs, prw), lambda i: (i, 0)),
            pl.BlockSpec((SUBLANES, prw), lambda i: (jnp.maximum(i * (bs // SUBLANES) - 1, 0), 0)),
            pl.BlockSpec((bs, prw), lambda i: (jnp.maximum(i - 1, npb), 0)),
            full(shift_p), full(shift_s), full(mu), full(w0), full(w2), full(a0), full(a2), full(g2),
        ],
        out_specs=[spec_o] * 5 + [pl.BlockSpec((bs, rw), lambda i: (i, 0))],
        out_shape=[jax.ShapeDtypeStruct((bp * rw, tp + ms // bp), F32)] * 5 + [jax.ShapeDtypeStruct((m, rw), F32)],
        compiler_params=_cparams("arbitrary"),
        name="rwkv_prep",
    )(p_rw, p_rw, p_rw, shift_p, shift_s, mu, w0, w2, a0, a2, g2)
    return outs[:5], outs[5]


def _rwkv_scan_body(r_ref, w_ref, k_ref, a_ref, v_ref, kk_p_ref, ka_p_ref, rk_p_ref, lnw_ref, lnb_ref, s0_ref,
                    o_ref, sout_ref, s_ref, r_s, w_s, nkk_s, b_s, km_s, v_s, o_s, *, vsplit, state_rows):
    tc_len, nv, lanes = v_s.shape
    kc = s_ref.shape[1]
    ti = pl.program_id(1)

    @pl.when(ti == 0)
    def _():
        if state_rows:
            s_ref[...] = s0_ref[...].reshape(lanes, nv * kc).T.reshape(nv, kc, lanes)
        else:
            s_ref[...] = s0_ref[...]

    def widen(x):
        return jnp.concatenate([x, x], axis=-1) if vsplit else x

    if vsplit:
        v_in = v_ref[...]
        v_s[...] = jnp.concatenate([v_in[:, :nv, :], v_in[:, nv:, :]], axis=-1)
    else:
        v_s[...] = v_ref[...]
    kvec = widen(k_ref[...])
    a_ = widen(a_ref[...])
    r_s[...] = widen(r_ref[...])
    w_s[...] = widen(w_ref[...])
    kk = kvec * kk_p_ref[...][None]
    nrm = jnp.sqrt(jnp.sum(kk * kk, axis=1, keepdims=True))
    kk = kk / jnp.maximum(nrm, 1e-12)
    nkk_s[...] = -kk
    b_s[...] = kk * a_
    km_s[...] = kvec * (1.0 + (a_ - 1.0) * ka_p_ref[...][None])

    def step(t, carry):
        nkk = nkk_s[t]
        b = b_s[t]
        km = km_s[t]
        w = w_s[t]
        r = r_s[t]
        vt = v_s[t]
        for vi in range(nv):
            sv = s_ref[vi]
            sa = jnp.sum(sv * nkk, axis=0, keepdims=True)
            sv = sv * w + sa * b + vt[vi:vi + 1, :] * km
            s_ref[vi] = sv
            o_s[t, pl.ds(vi, 1), :] = jnp.sum(sv * r, axis=0, keepdims=True)
        return carry

    lax.fori_loop(0, tc_len, step, 0)

    o = o_s[...]
    n_val = nv * (2 if vsplit else 1)

    def head_sum(x):
        if vsplit:
            x2 = x.reshape(tc_len * nv, lanes)
            x = (x2 + pltpu.roll(x2, lanes // 2, 1)).reshape(tc_len, nv, lanes)
        return jnp.sum(x, axis=1, keepdims=True)

    mu = head_sum(o) / n_val
    d = o - mu
    var = head_sum(d * d) / n_val
    o = d * lax.rsqrt(var + RW_GN_EPS) * lnw_ref[...][None] + lnb_ref[...][None]
    bonus = jnp.sum(r_s[...] * km_s[...] * rk_p_ref[...][None], axis=1, keepdims=True)
    o = o + bonus * v_s[...]
    if vsplit:
        o_ref[...] = jnp.concatenate([o[:, :, :lanes // 2], o[:, :, lanes // 2:]], axis=1)
    else:
        o_ref[...] = o

    @pl.when(ti == pl.num_programs(1) - 1)
    def _():
        if state_rows:
            sout_ref[...] = s_ref[...].reshape(nv * kc, lanes).T.reshape(sout_ref.shape)
        else:
            sout_ref[...] = s_ref[...]


def _rwkv_scan(r, w, k, a, v, kk_p, ka_p, rk_p, lnw, lnb, s0, *, t, tc, vsplit, layer=None, after=None):
    _, kc, kl = r.shape
    nv = v.shape[1] // 2 if vsplit else v.shape[1]
    ln = v.shape[2] * 2 if vsplit else v.shape[2]
    g = ln // LANES
    kspec = pl.BlockSpec((tc, kc, kl // g), lambda gi, ti: (ti, 0, gi))
    vspec = pl.BlockSpec((tc, nv, LANES), lambda gi, ti: (ti, 0, gi))
    pk = pl.BlockSpec((kc, LANES), lambda gi, ti: (0, 0))
    pv = pl.BlockSpec((nv, LANES), lambda gi, ti: (0, 0))
    state_rows = not vsplit
    operands = [r, w, k, a, v, kk_p, ka_p, rk_p, lnw, lnb, s0]
    in_specs = [kspec, kspec, kspec, kspec, kspec if vsplit else vspec, pk, pk, pk, pv, pv]
    aliases = {}
    if state_rows:
        nb, _, nh, sz = s0.shape
        assert sz == nv * kc and nb * nh == ln
        sspec = pl.BlockSpec((LANES // nh, None, nh, sz), lambda gi, ti: (gi, layer, 0, 0))
        in_specs.append(sspec)
        aliases = {10: 1}
        s_shape = s0.shape
    else:
        sspec = pl.BlockSpec((nv, kc, LANES), lambda gi, ti: (0, 0, gi))
        in_specs.append(sspec)
        s_shape = (nv, kc, ln)
    if after is not None:
        operands.append(after)
        in_specs.append(pl.BlockSpec(memory_space=pl.ANY))

    def body(*refs):
        n_in = len(operands)
        ins = refs[:11]
        _rwkv_scan_body(*ins, *refs[n_in:], vsplit=vsplit, state_rows=state_rows)

    return pl.pallas_call(
        body,
        grid=(g, t // tc),
        in_specs=in_specs,
        out_specs=[kspec if vsplit else vspec, sspec],
        out_shape=[jax.ShapeDtypeStruct((t, kc, kl) if vsplit else (t, nv, ln), F32),
                   jax.ShapeDtypeStruct(s_shape, F32)],
        scratch_shapes=[pltpu.VMEM((nv, kc, LANES), F32)] + [pltpu.VMEM((tc, kc, LANES), F32)] * 5
        + [pltpu.VMEM((tc, nv, LANES), F32)] * 2,
        input_output_aliases=aliases,
        compiler_params=_cparams("parallel", "arbitrary"),
        name="rwkv_scan",
    )(*operands)


def _ref_rows(gc, j):
    c, w = gc.shape
    b = 1 << j
    nt = c // SUBLANES
    gv = gc.reshape(nt, SUBLANES, w)
    if 2 * b <= SUBLANES:
        sub = lax.broadcasted_iota(jnp.int32, (1, SUBLANES, 1), 1)
        out = None
        for g0 in range(0, SUBLANES, 2 * b):
            piece = jnp.broadcast_to(gv[:, g0 + b - 1:g0 + b, :], gv.shape)
            out = piece if out is None else jnp.where(sub >= g0, piece, out)
        return out.reshape(c, w)
    tiles_per_group = 2 * b // SUBLANES
    pieces = []
    for ti in range(nt):
        src = (ti // tiles_per_group) * tiles_per_group + b // SUBLANES - 1
        pieces.append(jnp.broadcast_to(gv[src, SUBLANES - 1:SUBLANES, :], (SUBLANES, w)))
    return jnp.concatenate(pieces, axis=0)


def _hgrn_chunk(q, f, val, og, get_st, set_st, loglb, log1mlb, nrm_w, seq_len):
    c, width = q.shape
    nh = width // HG_DIM
    nseq = c // seq_len
    lg = seq_len.bit_length() - 1
    assert (1 << lg) == seq_len and c % SUBLANES == 0
    ls = jnp.minimum(f, 0.0) - jnp.log1p(jnp.exp(-jnp.abs(f)))
    b_ = log1mlb + ls
    g = jnp.maximum(loglb, b_) + jnp.log1p(jnp.exp(-jnp.abs(loglb - b_)))
    kk = _neg_expm1(g)
    qs = q * jax.nn.sigmoid(q)
    row = lax.broadcasted_iota(jnp.int32, (c, 1), 0)
    col = lax.broadcasted_iota(jnp.int32, (1, c), 1)
    tpos = row & (seq_len - 1)
    gc = g
    for j in range(lg):
        s = 1 << j
        gc = gc + jnp.where(tpos >= s, pltpu.roll(gc, s, 0), 0.0)
    qbs, kbs, sames = [qs.astype(BF16)], [kk.astype(BF16)], [row == col]
    for j in range(lg):
        e = jnp.exp(-jnp.abs(gc - _ref_rows(gc, j)))
        is_q = ((row >> j) & 1) == 1
        pk = (jnp.where(is_q, qs, kk) * e).astype(BF16)
        qbs.append(pk)
        kbs.append(pk)
        sames.append(((row >> (j + 1)) == (col >> (j + 1))) & is_q & (((col >> j) & 1) == 0))
    vb = val.astype(BF16)
    qg = (qs * jnp.exp(gc)).astype(BF16)
    k2s, eglast, in_seq = [], [], []
    for s_i in range(nseq):
        glast = gc[s_i * seq_len + seq_len - 1:s_i * seq_len + seq_len, :]
        eglast.append(jnp.exp(glast))
        if nseq == 1:
            in_seq.append(None)
            k2s.append((kk * jnp.exp(glast - gc)).astype(BF16))
        else:
            in_s = (row >> lg) == s_i
            in_seq.append(in_s)
            k2s.append(jnp.where(in_s, kk * jnp.exp(jnp.minimum(glast - gc, 0.0)), 0.0).astype(BF16))
    outs = []
    for h in range(nh):
        cs = slice(h * HG_DIM, (h + 1) * HG_DIM)
        amat = None
        for qb, kb, same in zip(qbs, kbs, sames):
            term = jnp.where(same, _dot_nt(qb[:, cs], kb[:, cs]), 0.0)
            amat = term if amat is None else amat + term
        o = _dot(amat.astype(BF16), vb[:, cs])
        for s_i in range(nseq):
            st = get_st(s_i, h)
            o_int = _dot_nt(qg[:, cs], st.astype(BF16))
            o = o + (o_int if nseq == 1 else jnp.where(in_seq[s_i], o_int, 0.0))
            set_st(s_i, h, st * eglast[s_i][:, cs] + _dot_tn(vb[:, cs], k2s[s_i][:, cs]))
        outs.append(o * lax.rsqrt(jnp.mean(o * o, axis=-1, keepdims=True) + HG_NORM_EPS))
    on = jnp.concatenate(outs, axis=1) * nrm_w
    return on * (og * jax.nn.sigmoid(og))


def _hgrn_prompt_body(q_ref, f_ref, i_ref, og_ref, s0_ref, loglb_ref, log1m_ref, nw_ref, o_ref, sout_ref, st_ref,
                      *, chunk):
    nh = st_ref.shape[0]
    tb = pl.program_id(1)

    @pl.when(tb == 0)
    def _():
        for h in range(nh):
            st_ref[h] = s0_ref[h].T

    def set_st(s_i, h, v):
        st_ref[h] = v

    def body(ci, carry):
        rows = pl.ds(pl.multiple_of(ci * chunk, chunk), chunk)
        y = _hgrn_chunk(q_ref[rows, :], f_ref[rows, :], i_ref[rows, :], og_ref[rows, :],
                        lambda s_i, h: st_ref[h], set_st, loglb_ref[...], log1m_ref[...], nw_ref[...], chunk)
        o_ref[rows, :] = y.astype(BF16)
        return carry

    lax.fori_loop(0, q_ref.shape[0] // chunk, body, 0)

    @pl.when(tb == pl.num_programs(1) - 1)
    def _():
        for h in range(nh):
            sout_ref[h] = st_ref[h].T


def _hgrn_prompt(p_hg, s0, loglb, log1m, nw, *, bp, tp, chunk):
    hgw = p_hg.shape[1] // 4
    nh = hgw // HG_DIM
    tb = _pick_block(tp, 512, chunk)
    nt = tp // tb
    cspec = lambda c: pl.BlockSpec((tb, hgw), lambda b, t: (b * nt + t, c))
    pspec = pl.BlockSpec((1, hgw), lambda b, t: (0, 0))
    sspec = pl.BlockSpec((None, nh, HG_DIM, HG_DIM), lambda b, t: (b, 0, 0, 0))
    return pl.pallas_call(
        functools.partial(_hgrn_prompt_body, chunk=chunk),
        grid=(bp, nt),
        in_specs=[cspec(0), cspec(1), cspec(2), cspec(3), sspec, pspec, pspec, pspec],
        out_specs=[pl.BlockSpec((tb, hgw), lambda b, t: (b * nt + t, 0)), sspec],
        out_shape=[jax.ShapeDtypeStruct((p_hg.shape[0], hgw), BF16), jax.ShapeDtypeStruct(s0.shape, F32)],
        scratch_shapes=[pltpu.VMEM((nh, HG_DIM, HG_DIM), F32)],
        compiler_params=_cparams("parallel", "arbitrary"),
        name="hgrn_prompt",
    )(p_hg, p_hg, p_hg, p_hg, s0, loglb, log1m, nw)


def _hgrn_sample_body(q_ref, f_ref, i_ref, og_ref, s0_ref, loglb_ref, log1m_ref, nw_ref, *rest, seq_len, nseq):
    o_ref, sout_ref = rest[-2:]
    c = seq_len * nseq

    def body(gi, carry):
        rows = pl.ds(pl.multiple_of(gi * c, c), c)

        def set_st(s_i, h, v):
            sout_ref[gi * nseq + s_i, h] = v.T

        y = _hgrn_chunk(q_ref[rows, :], f_ref[rows, :], i_ref[rows, :], og_ref[rows, :],
                        lambda s_i, h: s0_ref[gi * nseq + s_i, h].T, set_st,
                        loglb_ref[...], log1m_ref[...], nw_ref[...], seq_len)
        o_ref[rows, :] = y.astype(BF16)
        return carry

    lax.fori_loop(0, q_ref.shape[0] // c, body, 0)


def _hgrn_sample(p_hg_bm, s0, layer, loglb, log1m, nw, *, bs, ts, s_prev=None, after=None):
    hgw = p_hg_bm.shape[1] // 4
    nh = hgw // HG_DIM
    nseq = max(1, 16 // ts)
    assert bs % nseq == 0
    bb = _pick_block(bs, 8, nseq)
    rows = bb * ts
    cspec = lambda c: pl.BlockSpec((rows, hgw), lambda i: (i, c))
    pspec = pl.BlockSpec((1, hgw), lambda i: (0, 0))
    sspec = pl.BlockSpec((bb, None, nh, HG_DIM, HG_DIM), lambda i: (i, layer, 0, 0, 0))
    operands = [p_hg_bm, p_hg_bm, p_hg_bm, p_hg_bm, s0, loglb, log1m, nw]
    in_specs = [cspec(0), cspec(1), cspec(2), cspec(3), sspec, pspec, pspec, pspec]
    aliases = {}
    if s_prev is not None:
        operands.append(s_prev)
        in_specs.append(pl.BlockSpec(memory_space=pl.ANY))
        aliases = {len(operands) - 1: 1}
    if after is not None:
        operands.append(after)
        in_specs.append(pl.BlockSpec(memory_space=pl.ANY))
    return pl.pallas_call(
        functools.partial(_hgrn_sample_body, seq_len=ts, nseq=nseq),
        grid=(bs // bb,),
        in_specs=in_specs,
        out_specs=[pl.BlockSpec((rows, hgw), lambda i: (i, 0)), sspec],
        out_shape=[jax.ShapeDtypeStruct((bs * ts, hgw), BF16), jax.ShapeDtypeStruct(s0.shape, F32)],
        input_output_aliases=aliases,
        compiler_params=_cparams("parallel"),
        name="hgrn_sample",
    )(*operands)


def _lru_gates(xconv, wa_ref, wx_ref, ba, bx, lam):
    l = xconv.shape[1]
    blk = l // LRU_BLOCKS
    ga, gx = [], []
    for n in range(LRU_BLOCKS):
        xb = xconv[:, n * blk:(n + 1) * blk].astype(BF16)
        ga.append(_dot(xb, wa_ref[n]))
        gx.append(_dot(xb, wx_ref[n]))
    ga = jnp.concatenate(ga, axis=1) + ba
    gx = jnp.concatenate(gx, axis=1) + bx
    log_a = -LRU_C * jax.nn.sigmoid(ga) * _softplus(-lam)
    a = jnp.exp(log_a)
    mult = jnp.sqrt(1.0 - a * a)
    return a, mult * jax.nn.sigmoid(gx) * xconv


def _gelu_tanh(x):
    c = 0.7978845608028654
    return 0.5 * x * (1.0 + jnp.tanh(c * (x + 0.044715 * (x * x * x))))


def _odd_prompt_body(gate_ref, xb_ref, conv0_ref, h0_ref, cw_ref, cb_ref, wa_ref, wx_ref, ba_ref, bx_ref, lam_ref,
                     y_ref, convo_ref, ho_ref, ext_ref, hc_ref, a_s, b_s, h_s):
    rows = xb_ref.shape[0]
    hist = SUBLANES

    @pl.when(pl.program_id(1) == 0)
    def _():
        ext_ref[0:hist, :] = conv0_ref[...]
        hc_ref[...] = h0_ref[...]

    ext_ref[hist:hist + rows, :] = xb_ref[...]
    cw = cw_ref[...]
    xconv = cw[0:1, :] * ext_ref[pl.ds(hist - (CONV_W - 1), rows), :]
    for j in range(1, CONV_W):
        xconv = xconv + cw[j:j + 1, :] * ext_ref[pl.ds(hist - (CONV_W - 1) + j, rows), :]
    xconv = cb_ref[...] + xconv
    a, b = _lru_gates(xconv, wa_ref, wx_ref, ba_ref[...], bx_ref[...], lam_ref[...])
    width = a.shape[1]
    a = a.reshape(rows // SUBLANES, SUBLANES, width)
    b = b.reshape(rows // SUBLANES, SUBLANES, width)
    r8 = lax.broadcasted_iota(jnp.int32, (1, SUBLANES, 1), 1)
    for s in (1, 2, 4):
        m = r8 >= s
        a_sh = pltpu.roll(a, s, 1)
        b_sh = pltpu.roll(b, s, 1)
        b = jnp.where(m, a * b_sh + b, b)
        a = jnp.where(m, a * a_sh, a)
    a_s[...] = a.reshape(rows, width)
    b_s[...] = b.reshape(rows, width)

    def tile(j, hc):
        r0 = pl.multiple_of(j * SUBLANES, SUBLANES)
        h = a_s[pl.ds(r0, SUBLANES), :] * hc + b_s[pl.ds(r0, SUBLANES), :]
        h_s[pl.ds(r0, SUBLANES), :] = h
        return h[SUBLANES - 1:SUBLANES, :]

    hc = lax.fori_loop(0, rows // SUBLANES, tile, hc_ref[...])
    hc_ref[...] = hc
    ho_ref[...] = hc
    y_ref[...] = (h_s[...] * _gelu_tanh(gate_ref[...])).astype(BF16)
    tail = ext_ref[rows:rows + hist, :]
    ext_ref[0:hist, :] = tail
    convo_ref[...] = tail


def _odd_prompt(p_od, conv0, h0, layer, cw, cb, wa, wx, ba, bx, lam, *, bp, tp):
    l = p_od.shape[1] // 2
    rows = _pick_block(tp, 256, SUBLANES)
    nt = tp // rows
    lsel = lambda a: pl.BlockSpec((None,) + a.shape[1:], lambda b, t: (layer,) + (0,) * (a.ndim - 1))
    return pl.pallas_call(
        _odd_prompt_body,
        grid=(bp, nt),
        in_specs=[
            pl.BlockSpec((rows, l), lambda b, t: (b * nt + t, 0)),
            pl.BlockSpec((rows, l), lambda b, t: (b * nt + t, 1)),
            pl.BlockSpec((None, SUBLANES, l), lambda b, t: (b, 0, 0)),
            pl.BlockSpec((None, 1, l), lambda b, t: (b, 0, 0)),
            lsel(cw), lsel(cb), lsel(wa), lsel(wx), lsel(ba), lsel(bx), lsel(lam),
        ],
        out_specs=[
            pl.BlockSpec((rows, l), lambda b, t: (b * nt + t, 0)),
            pl.BlockSpec((None, SUBLANES, l), lambda b, t: (b, 0, 0)),
            pl.BlockSpec((None, 1, l), lambda b, t: (b, 0, 0)),
        ],
        out_shape=[jax.ShapeDtypeStruct((p_od.shape[0], l), BF16), jax.ShapeDtypeStruct((bp, SUBLANES, l), F32),
                   jax.ShapeDtypeStruct((bp, 1, l), F32)],
        scratch_shapes=[pltpu.VMEM((rows + SUBLANES, l), F32), pltpu.VMEM((1, l), F32),
                        pltpu.VMEM((rows, l), F32), pltpu.VMEM((rows, l), F32), pltpu.VMEM((rows, l), F32)],
        compiler_params=_cparams("parallel", "arbitrary"),
        name="odd_prompt",
    )(p_od, p_od, conv0, h0, cw, cb, wa, wx, ba, bx, lam)


def _odd_sample_body(gate_ref, xb_ref, conv0_ref, h0_ref, cw_ref, cb_ref, wa_ref, wx_ref, ba_ref, bx_ref, lam_ref,
                     yall_ref, y_ref, convo_ref, ho_ref):
    @pl.when(pl.program_id(0) == 0)
    def _():
        convo_ref[...] = conv0_ref[...]
        ho_ref[...] = h0_ref[...]

    xb = xb_ref[...]
    cw = cw_ref[...]
    xconv = cw[CONV_W - 1:CONV_W, :] * xb
    for j in range(CONV_W - 1):
        xconv = xconv + cw[j:j + 1, :] * convo_ref[j]
    xconv = cb_ref[...] + xconv
    a, b = _lru_gates(xconv, wa_ref, wx_ref, ba_ref[...], bx_ref[...], lam_ref[...])
    h = a * ho_ref[...] + b
    ho_ref[...] = h
    y_ref[...] = (h * _gelu_tanh(gate_ref[...])).astype(BF16)
    for j in range(CONV_W - 2):
        convo_ref[j] = convo_ref[j + 1]
    convo_ref[CONV_W - 2] = xb


def _odd_sample(p_od, conv0_t, h0, layer, cw, cb, wa, wx, ba, bx, lam, y_all, *, mp, bs, ts):
    l = p_od.shape[1] // 2
    assert mp % bs == 0
    off = mp // bs
    lsel = lambda a: pl.BlockSpec((None,) + a.shape[1:], lambda t: (layer,) + (0,) * (a.ndim - 1))
    full = lambda a: pl.BlockSpec(a.shape, lambda t: (0,) * a.ndim)
    return pl.pallas_call(
        _odd_sample_body,
        grid=(ts,),
        in_specs=[
            pl.BlockSpec((bs, l), lambda t: (off + t, 0)),
            pl.BlockSpec((bs, l), lambda t: (off + t, 1)),
            full(conv0_t), full(h0),
            lsel(cw), lsel(cb), lsel(wa), lsel(wx), lsel(ba), lsel(bx), lsel(lam),
            pl.BlockSpec(memory_space=pl.ANY),
        ],
        out_specs=[pl.BlockSpec((bs, l), lambda t: (off + t, 0)), full(conv0_t), full(h0)],
        out_shape=[jax.ShapeDtypeStruct(y_all.shape, BF16), jax.ShapeDtypeStruct(conv0_t.shape, F32),
                   jax.ShapeDtypeStruct(h0.shape, F32)],
        input_output_aliases={11: 0},
        compiler_params=_cparams("arbitrary"),
        name="odd_sample",
    )(p_od, p_od, conv0_t, h0, cw, cb, wa, wx, ba, bx, lam, y_all)


def _pad_last(a, n):
    return jnp.pad(a, [(0, 0)] * (a.ndim - 1) + [(0, n - a.shape[-1])])


def _round_up(n, m):
    return (n + m - 1) // m * m


class _RwLayout:
    def __init__(self, rw, dw, da, dg):
        self.rw, self.dw, self.da, self.dg = rw, dw, da, dg
        self.dwp, self.dap, self.dgp = (_round_up(d, LANES) for d in (dw, da, dg))
        self.width = 3 * rw + dw + da + dg
        self.padded = 3 * rw + self.dwp + self.dap + self.dgp

    def pad(self, a):
        o = 3 * self.rw
        parts = [a[..., :o], _pad_last(a[..., o:o + self.dw], self.dwp),
                 _pad_last(a[..., o + self.dw:o + self.dw + self.da], self.dap),
                 _pad_last(a[..., o + self.dw + self.da:self.width], self.dgp)]
        return jnp.concatenate(parts, axis=-1)

    def unpad(self, a):
        o = 3 * self.rw
        parts = [a[..., :o], a[..., o:o + self.dw], a[..., o + self.dwp:o + self.dwp + self.da],
                 a[..., o + self.dwp + self.dap:o + self.dwp + self.dap + self.dg]]
        return jnp.concatenate(parts, axis=-1)


def kernel(x_prompt, x_sample, state_rwkv_shift, state_rwkv, state_hgrn, state_conv, state_lru, ffn1_norm, ffn1_w_gu, ffn1_w_down, mix_norm, ffn2_norm, ffn2_w_gu, ffn2_w_down, ev_w_in, rw_mu, rw_w0, rw_w2, rw_a0, rw_a2, rw_g2, rw_k_k, rw_k_a, rw_r_k, rw_ln_w, rw_ln_b, hg_lb, hg_norm, ev_w_out, od_w_in, conv_w, conv_b, lru_wa, lru_ba, lru_wx, lru_bx, lru_lambda, od_w_out, final_norm):
    bp, tp, d = x_prompt.shape
    bs, ts, _ = x_sample.shape
    depth = ffn1_norm.shape[0]
    n_even = ev_w_in.shape[0]
    mp, ms = bp * tp, bs * ts
    rw = rw_w0.shape[1]
    nh_rw = rw // RW_HEAD
    hgw = hg_norm.shape[1]
    nh_hg = hgw // HG_DIM
    lay = _RwLayout(rw, rw_w2.shape[1], rw_a2.shape[1], rw_g2.shape[1])
    assert bp * nh_rw * 2 == LANES and (bs * nh_rw) % LANES == 0
    dt = x_prompt.dtype

    ffn_src = {1: (ffn1_norm, ffn1_w_gu, ffn1_w_down), 2: (ffn2_norm, ffn2_w_gu, ffn2_w_down)}
    ffn_w = (ffn1_w_gu[0].astype(BF16), ffn1_w_down[0].astype(BF16))

    def ffn_step(x, l, which, ffn_w):
        norm_g = ffn_src[which][0]
        nxt = (l, 2) if which == 1 else (l + 1, 1)
        if nxt[0] >= depth:
            return _ffn(x, norm_g, l, *ffn_w)[0], None
        x, c_gu, c_dn = _ffn(x, norm_g, l, *ffn_w, next_src=(ffn_src[nxt[1]][1], ffn_src[nxt[1]][2], nxt[0]))
        return x, (c_gu, c_dn)

    ev_out = ev_w_out.astype(BF16)
    od_in, od_out = od_w_in.astype(BF16), od_w_out.astype(BF16)

    wa_bf, wx_bf = lru_wa.astype(BF16), lru_wx.astype(BF16)

    x = jnp.concatenate([x_prompt.reshape(mp, d), x_sample.transpose(1, 0, 2).reshape(ms, d)], axis=0)

    lbs = jnp.cumsum(jax.nn.softmax(hg_lb.astype(F32), axis=0), axis=0)
    lb_all = lbs - lbs[0]

    def k_lanes_p(a):
        return a.reshape(bp * nh_rw, RW_HEAD, a.shape[1]).transpose(2, 1, 0)

    def k_lanes_p_inv(a):
        return a.transpose(2, 1, 0).reshape(bp, rw, tp)

    def k_lanes_s(a):
        a = a.reshape(bp, nh_rw, RW_HEAD, ts // bp, bs).transpose(3, 0, 2, 4, 1)
        return a.reshape(ts, RW_HEAD, bs * nh_rw)

    def k_lanes_s_inv(a):
        return a.reshape(ts, RW_HEAD, bs, nh_rw).transpose(0, 2, 3, 1).reshape(ms, rw)

    def kparam(p):
        return jnp.tile(p.reshape(nh_rw, RW_HEAD).T, (1, LANES // nh_rw))

    def vparam_p(p):
        a = p.reshape(nh_rw, 2, RW_HEAD // 2).transpose(2, 1, 0)[:, :, None, :]
        return jnp.broadcast_to(a, (RW_HEAD // 2, 2, bp, nh_rw)).reshape(RW_HEAD // 2, LANES)

    shifts_p, shifts_s, rws_p, hgs_p, convs_p, convs_s, lrus_p, lrus_s = ([] for _ in range(8))
    rw_rows = state_rwkv.astype(F32).reshape(bs, n_even, nh_rw, RW_HEAD * RW_HEAD)
    hg_new = None

    for l in range(depth):
        x, ffn_w = ffn_step(x, l, 1, ffn_w)
        if l % 2 == 0:
            e = l // 2
            w_in = ev_w_in[e]
            w_rw = lay.pad(w_in[:, :lay.width]).astype(BF16)
            w_hg = w_in[:, lay.width:].astype(BF16)
            p_rw = _norm_proj(x, mix_norm, l, w_rw, _pick_block(lay.padded, 1792, LANES))
            p_hg = _norm_proj(x, mix_norm, l, w_hg, _pick_block(4 * hgw, 1024, LANES))
            shift_p = jnp.zeros((bp, lay.padded), F32)
            shift_s = lay.pad(state_rwkv_shift[:, e].astype(F32))
            rkvwa, g_ = _rwkv_prep(
                p_rw, shift_p, shift_s, lay.pad(rw_mu[e])[None], rw_w0[e][None],
                _pad_last(rw_w2[e].T, lay.dwp).T.astype(BF16), rw_a0[e][None],
                _pad_last(rw_a2[e].T, lay.dap).T.astype(BF16),
                _pad_last(rw_g2[e].T, lay.dgp).T.astype(BF16), rw=rw, mp=mp, tp=tp, bs=bs)
            shifts_p.append(lay.unpad(jnp.concatenate([p_rw[b * tp + tp - 1:(b + 1) * tp] for b in range(bp)])))
            shifts_s.append(lay.unpad(p_rw[mp + (ts - 1) * bs:]))
            kk_p, ka_p = kparam(rw_k_k[e]), kparam(rw_k_a[e])
            rk_p = kparam(rw_r_k[e].reshape(rw))
            lb = lb_all[e][None]
            loglb, log1m = jnp.log(lb), jnp.log1p(-lb)
            nw = hg_norm[e][None]
            chunk = _pick_block(tp, HG_CHUNK, 1)
            oh_p, hs_p = _hgrn_prompt(p_hg, jnp.zeros((bp, nh_hg, HG_DIM, HG_DIM), F32), loglb, log1m, nw,
                                      bp=bp, tp=tp, chunk=chunk)
            r_p, k_p, v_p, w_p, a_p = rkvwa
            o_p, s_p = _rwkv_scan(
                k_lanes_p(r_p), k_lanes_p(w_p), k_lanes_p(k_p), k_lanes_p(a_p), k_lanes_p(v_p),
                kk_p, ka_p, rk_p, vparam_p(rw_ln_w[e]), vparam_p(rw_ln_b[e]),
                jnp.zeros((RW_HEAD // 2, RW_HEAD, LANES), F32), t=tp, tc=_pick_block(tp, 64, 1), vsplit=True,
                after=oh_p)
            rws_p.append(s_p.reshape(RW_HEAD // 2, RW_HEAD, 2, bp, nh_rw).transpose(3, 4, 2, 0, 1)
                         .reshape(bp, nh_rw, RW_HEAD, RW_HEAD))
            r_s, k_s, v_s, w_s, a_s = (a[:, tp:] for a in rkvwa)
            o_s, rw_rows = _rwkv_scan(
                k_lanes_s(r_s), k_lanes_s(w_s), k_lanes_s(k_s), k_lanes_s(a_s), k_lanes_s(v_s),
                kk_p, ka_p, rk_p, kparam(rw_ln_w[e]), kparam(rw_ln_b[e]),
                rw_rows, t=ts, tc=ts, vsplit=False, layer=e, after=o_p)
            o_pt, o_st = k_lanes_p_inv(o_p), k_lanes_s_inv(o_s)
            p_hg_bm = p_hg[mp:].reshape(ts, bs, 4 * hgw).transpose(1, 0, 2).reshape(ms, 4 * hgw)
            oh_s, hg_new = _hgrn_sample(p_hg_bm, state_hgrn.astype(F32), e, loglb, log1m, nw, bs=bs, ts=ts,
                                        s_prev=hg_new, after=o_p)
            oh_s = oh_s.reshape(bs, ts, hgw).transpose(1, 0, 2).reshape(ms, hgw)
            hgs_p.append(hs_p)
            o_hg = oh_p.at[mp:].set(oh_s)
            x = _proj_res_even(x, o_pt, o_st, g_, o_hg, ev_out, e)
        else:
            o_i = l // 2
            p_od = _norm_proj(x, mix_norm, l, od_in, _pick_block(od_in.shape[-1], 1024, LANES), w_layer=o_i)
            lsz = p_od.shape[1] // 2
            vec = lambda a: a[:, None, :]
            args = (conv_w, vec(conv_b), wa_bf, wx_bf, vec(lru_ba), vec(lru_bx), vec(lru_lambda))
            conv0_p = jnp.zeros((bp, SUBLANES, lsz), F32)
            y_p, c_p, h_p = _odd_prompt(p_od, conv0_p, jnp.zeros((bp, 1, lsz), F32), o_i, *args, bp=bp, tp=tp)
            conv0_s = state_conv[:, o_i].astype(F32).transpose(1, 0, 2)
            y_all, c_s, h_s = _odd_sample(p_od, conv0_s, state_lru[:, o_i].astype(F32), o_i, *args, y_p,
                                          mp=mp, bs=bs, ts=ts)
            convs_p.append(c_p[:, SUBLANES - (CONV_W - 1):])
            convs_s.append(c_s.transpose(1, 0, 2))
            lrus_p.append(h_p[:, 0])
            lrus_s.append(h_s)
            x = _proj_res(x, y_all, od_out, o_i)
        x, ffn_w = ffn_step(x, l, 2, ffn_w)

    y_p, y_s = _final_norm(x, final_norm, mp)
    y_prompt = y_p.reshape(bp, tp, d)
    y_sample = y_s.reshape(ts, bs, d).transpose(1, 0, 2)
    st = lambda xs: jnp.stack(xs, axis=1).astype(dt)
    sample_rwkv = rw_rows.reshape(state_rwkv.shape).astype(dt)
    return (y_prompt, y_sample, st(shifts_p), st(rws_p), st(hgs_p), st(convs_p), st(lrus_p),
            st(shifts_s), sample_rwkv, hg_new.astype(dt), st(convs_s), st(lrus_s))
```

```python
import functools
import math

import jax
import jax.numpy as jnp
from jax import lax
from jax.experimental import pallas as pl
from jax.experimental.pallas import tpu as pltpu

F32 = jnp.float32
BF16 = jnp.bfloat16

NORM_EPS = 1e-6
RW_GN_EPS = 64e-5
HG_NORM_EPS = 1e-5
LRU_C = 8.0
RW_HEAD = 64
HG_DIM = 128
LRU_BLOCKS = 8
CONV_W = 4
LANES = 128
SUBLANES = 8
VMEM_LIMIT = 62 * 1024 * 1024
MAX_BM = 1088
HG_CHUNK = 64


def _cparams(*sem):
    return pltpu.CompilerParams(dimension_semantics=sem, vmem_limit_bytes=VMEM_LIMIT)


def _pick_block(n, cap, mult):
    best = None
    for d in range(mult, min(n, cap) + 1, mult):
        if n % d == 0:
            best = d
    assert best is not None, (n, cap, mult)
    return best


def _gcd(a, b):
    return math.gcd(a, b)


def _dot(a, b):
    return jnp.dot(a, b, preferred_element_type=F32)


def _dot_nt(a, b):
    return lax.dot_general(a, b, (((1,), (1,)), ((), ())), preferred_element_type=F32)


def _dot_tn(a, b):
    return lax.dot_general(a, b, (((0,), (0,)), ((), ())), preferred_element_type=F32)


def _softplus(x):
    return jnp.maximum(x, 0.0) + jnp.log1p(jnp.exp(-jnp.abs(x)))


def _neg_expm1(y):
    return 1.0 - jnp.exp(y)


def _rms_rows(x, g):
    ms = jnp.mean(x * x, axis=-1, keepdims=True)
    return x * lax.rsqrt(ms + NORM_EPS) * g


def _ffn_body(x_ref, g_ref, wg_ref, wu_ref, wd_ref, *rest, cast_next):
    if cast_next:
        sgu_ref, sdn_ref, o_ref, cgu_ref, cdn_ref, xn_ref = rest
        cgu_ref[...] = sgu_ref[...].astype(BF16)
        cdn_ref[...] = sdn_ref[...].astype(BF16)
    else:
        o_ref, xn_ref = rest

    @pl.when(pl.program_id(1) == 0)
    def _():
        x = x_ref[...]
        xn_ref[...] = _rms_rows(x, g_ref[...]).astype(BF16)
        o_ref[...] = x

    xn = xn_ref[...]
    gate = _dot(xn, wg_ref[...])
    up = _dot(xn, wu_ref[...])
    act = (0.5 * (gate * jax.nn.sigmoid(gate)) * up).astype(BF16)
    o_ref[...] += _dot(act, wd_ref[...])


def _ffn(x, norm_g, layer, w_gu, w_down, next_src=None):
    m, d = x.shape
    f = w_down.shape[0]
    bm = _pick_block(m, MAX_BM, 16)
    bf = _pick_block(f, 512, LANES)
    ni, nf = m // bm, f // bf
    in_specs = [
        pl.BlockSpec((bm, d), lambda i, j: (i, 0)),
        pl.BlockSpec((None, 1, d), lambda i, j: (layer, 0, 0)),
        pl.BlockSpec((d, bf), lambda i, j: (0, j)),
        pl.BlockSpec((d, bf), lambda i, j: (0, j + nf)),
        pl.BlockSpec((bf, d), lambda i, j: (j, 0)),
    ]
    operands = [x, norm_g[:, None, :], w_gu, w_gu, w_down]
    out_specs = [pl.BlockSpec((bm, d), lambda i, j: (i, 0))]
    out_shape = [jax.ShapeDtypeStruct((m, d), F32)]
    if next_src is not None:
        s_gu, s_dn, nl = next_src
        assert d % ni == 0 and (2 * f) % nf == 0 and (d // ni) % 16 == 0 and (d // ni) % LANES == 0
        gu_blk, dn_blk = (d // ni, 2 * f // nf), (f // nf, d // ni)
        in_specs += [pl.BlockSpec((None,) + gu_blk, lambda i, j: (nl, i, j)),
                     pl.BlockSpec((None,) + dn_blk, lambda i, j: (nl, j, i))]
        operands += [s_gu, s_dn]
        out_specs += [pl.BlockSpec(gu_blk, lambda i, j: (i, j)), pl.BlockSpec(dn_blk, lambda i, j: (j, i))]
        out_shape += [jax.ShapeDtypeStruct((d, 2 * f), BF16), jax.ShapeDtypeStruct((f, d), BF16)]
    return pl.pallas_call(
        functools.partial(_ffn_body, cast_next=next_src is not None),
        grid=(ni, nf),
        in_specs=in_specs,
        out_specs=out_specs,
        out_shape=out_shape,
        scratch_shapes=[pltpu.VMEM((bm, d), BF16)],
        compiler_params=_cparams("parallel", "arbitrary"),
        name="ffn",
    )(*operands)


def _norm_proj_body(x_ref, g_ref, w_ref, o_ref, xn_ref):
    @pl.when(pl.program_id(1) == 0)
    def _():
        xn_ref[...] = _rms_rows(x_ref[...], g_ref[...]).astype(BF16)

    o_ref[...] = _dot(xn_ref[...], w_ref[...])


def _w_spec(w, blk, idx, w_layer):
    if w.ndim == 2:
        return pl.BlockSpec(blk, idx)
    return pl.BlockSpec((None,) + blk, lambda i, j: (w_layer,) + idx(i, j))


def _norm_proj(x, norm_g, layer, w, bn, w_layer=None):
    m, d = x.shape
    n = w.shape[-1]
    bm = _pick_block(m, MAX_BM, 16)
    return pl.pallas_call(
        _norm_proj_body,
        grid=(m // bm, n // bn),
        in_specs=[
            pl.BlockSpec((bm, d), lambda i, j: (i, 0)),
            pl.BlockSpec((None, 1, d), lambda i, j: (layer, 0, 0)),
            _w_spec(w, (d, bn), lambda i, j: (0, j), w_layer),
        ],
        out_specs=pl.BlockSpec((bm, bn), lambda i, j: (i, j)),
        out_shape=jax.ShapeDtypeStruct((m, n), F32),
        scratch_shapes=[pltpu.VMEM((bm, d), BF16)],
        compiler_params=_cparams("parallel", "arbitrary"),
        name="norm_proj",
    )(x, norm_g[:, None, :], w)


def _proj_res_body(x_ref, y_ref, w_ref, o_ref):
    o_ref[...] = x_ref[...] + _dot(y_ref[...], w_ref[...])


def _proj_res(x, y, w, w_layer):
    m, d = x.shape
    k = y.shape[1]
    bm = _pick_block(m, 512, 16)
    bn = _pick_block(d, 2048, LANES)
    return pl.pallas_call(
        _proj_res_body,
        grid=(m // bm, d // bn),
        in_specs=[
            pl.BlockSpec((bm, bn), lambda i, j: (i, j)),
            pl.BlockSpec((bm, k), lambda i, j: (i, 0)),
            _w_spec(w, (k, bn), lambda i, j: (0, j), w_layer),
        ],
        out_specs=pl.BlockSpec((bm, bn), lambda i, j: (i, j)),
        out_shape=jax.ShapeDtypeStruct((m, d), F32),
        compiler_params=_cparams("parallel", "arbitrary"),
        name="proj_res",
    )(x, y, w)


def _proj_res_even_body(x_ref, opt_ref, os_ref, g_ref, ohg_ref, w1_ref, w2_ref, o_ref, *, n_prompt_blk):
    o_rw = jnp.where(pl.program_id(0) < n_prompt_blk, opt_ref[...].T, os_ref[...])
    y1 = (o_rw * g_ref[...]).astype(BF16)
    o_ref[...] = x_ref[...] + _dot(y1, w1_ref[...]) + _dot(ohg_ref[...], w2_ref[...])


def _proj_res_even(x, o_pt, o_s, g, o_hg, w, w_layer):
    m, d = x.shape
    bp, rw, tp = o_pt.shape
    ms = o_s.shape[0]
    hg = o_hg.shape[1]
    assert rw == hg and bp * tp + ms == m
    bm = _pick_block(_gcd(tp, ms), 512, LANES)
    bn = _pick_block(d, 2048, LANES)
    npb, tpb = bp * tp // bm, tp // bm

    def pt_idx(i, j):
        ic = jnp.minimum(i, npb - 1)
        return ic // tpb, 0, ic % tpb

    return pl.pallas_call(
        functools.partial(_proj_res_even_body, n_prompt_blk=npb),
        grid=(m // bm, d // bn),
        in_specs=[
            pl.BlockSpec((bm, bn), lambda i, j: (i, j)),
            pl.BlockSpec((None, rw, bm), pt_idx),
            pl.BlockSpec((bm, rw), lambda i, j: (jnp.maximum(i - npb, 0), 0)),
            pl.BlockSpec((bm, rw), lambda i, j: (i, 0)),
            pl.BlockSpec((bm, hg), lambda i, j: (i, 0)),
            _w_spec(w, (rw, bn), lambda i, j: (0, j), w_layer),
            _w_spec(w, (hg, bn), lambda i, j: (1, j), w_layer),
        ],
        out_specs=pl.BlockSpec((bm, bn), lambda i, j: (i, j)),
        out_shape=jax.ShapeDtypeStruct((m, d), F32),
        compiler_params=_cparams("parallel", "arbitrary"),
        name="proj_res_even",
    )(x, o_pt, o_s, g, o_hg, w, w)


def _final_norm_body(x_ref, g_ref, op_ref, os_ref, *, n_prompt_blk):
    y = _rms_rows(x_ref[...], g_ref[...])
    i = pl.program_id(0)

    @pl.when(i < n_prompt_blk)
    def _():
        op_ref[...] = y

    @pl.when(i >= n_prompt_blk)
    def _():
        os_ref[...] = y


def _final_norm(x, g, mp):
    m, d = x.shape
    ms = m - mp
    bm = _pick_block(_gcd(mp, ms), MAX_BM, SUBLANES)
    npb = mp // bm
    return pl.pallas_call(
        functools.partial(_final_norm_body, n_prompt_blk=npb),
        grid=(m // bm,),
        in_specs=[pl.BlockSpec((bm, d), lambda i: (i, 0)), pl.BlockSpec((1, d), lambda i: (0, 0))],
        out_specs=[pl.BlockSpec((bm, d), lambda i: (jnp.minimum(i, npb - 1), 0)),
                   pl.BlockSpec((bm, d), lambda i: (jnp.maximum(i - npb, 0), 0))],
        out_shape=[jax.ShapeDtypeStruct((mp, d), F32), jax.ShapeDtypeStruct((ms, d), F32)],
        compiler_params=_cparams("arbitrary"),
        name="final_norm",
    )(x, g[None, :])


def _rwkv_prep_body(cur_ref, tail_ref, prevs_ref, shiftp_ref, shifts_ref, mu_ref, w0_ref, w2_ref, a0_ref, a2_ref,
                    g2_ref, r_ref, k_ref, v_ref, w_ref, a_ref, g_ref, *, rw, dwp, dap, n_prompt_blk, blk_per_seq):
    i = pl.program_id(0)
    rows = cur_ref.shape[0]
    prw = cur_ref.shape[1]
    seq = jnp.minimum(i // blk_per_seq, shiftp_ref.shape[0] - 1)
    first = (i % blk_per_seq) == 0
    is_sample = i >= n_prompt_blk
    rid = lax.broadcasted_iota(jnp.int32, (rows, 1), 0)
    shift_row = shiftp_ref[pl.ds(seq, 1), :]

    def shifted(c0, c1):
        cs = slice(c0, c1)
        cur = cur_ref[:, cs]
        row0 = jnp.where(first, shift_row[:, cs], tail_ref[SUBLANES - 1:SUBLANES, cs])
        prev_p = jnp.where(rid == 0, row0, pltpu.roll(cur, 1, 0))
        prev_s = jnp.where(i == n_prompt_blk, shifts_ref[:, cs], prevs_ref[:, cs])
        prev = jnp.where(is_sample, prev_s, prev_p)
        return cur + (prev - cur) * mu_ref[:, cs]

    r_ref[...] = shifted(0, rw).T
    k_ref[...] = shifted(rw, 2 * rw).T
    v_ref[...] = shifted(2 * rw, 3 * rw).T
    o = 3 * rw
    lw = w0_ref[...] + _dot(jnp.tanh(shifted(o, o + dwp)).astype(BF16), w2_ref[...])
    w_log = -_softplus(-lw) - 0.5
    w_ref[...] = jnp.exp(-jnp.exp(w_log)).T
    a_ref[...] = jax.nn.sigmoid(a0_ref[...] + _dot(shifted(o + dwp, o + dwp + dap).astype(BF16), a2_ref[...])).T
    g_ref[...] = _dot(jax.nn.sigmoid(shifted(o + dwp + dap, prw)).astype(BF16), g2_ref[...])


def _rwkv_prep(p_rw, shift_p, shift_s, mu, w0, w2, a0, a2, g2, *, rw, mp, tp, bs):
    m, prw = p_rw.shape
    ms = m - mp
    assert mp % bs == 0 and tp % bs == 0 and ms % bs == 0
    npb = mp // bs
    dwp, dap = w2.shape[0], a2.shape[0]
    full = lambda a: pl.BlockSpec(a.shape, lambda i: (0,) * a.ndim)
    bps = tp // bs
    bp = mp // tp

    ts = ms // bs
    assert ts % bp == 0

    def out_idx(i):
        j = i - npb
        return jnp.where(i < npb, i // bps, j % bp), jnp.where(i < npb, i % bps, bps + j // bp)

    spec_o = pl.BlockSpec((rw, bs), out_idx)
    body = functools.partial(_rwkv_prep_body, rw=rw, dwp=dwp, dap=dap, n_prompt_blk=npb, blk_per_seq=tp // bs)
    outs = pl.pallas_call(
        body,
        grid=(m // bs,),
        in_specs=[
            pl.BlockSpec((bs, prw), lambda i: (i, 0)),
            pl.BlockSpec((SUBLANES, prw), lambda i: (jnp.maximum(i * (bs // SUBLANES) - 1, 0), 0)),
            pl.BlockSpec((bs, prw), lambda i: (jnp.maximum(i - 1, npb), 0)),
            full(shift_p), full(shift_s), full(mu), full(w0), full(w2), full(a0), full(a2), full(g2),
        ],
        out_specs=[spec_o] * 5 + [pl.BlockSpec((bs, rw), lambda i: (i, 0))],
        out_shape=[jax.ShapeDtypeStruct((bp * rw, tp + ms // bp), F32)] * 5 + [jax.ShapeDtypeStruct((m, rw), F32)],
        compiler_params=_cparams("arbitrary"),
        name="rwkv_prep",
    )(p_rw, p_rw, p_rw, shift_p, shift_s, mu, w0, w2, a0, a2, g2)
    return outs[:5], outs[5]


def _rwkv_scan_body(r_ref, w_ref, k_ref, a_ref, v_ref, kk_p_ref, ka_p_ref, rk_p_ref, lnw_ref, lnb_ref, s0_ref,
                    o_ref, sout_ref, s_ref, r_s, w_s, nkk_s, b_s, km_s, v_s, o_s, *, vsplit, state_rows):
    tc_len, nv, lanes = v_s.shape
    kc = s_ref.shape[1]
    ti = pl.program_id(1)

    @pl.when(ti == 0)
    def _():
        if state_rows:
            s_ref[...] = s0_ref[...].reshape(lanes, nv * kc).T.reshape(nv, kc, lanes)
        else:
            s_ref[...] = s0_ref[...]

    def widen(x):
        return jnp.concatenate([x, x], axis=-1) if vsplit else x

    if vsplit:
        v_in = v_ref[...]
        v_s[...] = jnp.concatenate([v_in[:, :nv, :], v_in[:, nv:, :]], axis=-1)
    else:
        v_s[...] = v_ref[...]
    kvec = widen(k_ref[...])
    a_ = widen(a_ref[...])
    r_s[...] = widen(r_ref[...])
    w_s[...] = widen(w_ref[...])
    kk = kvec * kk_p_ref[...][None]
    nrm = jnp.sqrt(jnp.sum(kk * kk, axis=1, keepdims=True))
    kk = kk / jnp.maximum(nrm, 1e-12)
    nkk_s[...] = -kk
    b_s[...] = kk * a_
    km_s[...] = kvec * (1.0 + (a_ - 1.0) * ka_p_ref[...][None])

    def step(t, carry):
        nkk = nkk_s[t]
        b = b_s[t]
        km = km_s[t]
        w = w_s[t]
        r = r_s[t]
        vt = v_s[t]
        for vi in range(nv):
            sv = s_ref[vi]
            sa = jnp.sum(sv * nkk, axis=0, keepdims=True)
            sv = sv * w + sa * b + vt[vi:vi + 1, :] * km
            s_ref[vi] = sv
            o_s[t, pl.ds(vi, 1), :] = jnp.sum(sv * r, axis=0, keepdims=True)
        return carry

    lax.fori_loop(0, tc_len, step, 0)

    o = o_s[...]
    n_val = nv * (2 if vsplit else 1)

    def head_sum(x):
        if vsplit:
            x2 = x.reshape(tc_len * nv, lanes)
            x = (x2 + pltpu.roll(x2, lanes // 2, 1)).reshape(tc_len, nv, lanes)
        return jnp.sum(x, axis=1, keepdims=True)

    mu = head_sum(o) / n_val
    d = o - mu
    var = head_sum(d * d) / n_val
    o = d * lax.rsqrt(var + RW_GN_EPS) * lnw_ref[...][None] + lnb_ref[...][None]
    bonus = jnp.sum(r_s[...] * km_s[...] * rk_p_ref[...][None], axis=1, keepdims=True)
    o = o + bonus * v_s[...]
    if vsplit:
        o_ref[...] = jnp.concatenate([o[:, :, :lanes // 2], o[:, :, lanes // 2:]], axis=1)
    else:
        o_ref[...] = o

    @pl.when(ti == pl.num_programs(1) - 1)
    def _():
        if state_rows:
            sout_ref[...] = s_ref[...].reshape(nv * kc, lanes).T.reshape(sout_ref.shape)
        else:
            sout_ref[...] = s_ref[...]


def _rwkv_scan(r, w, k, a, v, kk_p, ka_p, rk_p, lnw, lnb, s0, *, t, tc, vsplit, layer=None, after=None):
    _, kc, kl = r.shape
    nv = v.shape[1] // 2 if vsplit else v.shape[1]
    ln = v.shape[2] * 2 if vsplit else v.shape[2]
    g = ln // LANES
    kspec = pl.BlockSpec((tc, kc, kl // g), lambda gi, ti: (ti, 0, gi))
    vspec = pl.BlockSpec((tc, nv, LANES), lambda gi, ti: (ti, 0, gi))
    pk = pl.BlockSpec((kc, LANES), lambda gi, ti: (0, 0))
    pv = pl.BlockSpec((nv, LANES), lambda gi, ti: (0, 0))
    state_rows = not vsplit
    operands = [r, w, k, a, v, kk_p, ka_p, rk_p, lnw, lnb, s0]
    in_specs = [kspec, kspec, kspec, kspec, kspec if vsplit else vspec, pk, pk, pk, pv, pv]
    aliases = {}
    if state_rows:
        nb, _, nh, sz = s0.shape
        assert sz == nv * kc and nb * nh == ln
        sspec = pl.BlockSpec((LANES // nh, None, nh, sz), lambda gi, ti: (gi, layer, 0, 0))
        in_specs.append(sspec)
        aliases = {10: 1}
        s_shape = s0.shape
    else:
        sspec = pl.BlockSpec((nv, kc, LANES), lambda gi, ti: (0, 0, gi))
        in_specs.append(sspec)
        s_shape = (nv, kc, ln)
    if after is not None:
        operands.append(after)
        in_specs.append(pl.BlockSpec(memory_space=pl.ANY))

    def body(*refs):
        n_in = len(operands)
        ins = refs[:11]
        _rwkv_scan_body(*ins, *refs[n_in:], vsplit=vsplit, state_rows=state_rows)

    return pl.pallas_call(
        body,
        grid=(g, t // tc),
        in_specs=in_specs,
        out_specs=[kspec if vsplit else vspec, sspec],
        out_shape=[jax.ShapeDtypeStruct((t, kc, kl) if vsplit else (t, nv, ln), F32),
                   jax.ShapeDtypeStruct(s_shape, F32)],
        scratch_shapes=[pltpu.VMEM((nv, kc, LANES), F32)] + [pltpu.VMEM((tc, kc, LANES), F32)] * 5
        + [pltpu.VMEM((tc, nv, LANES), F32)] * 2,
        input_output_aliases=aliases,
        compiler_params=_cparams("parallel", "arbitrary"),
        name="rwkv_scan",
    )(*operands)


def _ref_rows(gc, j):
    c, w = gc.shape
    b = 1 << j
    nt = c // SUBLANES
    gv = gc.reshape(nt, SUBLANES, w)
    if 2 * b <= SUBLANES:
        sub = lax.broadcasted_iota(jnp.int32, (1, SUBLANES, 1), 1)
        out = None
        for g0 in range(0, SUBLANES, 2 * b):
            piece = jnp.broadcast_to(gv[:, g0 + b - 1:g0 + b, :], gv.shape)
            out = piece if out is None else jnp.where(sub >= g0, piece, out)
        return out.reshape(c, w)
    tiles_per_group = 2 * b // SUBLANES
    pieces = []
    for ti in range(nt):
        src = (ti // tiles_per_group) * tiles_per_group + b // SUBLANES - 1
        pieces.append(jnp.broadcast_to(gv[src, SUBLANES - 1:SUBLANES, :], (SUBLANES, w)))
    return jnp.concatenate(pieces, axis=0)


def _hgrn_chunk(q, f, val, og, get_st, set_st, loglb, log1mlb, nrm_w, seq_len):
    c, width = q.shape
    nh = width // HG_DIM
    nseq = c // seq_len
    lg = seq_len.bit_length() - 1
    assert (1 << lg) == seq_len and c % SUBLANES == 0
    ls = jnp.minimum(f, 0.0) - jnp.log1p(jnp.exp(-jnp.abs(f)))
    b_ = log1mlb + ls
    g = jnp.maximum(loglb, b_) + jnp.log1p(jnp.exp(-jnp.abs(loglb - b_)))
    kk = _neg_expm1(g)
    qs = q * jax.nn.sigmoid(q)
    row = lax.broadcasted_iota(jnp.int32, (c, 1), 0)
    col = lax.broadcasted_iota(jnp.int32, (1, c), 1)
    tpos = row & (seq_len - 1)
    gc = g
    for j in range(lg):
        s = 1 << j
        gc = gc + jnp.where(tpos >= s, pltpu.roll(gc, s, 0), 0.0)
    qbs, kbs, sames = [qs.astype(BF16)], [kk.astype(BF16)], [row == col]
    for j in range(lg):
        e = jnp.exp(-jnp.abs(gc - _ref_rows(gc, j)))
        is_q = ((row >> j) & 1) == 1
        pk = (jnp.where(is_q, qs, kk) * e).astype(BF16)
        qbs.append(pk)
        kbs.append(pk)
        sames.append(((row >> (j + 1)) == (col >> (j + 1))) & is_q & (((col >> j) & 1) == 0))
    vb = val.astype(BF16)
    qg = (qs * jnp.exp(gc)).astype(BF16)
    k2s, eglast, in_seq = [], [], []
    for s_i in range(nseq):
        glast = gc[s_i * seq_len + seq_len - 1:s_i * seq_len + seq_len, :]
        eglast.append(jnp.exp(glast))
        if nseq == 1:
            in_seq.append(None)
            k2s.append((kk * jnp.exp(glast - gc)).astype(BF16))
        else:
            in_s = (row >> lg) == s_i
            in_seq.append(in_s)
            k2s.append(jnp.where(in_s, kk * jnp.exp(jnp.minimum(glast - gc, 0.0)), 0.0).astype(BF16))
    outs = []
    for h in range(nh):
        cs = slice(h * HG_DIM, (h + 1) * HG_DIM)
        amat = None
        for qb, kb, same in zip(qbs, kbs, sames):
            term = jnp.where(same, _dot_nt(qb[:, cs], kb[:, cs]), 0.0)
            amat = term if amat is None else amat + term
        o = _dot(amat.astype(BF16), vb[:, cs])
        for s_i in range(nseq):
            st = get_st(s_i, h)
            o_int = _dot_nt(qg[:, cs], st.astype(BF16))
            o = o + (o_int if nseq == 1 else jnp.where(in_seq[s_i], o_int, 0.0))
            set_st(s_i, h, st * eglast[s_i][:, cs] + _dot_tn(vb[:, cs], k2s[s_i][:, cs]))
        outs.append(o * lax.rsqrt(jnp.mean(o * o, axis=-1, keepdims=True) + HG_NORM_EPS))
    on = jnp.concatenate(outs, axis=1) * nrm_w
    return on * (og * jax.nn.sigmoid(og))


def _hgrn_prompt_body(q_ref, f_ref, i_ref, og_ref, s0_ref, loglb_ref, log1m_ref, nw_ref, o_ref, sout_ref, st_ref,
                      *, chunk):
    nh = st_ref.shape[0]
    tb = pl.program_id(1)

    @pl.when(tb == 0)
    def _():
        for h in range(nh):
            st_ref[h] = s0_ref[h].T

    def set_st(s_i, h, v):
        st_ref[h] = v

    def body(ci, carry):
        rows = pl.ds(pl.multiple_of(ci * chunk, chunk), chunk)
        y = _hgrn_chunk(q_ref[rows, :], f_ref[rows, :], i_ref[rows, :], og_ref[rows, :],
                        lambda s_i, h: st_ref[h], set_st, loglb_ref[...], log1m_ref[...], nw_ref[...], chunk)
        o_ref[rows, :] = y.astype(BF16)
        return carry

    lax.fori_loop(0, q_ref.shape[0] // chunk, body, 0)

    @pl.when(tb == pl.num_programs(1) - 1)
    def _():
        for h in range(nh):
            sout_ref[h] = st_ref[h].T


def _hgrn_prompt(p_hg, s0, loglb, log1m, nw, *, bp, tp, chunk):
    hgw = p_hg.shape[1] // 4
    nh = hgw // HG_DIM
    tb = _pick_block(tp, 512, chunk)
    nt = tp // tb
    cspec = lambda c: pl.BlockSpec((tb, hgw), lambda b, t: (b * nt + t, c))
    pspec = pl.BlockSpec((1, hgw), lambda b, t: (0, 0))
    sspec = pl.BlockSpec((None, nh, HG_DIM, HG_DIM), lambda b, t: (b, 0, 0, 0))
    return pl.pallas_call(
        functools.partial(_hgrn_prompt_body, chunk=chunk),
        grid=(bp, nt),
        in_specs=[cspec(0), cspec(1), cspec(2), cspec(3), sspec, pspec, pspec, pspec],
        out_specs=[pl.BlockSpec((tb, hgw), lambda b, t: (b * nt + t, 0)), sspec],
        out_shape=[jax.ShapeDtypeStruct((p_hg.shape[0], hgw), BF16), jax.ShapeDtypeStruct(s0.shape, F32)],
        scratch_shapes=[pltpu.VMEM((nh, HG_DIM, HG_DIM), F32)],
        compiler_params=_cparams("parallel", "arbitrary"),
        name="hgrn_prompt",
    )(p_hg, p_hg, p_hg, p_hg, s0, loglb, log1m, nw)


def _hgrn_sample_body(q_ref, f_ref, i_ref, og_ref, s0_ref, loglb_ref, log1m_ref, nw_ref, *rest, seq_len, nseq):
    o_ref, sout_ref = rest[-2:]
    c = seq_len * nseq

    def body(gi, carry):
        rows = pl.ds(pl.multiple_of(gi * c, c), c)

        def set_st(s_i, h, v):
            sout_ref[gi * nseq + s_i, h] = v.T

        y = _hgrn_chunk(q_ref[rows, :], f_ref[rows, :], i_ref[rows, :], og_ref[rows, :],
                        lambda s_i, h: s0_ref[gi * nseq + s_i, h].T, set_st,
                        loglb_ref[...], log1m_ref[...], nw_ref[...], seq_len)
        o_ref[rows, :] = y.astype(BF16)
        return carry

    lax.fori_loop(0, q_ref.shape[0] // c, body, 0)


def _hgrn_sample(p_hg_bm, s0, layer, loglb, log1m, nw, *, bs, ts, s_prev=None, after=None):
    hgw = p_hg_bm.shape[1] // 4
    nh = hgw // HG_DIM
    nseq = max(1, 16 // ts)
    assert bs % nseq == 0
    bb = _pick_block(bs, 8, nseq)
    rows = bb * ts
    cspec = lambda c: pl.BlockSpec((rows, hgw), lambda i: (i, c))
    pspec = pl.BlockSpec((1, hgw), lambda i: (0, 0))
    sspec = pl.BlockSpec((bb, None, nh, HG_DIM, HG_DIM), lambda i: (i, layer, 0, 0, 0))
    operands = [p_hg_bm, p_hg_bm, p_hg_bm, p_hg_bm, s0, loglb, log1m, nw]
    in_specs = [cspec(0), cspec(1), cspec(2), cspec(3), sspec, pspec, pspec, pspec]
    aliases = {}
    if s_prev is not None:
        operands.append(s_prev)
        in_specs.append(pl.BlockSpec(memory_space=pl.ANY))
        aliases = {len(operands) - 1: 1}
    if after is not None:
        operands.append(after)
        in_specs.append(pl.BlockSpec(memory_space=pl.ANY))
    return pl.pallas_call(
        functools.partial(_hgrn_sample_body, seq_len=ts, nseq=nseq),
        grid=(bs // bb,),
        in_specs=in_specs,
        out_specs=[pl.BlockSpec((rows, hgw), lambda i: (i, 0)), sspec],
        out_shape=[jax.ShapeDtypeStruct((bs * ts, hgw), BF16), jax.ShapeDtypeStruct(s0.shape, F32)],
        input_output_aliases=aliases,
        compiler_params=_cparams("parallel"),
        name="hgrn_sample",
    )(*operands)


def _lru_gates(xconv, wa_ref, wx_ref, ba, bx, lam):
    l = xconv.shape[1]
    blk = l // LRU_BLOCKS
    ga, gx = [], []
    for n in range(LRU_BLOCKS):
        xb = xconv[:, n * blk:(n + 1) * blk].astype(BF16)
        ga.append(_dot(xb, wa_ref[n]))
        gx.append(_dot(xb, wx_ref[n]))
    ga = jnp.concatenate(ga, axis=1) + ba
    gx = jnp.concatenate(gx, axis=1) + bx
    log_a = -LRU_C * jax.nn.sigmoid(ga) * _softplus(-lam)
    a = jnp.exp(log_a)
    mult = jnp.sqrt(1.0 - a * a)
    return a, mult * jax.nn.sigmoid(gx) * xconv


def _gelu_tanh(x):
    c = 0.7978845608028654
    return 0.5 * x * (1.0 + jnp.tanh(c * (x + 0.044715 * (x * x * x))))


def _odd_prompt_body(gate_ref, xb_ref, conv0_ref, h0_ref, cw_ref, cb_ref, wa_ref, wx_ref, ba_ref, bx_ref, lam_ref,
                     y_ref, convo_ref, ho_ref, ext_ref, hc_ref, a_s, b_s, h_s):
    rows = xb_ref.shape[0]
    hist = SUBLANES

    @pl.when(pl.program_id(1) == 0)
    def _():
        ext_ref[0:hist, :] = conv0_ref[...]
        hc_ref[...] = h0_ref[...]

    ext_ref[hist:hist + rows, :] = xb_ref[...]
    cw = cw_ref[...]
    xconv = cw[0:1, :] * ext_ref[pl.ds(hist - (CONV_W - 1), rows), :]
    for j in range(1, CONV_W):
        xconv = xconv + cw[j:j + 1, :] * ext_ref[pl.ds(hist - (CONV_W - 1) + j, rows), :]
    xconv = cb_ref[...] + xconv
    a, b = _lru_gates(xconv, wa_ref, wx_ref, ba_ref[...], bx_ref[...], lam_ref[...])
    width = a.shape[1]
    a = a.reshape(rows // SUBLANES, SUBLANES, width)
    b = b.reshape(rows // SUBLANES, SUBLANES, width)
    r8 = lax.broadcasted_iota(jnp.int32, (1, SUBLANES, 1), 1)
    for s in (1, 2, 4):
        m = r8 >= s
        a_sh = pltpu.roll(a, s, 1)
        b_sh = pltpu.roll(b, s, 1)
        b = jnp.where(m, a * b_sh + b, b)
        a = jnp.where(m, a * a_sh, a)
    a_s[...] = a.reshape(rows, width)
    b_s[...] = b.reshape(rows, width)

    def tile(j, hc):
        r0 = pl.multiple_of(j * SUBLANES, SUBLANES)
        h = a_s[pl.ds(r0, SUBLANES), :] * hc + b_s[pl.ds(r0, SUBLANES), :]
        h_s[pl.ds(r0, SUBLANES), :] = h
        return h[SUBLANES - 1:SUBLANES, :]

    hc = lax.fori_loop(0, rows // SUBLANES, tile, hc_ref[...])
    hc_ref[...] = hc
    ho_ref[...] = hc
    y_ref[...] = (h_s[...] * _gelu_tanh(gate_ref[...])).astype(BF16)
    tail = ext_ref[rows:rows + hist, :]
    ext_ref[0:hist, :] = tail
    convo_ref[...] = tail


def _odd_prompt(p_od, conv0, h0, layer, cw, cb, wa, wx, ba, bx, lam, *, bp, tp):
    l = p_od.shape[1] // 2
    rows = _pick_block(tp, 256, SUBLANES)
    nt = tp // rows
    lsel = lambda a: pl.BlockSpec((None,) + a.shape[1:], lambda b, t: (layer,) + (0,) * (a.ndim - 1))
    return pl.pallas_call(
        _odd_prompt_body,
        grid=(bp, nt),
        in_specs=[
            pl.BlockSpec((rows, l), lambda b, t: (b * nt + t, 0)),
            pl.BlockSpec((rows, l), lambda b, t: (b * nt + t, 1)),
            pl.BlockSpec((None, SUBLANES, l), lambda b, t: (b, 0, 0)),
            pl.BlockSpec((None, 1, l), lambda b, t: (b, 0, 0)),
            lsel(cw), lsel(cb), lsel(wa), lsel(wx), lsel(ba), lsel(bx), lsel(lam),
        ],
        out_specs=[
            pl.BlockSpec((rows, l), lambda b, t: (b * nt + t, 0)),
            pl.BlockSpec((None, SUBLANES, l), lambda b, t: (b, 0, 0)),
            pl.BlockSpec((None, 1, l), lambda b, t: (b, 0, 0)),
        ],
        out_shape=[jax.ShapeDtypeStruct((p_od.shape[0], l), BF16), jax.ShapeDtypeStruct((bp, SUBLANES, l), F32),
                   jax.ShapeDtypeStruct((bp, 1, l), F32)],
        scratch_shapes=[pltpu.VMEM((rows + SUBLANES, l), F32), pltpu.VMEM((1, l), F32),
                        pltpu.VMEM((rows, l), F32), pltpu.VMEM((rows, l), F32), pltpu.VMEM((rows, l), F32)],
        compiler_params=_cparams("parallel", "arbitrary"),
        name="odd_prompt",
    )(p_od, p_od, conv0, h0, cw, cb, wa, wx, ba, bx, lam)


def _odd_sample_body(gate_ref, xb_ref, conv0_ref, h0_ref, cw_ref, cb_ref, wa_ref, wx_ref, ba_ref, bx_ref, lam_ref,
                     yall_ref, y_ref, convo_ref, ho_ref):
    @pl.when(pl.program_id(0) == 0)
    def _():
        convo_ref[...] = conv0_ref[...]
        ho_ref[...] = h0_ref[...]

    xb = xb_ref[...]
    cw = cw_ref[...]
    xconv = cw[CONV_W - 1:CONV_W, :] * xb
    for j in range(CONV_W - 1):
        xconv = xconv + cw[j:j + 1, :] * convo_ref[j]
    xconv = cb_ref[...] + xconv
    a, b = _lru_gates(xconv, wa_ref, wx_ref, ba_ref[...], bx_ref[...], lam_ref[...])
    h = a * ho_ref[...] + b
    ho_ref[...] = h
    y_ref[...] = (h * _gelu_tanh(gate_ref[...])).astype(BF16)
    for j in range(CONV_W - 2):
        convo_ref[j] = convo_ref[j + 1]
    convo_ref[CONV_W - 2] = xb


def _odd_sample(p_od, conv0_t, h0, layer, cw, cb, wa, wx, ba, bx, lam, y_all, *, mp, bs, ts):
    l = p_od.shape[1] // 2
    assert mp % bs == 0
    off = mp // bs
    lsel = lambda a: pl.BlockSpec((None,) + a.shape[1:], lambda t: (layer,) + (0,) * (a.ndim - 1))
    full = lambda a: pl.BlockSpec(a.shape, lambda t: (0,) * a.ndim)
    return pl.pallas_call(
        _odd_sample_body,
        grid=(ts,),
        in_specs=[
            pl.BlockSpec((bs, l), lambda t: (off + t, 0)),
            pl.BlockSpec((bs, l), lambda t: (off + t, 1)),
            full(conv0_t), full(h0),
            lsel(cw), lsel(cb), lsel(wa), lsel(wx), lsel(ba), lsel(bx), lsel(lam),
            pl.BlockSpec(memory_space=pl.ANY),
        ],
        out_specs=[pl.BlockSpec((bs, l), lambda t: (off + t, 0)), full(conv0_t), full(h0)],
        out_shape=[jax.ShapeDtypeStruct(y_all.shape, BF16), jax.ShapeDtypeStruct(conv0_t.shape, F32),
                   jax.ShapeDtypeStruct(h0.shape, F32)],
        input_output_aliases={11: 0},
        compiler_params=_cparams("arbitrary"),
        name="odd_sample",
    )(p_od, p_od, conv0_t, h0, cw, cb, wa, wx, ba, bx, lam, y_all)


def _pad_last(a, n):
    return jnp.pad(a, [(0, 0)] * (a.ndim - 1) + [(0, n - a.shape[-1])])


def _round_up(n, m):
    return (n + m - 1) // m * m


class _RwLayout:
    def __init__(self, rw, dw, da, dg):
        self.rw, self.dw, self.da, self.dg = rw, dw, da, dg
        self.dwp, self.dap, self.dgp = (_round_up(d, LANES) for d in (dw, da, dg))
        self.width = 3 * rw + dw + da + dg
        self.padded = 3 * rw + self.dwp + self.dap + self.dgp

    def pad(self, a):
        o = 3 * self.rw
        parts = [a[..., :o], _pad_last(a[..., o:o + self.dw], self.dwp),
                 _pad_last(a[..., o + self.dw:o + self.dw + self.da], self.dap),
                 _pad_last(a[..., o + self.dw + self.da:self.width], self.dgp)]
        return jnp.concatenate(parts, axis=-1)

    def unpad(self, a):
        o = 3 * self.rw
        parts = [a[..., :o], a[..., o:o + self.dw], a[..., o + self.dwp:o + self.dwp + self.da],
                 a[..., o + self.dwp + self.dap:o + self.dwp + self.dap + self.dg]]
        return jnp.concatenate(parts, axis=-1)


def kernel(x_prompt, x_sample, state_rwkv_shift, state_rwkv, state_hgrn, state_conv, state_lru, ffn1_norm, ffn1_w_gu, ffn1_w_down, mix_norm, ffn2_norm, ffn2_w_gu, ffn2_w_down, ev_w_in, rw_mu, rw_w0, rw_w2, rw_a0, rw_a2, rw_g2, rw_k_k, rw_k_a, rw_r_k, rw_ln_w, rw_ln_b, hg_lb, hg_norm, ev_w_out, od_w_in, conv_w, conv_b, lru_wa, lru_ba, lru_wx, lru_bx, lru_lambda, od_w_out, final_norm):
    bp, tp, d = x_prompt.shape
    bs, ts, _ = x_sample.shape
    depth = ffn1_norm.shape[0]
    n_even = ev_w_in.shape[0]
    mp, ms = bp * tp, bs * ts
    rw = rw_w0.shape[1]
    nh_rw = rw // RW_HEAD
    hgw = hg_norm.shape[1]
    nh_hg = hgw // HG_DIM
    lay = _RwLayout(rw, rw_w2.shape[1], rw_a2.shape[1], rw_g2.shape[1])
    assert bp * nh_rw * 2 == LANES and (bs * nh_rw) % LANES == 0
    dt = x_prompt.dtype

    ffn_src = {1: (ffn1_norm, ffn1_w_gu, ffn1_w_down), 2: (ffn2_norm, ffn2_w_gu, ffn2_w_down)}
    ffn_w = (ffn1_w_gu[0].astype(BF16), ffn1_w_down[0].astype(BF16))

    def ffn_step(x, l, which, ffn_w):
        norm_g = ffn_src[which][0]
        nxt = (l, 2) if which == 1 else (l + 1, 1)
        if nxt[0] >= depth:
            return _ffn(x, norm_g, l, *ffn_w)[0], None
        x, c_gu, c_dn = _ffn(x, norm_g, l, *ffn_w, next_src=(ffn_src[nxt[1]][1], ffn_src[nxt[1]][2], nxt[0]))
        return x, (c_gu, c_dn)

    ev_out = ev_w_out.astype(BF16)
    od_in, od_out = od_w_in.astype(BF16), od_w_out.astype(BF16)

    wa_bf, wx_bf = lru_wa.astype(BF16), lru_wx.astype(BF16)

    x = jnp.concatenate([x_prompt.reshape(mp, d), x_sample.transpose(1, 0, 2).reshape(ms, d)], axis=0)

    lbs = jnp.cumsum(jax.nn.softmax(hg_lb.astype(F32), axis=0), axis=0)
    lb_all = lbs - lbs[0]

    def k_lanes_p(a):
        return a.reshape(bp * nh_rw, RW_HEAD, a.shape[1]).transpose(2, 1, 0)

    def k_lanes_p_inv(a):
        return a.transpose(2, 1, 0).reshape(bp, rw, tp)

    def k_lanes_s(a):
        a = a.reshape(bp, nh_rw, RW_HEAD, ts // bp, bs).transpose(3, 0, 2, 4, 1)
        return a.reshape(ts, RW_HEAD, bs * nh_rw)

    def k_lanes_s_inv(a):
        return a.reshape(ts, RW_HEAD, bs, nh_rw).transpose(0, 2, 3, 1).reshape(ms, rw)

    def kparam(p):
        return jnp.tile(p.reshape(nh_rw, RW_HEAD).T, (1, LANES // nh_rw))

    def vparam_p(p):
        a = p.reshape(nh_rw, 2, RW_HEAD // 2).transpose(2, 1, 0)[:, :, None, :]
        return jnp.broadcast_to(a, (RW_HEAD // 2, 2, bp, nh_rw)).reshape(RW_HEAD // 2, LANES)

    shifts_p, shifts_s, rws_p, hgs_p, convs_p, convs_s, lrus_p, lrus_s = ([] for _ in range(8))
    rw_rows = state_rwkv.astype(F32).reshape(bs, n_even, nh_rw, RW_HEAD * RW_HEAD)
    hg_new = None

    for l in range(depth):
        x, ffn_w = ffn_step(x, l, 1, ffn_w)
        if l % 2 == 0:
            e = l // 2
            w_in = ev_w_in[e]
            w_rw = lay.pad(w_in[:, :lay.width]).astype(BF16)
            w_hg = w_in[:, lay.width:].astype(BF16)
            p_rw = _norm_proj(x, mix_norm, l, w_rw, _pick_block(lay.padded, 1792, LANES))
            p_hg = _norm_proj(x, mix_norm, l, w_hg, _pick_block(4 * hgw, 1024, LANES))
            shift_p = jnp.zeros((bp, lay.padded), F32)
            shift_s = lay.pad(state_rwkv_shift[:, e].astype(F32))
            rkvwa, g_ = _rwkv_prep(
                p_rw, shift_p, shift_s, lay.pad(rw_mu[e])[None], rw_w0[e][None],
                _pad_last(rw_w2[e].T, lay.dwp).T.astype(BF16), rw_a0[e][None],
                _pad_last(rw_a2[e].T, lay.dap).T.astype(BF16),
                _pad_last(rw_g2[e].T, lay.dgp).T.astype(BF16), rw=rw, mp=mp, tp=tp, bs=bs)
            shifts_p.append(lay.unpad(jnp.concatenate([p_rw[b * tp + tp - 1:(b + 1) * tp] for b in range(bp)])))
            shifts_s.append(lay.unpad(p_rw[mp + (ts - 1) * bs:]))
            kk_p, ka_p = kparam(rw_k_k[e]), kparam(rw_k_a[e])
            rk_p = kparam(rw_r_k[e].reshape(rw))
            lb = lb_all[e][None]
            loglb, log1m = jnp.log(lb), jnp.log1p(-lb)
            nw = hg_norm[e][None]
            chunk = _pick_block(tp, HG_CHUNK, 1)
            oh_p, hs_p = _hgrn_prompt(p_hg, jnp.zeros((bp, nh_hg, HG_DIM, HG_DIM), F32), loglb, log1m, nw,
                                      bp=bp, tp=tp, chunk=chunk)
            r_p, k_p, v_p, w_p, a_p = rkvwa
            o_p, s_p = _rwkv_scan(
                k_lanes_p(r_p), k_lanes_p(w_p), k_lanes_p(k_p), k_lanes_p(a_p), k_lanes_p(v_p),
                kk_p, ka_p, rk_p, vparam_p(rw_ln_w[e]), vparam_p(rw_ln_b[e]),
                jnp.zeros((RW_HEAD // 2, RW_HEAD, LANES), F32), t=tp, tc=_pick_block(tp, 64, 1), vsplit=True,
                after=oh_p)
            rws_p.append(s_p.reshape(RW_HEAD // 2, RW_HEAD, 2, bp, nh_rw).transpose(3, 4, 2, 0, 1)
                         .reshape(bp, nh_rw, RW_HEAD, RW_HEAD))
            r_s, k_s, v_s, w_s, a_s = (a[:, tp:] for a in rkvwa)
            o_s, rw_rows = _rwkv_scan(
                k_lanes_s(r_s), k_lanes_s(w_s), k_lanes_s(k_s), k_lanes_s(a_s), k_lanes_s(v_s),
                kk_p, ka_p, rk_p, kparam(rw_ln_w[e]), kparam(rw_ln_b[e]),
                rw_rows, t=ts, tc=ts, vsplit=False, layer=e, after=o_p)
            o_pt, o_st = k_lanes_p_inv(o_p), k_lanes_s_inv(o_s)
            p_hg_bm = p_hg[mp:].reshape(ts, bs, 4 * hgw).transpose(1, 0, 2).reshape(ms, 4 * hgw)
            oh_s, hg_new = _hgrn_sample(p_hg_bm, state_hgrn.astype(F32), e, loglb, log1m, nw, bs=bs, ts=ts,
                                        s_prev=hg_new, after=o_p)
            oh_s = oh_s.reshape(bs, ts, hgw).transpose(1, 0, 2).reshape(ms, hgw)
            hgs_p.append(hs_p)
            o_hg = oh_p.at[mp:].set(oh_s)
            x = _proj_res_even(x, o_pt, o_st, g_, o_hg, ev_out, e)
        else:
            o_i = l // 2
            p_od = _norm_proj(x, mix_norm, l, od_in, _pick_block(od_in.shape[-1], 1024, LANES), w_layer=o_i)
            lsz = p_od.shape[1] // 2
            vec = lambda a: a[:, None, :]
            args = (conv_w, vec(conv_b), wa_bf, wx_bf, vec(lru_ba), vec(lru_bx), vec(lru_lambda))
            conv0_p = jnp.zeros((bp, SUBLANES, lsz), F32)
            y_p, c_p, h_p = _odd_prompt(p_od, conv0_p, jnp.zeros((bp, 1, lsz), F32), o_i, *args, bp=bp, tp=tp)
            conv0_s = state_conv[:, o_i].astype(F32).transpose(1, 0, 2)
            y_all, c_s, h_s = _odd_sample(p_od, conv0_s, state_lru[:, o_i].astype(F32), o_i, *args, y_p,
                                          mp=mp, bs=bs, ts=ts)
            convs_p.append(c_p[:, SUBLANES - (CONV_W - 1):])
            convs_s.append(c_s.transpose(1, 0, 2))
            lrus_p.append(h_p[:, 0])
            lrus_s.append(h_s)
            x = _proj_res(x, y_all, od_out, o_i)
        x, ffn_w = ffn_step(x, l, 2, ffn_w)

    y_p, y_s = _final_norm(x, final_norm, mp)
    y_prompt = y_p.reshape(bp, tp, d)
    y_sample = y_s.reshape(ts, bs, d).transpose(1, 0, 2)
    st = lambda xs: jnp.stack(xs, axis=1).astype(dt)
    sample_rwkv = rw_rows.reshape(state_rwkv.shape).astype(dt)
    return (y_prompt, y_sample, st(shifts_p), st(rws_p), st(hgs_p), st(convs_p), st(lrus_p),
            st(shifts_s), sample_rwkv, hg_new.astype(dt), st(convs_s), st(lrus_s))
```

```python
import functools
import math

import jax
import jax.numpy as jnp
from jax import lax
from jax.experimental import pallas as pl
from jax.experimental.pallas import tpu as pltpu

F32 = jnp.float32
BF16 = jnp.bfloat16

NORM_EPS = 1e-6
RW_GN_EPS = 64e-5
HG_NORM_EPS = 1e-5
LRU_C = 8.0
RW_HEAD = 64
HG_DIM = 128
LRU_BLOCKS = 8
CONV_W = 4
LANES = 128
SUBLANES = 8
VMEM_LIMIT = 62 * 1024 * 1024
MAX_BM = 1088
HG_CHUNK = 64


def _cparams(*sem):
    return pltpu.CompilerParams(dimension_semantics=sem, vmem_limit_bytes=VMEM_LIMIT)


def _pick_block(n, cap, mult):
    best = None
    for d in range(mult, min(n, cap) + 1, mult):
        if n % d == 0:
            best = d
    assert best is not None, (n, cap, mult)
    return best


def _gcd(a, b):
    return math.gcd(a, b)


def _dot(a, b):
    return jnp.dot(a, b, preferred_element_type=F32)


def _dot_nt(a, b):
    return lax.dot_general(a, b, (((1,), (1,)), ((), ())), preferred_element_type=F32)


def _dot_tn(a, b):
    return lax.dot_general(a, b, (((0,), (0,)), ((), ())), preferred_element_type=F32)


def _softplus(x):
    return jnp.maximum(x, 0.0) + jnp.log1p(jnp.exp(-jnp.abs(x)))


def _neg_expm1(y):
    return 1.0 - jnp.exp(y)


def _rms_rows(x, g):
    ms = jnp.mean(x * x, axis=-1, keepdims=True)
    return x * lax.rsqrt(ms + NORM_EPS) * g


def _ffn_body(x_ref, g_ref, wg_ref, wu_ref, wd_ref, *rest, cast_next):
    if cast_next:
        sgu_ref, sdn_ref, o_ref, cgu_ref, cdn_ref, xn_ref = rest
        cgu_ref[...] = sgu_ref[...].astype(BF16)
        cdn_ref[...] = sdn_ref[...].astype(BF16)
    else:
        o_ref, xn_ref = rest

    @pl.when(pl.program_id(1) == 0)
    def _():
        x = x_ref[...]
        xn_ref[...] = _rms_rows(x, g_ref[...]).astype(BF16)
        o_ref[...] = x

    xn = xn_ref[...]
    gate = _dot(xn, wg_ref[...])
    up = _dot(xn, wu_ref[...])
    act = (0.5 * (gate * jax.nn.sigmoid(gate)) * up).astype(BF16)
    o_ref[...] += _dot(act, wd_ref[...])


def _ffn(x, norm_g, layer, w_gu, w_down, next_src=None):
    m, d = x.shape
    f = w_down.shape[0]
    bm = _pick_block(m, MAX_BM, 16)
    bf = _pick_block(f, 512, LANES)
    ni, nf = m // bm, f // bf
    in_specs = [
        pl.BlockSpec((bm, d), lambda i, j: (i, 0)),
        pl.BlockSpec((None, 1, d), lambda i, j: (layer, 0, 0)),
        pl.BlockSpec((d, bf), lambda i, j: (0, j)),
        pl.BlockSpec((d, bf), lambda i, j: (0, j + nf)),
        pl.BlockSpec((bf, d), lambda i, j: (j, 0)),
    ]
    operands = [x, norm_g[:, None, :], w_gu, w_gu, w_down]
    out_specs = [pl.BlockSpec((bm, d), lambda i, j: (i, 0))]
    out_shape = [jax.ShapeDtypeStruct((m, d), F32)]
    if next_src is not None:
        s_gu, s_dn, nl = next_src
        assert d % ni == 0 and (2 * f) % nf == 0 and (d // ni) % 16 == 0 and (d // ni) % LANES == 0
        gu_blk, dn_blk = (d // ni, 2 * f // nf), (f // nf, d // ni)
        in_specs += [pl.BlockSpec((None,) + gu_blk, lambda i, j: (nl, i, j)),
                     pl.BlockSpec((None,) + dn_blk, lambda i, j: (nl, j, i))]
        operands += [s_gu, s_dn]
        out_specs += [pl.BlockSpec(gu_blk, lambda i, j: (i, j)), pl.BlockSpec(dn_blk, lambda i, j: (j, i))]
        out_shape += [jax.ShapeDtypeStruct((d, 2 * f), BF16), jax.ShapeDtypeStruct((f, d), BF16)]
    return pl.pallas_call(
        functools.partial(_ffn_body, cast_next=next_src is not None),
        grid=(ni, nf),
        in_specs=in_specs,
        out_specs=out_specs,
        out_shape=out_shape,
        scratch_shapes=[pltpu.VMEM((bm, d), BF16)],
        compiler_params=_cparams("parallel", "arbitrary"),
        name="ffn",
    )(*operands)


def _norm_proj_body(x_ref, g_ref, w_ref, o_ref, xn_ref):
    @pl.when(pl.program_id(1) == 0)
    def _():
        xn_ref[...] = _rms_rows(x_ref[...], g_ref[...]).astype(BF16)

    o_ref[...] = _dot(xn_ref[...], w_ref[...])


def _w_spec(w, blk, idx, w_layer):
    if w.ndim == 2:
        return pl.BlockSpec(blk, idx)
    return pl.BlockSpec((None,) + blk, lambda i, j: (w_layer,) + idx(i, j))


def _norm_proj(x, norm_g, layer, w, bn, w_layer=None):
    m, d = x.shape
    n = w.shape[-1]
    bm = _pick_block(m, 512, 16)
    return pl.pallas_call(
        _norm_proj_body,
        grid=(m // bm, n // bn),
        in_specs=[
            pl.BlockSpec((bm, d), lambda i, j: (i, 0)),
            pl.BlockSpec((None, 1, d), lambda i, j: (layer, 0, 0)),
            _w_spec(w, (d, bn), lambda i, j: (0, j), w_layer),
        ],
        out_specs=pl.BlockSpec((bm, bn), lambda i, j: (i, j)),
        out_shape=jax.ShapeDtypeStruct((m, n), F32),
        scratch_shapes=[pltpu.VMEM((bm, d), BF16)],
        compiler_params=_cparams("parallel", "arbitrary"),
        name="norm_proj",
    )(x, norm_g[:, None, :], w)


def _proj_res_body(x_ref, y_ref, w_ref, o_ref):
    o_ref[...] = x_ref[...] + _dot(y_ref[...], w_ref[...])


def _proj_res(x, y, w, w_layer):
    m, d = x.shape
    k = y.shape[1]
    bm = _pick_block(m, 512, 16)
    bn = _pick_block(d, 2048, LANES)
    return pl.pallas_call(
        _proj_res_body,
        grid=(m // bm, d // bn),
        in_specs=[
            pl.BlockSpec((bm, bn), lambda i, j: (i, j)),
            pl.BlockSpec((bm, k), lambda i, j: (i, 0)),
            _w_spec(w, (k, bn), lambda i, j: (0, j), w_layer),
        ],
        out_specs=pl.BlockSpec((bm, bn), lambda i, j: (i, j)),
        out_shape=jax.ShapeDtypeStruct((m, d), F32),
        compiler_params=_cparams("parallel", "arbitrary"),
        name="proj_res",
    )(x, y, w)


def _proj_res_even_body(x_ref, opt_ref, os_ref, g_ref, ohg_ref, w1_ref, w2_ref, o_ref, *, n_prompt_blk):
    o_rw = jnp.where(pl.program_id(0) < n_prompt_blk, opt_ref[...].T, os_ref[...])
    y1 = (o_rw * g_ref[...]).astype(BF16)
    o_ref[...] = x_ref[...] + _dot(y1, w1_ref[...]) + _dot(ohg_ref[...], w2_ref[...])


def _proj_res_even(x, o_pt, o_s, g, o_hg, w, w_layer):
    m, d = x.shape
    bp, rw, tp = o_pt.shape
    ms = o_s.shape[0]
    hg = o_hg.shape[1]
    assert rw == hg and bp * tp + ms == m
    bm = _pick_block(_gcd(tp, ms), 512, LANES)
    bn = _pick_block(d, 2048, LANES)
    npb, tpb = bp * tp // bm, tp // bm

    def pt_idx(i, j):
        ic = jnp.minimum(i, npb - 1)
        return ic // tpb, 0, ic % tpb

    return pl.pallas_call(
        functools.partial(_proj_res_even_body, n_prompt_blk=npb),
        grid=(m // bm, d // bn),
        in_specs=[
            pl.BlockSpec((bm, bn), lambda i, j: (i, j)),
            pl.BlockSpec((None, rw, bm), pt_idx),
            pl.BlockSpec((bm, rw), lambda i, j: (jnp.maximum(i - npb, 0), 0)),
            pl.BlockSpec((bm, rw), lambda i, j: (i, 0)),
            pl.BlockSpec((bm, hg), lambda i, j: (i, 0)),
            _w_spec(w, (rw, bn), lambda i, j: (0, j), w_layer),
            _w_spec(w, (hg, bn), lambda i, j: (1, j), w_layer),
        ],
        out_specs=pl.BlockSpec((bm, bn), lambda i, j: (i, j)),
        out_shape=jax.ShapeDtypeStruct((m, d), F32),
        compiler_params=_cparams("parallel", "arbitrary"),
        name="proj_res_even",
    )(x, o_pt, o_s, g, o_hg, w, w)


def _final_norm_body(x_ref, g_ref, op_ref, os_ref, *, n_prompt_blk):
    y = _rms_rows(x_ref[...], g_ref[...])
    i = pl.program_id(0)

    @pl.when(i < n_prompt_blk)
    def _():
        op_ref[...] = y

    @pl.when(i >= n_prompt_blk)
    def _():
        os_ref[...] = y


def _final_norm(x, g, mp):
    m, d = x.shape
    ms = m - mp
    bm = _pick_block(_gcd(mp, ms), MAX_BM, SUBLANES)
    npb = mp // bm
    return pl.pallas_call(
        functools.partial(_final_norm_body, n_prompt_blk=npb),
        grid=(m // bm,),
        in_specs=[pl.BlockSpec((bm, d), lambda i: (i, 0)), pl.BlockSpec((1, d), lambda i: (0, 0))],
        out_specs=[pl.BlockSpec((bm, d), lambda i: (jnp.minimum(i, npb - 1), 0)),
                   pl.BlockSpec((bm, d), lambda i: (jnp.maximum(i - npb, 0), 0))],
        out_shape=[jax.ShapeDtypeStruct((mp, d), F32), jax.ShapeDtypeStruct((ms, d), F32)],
        compiler_params=_cparams("arbitrary"),
        name="final_norm",
    )(x, g[None, :])


def _rwkv_prep_body(cur_ref, tail_ref, prevs_ref, shiftp_ref, shifts_ref, mu_ref, w0_ref, w2_ref, a0_ref, a2_ref,
                    g2_ref, r_ref, k_ref, v_ref, w_ref, a_ref, g_ref, *, rw, dwp, dap, n_prompt_blk, blk_per_seq):
    i = pl.program_id(0)
    rows = cur_ref.shape[0]
    prw = cur_ref.shape[1]
    seq = jnp.minimum(i // blk_per_seq, shiftp_ref.shape[0] - 1)
    first = (i % blk_per_seq) == 0
    is_sample = i >= n_prompt_blk
    rid = lax.broadcasted_iota(jnp.int32, (rows, 1), 0)
    shift_row = shiftp_ref[pl.ds(seq, 1), :]

    def shifted(c0, c1):
        cs = slice(c0, c1)
        cur = cur_ref[:, cs]
        row0 = jnp.where(first, shift_row[:, cs], tail_ref[SUBLANES - 1:SUBLANES, cs])
        prev_p = jnp.where(rid == 0, row0, pltpu.roll(cur, 1, 0))
        prev_s = jnp.where(i == n_prompt_blk, shifts_ref[:, cs], prevs_ref[:, cs])
        prev = jnp.where(is_sample, prev_s, prev_p)
        return cur + (prev - cur) * mu_ref[:, cs]

    r_ref[...] = shifted(0, rw).T
    k_ref[...] = shifted(rw, 2 * rw).T
    v_ref[...] = shifted(2 * rw, 3 * rw).T
    o = 3 * rw
    lw = w0_ref[...] + _dot(jnp.tanh(shifted(o, o + dwp)).astype(BF16), w2_ref[...])
    w_log = -_softplus(-lw) - 0.5
    w_ref[...] = jnp.exp(-jnp.exp(w_log)).T
    a_ref[...] = jax.nn.sigmoid(a0_ref[...] + _dot(shifted(o + dwp, o + dwp + dap).astype(BF16), a2_ref[...])).T
    g_ref[...] = _dot(jax.nn.sigmoid(shifted(o + dwp + dap, prw)).astype(BF16), g2_ref[...])


def _rwkv_prep(p_rw, shift_p, shift_s, mu, w0, w2, a0, a2, g2, *, rw, mp, tp, bs):
    m, prw = p_rw.shape
    ms = m - mp
    assert mp % bs == 0 and tp % bs == 0 and ms % bs == 0
    npb = mp // bs
    dwp, dap = w2.shape[0], a2.shape[0]
    full = lambda a: pl.BlockSpec(a.shape, lambda i: (0,) * a.ndim)
    bps = tp // bs
    bp = mp // tp

    ts = ms // bs
    assert ts % bp == 0

    def out_idx(i):
        j = i - npb
        return jnp.where(i < npb, i // bps, j % bp), jnp.where(i < npb, i % bps, bps + j // bp)

    spec_o = pl.BlockSpec((rw, bs), out_idx)
    body = functools.partial(_rwkv_prep_body, rw=rw, dwp=dwp, dap=dap, n_prompt_blk=npb, blk_per_seq=tp // bs)
    outs = pl.pallas_call(
        body,
        grid=(m // bs,),
        in_specs=[
            pl.BlockSpec((bs, prw), lambda i: (i, 0)),
            pl.BlockSpec((SUBLANES, prw), lambda i: (jnp.maximum(i * (bs // SUBLANES) - 1, 0), 0)),
            pl.BlockSpec((bs, prw), lambda i: (jnp.maximum(i - 1, npb), 0)),
            full(shift_p), full(shift_s), full(mu), full(w0), full(w2), full(a0), full(a2), full(g2),
        ],
        out_specs=[spec_o] * 5 + [pl.BlockSpec((bs, rw), lambda i: (i, 0))],
        out_shape=[jax.ShapeDtypeStruct((bp * rw, tp + ms // bp), F32)] * 5 + [jax.ShapeDtypeStruct((m, rw), F32)],
        compiler_params=_cparams("arbitrary"),
        name="rwkv_prep",
    )(p_rw, p_rw, p_rw, shift_p, shift_s, mu, w0, w2, a0, a2, g2)
    return outs[:5], outs[5]


def _rwkv_scan_body(r_ref, w_ref, k_ref, a_ref, v_ref, kk_p_ref, ka_p_ref, rk_p_ref, lnw_ref, lnb_ref, s0_ref,
                    o_ref, sout_ref, s_ref, r_s, w_s, nkk_s, b_s, km_s, v_s, o_s, *, vsplit, state_rows):
    tc_len, nv, lanes = v_s.shape
    kc = s_ref.shape[1]
    ti = pl.program_id(1)

    @pl.when(ti == 0)
    def _():
        if state_rows:
            s_ref[...] = s0_ref[...].reshape(lanes, nv * kc).T.reshape(nv, kc, lanes)
        else:
            s_ref[...] = s0_ref[...]

    def widen(x):
        return jnp.concatenate([x, x], axis=-1) if vsplit else x

    if vsplit:
        v_in = v_ref[...]
        v_s[...] = jnp.concatenate([v_in[:, :nv, :], v_in[:, nv:, :]], axis=-1)
    else:
        v_s[...] = v_ref[...]
    kvec = widen(k_ref[...])
    a_ = widen(a_ref[...])
    r_s[...] = widen(r_ref[...])
    w_s[...] = widen(w_ref[...])
    kk = kvec * kk_p_ref[...][None]
    nrm = jnp.sqrt(jnp.sum(kk * kk, axis=1, keepdims=True))
    kk = kk / jnp.maximum(nrm, 1e-12)
    nkk_s[...] = -kk
    b_s[...] = kk * a_
    km_s[...] = kvec * (1.0 + (a_ - 1.0) * ka_p_ref[...][None])

    def step(t, carry):
        nkk = nkk_s[t]
        b = b_s[t]
        km = km_s[t]
        w = w_s[t]
        r = r_s[t]
        vt = v_s[t]
        for vi in range(nv):
            sv = s_ref[vi]
            sa = jnp.sum(sv * nkk, axis=0, keepdims=True)
            sv = sv * w + sa * b + vt[vi:vi + 1, :] * km
            s_ref[vi] = sv
            o_s[t, pl.ds(vi, 1), :] = jnp.sum(sv * r, axis=0, keepdims=True)
        return carry

    lax.fori_loop(0, tc_len, step, 0)

    o = o_s[...]
    n_val = nv * (2 if vsplit else 1)

    def head_sum(x):
        if vsplit:
            x2 = x.reshape(tc_len * nv, lanes)
            x = (x2 + pltpu.roll(x2, lanes // 2, 1)).reshape(tc_len, nv, lanes)
        return jnp.sum(x, axis=1, keepdims=True)

    mu = head_sum(o) / n_val
    d = o - mu
    var = head_sum(d * d) / n_val
    o = d * lax.rsqrt(var + RW_GN_EPS) * lnw_ref[...][None] + lnb_ref[...][None]
    bonus = jnp.sum(r_s[...] * km_s[...] * rk_p_ref[...][None], axis=1, keepdims=True)
    o = o + bonus * v_s[...]
    if vsplit:
        o_ref[...] = jnp.concatenate([o[:, :, :lanes // 2], o[:, :, lanes // 2:]], axis=1)
    else:
        o_ref[...] = o

    @pl.when(ti == pl.num_programs(1) - 1)
    def _():
        if state_rows:
            sout_ref[...] = s_ref[...].reshape(nv * kc, lanes).T.reshape(sout_ref.shape)
        else:
            sout_ref[...] = s_ref[...]


def _rwkv_scan(r, w, k, a, v, kk_p, ka_p, rk_p, lnw, lnb, s0, *, t, tc, vsplit, layer=None, after=None):
    _, kc, kl = r.shape
    nv = v.shape[1] // 2 if vsplit else v.shape[1]
    ln = v.shape[2] * 2 if vsplit else v.shape[2]
    g = ln // LANES
    kspec = pl.BlockSpec((tc, kc, kl // g), lambda gi, ti: (ti, 0, gi))
    vspec = pl.BlockSpec((tc, nv, LANES), lambda gi, ti: (ti, 0, gi))
    pk = pl.BlockSpec((kc, LANES), lambda gi, ti: (0, 0))
    pv = pl.BlockSpec((nv, LANES), lambda gi, ti: (0, 0))
    state_rows = not vsplit
    operands = [r, w, k, a, v, kk_p, ka_p, rk_p, lnw, lnb, s0]
    in_specs = [kspec, kspec, kspec, kspec, kspec if vsplit else vspec, pk, pk, pk, pv, pv]
    aliases = {}
    if state_rows:
        nb, _, nh, sz = s0.shape
        assert sz == nv * kc and nb * nh == ln
        sspec = pl.BlockSpec((LANES // nh, None, nh, sz), lambda gi, ti: (gi, layer, 0, 0))
        in_specs.append(sspec)
        aliases = {10: 1}
        s_shape = s0.shape
    else:
        sspec = pl.BlockSpec((nv, kc, LANES), lambda gi, ti: (0, 0, gi))
        in_specs.append(sspec)
        s_shape = (nv, kc, ln)
    if after is not None:
        operands.append(after)
        in_specs.append(pl.BlockSpec(memory_space=pl.ANY))

    def body(*refs):
        n_in = len(operands)
        ins = refs[:11]
        _rwkv_scan_body(*ins, *refs[n_in:], vsplit=vsplit, state_rows=state_rows)

    return pl.pallas_call(
        body,
        grid=(g, t // tc),
        in_specs=in_specs,
        out_specs=[kspec if vsplit else vspec, sspec],
        out_shape=[jax.ShapeDtypeStruct((t, kc, kl) if vsplit else (t, nv, ln), F32),
                   jax.ShapeDtypeStruct(s_shape, F32)],
        scratch_shapes=[pltpu.VMEM((nv, kc, LANES), F32)] + [pltpu.VMEM((tc, kc, LANES), F32)] * 5
        + [pltpu.VMEM((tc, nv, LANES), F32)] * 2,
        input_output_aliases=aliases,
        compiler_params=_cparams("parallel", "arbitrary"),
        name="rwkv_scan",
    )(*operands)


def _ref_rows(gc, j):
    c, w = gc.shape
    b = 1 << j
    nt = c // SUBLANES
    gv = gc.reshape(nt, SUBLANES, w)
    if 2 * b <= SUBLANES:
        sub = lax.broadcasted_iota(jnp.int32, (1, SUBLANES, 1), 1)
        out = None
        for g0 in range(0, SUBLANES, 2 * b):
            piece = jnp.broadcast_to(gv[:, g0 + b - 1:g0 + b, :], gv.shape)
            out = piece if out is None else jnp.where(sub >= g0, piece, out)
        return out.reshape(c, w)
    tiles_per_group = 2 * b // SUBLANES
    pieces = []
    for ti in range(nt):
        src = (ti // tiles_per_group) * tiles_per_group + b // SUBLANES - 1
        pieces.append(jnp.broadcast_to(gv[src, SUBLANES - 1:SUBLANES, :], (SUBLANES, w)))
    return jnp.concatenate(pieces, axis=0)


def _hgrn_chunk(q, f, val, og, get_st, set_st, loglb, log1mlb, nrm_w, seq_len):
    c, width = q.shape
    nh = width // HG_DIM
    nseq = c // seq_len
    lg = seq_len.bit_length() - 1
    assert (1 << lg) == seq_len and c % SUBLANES == 0
    ls = jnp.minimum(f, 0.0) - jnp.log1p(jnp.exp(-jnp.abs(f)))
    b_ = log1mlb + ls
    g = jnp.maximum(loglb, b_) + jnp.log1p(jnp.exp(-jnp.abs(loglb - b_)))
    kk = _neg_expm1(g)
    qs = q * jax.nn.sigmoid(q)
    row = lax.broadcasted_iota(jnp.int32, (c, 1), 0)
    col = lax.broadcasted_iota(jnp.int32, (1, c), 1)
    tpos = row & (seq_len - 1)
    gc = g
    for j in range(lg):
        s = 1 << j
        gc = gc + jnp.where(tpos >= s, pltpu.roll(gc, s, 0), 0.0)
    qbs, kbs, sames = [qs.astype(BF16)], [kk.astype(BF16)], [row == col]
    for j in range(lg):
        e = jnp.exp(-jnp.abs(gc - _ref_rows(gc, j)))
        is_q = ((row >> j) & 1) == 1
        pk = (jnp.where(is_q, qs, kk) * e).astype(BF16)
        qbs.append(pk)
        kbs.append(pk)
        sames.append(((row >> (j + 1)) == (col >> (j + 1))) & is_q & (((col >> j) & 1) == 0))
    vb = val.astype(BF16)
    qg = (qs * jnp.exp(gc)).astype(BF16)
    k2s, eglast, in_seq = [], [], []
    for s_i in range(nseq):
        glast = gc[s_i * seq_len + seq_len - 1:s_i * seq_len + seq_len, :]
        eglast.append(jnp.exp(glast))
        if nseq == 1:
            in_seq.append(None)
            k2s.append((kk * jnp.exp(glast - gc)).astype(BF16))
        else:
            in_s = (row >> lg) == s_i
            in_seq.append(in_s)
            k2s.append(jnp.where(in_s, kk * jnp.exp(jnp.minimum(glast - gc, 0.0)), 0.0).astype(BF16))
    outs = []
    for h in range(nh):
        cs = slice(h * HG_DIM, (h + 1) * HG_DIM)
        amat = None
        for qb, kb, same in zip(qbs, kbs, sames):
            term = jnp.where(same, _dot_nt(qb[:, cs], kb[:, cs]), 0.0)
            amat = term if amat is None else amat + term
        o = _dot(amat.astype(BF16), vb[:, cs])
        for s_i in range(nseq):
            st = get_st(s_i, h)
            o_int = _dot_nt(qg[:, cs], st.astype(BF16))
            o = o + (o_int if nseq == 1 else jnp.where(in_seq[s_i], o_int, 0.0))
            set_st(s_i, h, st * eglast[s_i][:, cs] + _dot_tn(vb[:, cs], k2s[s_i][:, cs]))
        outs.append(o * lax.rsqrt(jnp.mean(o * o, axis=-1, keepdims=True) + HG_NORM_EPS))
    on = jnp.concatenate(outs, axis=1) * nrm_w
    return on * (og * jax.nn.sigmoid(og))


def _hgrn_prompt_body(q_ref, f_ref, i_ref, og_ref, s0_ref, loglb_ref, log1m_ref, nw_ref, o_ref, sout_ref, st_ref,
                      *, chunk):
    nh = st_ref.shape[0]
    tb = pl.program_id(1)

    @pl.when(tb == 0)
    def _():
        for h in range(nh):
            st_ref[h] = s0_ref[h].T

    def set_st(s_i, h, v):
        st_ref[h] = v

    def body(ci, carry):
        rows = pl.ds(pl.multiple_of(ci * chunk, chunk), chunk)
        y = _hgrn_chunk(q_ref[rows, :], f_ref[rows, :], i_ref[rows, :], og_ref[rows, :],
                        lambda s_i, h: st_ref[h], set_st, loglb_ref[...], log1m_ref[...], nw_ref[...], chunk)
        o_ref[rows, :] = y.astype(BF16)
        return carry

    lax.fori_loop(0, q_ref.shape[0] // chunk, body, 0)

    @pl.when(tb == pl.num_programs(1) - 1)
    def _():
        for h in range(nh):
            sout_ref[h] = st_ref[h].T


def _hgrn_prompt(p_hg, s0, loglb, log1m, nw, *, bp, tp, chunk):
    hgw = p_hg.shape[1] // 4
    nh = hgw // HG_DIM
    tb = _pick_block(tp, 512, chunk)
    nt = tp // tb
    cspec = lambda c: pl.BlockSpec((tb, hgw), lambda b, t: (b * nt + t, c))
    pspec = pl.BlockSpec((1, hgw), lambda b, t: (0, 0))
    sspec = pl.BlockSpec((None, nh, HG_DIM, HG_DIM), lambda b, t: (b, 0, 0, 0))
    return pl.pallas_call(
        functools.partial(_hgrn_prompt_body, chunk=chunk),
        grid=(bp, nt),
        in_specs=[cspec(0), cspec(1), cspec(2), cspec(3), sspec, pspec, pspec, pspec],
        out_specs=[pl.BlockSpec((tb, hgw), lambda b, t: (b * nt + t, 0)), sspec],
        out_shape=[jax.ShapeDtypeStruct((p_hg.shape[0], hgw), BF16), jax.ShapeDtypeStruct(s0.shape, F32)],
        scratch_shapes=[pltpu.VMEM((nh, HG_DIM, HG_DIM), F32)],
        compiler_params=_cparams("parallel", "arbitrary"),
        name="hgrn_prompt",
    )(p_hg, p_hg, p_hg, p_hg, s0, loglb, log1m, nw)


def _hgrn_sample_body(q_ref, f_ref, i_ref, og_ref, s0_ref, loglb_ref, log1m_ref, nw_ref, *rest, seq_len, nseq):
    o_ref, sout_ref = rest[-2:]
    c = seq_len * nseq

    def body(gi, carry):
        rows = pl.ds(pl.multiple_of(gi * c, c), c)

        def set_st(s_i, h, v):
            sout_ref[gi * nseq + s_i, h] = v.T

        y = _hgrn_chunk(q_ref[rows, :], f_ref[rows, :], i_ref[rows, :], og_ref[rows, :],
                        lambda s_i, h: s0_ref[gi * nseq + s_i, h].T, set_st,
                        loglb_ref[...], log1m_ref[...], nw_ref[...], seq_len)
        o_ref[rows, :] = y.astype(BF16)
        return carry

    lax.fori_loop(0, q_ref.shape[0] // c, body, 0)


def _hgrn_sample(p_hg_bm, s0, layer, loglb, log1m, nw, *, bs, ts, s_prev=None, after=None):
    hgw = p_hg_bm.shape[1] // 4
    nh = hgw // HG_DIM
    nseq = max(1, 16 // ts)
    assert bs % nseq == 0
    bb = _pick_block(bs, 8, nseq)
    rows = bb * ts
    cspec = lambda c: pl.BlockSpec((rows, hgw), lambda i: (i, c))
    pspec = pl.BlockSpec((1, hgw), lambda i: (0, 0))
    sspec = pl.BlockSpec((bb, None, nh, HG_DIM, HG_DIM), lambda i: (i, layer, 0, 0, 0))
    operands = [p_hg_bm, p_hg_bm, p_hg_bm, p_hg_bm, s0, loglb, log1m, nw]
    in_specs = [cspec(0), cspec(1), cspec(2), cspec(3), sspec, pspec, pspec, pspec]
    aliases = {}
    if s_prev is not None:
        operands.append(s_prev)
        in_specs.append(pl.BlockSpec(memory_space=pl.ANY))
        aliases = {len(operands) - 1: 1}
    if after is not None:
        operands.append(after)
        in_specs.append(pl.BlockSpec(memory_space=pl.ANY))
    return pl.pallas_call(
        functools.partial(_hgrn_sample_body, seq_len=ts, nseq=nseq),
        grid=(bs // bb,),
        in_specs=in_specs,
        out_specs=[pl.BlockSpec((rows, hgw), lambda i: (i, 0)), sspec],
        out_shape=[jax.ShapeDtypeStruct((bs * ts, hgw), BF16), jax.ShapeDtypeStruct(s0.shape, F32)],
        input_output_aliases=aliases,
        compiler_params=_cparams("parallel"),
        name="hgrn_sample",
    )(*operands)


def _lru_gates(xconv, wa_ref, wx_ref, ba, bx, lam):
    l = xconv.shape[1]
    blk = l // LRU_BLOCKS
    ga, gx = [], []
    for n in range(LRU_BLOCKS):
        xb = xconv[:, n * blk:(n + 1) * blk].astype(BF16)
        ga.append(_dot(xb, wa_ref[n]))
        gx.append(_dot(xb, wx_ref[n]))
    ga = jnp.concatenate(ga, axis=1) + ba
    gx = jnp.concatenate(gx, axis=1) + bx
    log_a = -LRU_C * jax.nn.sigmoid(ga) * _softplus(-lam)
    a = jnp.exp(log_a)
    mult = jnp.sqrt(1.0 - a * a)
    return a, mult * jax.nn.sigmoid(gx) * xconv


def _gelu_tanh(x):
    c = 0.7978845608028654
    return 0.5 * x * (1.0 + jnp.tanh(c * (x + 0.044715 * (x * x * x))))


def _odd_prompt_body(gate_ref, xb_ref, conv0_ref, h0_ref, cw_ref, cb_ref, wa_ref, wx_ref, ba_ref, bx_ref, lam_ref,
                     y_ref, convo_ref, ho_ref, ext_ref, hc_ref, a_s, b_s, h_s):
    rows = xb_ref.shape[0]
    hist = SUBLANES

    @pl.when(pl.program_id(1) == 0)
    def _():
        ext_ref[0:hist, :] = conv0_ref[...]
        hc_ref[...] = h0_ref[...]

    ext_ref[hist:hist + rows, :] = xb_ref[...]
    cw = cw_ref[...]
    xconv = cw[0:1, :] * ext_ref[pl.ds(hist - (CONV_W - 1), rows), :]
    for j in range(1, CONV_W):
        xconv = xconv + cw[j:j + 1, :] * ext_ref[pl.ds(hist - (CONV_W - 1) + j, rows), :]
    xconv = cb_ref[...] + xconv
    a, b = _lru_gates(xconv, wa_ref, wx_ref, ba_ref[...], bx_ref[...], lam_ref[...])
    width = a.shape[1]
    a = a.reshape(rows // SUBLANES, SUBLANES, width)
    b = b.reshape(rows // SUBLANES, SUBLANES, width)
    r8 = lax.broadcasted_iota(jnp.int32, (1, SUBLANES, 1), 1)
    for s in (1, 2, 4):
        m = r8 >= s
        a_sh = pltpu.roll(a, s, 1)
        b_sh = pltpu.roll(b, s, 1)
        b = jnp.where(m, a * b_sh + b, b)
        a = jnp.where(m, a * a_sh, a)
    a_s[...] = a.reshape(rows, width)
    b_s[...] = b.reshape(rows, width)

    def tile(j, hc):
        r0 = pl.multiple_of(j * SUBLANES, SUBLANES)
        h = a_s[pl.ds(r0, SUBLANES), :] * hc + b_s[pl.ds(r0, SUBLANES), :]
        h_s[pl.ds(r0, SUBLANES), :] = h
        return h[SUBLANES - 1:SUBLANES, :]

    hc = lax.fori_loop(0, rows // SUBLANES, tile, hc_ref[...])
    hc_ref[...] = hc
    ho_ref[...] = hc
    y_ref[...] = (h_s[...] * _gelu_tanh(gate_ref[...])).astype(BF16)
    tail = ext_ref[rows:rows + hist, :]
    ext_ref[0:hist, :] = tail
    convo_ref[...] = tail


def _odd_prompt(p_od, conv0, h0, layer, cw, cb, wa, wx, ba, bx, lam, *, bp, tp):
    l = p_od.shape[1] // 2
    rows = _pick_block(tp, 256, SUBLANES)
    nt = tp // rows
    lsel = lambda a: pl.BlockSpec((None,) + a.shape[1:], lambda b, t: (layer,) + (0,) * (a.ndim - 1))
    return pl.pallas_call(
        _odd_prompt_body,
        grid=(bp, nt),
        in_specs=[
            pl.BlockSpec((rows, l), lambda b, t: (b * nt + t, 0)),
            pl.BlockSpec((rows, l), lambda b, t: (b * nt + t, 1)),
            pl.BlockSpec((None, SUBLANES, l), lambda b, t: (b, 0, 0)),
            pl.BlockSpec((None, 1, l), lambda b, t: (b, 0, 0)),
            lsel(cw), lsel(cb), lsel(wa), lsel(wx), lsel(ba), lsel(bx), lsel(lam),
        ],
        out_specs=[
            pl.BlockSpec((rows, l), lambda b, t: (b * nt + t, 0)),
            pl.BlockSpec((None, SUBLANES, l), lambda b, t: (b, 0, 0)),
            pl.BlockSpec((None, 1, l), lambda b, t: (b, 0, 0)),
        ],
        out_shape=[jax.ShapeDtypeStruct((p_od.shape[0], l), BF16), jax.ShapeDtypeStruct((bp, SUBLANES, l), F32),
                   jax.ShapeDtypeStruct((bp, 1, l), F32)],
        scratch_shapes=[pltpu.VMEM((rows + SUBLANES, l), F32), pltpu.VMEM((1, l), F32),
                        pltpu.VMEM((rows, l), F32), pltpu.VMEM((rows, l), F32), pltpu.VMEM((rows, l), F32)],
        compiler_params=_cparams("parallel", "arbitrary"),
        name="odd_prompt",
    )(p_od, p_od, conv0, h0, cw, cb, wa, wx, ba, bx, lam)


def _odd_sample_body(gate_ref, xb_ref, conv0_ref, h0_ref, cw_ref, cb_ref, wa_ref, wx_ref, ba_ref, bx_ref, lam_ref,
                     yall_ref, y_ref, convo_ref, ho_ref):
    @pl.when(pl.program_id(0) == 0)
    def _():
        convo_ref[...] = conv0_ref[...]
        ho_ref[...] = h0_ref[...]

    xb = xb_ref[...]
    cw = cw_ref[...]
    xconv = cw[CONV_W - 1:CONV_W, :] * xb
    for j in range(CONV_W - 1):
        xconv = xconv + cw[j:j + 1, :] * convo_ref[j]
    xconv = cb_ref[...] + xconv
    a, b = _lru_gates(xconv, wa_ref, wx_ref, ba_ref[...], bx_ref[...], lam_ref[...])
    h = a * ho_ref[...] + b
    ho_ref[...] = h
    y_ref[...] = (h * _gelu_tanh(gate_ref[...])).astype(BF16)
    for j in range(CONV_W - 2):
        convo_ref[j] = convo_ref[j + 1]
    convo_ref[CONV_W - 2] = xb


def _odd_sample(p_od, conv0_t, h0, layer, cw, cb, wa, wx, ba, bx, lam, y_all, *, mp, bs, ts):
    l = p_od.shape[1] // 2
    assert mp % bs == 0
    off = mp // bs
    lsel = lambda a: pl.BlockSpec((None,) + a.shape[1:], lambda t: (layer,) + (0,) * (a.ndim - 1))
    full = lambda a: pl.BlockSpec(a.shape, lambda t: (0,) * a.ndim)
    return pl.pallas_call(
        _odd_sample_body,
        grid=(ts,),
        in_specs=[
            pl.BlockSpec((bs, l), lambda t: (off + t, 0)),
            pl.BlockSpec((bs, l), lambda t: (off + t, 1)),
            full(conv0_t), full(h0),
            lsel(cw), lsel(cb), lsel(wa), lsel(wx), lsel(ba), lsel(bx), lsel(lam),
            pl.BlockSpec(memory_space=pl.ANY),
        ],
        out_specs=[pl.BlockSpec((bs, l), lambda t: (off + t, 0)), full(conv0_t), full(h0)],
        out_shape=[jax.ShapeDtypeStruct(y_all.shape, BF16), jax.ShapeDtypeStruct(conv0_t.shape, F32),
                   jax.ShapeDtypeStruct(h0.shape, F32)],
        input_output_aliases={11: 0},
        compiler_params=_cparams("arbitrary"),
        name="odd_sample",
    )(p_od, p_od, conv0_t, h0, cw, cb, wa, wx, ba, bx, lam, y_all)


def _pad_last(a, n):
    return jnp.pad(a, [(0, 0)] * (a.ndim - 1) + [(0, n - a.shape[-1])])


def _round_up(n, m):
    return (n + m - 1) // m * m


class _RwLayout:
    def __init__(self, rw, dw, da, dg):
        self.rw, self.dw, self.da, self.dg = rw, dw, da, dg
        self.dwp, self.dap, self.dgp = (_round_up(d, LANES) for d in (dw, da, dg))
        self.width = 3 * rw + dw + da + dg
        self.padded = 3 * rw + self.dwp + self.dap + self.dgp

    def pad(self, a):
        o = 3 * self.rw
        parts = [a[..., :o], _pad_last(a[..., o:o + self.dw], self.dwp),
                 _pad_last(a[..., o + self.dw:o + self.dw + self.da], self.dap),
                 _pad_last(a[..., o + self.dw + self.da:self.width], self.dgp)]
        return jnp.concatenate(parts, axis=-1)

    def unpad(self, a):
        o = 3 * self.rw
        parts = [a[..., :o], a[..., o:o + self.dw], a[..., o + self.dwp:o + self.dwp + self.da],
                 a[..., o + self.dwp + self.dap:o + self.dwp + self.dap + self.dg]]
        return jnp.concatenate(parts, axis=-1)


def kernel(x_prompt, x_sample, state_rwkv_shift, state_rwkv, state_hgrn, state_conv, state_lru, ffn1_norm, ffn1_w_gu, ffn1_w_down, mix_norm, ffn2_norm, ffn2_w_gu, ffn2_w_down, ev_w_in, rw_mu, rw_w0, rw_w2, rw_a0, rw_a2, rw_g2, rw_k_k, rw_k_a, rw_r_k, rw_ln_w, rw_ln_b, hg_lb, hg_norm, ev_w_out, od_w_in, conv_w, conv_b, lru_wa, lru_ba, lru_wx, lru_bx, lru_lambda, od_w_out, final_norm):
    bp, tp, d = x_prompt.shape
    bs, ts, _ = x_sample.shape
    depth = ffn1_norm.shape[0]
    n_even = ev_w_in.shape[0]
    mp, ms = bp * tp, bs * ts
    rw = rw_w0.shape[1]
    nh_rw = rw // RW_HEAD
    hgw = hg_norm.shape[1]
    nh_hg = hgw // HG_DIM
    lay = _RwLayout(rw, rw_w2.shape[1], rw_a2.shape[1], rw_g2.shape[1])
    assert bp * nh_rw * 2 == LANES and (bs * nh_rw) % LANES == 0
    dt = x_prompt.dtype

    ffn_src = {1: (ffn1_norm, ffn1_w_gu, ffn1_w_down), 2: (ffn2_norm, ffn2_w_gu, ffn2_w_down)}
    ffn_w = (ffn1_w_gu[0].astype(BF16), ffn1_w_down[0].astype(BF16))

    def ffn_step(x, l, which, ffn_w):
        norm_g = ffn_src[which][0]
        nxt = (l, 2) if which == 1 else (l + 1, 1)
        if nxt[0] >= depth:
            return _ffn(x, norm_g, l, *ffn_w)[0], None
        x, c_gu, c_dn = _ffn(x, norm_g, l, *ffn_w, next_src=(ffn_src[nxt[1]][1], ffn_src[nxt[1]][2], nxt[0]))
        return x, (c_gu, c_dn)

    ev_out = ev_w_out.astype(BF16)
    od_in, od_out = od_w_in.astype(BF16), od_w_out.astype(BF16)

    wa_bf, wx_bf = lru_wa.astype(BF16), lru_wx.astype(BF16)

    x = jnp.concatenate([x_prompt.reshape(mp, d), x_sample.transpose(1, 0, 2).reshape(ms, d)], axis=0)

    lbs = jnp.cumsum(jax.nn.softmax(hg_lb.astype(F32), axis=0), axis=0)
    lb_all = lbs - lbs[0]

    def k_lanes_p(a):
        return a.reshape(bp * nh_rw, RW_HEAD, a.shape[1]).transpose(2, 1, 0)

    def k_lanes_p_inv(a):
        return a.transpose(2, 1, 0).reshape(bp, rw, tp)

    def k_lanes_s(a):
        a = a.reshape(bp, nh_rw, RW_HEAD, ts // bp, bs).transpose(3, 0, 2, 4, 1)
        return a.reshape(ts, RW_HEAD, bs * nh_rw)

    def k_lanes_s_inv(a):
        return a.reshape(ts, RW_HEAD, bs, nh_rw).transpose(0, 2, 3, 1).reshape(ms, rw)

    def kparam(p):
        return jnp.tile(p.reshape(nh_rw, RW_HEAD).T, (1, LANES // nh_rw))

    def vparam_p(p):
        a = p.reshape(nh_rw, 2, RW_HEAD // 2).transpose(2, 1, 0)[:, :, None, :]
        return jnp.broadcast_to(a, (RW_HEAD // 2, 2, bp, nh_rw)).reshape(RW_HEAD // 2, LANES)

    shifts_p, shifts_s, rws_p, hgs_p, convs_p, convs_s, lrus_p, lrus_s = ([] for _ in range(8))
    rw_rows = state_rwkv.astype(F32).reshape(bs, n_even, nh_rw, RW_HEAD * RW_HEAD)
    hg_new = None

    for l in range(depth):
        x, ffn_w = ffn_step(x, l, 1, ffn_w)
        if l % 2 == 0:
            e = l // 2
            w_in = ev_w_in[e]
            w_rw = lay.pad(w_in[:, :lay.width]).astype(BF16)
            w_hg = w_in[:, lay.width:].astype(BF16)
            p_rw = _norm_proj(x, mix_norm, l, w_rw, lay.padded)
            p_hg = _norm_proj(x, mix_norm, l, w_hg, 4 * hgw)
            shift_p = jnp.zeros((bp, lay.padded), F32)
            shift_s = lay.pad(state_rwkv_shift[:, e].astype(F32))
            rkvwa, g_ = _rwkv_prep(
                p_rw, shift_p, shift_s, lay.pad(rw_mu[e])[None], rw_w0[e][None],
                _pad_last(rw_w2[e].T, lay.dwp).T.astype(BF16), rw_a0[e][None],
                _pad_last(rw_a2[e].T, lay.dap).T.astype(BF16),
                _pad_last(rw_g2[e].T, lay.dgp).T.astype(BF16), rw=rw, mp=mp, tp=tp, bs=bs)
            shifts_p.append(lay.unpad(jnp.concatenate([p_rw[b * tp + tp - 1:(b + 1) * tp] for b in range(bp)])))
            shifts_s.append(lay.unpad(p_rw[mp + (ts - 1) * bs:]))
            kk_p, ka_p = kparam(rw_k_k[e]), kparam(rw_k_a[e])
            rk_p = kparam(rw_r_k[e].reshape(rw))
            lb = lb_all[e][None]
            loglb, log1m = jnp.log(lb), jnp.log1p(-lb)
            nw = hg_norm[e][None]
            chunk = _pick_block(tp, HG_CHUNK, 1)
            oh_p, hs_p = _hgrn_prompt(p_hg, jnp.zeros((bp, nh_hg, HG_DIM, HG_DIM), F32), loglb, log1m, nw,
                                      bp=bp, tp=tp, chunk=chunk)
            r_p, k_p, v_p, w_p, a_p = rkvwa
            o_p, s_p = _rwkv_scan(
                k_lanes_p(r_p), k_lanes_p(w_p), k_lanes_p(k_p), k_lanes_p(a_p), k_lanes_p(v_p),
                kk_p, ka_p, rk_p, vparam_p(rw_ln_w[e]), vparam_p(rw_ln_b[e]),
                jnp.zeros((RW_HEAD // 2, RW_HEAD, LANES), F32), t=tp, tc=_pick_block(tp, 64, 1), vsplit=True,
                after=oh_p)
            rws_p.append(s_p.reshape(RW_HEAD // 2, RW_HEAD, 2, bp, nh_rw).transpose(3, 4, 2, 0, 1)
                         .reshape(bp, nh_rw, RW_HEAD, RW_HEAD))
            r_s, k_s, v_s, w_s, a_s = (a[:, tp:] for a in rkvwa)
            o_s, rw_rows = _rwkv_scan(
                k_lanes_s(r_s), k_lanes_s(w_s), k_lanes_s(k_s), k_lanes_s(a_s), k_lanes_s(v_s),
                kk_p, ka_p, rk_p, kparam(rw_ln_w[e]), kparam(rw_ln_b[e]),
                rw_rows, t=ts, tc=ts, vsplit=False, layer=e, after=o_p)
            o_pt, o_st = k_lanes_p_inv(o_p), k_lanes_s_inv(o_s)
            p_hg_bm = p_hg[mp:].reshape(ts, bs, 4 * hgw).transpose(1, 0, 2).reshape(ms, 4 * hgw)
            oh_s, hg_new = _hgrn_sample(p_hg_bm, state_hgrn.astype(F32), e, loglb, log1m, nw, bs=bs, ts=ts,
                                        s_prev=hg_new, after=o_p)
            oh_s = oh_s.reshape(bs, ts, hgw).transpose(1, 0, 2).reshape(ms, hgw)
            hgs_p.append(hs_p)
            o_hg = oh_p.at[mp:].set(oh_s)
            x = _proj_res_even(x, o_pt, o_st, g_, o_hg, ev_out, e)
        else:
            o_i = l // 2
            p_od = _norm_proj(x, mix_norm, l, od_in, od_in.shape[-1], w_layer=o_i)
            lsz = p_od.shape[1] // 2
            vec = lambda a: a[:, None, :]
            args = (conv_w, vec(conv_b), wa_bf, wx_bf, vec(lru_ba), vec(lru_bx), vec(lru_lambda))
            conv0_p = jnp.zeros((bp, SUBLANES, lsz), F32)
            y_p, c_p, h_p = _odd_prompt(p_od, conv0_p, jnp.zeros((bp, 1, lsz), F32), o_i, *args, bp=bp, tp=tp)
            conv0_s = state_conv[:, o_i].astype(F32).transpose(1, 0, 2)
            y_all, c_s, h_s = _odd_sample(p_od, conv0_s, state_lru[:, o_i].astype(F32), o_i, *args, y_p,
                                          mp=mp, bs=bs, ts=ts)
            convs_p.append(c_p[:, SUBLANES - (CONV_W - 1):])
            convs_s.append(c_s.transpose(1, 0, 2))
            lrus_p.append(h_p[:, 0])
            lrus_s.append(h_s)
            x = _proj_res(x, y_all, od_out, o_i)
        x, ffn_w = ffn_step(x, l, 2, ffn_w)

    y_p, y_s = _final_norm(x, final_norm, mp)
    y_prompt = y_p.reshape(bp, tp, d)
    y_sample = y_s.reshape(ts, bs, d).transpose(1, 0, 2)
    st = lambda xs: jnp.stack(xs, axis=1).astype(dt)
    sample_rwkv = rw_rows.reshape(state_rwkv.shape).astype(dt)
    return (y_prompt, y_sample, st(shifts_p), st(rws_p), st(hgs_p), st(convs_p), st(lrus_p),
            st(shifts_s), sample_rwkv, hg_new.astype(dt), st(convs_s), st(lrus_s))
```

```python
import functools
import math

import jax
import jax.numpy as jnp
from jax import lax
from jax.experimental import pallas as pl
from jax.experimental.pallas import tpu as pltpu

F32 = jnp.float32
BF16 = jnp.bfloat16

NORM_EPS = 1e-6
RW_GN_EPS = 64e-5
HG_NORM_EPS = 1e-5
LRU_C = 8.0
RW_HEAD = 64
HG_DIM = 128
LRU_BLOCKS = 8
CONV_W = 4
LANES = 128
SUBLANES = 8
VMEM_LIMIT = 62 * 1024 * 1024
MAX_BM = 1088
HG_CHUNK = 64


def _cparams(*sem):
    return pltpu.CompilerParams(dimension_semantics=sem, vmem_limit_bytes=VMEM_LIMIT)


def _pick_block(n, cap, mult):
    best = None
    for d in range(mult, min(n, cap) + 1, mult):
        if n % d == 0:
            best = d
    assert best is not None, (n, cap, mult)
    return best


def _gcd(a, b):
    return math.gcd(a, b)


def _dot(a, b):
    return jnp.dot(a, b, preferred_element_type=F32)


def _dot_nt(a, b):
    return lax.dot_general(a, b, (((1,), (1,)), ((), ())), preferred_element_type=F32)


def _dot_tn(a, b):
    return lax.dot_general(a, b, (((0,), (0,)), ((), ())), preferred_element_type=F32)


def _softplus(x):
    return jnp.maximum(x, 0.0) + jnp.log1p(jnp.exp(-jnp.abs(x)))


def _neg_expm1(y):
    return 1.0 - jnp.exp(y)


def _rms_rows(x, g):
    ms = jnp.mean(x * x, axis=-1, keepdims=True)
    return x * lax.rsqrt(ms + NORM_EPS) * g


def _ffn_body(x_ref, g_ref, wg_ref, wu_ref, wd_ref, *rest, cast_next):
    if cast_next:
        sgu_ref, sdn_ref, o_ref, cgu_ref, cdn_ref, xn_ref = rest
        cgu_ref[...] = sgu_ref[...].astype(BF16)
        cdn_ref[...] = sdn_ref[...].astype(BF16)
    else:
        o_ref, xn_ref = rest

    @pl.when(pl.program_id(1) == 0)
    def _():
        x = x_ref[...]
        xn_ref[...] = _rms_rows(x, g_ref[...]).astype(BF16)
        o_ref[...] = x

    xn = xn_ref[...]
    gate = _dot(xn, wg_ref[...])
    up = _dot(xn, wu_ref[...])
    act = (0.5 * (gate * jax.nn.sigmoid(gate)) * up).astype(BF16)
    o_ref[...] += _dot(act, wd_ref[...])


def _ffn(x, norm_g, layer, w_gu, w_down, next_src=None):
    m, d = x.shape
    f = w_down.shape[0]
    bm = _pick_block(m, MAX_BM, 16)
    bf = _pick_block(f, 512, LANES)
    ni, nf = m // bm, f // bf
    in_specs = [
        pl.BlockSpec((bm, d), lambda i, j: (i, 0)),
        pl.BlockSpec((None, 1, d), lambda i, j: (layer, 0, 0)),
        pl.BlockSpec((d, bf), lambda i, j: (0, j)),
        pl.BlockSpec((d, bf), lambda i, j: (0, j + nf)),
        pl.BlockSpec((bf, d), lambda i, j: (j, 0)),
    ]
    operands = [x, norm_g[:, None, :], w_gu, w_gu, w_down]
    out_specs = [pl.BlockSpec((bm, d), lambda i, j: (i, 0))]
    out_shape = [jax.ShapeDtypeStruct((m, d), F32)]
    if next_src is not None:
        s_gu, s_dn, nl = next_src
        assert d % ni == 0 and (2 * f) % nf == 0 and (d // ni) % 16 == 0 and (d // ni) % LANES == 0
        gu_blk, dn_blk = (d // ni, 2 * f // nf), (f // nf, d // ni)
        in_specs += [pl.BlockSpec((None,) + gu_blk, lambda i, j: (nl, i, j)),
                     pl.BlockSpec((None,) + dn_blk, lambda i, j: (nl, j, i))]
        operands += [s_gu, s_dn]
        out_specs += [pl.BlockSpec(gu_blk, lambda i, j: (i, j)), pl.BlockSpec(dn_blk, lambda i, j: (j, i))]
        out_shape += [jax.ShapeDtypeStruct((d, 2 * f), BF16), jax.ShapeDtypeStruct((f, d), BF16)]
    return pl.pallas_call(
        functools.partial(_ffn_body, cast_next=next_src is not None),
        grid=(ni, nf),
        in_specs=in_specs,
        out_specs=out_specs,
        out_shape=out_shape,
        scratch_shapes=[pltpu.VMEM((bm, d), BF16)],
        compiler_params=_cparams("parallel", "arbitrary"),
        name="ffn",
    )(*operands)


def _norm_proj_body(x_ref, g_ref, w_ref, o_ref, xn_ref):
    @pl.when(pl.program_id(1) == 0)
    def _():
        xn_ref[...] = _rms_rows(x_ref[...], g_ref[...]).astype(BF16)

    o_ref[...] = _dot(xn_ref[...], w_ref[...])


def _w_spec(w, blk, idx, w_layer):
    if w.ndim == 2:
        return pl.BlockSpec(blk, idx)
    return pl.BlockSpec((None,) + blk, lambda i, j: (w_layer,) + idx(i, j))


def _norm_proj(x, norm_g, layer, w, bn, w_layer=None):
    m, d = x.shape
    n = w.shape[-1]
    bm = _pick_block(m, 512, 16)
    return pl.pallas_call(
        _norm_proj_body,
        grid=(m // bm, n // bn),
        in_specs=[
            pl.BlockSpec((bm, d), lambda i, j: (i, 0)),
            pl.BlockSpec((None, 1, d), lambda i, j: (layer, 0, 0)),
            _w_spec(w, (d, bn), lambda i, j: (0, j), w_layer),
        ],
        out_specs=pl.BlockSpec((bm, bn), lambda i, j: (i, j)),
        out_shape=jax.ShapeDtypeStruct((m, n), F32),
        scratch_shapes=[pltpu.VMEM((bm, d), BF16)],
        compiler_params=_cparams("parallel", "arbitrary"),
        name="norm_proj",
    )(x, norm_g[:, None, :], w)


def _proj_res_body(x_ref, yp_ref, ys_ref, w_ref, o_ref, *, n_prompt_blk):
    y = jnp.where(pl.program_id(0) < n_prompt_blk, yp_ref[...], ys_ref[...])
    o_ref[...] = x_ref[...] + _dot(y, w_ref[...])


def _proj_res(x, y_p, y_s, w, w_layer):
    m, d = x.shape
    mp, k = y_p.shape
    ms = y_s.shape[0]
    assert mp + ms == m
    bm = _pick_block(_gcd(mp, ms), 512, 16)
    bn = _pick_block(d, 2048, LANES)
    npb = mp // bm
    return pl.pallas_call(
        functools.partial(_proj_res_body, n_prompt_blk=npb),
        grid=(m // bm, d // bn),
        in_specs=[
            pl.BlockSpec((bm, bn), lambda i, j: (i, j)),
            pl.BlockSpec((bm, k), lambda i, j: (jnp.minimum(i, npb - 1), 0)),
            pl.BlockSpec((bm, k), lambda i, j: (jnp.maximum(i - npb, 0), 0)),
            _w_spec(w, (k, bn), lambda i, j: (0, j), w_layer),
        ],
        out_specs=pl.BlockSpec((bm, bn), lambda i, j: (i, j)),
        out_shape=jax.ShapeDtypeStruct((m, d), F32),
        compiler_params=_cparams("parallel", "arbitrary"),
        name="proj_res",
    )(x, y_p, y_s, w)


def _proj_res_even_body(x_ref, opt_ref, os_ref, g_ref, hp_ref, hs_ref, w1_ref, w2_ref, o_ref, *, n_prompt_blk):
    is_prompt = pl.program_id(0) < n_prompt_blk
    o_rw = jnp.where(is_prompt, opt_ref[...].T, os_ref[...])
    o_hg = jnp.where(is_prompt, hp_ref[...], hs_ref[...])
    y1 = (o_rw * g_ref[...]).astype(BF16)
    o_ref[...] = x_ref[...] + _dot(y1, w1_ref[...]) + _dot(o_hg, w2_ref[...])


def _proj_res_even(x, o_pt, o_s, g, oh_p, oh_s, w, w_layer):
    m, d = x.shape
    bp, rw, tp = o_pt.shape
    ms = o_s.shape[0]
    hg = oh_p.shape[1]
    assert rw == hg and bp * tp + ms == m
    bm = _pick_block(_gcd(tp, ms), 512, LANES)
    bn = _pick_block(d, 2048, LANES)
    npb, tpb = bp * tp // bm, tp // bm

    def pt_idx(i, j):
        ic = jnp.minimum(i, npb - 1)
        return ic // tpb, 0, ic % tpb

    return pl.pallas_call(
        functools.partial(_proj_res_even_body, n_prompt_blk=npb),
        grid=(m // bm, d // bn),
        in_specs=[
            pl.BlockSpec((bm, bn), lambda i, j: (i, j)),
            pl.BlockSpec((None, rw, bm), pt_idx),
            pl.BlockSpec((bm, rw), lambda i, j: (jnp.maximum(i - npb, 0), 0)),
            pl.BlockSpec((bm, rw), lambda i, j: (i, 0)),
            pl.BlockSpec((bm, hg), lambda i, j: (jnp.minimum(i, npb - 1), 0)),
            pl.BlockSpec((bm, hg), lambda i, j: (jnp.maximum(i - npb, 0), 0)),
            _w_spec(w, (rw, bn), lambda i, j: (0, j), w_layer),
            _w_spec(w, (hg, bn), lambda i, j: (1, j), w_layer),
        ],
        out_specs=pl.BlockSpec((bm, bn), lambda i, j: (i, j)),
        out_shape=jax.ShapeDtypeStruct((m, d), F32),
        compiler_params=_cparams("parallel", "arbitrary"),
        name="proj_res_even",
    )(x, o_pt, o_s, g, oh_p, oh_s, w, w)


def _final_norm_body(x_ref, g_ref, op_ref, os_ref, *, n_prompt_blk):
    y = _rms_rows(x_ref[...], g_ref[...])
    i = pl.program_id(0)

    @pl.when(i < n_prompt_blk)
    def _():
        op_ref[...] = y

    @pl.when(i >= n_prompt_blk)
    def _():
        os_ref[...] = y


def _final_norm(x, g, mp):
    m, d = x.shape
    ms = m - mp
    bm = _pick_block(_gcd(mp, ms), MAX_BM, SUBLANES)
    npb = mp // bm
    return pl.pallas_call(
        functools.partial(_final_norm_body, n_prompt_blk=npb),
        grid=(m // bm,),
        in_specs=[pl.BlockSpec((bm, d), lambda i: (i, 0)), pl.BlockSpec((1, d), lambda i: (0, 0))],
        out_specs=[pl.BlockSpec((bm, d), lambda i: (jnp.minimum(i, npb - 1), 0)),
                   pl.BlockSpec((bm, d), lambda i: (jnp.maximum(i - npb, 0), 0))],
        out_shape=[jax.ShapeDtypeStruct((mp, d), F32), jax.ShapeDtypeStruct((ms, d), F32)],
        compiler_params=_cparams("arbitrary"),
        name="final_norm",
    )(x, g[None, :])


def _rwkv_prep_body(cur_ref, tail_ref, prevs_ref, shiftp_ref, shifts_ref, mu_ref, w0_ref, w2_ref, a0_ref, a2_ref,
                    g2_ref, r_ref, k_ref, v_ref, w_ref, a_ref, g_ref, *, rw, dwp, dap, n_prompt_blk, blk_per_seq):
    i = pl.program_id(0)
    rows = cur_ref.shape[0]
    prw = cur_ref.shape[1]
    seq = jnp.minimum(i // blk_per_seq, shiftp_ref.shape[0] - 1)
    first = (i % blk_per_seq) == 0
    is_sample = i >= n_prompt_blk
    rid = lax.broadcasted_iota(jnp.int32, (rows, 1), 0)
    shift_row = shiftp_ref[pl.ds(seq, 1), :]

    def shifted(c0, c1):
        cs = slice(c0, c1)
        cur = cur_ref[:, cs]
        row0 = jnp.where(first, shift_row[:, cs], tail_ref[SUBLANES - 1:SUBLANES, cs])
        prev_p = jnp.where(rid == 0, row0, pltpu.roll(cur, 1, 0))
        prev_s = jnp.where(i == n_prompt_blk, shifts_ref[:, cs], prevs_ref[:, cs])
        prev = jnp.where(is_sample, prev_s, prev_p)
        return cur + (prev - cur) * mu_ref[:, cs]

    r_ref[...] = shifted(0, rw).T
    k_ref[...] = shifted(rw, 2 * rw).T
    v_ref[...] = shifted(2 * rw, 3 * rw).T
    o = 3 * rw
    lw = w0_ref[...] + _dot(jnp.tanh(shifted(o, o + dwp)).astype(BF16), w2_ref[...])
    w_log = -_softplus(-lw) - 0.5
    w_ref[...] = jnp.exp(-jnp.exp(w_log)).T
    a_ref[...] = jax.nn.sigmoid(a0_ref[...] + _dot(shifted(o + dwp, o + dwp + dap).astype(BF16), a2_ref[...])).T
    g_ref[...] = _dot(jax.nn.sigmoid(shifted(o + dwp + dap, prw)).astype(BF16), g2_ref[...])


def _rwkv_prep(p_rw, shift_p, shift_s, mu, w0, w2, a0, a2, g2, *, rw, mp, tp, bs):
    m, prw = p_rw.shape
    ms = m - mp
    assert mp % bs == 0 and tp % bs == 0 and ms % bs == 0
    npb = mp // bs
    dwp, dap = w2.shape[0], a2.shape[0]
    full = lambda a: pl.BlockSpec(a.shape, lambda i: (0,) * a.ndim)
    bps = tp // bs
    bp = mp // tp

    ts = ms // bs
    assert ts % bp == 0

    def out_idx(i):
        j = i - npb
        return jnp.where(i < npb, i // bps, j % bp), jnp.where(i < npb, i % bps, bps + j // bp)

    spec_o = pl.BlockSpec((rw, bs), out_idx)
    body = functools.partial(_rwkv_prep_body, rw=rw, dwp=dwp, dap=dap, n_prompt_blk=npb, blk_per_seq=tp // bs)
    outs = pl.pallas_call(
        body,
        grid=(m // bs,),
        in_specs=[
            pl.BlockSpec((bs, prw), lambda i: (i, 0)),
            pl.BlockSpec((SUBLANES, prw), lambda i: (jnp.maximum(i * (bs // SUBLANES) - 1, 0), 0)),
            pl.BlockSpec((bs, prw), lambda i: (jnp.maximum(i - 1, npb), 0)),
            full(shift_p), full(shift_s), full(mu), full(w0), full(w2), full(a0), full(a2), full(g2),
        ],
        out_specs=[spec_o] * 5 + [pl.BlockSpec((bs, rw), lambda i: (i, 0))],
        out_shape=[jax.ShapeDtypeStruct((bp * rw, tp + ms // bp), F32)] * 5 + [jax.ShapeDtypeStruct((m, rw), F32)],
        compiler_params=_cparams("arbitrary"),
        name="rwkv_prep",
    )(p_rw, p_rw, p_rw, shift_p, shift_s, mu, w0, w2, a0, a2, g2)
    return outs[:5], outs[5]


def _rwkv_scan_body(r_ref, w_ref, k_ref, a_ref, v_ref, kk_p_ref, ka_p_ref, rk_p_ref, lnw_ref, lnb_ref, s0_ref,
                    o_ref, sout_ref, s_ref, r_s, w_s, nkk_s, b_s, km_s, v_s, o_s, *, vsplit, state_rows):
    tc_len, nv, lanes = v_s.shape
    kc = s_ref.shape[1]
    ti = pl.program_id(1)

    @pl.when(ti == 0)
    def _():
        if state_rows:
            s_ref[...] = s0_ref[...].reshape(lanes, nv * kc).T.reshape(nv, kc, lanes)
        else:
            s_ref[...] = s0_ref[...]

    def widen(x):
        return jnp.concatenate([x, x], axis=-1) if vsplit else x

    if vsplit:
        v_in = v_ref[...]
        v_s[...] = jnp.concatenate([v_in[:, :nv, :], v_in[:, nv:, :]], axis=-1)
    else:
        v_s[...] = v_ref[...]
    kvec = widen(k_ref[...])
    a_ = widen(a_ref[...])
    r_s[...] = widen(r_ref[...])
    w_s[...] = widen(w_ref[...])
    kk = kvec * kk_p_ref[...][None]
    nrm = jnp.sqrt(jnp.sum(kk * kk, axis=1, keepdims=True))
    kk = kk / jnp.maximum(nrm, 1e-12)
    nkk_s[...] = -kk
    b_s[...] = kk * a_
    km_s[...] = kvec * (1.0 + (a_ - 1.0) * ka_p_ref[...][None])

    def step(t, carry):
        nkk = nkk_s[t]
        b = b_s[t]
        km = km_s[t]
        w = w_s[t]
        r = r_s[t]
        vt = v_s[t]
        for vi in range(nv):
            sv = s_ref[vi]
            sa = jnp.sum(sv * nkk, axis=0, keepdims=True)
            sv = sv * w + sa * b + vt[vi:vi + 1, :] * km
            s_ref[vi] = sv
            o_s[t, pl.ds(vi, 1), :] = jnp.sum(sv * r, axis=0, keepdims=True)
        return carry

    lax.fori_loop(0, tc_len, step, 0)

    o = o_s[...]
    n_val = nv * (2 if vsplit else 1)

    def head_sum(x):
        if vsplit:
            x2 = x.reshape(tc_len * nv, lanes)
            x = (x2 + pltpu.roll(x2, lanes // 2, 1)).reshape(tc_len, nv, lanes)
        return jnp.sum(x, axis=1, keepdims=True)

    mu = head_sum(o) / n_val
    d = o - mu
    var = head_sum(d * d) / n_val
    o = d * lax.rsqrt(var + RW_GN_EPS) * lnw_ref[...][None] + lnb_ref[...][None]
    bonus = jnp.sum(r_s[...] * km_s[...] * rk_p_ref[...][None], axis=1, keepdims=True)
    o = o + bonus * v_s[...]
    if vsplit:
        o_ref[...] = jnp.concatenate([o[:, :, :lanes // 2], o[:, :, lanes // 2:]], axis=1)
    else:
        o_ref[...] = o

    @pl.when(ti == pl.num_programs(1) - 1)
    def _():
        if state_rows:
            sout_ref[...] = s_ref[...].reshape(nv * kc, lanes).T.reshape(sout_ref.shape)
        else:
            sout_ref[...] = s_ref[...]


def _rwkv_scan(r, w, k, a, v, kk_p, ka_p, rk_p, lnw, lnb, s0, *, t, tc, vsplit, layer=None, after=None):
    _, kc, kl = r.shape
    nv = v.shape[1] // 2 if vsplit else v.shape[1]
    ln = v.shape[2] * 2 if vsplit else v.shape[2]
    g = ln // LANES
    kspec = pl.BlockSpec((tc, kc, kl // g), lambda gi, ti: (ti, 0, gi))
    vspec = pl.BlockSpec((tc, nv, LANES), lambda gi, ti: (ti, 0, gi))
    pk = pl.BlockSpec((kc, LANES), lambda gi, ti: (0, 0))
    pv = pl.BlockSpec((nv, LANES), lambda gi, ti: (0, 0))
    state_rows = not vsplit
    operands = [r, w, k, a, v, kk_p, ka_p, rk_p, lnw, lnb, s0]
    in_specs = [kspec, kspec, kspec, kspec, kspec if vsplit else vspec, pk, pk, pk, pv, pv]
    aliases = {}
    if state_rows:
        nb, _, nh, sz = s0.shape
        assert sz == nv * kc and nb * nh == ln
        sspec = pl.BlockSpec((LANES // nh, None, nh, sz), lambda gi, ti: (gi, layer, 0, 0))
        in_specs.append(sspec)
        aliases = {10: 1}
        s_shape = s0.shape
    else:
        sspec = pl.BlockSpec((nv, kc, LANES), lambda gi, ti: (0, 0, gi))
        in_specs.append(sspec)
        s_shape = (nv, kc, ln)
    if after is not None:
        operands.append(after)
        in_specs.append(pl.BlockSpec(memory_space=pl.ANY))

    def body(*refs):
        n_in = len(operands)
        ins = refs[:11]
        _rwkv_scan_body(*ins, *refs[n_in:], vsplit=vsplit, state_rows=state_rows)

    return pl.pallas_call(
        body,
        grid=(g, t // tc),
        in_specs=in_specs,
        out_specs=[kspec if vsplit else vspec, sspec],
        out_shape=[jax.ShapeDtypeStruct((t, kc, kl) if vsplit else (t, nv, ln), F32),
                   jax.ShapeDtypeStruct(s_shape, F32)],
        scratch_shapes=[pltpu.VMEM((nv, kc, LANES), F32)] + [pltpu.VMEM((tc, kc, LANES), F32)] * 5
        + [pltpu.VMEM((tc, nv, LANES), F32)] * 2,
        input_output_aliases=aliases,
        compiler_params=_cparams("parallel", "arbitrary"),
        name="rwkv_scan",
    )(*operands)


def _ref_rows(gc, j):
    c, w = gc.shape
    b = 1 << j
    nt = c // SUBLANES
    gv = gc.reshape(nt, SUBLANES, w)
    if 2 * b <= SUBLANES:
        sub = lax.broadcasted_iota(jnp.int32, (1, SUBLANES, 1), 1)
        out = None
        for g0 in range(0, SUBLANES, 2 * b):
            piece = jnp.broadcast_to(gv[:, g0 + b - 1:g0 + b, :], gv.shape)
            out = piece if out is None else jnp.where(sub >= g0, piece, out)
        return out.reshape(c, w)
    tiles_per_group = 2 * b // SUBLANES
    pieces = []
    for ti in range(nt):
        src = (ti // tiles_per_group) * tiles_per_group + b // SUBLANES - 1
        pieces.append(jnp.broadcast_to(gv[src, SUBLANES - 1:SUBLANES, :], (SUBLANES, w)))
    return jnp.concatenate(pieces, axis=0)


def _hgrn_chunk(q, f, val, og, get_st, set_st, loglb, log1mlb, nrm_w, seq_len):
    c, width = q.shape
    nh = width // HG_DIM
    nseq = c // seq_len
    lg = seq_len.bit_length() - 1
    assert (1 << lg) == seq_len and c % SUBLANES == 0
    ls = jnp.minimum(f, 0.0) - jnp.log1p(jnp.exp(-jnp.abs(f)))
    b_ = log1mlb + ls
    g = jnp.maximum(loglb, b_) + jnp.log1p(jnp.exp(-jnp.abs(loglb - b_)))
    kk = _neg_expm1(g)
    qs = q * jax.nn.sigmoid(q)
    row = lax.broadcasted_iota(jnp.int32, (c, 1), 0)
    col = lax.broadcasted_iota(jnp.int32, (1, c), 1)
    tpos = row & (seq_len - 1)
    gc = g
    for j in range(lg):
        s = 1 << j
        gc = gc + jnp.where(tpos >= s, pltpu.roll(gc, s, 0), 0.0)
    qbs, kbs, sames = [qs.astype(BF16)], [kk.astype(BF16)], [row == col]
    for j in range(lg):
        e = jnp.exp(-jnp.abs(gc - _ref_rows(gc, j)))
        is_q = ((row >> j) & 1) == 1
        pk = (jnp.where(is_q, qs, kk) * e).astype(BF16)
        qbs.append(pk)
        kbs.append(pk)
        sames.append(((row >> (j + 1)) == (col >> (j + 1))) & is_q & (((col >> j) & 1) == 0))
    vb = val.astype(BF16)
    qg = (qs * jnp.exp(gc)).astype(BF16)
    k2s, eglast, in_seq = [], [], []
    for s_i in range(nseq):
        glast = gc[s_i * seq_len + seq_len - 1:s_i * seq_len + seq_len, :]
        eglast.append(jnp.exp(glast))
        if nseq == 1:
            in_seq.append(None)
            k2s.append((kk * jnp.exp(glast - gc)).astype(BF16))
        else:
            in_s = (row >> lg) == s_i
            in_seq.append(in_s)
            k2s.append(jnp.where(in_s, kk * jnp.exp(jnp.minimum(glast - gc, 0.0)), 0.0).astype(BF16))
    outs = []
    for h in range(nh):
        cs = slice(h * HG_DIM, (h + 1) * HG_DIM)
        amat = None
        for qb, kb, same in zip(qbs, kbs, sames):
            term = jnp.where(same, _dot_nt(qb[:, cs], kb[:, cs]), 0.0)
            amat = term if amat is None else amat + term
        o = _dot(amat.astype(BF16), vb[:, cs])
        for s_i in range(nseq):
            st = get_st(s_i, h)
            o_int = _dot_nt(qg[:, cs], st.astype(BF16))
            o = o + (o_int if nseq == 1 else jnp.where(in_seq[s_i], o_int, 0.0))
            set_st(s_i, h, st * eglast[s_i][:, cs] + _dot_tn(vb[:, cs], k2s[s_i][:, cs]))
        outs.append(o * lax.rsqrt(jnp.mean(o * o, axis=-1, keepdims=True) + HG_NORM_EPS))
    on = jnp.concatenate(outs, axis=1) * nrm_w
    return on * (og * jax.nn.sigmoid(og))


def _hgrn_prompt_body(q_ref, f_ref, i_ref, og_ref, s0_ref, loglb_ref, log1m_ref, nw_ref, o_ref, sout_ref, st_ref,
                      *, chunk):
    nh = st_ref.shape[0]
    tb = pl.program_id(1)

    @pl.when(tb == 0)
    def _():
        for h in range(nh):
            st_ref[h] = s0_ref[h].T

    def set_st(s_i, h, v):
        st_ref[h] = v

    def body(ci, carry):
        rows = pl.ds(pl.multiple_of(ci * chunk, chunk), chunk)
        y = _hgrn_chunk(q_ref[rows, :], f_ref[rows, :], i_ref[rows, :], og_ref[rows, :],
                        lambda s_i, h: st_ref[h], set_st, loglb_ref[...], log1m_ref[...], nw_ref[...], chunk)
        o_ref[rows, :] = y.astype(BF16)
        return carry

    lax.fori_loop(0, q_ref.shape[0] // chunk, body, 0)

    @pl.when(tb == pl.num_programs(1) - 1)
    def _():
        for h in range(nh):
            sout_ref[h] = st_ref[h].T


def _hgrn_prompt(p_hg, s0, loglb, log1m, nw, *, bp, tp, chunk):
    hgw = p_hg.shape[1] // 4
    nh = hgw // HG_DIM
    tb = _pick_block(tp, 512, chunk)
    nt = tp // tb
    cspec = lambda c: pl.BlockSpec((tb, hgw), lambda b, t: (b * nt + t, c))
    pspec = pl.BlockSpec((1, hgw), lambda b, t: (0, 0))
    sspec = pl.BlockSpec((None, nh, HG_DIM, HG_DIM), lambda b, t: (b, 0, 0, 0))
    return pl.pallas_call(
        functools.partial(_hgrn_prompt_body, chunk=chunk),
        grid=(bp, nt),
        in_specs=[cspec(0), cspec(1), cspec(2), cspec(3), sspec, pspec, pspec, pspec],
        out_specs=[pl.BlockSpec((tb, hgw), lambda b, t: (b * nt + t, 0)), sspec],
        out_shape=[jax.ShapeDtypeStruct((bp * tp, hgw), BF16), jax.ShapeDtypeStruct(s0.shape, F32)],
        scratch_shapes=[pltpu.VMEM((nh, HG_DIM, HG_DIM), F32)],
        compiler_params=_cparams("parallel", "arbitrary"),
        name="hgrn_prompt",
    )(p_hg, p_hg, p_hg, p_hg, s0, loglb, log1m, nw)


def _hgrn_sample_body(q_ref, f_ref, i_ref, og_ref, s0_ref, loglb_ref, log1m_ref, nw_ref, *rest, seq_len, nseq):
    o_ref, sout_ref = rest[-2:]
    c = seq_len * nseq

    def body(gi, carry):
        rows = pl.ds(pl.multiple_of(gi * c, c), c)

        def set_st(s_i, h, v):
            sout_ref[gi * nseq + s_i, h] = v.T

        y = _hgrn_chunk(q_ref[rows, :], f_ref[rows, :], i_ref[rows, :], og_ref[rows, :],
                        lambda s_i, h: s0_ref[gi * nseq + s_i, h].T, set_st,
                        loglb_ref[...], log1m_ref[...], nw_ref[...], seq_len)
        o_ref[rows, :] = y.astype(BF16)
        return carry

    lax.fori_loop(0, q_ref.shape[0] // c, body, 0)


def _hgrn_sample(p_hg_bm, s0, layer, loglb, log1m, nw, *, bs, ts, s_prev=None, after=None):
    hgw = p_hg_bm.shape[1] // 4
    nh = hgw // HG_DIM
    nseq = max(1, 16 // ts)
    assert bs % nseq == 0
    bb = _pick_block(bs, 8, nseq)
    rows = bb * ts
    cspec = lambda c: pl.BlockSpec((rows, hgw), lambda i: (i, c))
    pspec = pl.BlockSpec((1, hgw), lambda i: (0, 0))
    sspec = pl.BlockSpec((bb, None, nh, HG_DIM, HG_DIM), lambda i: (i, layer, 0, 0, 0))
    operands = [p_hg_bm, p_hg_bm, p_hg_bm, p_hg_bm, s0, loglb, log1m, nw]
    in_specs = [cspec(0), cspec(1), cspec(2), cspec(3), sspec, pspec, pspec, pspec]
    aliases = {}
    if s_prev is not None:
        operands.append(s_prev)
        in_specs.append(pl.BlockSpec(memory_space=pl.ANY))
        aliases = {len(operands) - 1: 1}
    if after is not None:
        operands.append(after)
        in_specs.append(pl.BlockSpec(memory_space=pl.ANY))
    return pl.pallas_call(
        functools.partial(_hgrn_sample_body, seq_len=ts, nseq=nseq),
        grid=(bs // bb,),
        in_specs=in_specs,
        out_specs=[pl.BlockSpec((rows, hgw), lambda i: (i, 0)), sspec],
        out_shape=[jax.ShapeDtypeStruct((bs * ts, hgw), BF16), jax.ShapeDtypeStruct(s0.shape, F32)],
        input_output_aliases=aliases,
        compiler_params=_cparams("parallel"),
        name="hgrn_sample",
    )(*operands)


def _lru_gates(xconv, wa_ref, wx_ref, ba, bx, lam):
    l = xconv.shape[1]
    blk = l // LRU_BLOCKS
    ga, gx = [], []
    for n in range(LRU_BLOCKS):
        xb = xconv[:, n * blk:(n + 1) * blk].astype(BF16)
        ga.append(_dot(xb, wa_ref[n]))
        gx.append(_dot(xb, wx_ref[n]))
    ga = jnp.concatenate(ga, axis=1) + ba
    gx = jnp.concatenate(gx, axis=1) + bx
    log_a = -LRU_C * jax.nn.sigmoid(ga) * _softplus(-lam)
    a = jnp.exp(log_a)
    mult = jnp.sqrt(1.0 - a * a)
    return a, mult * jax.nn.sigmoid(gx) * xconv


def _gelu_tanh(x):
    c = 0.7978845608028654
    return 0.5 * x * (1.0 + jnp.tanh(c * (x + 0.044715 * (x * x * x))))


def _odd_prompt_body(gate_ref, xb_ref, conv0_ref, h0_ref, cw_ref, cb_ref, wa_ref, wx_ref, ba_ref, bx_ref, lam_ref,
                     y_ref, convo_ref, ho_ref, ext_ref, hc_ref, a_s, b_s, h_s):
    rows = xb_ref.shape[0]
    hist = SUBLANES

    @pl.when(pl.program_id(1) == 0)
    def _():
        ext_ref[0:hist, :] = conv0_ref[...]
        hc_ref[...] = h0_ref[...]

    ext_ref[hist:hist + rows, :] = xb_ref[...]
    cw = cw_ref[...]
    xconv = cw[0:1, :] * ext_ref[pl.ds(hist - (CONV_W - 1), rows), :]
    for j in range(1, CONV_W):
        xconv = xconv + cw[j:j + 1, :] * ext_ref[pl.ds(hist - (CONV_W - 1) + j, rows), :]
    xconv = cb_ref[...] + xconv
    a, b = _lru_gates(xconv, wa_ref, wx_ref, ba_ref[...], bx_ref[...], lam_ref[...])
    width = a.shape[1]
    a = a.reshape(rows // SUBLANES, SUBLANES, width)
    b = b.reshape(rows // SUBLANES, SUBLANES, width)
    r8 = lax.broadcasted_iota(jnp.int32, (1, SUBLANES, 1), 1)
    for s in (1, 2, 4):
        m = r8 >= s
        a_sh = pltpu.roll(a, s, 1)
        b_sh = pltpu.roll(b, s, 1)
        b = jnp.where(m, a * b_sh + b, b)
        a = jnp.where(m, a * a_sh, a)
    a_s[...] = a.reshape(rows, width)
    b_s[...] = b.reshape(rows, width)

    def tile(j, hc):
        r0 = pl.multiple_of(j * SUBLANES, SUBLANES)
        h = a_s[pl.ds(r0, SUBLANES), :] * hc + b_s[pl.ds(r0, SUBLANES), :]
        h_s[pl.ds(r0, SUBLANES), :] = h
        return h[SUBLANES - 1:SUBLANES, :]

    hc = lax.fori_loop(0, rows // SUBLANES, tile, hc_ref[...])
    hc_ref[...] = hc
    ho_ref[...] = hc
    y_ref[...] = (h_s[...] * _gelu_tanh(gate_ref[...])).astype(BF16)
    tail = ext_ref[rows:rows + hist, :]
    ext_ref[0:hist, :] = tail
    convo_ref[...] = tail


def _odd_prompt(p_od, conv0, h0, layer, cw, cb, wa, wx, ba, bx, lam, *, bp, tp):
    l = p_od.shape[1] // 2
    rows = _pick_block(tp, 256, SUBLANES)
    nt = tp // rows
    lsel = lambda a: pl.BlockSpec((None,) + a.shape[1:], lambda b, t: (layer,) + (0,) * (a.ndim - 1))
    return pl.pallas_call(
        _odd_prompt_body,
        grid=(bp, nt),
        in_specs=[
            pl.BlockSpec((rows, l), lambda b, t: (b * nt + t, 0)),
            pl.BlockSpec((rows, l), lambda b, t: (b * nt + t, 1)),
            pl.BlockSpec((None, SUBLANES, l), lambda b, t: (b, 0, 0)),
            pl.BlockSpec((None, 1, l), lambda b, t: (b, 0, 0)),
            lsel(cw), lsel(cb), lsel(wa), lsel(wx), lsel(ba), lsel(bx), lsel(lam),
        ],
        out_specs=[
            pl.BlockSpec((rows, l), lambda b, t: (b * nt + t, 0)),
            pl.BlockSpec((None, SUBLANES, l), lambda b, t: (b, 0, 0)),
            pl.BlockSpec((None, 1, l), lambda b, t: (b, 0, 0)),
        ],
        out_shape=[jax.ShapeDtypeStruct((bp * tp, l), BF16), jax.ShapeDtypeStruct((bp, SUBLANES, l), F32),
                   jax.ShapeDtypeStruct((bp, 1, l), F32)],
        scratch_shapes=[pltpu.VMEM((rows + SUBLANES, l), F32), pltpu.VMEM((1, l), F32),
                        pltpu.VMEM((rows, l), F32), pltpu.VMEM((rows, l), F32), pltpu.VMEM((rows, l), F32)],
        compiler_params=_cparams("parallel", "arbitrary"),
        name="odd_prompt",
    )(p_od, p_od, conv0, h0, cw, cb, wa, wx, ba, bx, lam)


def _odd_sample_body(gate_ref, xb_ref, conv0_ref, h0_ref, cw_ref, cb_ref, wa_ref, wx_ref, ba_ref, bx_ref, lam_ref,
                     y_ref, convo_ref, ho_ref):
    @pl.when(pl.program_id(0) == 0)
    def _():
        convo_ref[...] = conv0_ref[...]
        ho_ref[...] = h0_ref[...]

    xb = xb_ref[...]
    cw = cw_ref[...]
    xconv = cw[CONV_W - 1:CONV_W, :] * xb
    for j in range(CONV_W - 1):
        xconv = xconv + cw[j:j + 1, :] * convo_ref[j]
    xconv = cb_ref[...] + xconv
    a, b = _lru_gates(xconv, wa_ref, wx_ref, ba_ref[...], bx_ref[...], lam_ref[...])
    h = a * ho_ref[...] + b
    ho_ref[...] = h
    y_ref[...] = (h * _gelu_tanh(gate_ref[...])).astype(BF16)
    for j in range(CONV_W - 2):
        convo_ref[j] = convo_ref[j + 1]
    convo_ref[CONV_W - 2] = xb


def _odd_sample(p_od, conv0_t, h0, layer, cw, cb, wa, wx, ba, bx, lam, *, mp, bs, ts):
    l = p_od.shape[1] // 2
    assert mp % bs == 0
    off = mp // bs
    lsel = lambda a: pl.BlockSpec((None,) + a.shape[1:], lambda t: (layer,) + (0,) * (a.ndim - 1))
    full = lambda a: pl.BlockSpec(a.shape, lambda t: (0,) * a.ndim)
    return pl.pallas_call(
        _odd_sample_body,
        grid=(ts,),
        in_specs=[
            pl.BlockSpec((bs, l), lambda t: (off + t, 0)),
            pl.BlockSpec((bs, l), lambda t: (off + t, 1)),
            full(conv0_t), full(h0),
            lsel(cw), lsel(cb), lsel(wa), lsel(wx), lsel(ba), lsel(bx), lsel(lam),
        ],
        out_specs=[pl.BlockSpec((bs, l), lambda t: (t, 0)), full(conv0_t), full(h0)],
        out_shape=[jax.ShapeDtypeStruct((bs * ts, l), BF16), jax.ShapeDtypeStruct(conv0_t.shape, F32),
                   jax.ShapeDtypeStruct(h0.shape, F32)],
        compiler_params=_cparams("arbitrary"),
        name="odd_sample",
    )(p_od, p_od, conv0_t, h0, cw, cb, wa, wx, ba, bx, lam)


def _pad_last(a, n):
    return jnp.pad(a, [(0, 0)] * (a.ndim - 1) + [(0, n - a.shape[-1])])


def _round_up(n, m):
    return (n + m - 1) // m * m


class _RwLayout:
    def __init__(self, rw, dw, da, dg):
        self.rw, self.dw, self.da, self.dg = rw, dw, da, dg
        self.dwp, self.dap, self.dgp = (_round_up(d, LANES) for d in (dw, da, dg))
        self.width = 3 * rw + dw + da + dg
        self.padded = 3 * rw + self.dwp + self.dap + self.dgp

    def pad(self, a):
        o = 3 * self.rw
        parts = [a[..., :o], _pad_last(a[..., o:o + self.dw], self.dwp),
                 _pad_last(a[..., o + self.dw:o + self.dw + self.da], self.dap),
                 _pad_last(a[..., o + self.dw + self.da:self.width], self.dgp)]
        return jnp.concatenate(parts, axis=-1)

    def unpad(self, a):
        o = 3 * self.rw
        parts = [a[..., :o], a[..., o:o + self.dw], a[..., o + self.dwp:o + self.dwp + self.da],
                 a[..., o + self.dwp + self.dap:o + self.dwp + self.dap + self.dg]]
        return jnp.concatenate(parts, axis=-1)


def kernel(x_prompt, x_sample, state_rwkv_shift, state_rwkv, state_hgrn, state_conv, state_lru, ffn1_norm, ffn1_w_gu, ffn1_w_down, mix_norm, ffn2_norm, ffn2_w_gu, ffn2_w_down, ev_w_in, rw_mu, rw_w0, rw_w2, rw_a0, rw_a2, rw_g2, rw_k_k, rw_k_a, rw_r_k, rw_ln_w, rw_ln_b, hg_lb, hg_norm, ev_w_out, od_w_in, conv_w, conv_b, lru_wa, lru_ba, lru_wx, lru_bx, lru_lambda, od_w_out, final_norm):
    bp, tp, d = x_prompt.shape
    bs, ts, _ = x_sample.shape
    depth = ffn1_norm.shape[0]
    n_even = ev_w_in.shape[0]
    mp, ms = bp * tp, bs * ts
    rw = rw_w0.shape[1]
    nh_rw = rw // RW_HEAD
    hgw = hg_norm.shape[1]
    nh_hg = hgw // HG_DIM
    lay = _RwLayout(rw, rw_w2.shape[1], rw_a2.shape[1], rw_g2.shape[1])
    assert bp * nh_rw * 2 == LANES and (bs * nh_rw) % LANES == 0
    dt = x_prompt.dtype

    ffn_src = {1: (ffn1_norm, ffn1_w_gu, ffn1_w_down), 2: (ffn2_norm, ffn2_w_gu, ffn2_w_down)}
    ffn_w = (ffn1_w_gu[0].astype(BF16), ffn1_w_down[0].astype(BF16))

    def ffn_step(x, l, which, ffn_w):
        norm_g = ffn_src[which][0]
        nxt = (l, 2) if which == 1 else (l + 1, 1)
        if nxt[0] >= depth:
            return _ffn(x, norm_g, l, *ffn_w)[0], None
        x, c_gu, c_dn = _ffn(x, norm_g, l, *ffn_w, next_src=(ffn_src[nxt[1]][1], ffn_src[nxt[1]][2], nxt[0]))
        return x, (c_gu, c_dn)

    ev_out = ev_w_out.astype(BF16)
    od_in, od_out = od_w_in.astype(BF16), od_w_out.astype(BF16)

    wa_bf, wx_bf = lru_wa.astype(BF16), lru_wx.astype(BF16)

    x = jnp.concatenate([x_prompt.reshape(mp, d), x_sample.transpose(1, 0, 2).reshape(ms, d)], axis=0)

    lbs = jnp.cumsum(jax.nn.softmax(hg_lb.astype(F32), axis=0), axis=0)
    lb_all = lbs - lbs[0]

    def k_lanes_p(a):
        return a.reshape(bp * nh_rw, RW_HEAD, a.shape[1]).transpose(2, 1, 0)

    def k_lanes_p_inv(a):
        return a.transpose(2, 1, 0).reshape(bp, rw, tp)

    def k_lanes_s(a):
        a = a.reshape(bp, nh_rw, RW_HEAD, ts // bp, bs).transpose(3, 0, 2, 4, 1)
        return a.reshape(ts, RW_HEAD, bs * nh_rw)

    def k_lanes_s_inv(a):
        return a.reshape(ts, RW_HEAD, bs, nh_rw).transpose(0, 2, 3, 1).reshape(ms, rw)

    def kparam(p):
        return jnp.tile(p.reshape(nh_rw, RW_HEAD).T, (1, LANES // nh_rw))

    def vparam_p(p):
        a = p.reshape(nh_rw, 2, RW_HEAD // 2).transpose(2, 1, 0)[:, :, None, :]
        return jnp.broadcast_to(a, (RW_HEAD // 2, 2, bp, nh_rw)).reshape(RW_HEAD // 2, LANES)

    shifts_p, shifts_s, rws_p, hgs_p, convs_p, convs_s, lrus_p, lrus_s = ([] for _ in range(8))
    rw_rows = state_rwkv.astype(F32).reshape(bs, n_even, nh_rw, RW_HEAD * RW_HEAD)
    hg_new = jnp.zeros(state_hgrn.shape, F32)

    for l in range(depth):
        x, ffn_w = ffn_step(x, l, 1, ffn_w)
        if l % 2 == 0:
            e = l // 2
            w_in = ev_w_in[e]
            w_rw = lay.pad(w_in[:, :lay.width]).astype(BF16)
            w_hg = w_in[:, lay.width:].astype(BF16)
            p_rw = _norm_proj(x, mix_norm, l, w_rw, lay.padded)
            p_hg = _norm_proj(x, mix_norm, l, w_hg, 4 * hgw)
            shift_p = jnp.zeros((bp, lay.padded), F32)
            shift_s = lay.pad(state_rwkv_shift[:, e].astype(F32))
            rkvwa, g_ = _rwkv_prep(
                p_rw, shift_p, shift_s, lay.pad(rw_mu[e])[None], rw_w0[e][None],
                _pad_last(rw_w2[e].T, lay.dwp).T.astype(BF16), rw_a0[e][None],
                _pad_last(rw_a2[e].T, lay.dap).T.astype(BF16),
                _pad_last(rw_g2[e].T, lay.dgp).T.astype(BF16), rw=rw, mp=mp, tp=tp, bs=bs)
            shifts_p.append(lay.unpad(jnp.concatenate([p_rw[b * tp + tp - 1:(b + 1) * tp] for b in range(bp)])))
            shifts_s.append(lay.unpad(p_rw[mp + (ts - 1) * bs:]))
            kk_p, ka_p = kparam(rw_k_k[e]), kparam(rw_k_a[e])
            rk_p = kparam(rw_r_k[e].reshape(rw))
            lb = lb_all[e][None]
            loglb, log1m = jnp.log(lb), jnp.log1p(-lb)
            nw = hg_norm[e][None]
            chunk = _pick_block(tp, HG_CHUNK, 1)
            oh_p, hs_p = _hgrn_prompt(p_hg, jnp.zeros((bp, nh_hg, HG_DIM, HG_DIM), F32), loglb, log1m, nw,
                                      bp=bp, tp=tp, chunk=chunk)
            r_p, k_p, v_p, w_p, a_p = rkvwa
            o_p, s_p = _rwkv_scan(
                k_lanes_p(r_p), k_lanes_p(w_p), k_lanes_p(k_p), k_lanes_p(a_p), k_lanes_p(v_p),
                kk_p, ka_p, rk_p, vparam_p(rw_ln_w[e]), vparam_p(rw_ln_b[e]),
                jnp.zeros((RW_HEAD // 2, RW_HEAD, LANES), F32), t=tp, tc=_pick_block(tp, 64, 1), vsplit=True,
                after=oh_p)
            rws_p.append(s_p.reshape(RW_HEAD // 2, RW_HEAD, 2, bp, nh_rw).transpose(3, 4, 2, 0, 1)
                         .reshape(bp, nh_rw, RW_HEAD, RW_HEAD))
            r_s, k_s, v_s, w_s, a_s = (a[:, tp:] for a in rkvwa)
            o_s, rw_rows = _rwkv_scan(
                k_lanes_s(r_s), k_lanes_s(w_s), k_lanes_s(k_s), k_lanes_s(a_s), k_lanes_s(v_s),
                kk_p, ka_p, rk_p, kparam(rw_ln_w[e]), kparam(rw_ln_b[e]),
                rw_rows, t=ts, tc=ts, vsplit=False, layer=e, after=o_p)
            o_pt, o_st = k_lanes_p_inv(o_p), k_lanes_s_inv(o_s)
            p_hg_bm = p_hg[mp:].reshape(ts, bs, 4 * hgw).transpose(1, 0, 2).reshape(ms, 4 * hgw)
            oh_s, hg_new = _hgrn_sample(p_hg_bm, state_hgrn.astype(F32), e, loglb, log1m, nw, bs=bs, ts=ts,
                                        s_prev=hg_new, after=o_p)
            oh_s = oh_s.reshape(bs, ts, hgw).transpose(1, 0, 2).reshape(ms, hgw)
            hgs_p.append(hs_p)
            x = _proj_res_even(x, o_pt, o_st, g_, oh_p, oh_s, ev_out, e)
        else:
            o_i = l // 2
            p_od = _norm_proj(x, mix_norm, l, od_in, od_in.shape[-1], w_layer=o_i)
            lsz = p_od.shape[1] // 2
            vec = lambda a: a[:, None, :]
            args = (conv_w, vec(conv_b), wa_bf, wx_bf, vec(lru_ba), vec(lru_bx), vec(lru_lambda))
            conv0_p = jnp.zeros((bp, SUBLANES, lsz), F32)
            y_p, c_p, h_p = _odd_prompt(p_od, conv0_p, jnp.zeros((bp, 1, lsz), F32), o_i, *args, bp=bp, tp=tp)
            conv0_s = state_conv[:, o_i].astype(F32).transpose(1, 0, 2)
            y_s, c_s, h_s = _odd_sample(p_od, conv0_s, state_lru[:, o_i].astype(F32), o_i, *args,
                                        mp=mp, bs=bs, ts=ts)
            convs_p.append(c_p[:, SUBLANES - (CONV_W - 1):])
            convs_s.append(c_s.transpose(1, 0, 2))
            lrus_p.append(h_p[:, 0])
            lrus_s.append(h_s)
            x = _proj_res(x, y_p, y_s, od_out, o_i)
        x, ffn_w = ffn_step(x, l, 2, ffn_w)

    y_p, y_s = _final_norm(x, final_norm, mp)
    y_prompt = y_p.reshape(bp, tp, d)
    y_sample = y_s.reshape(ts, bs, d).transpose(1, 0, 2)
    st = lambda xs: jnp.stack(xs, axis=1).astype(dt)
    sample_rwkv = rw_rows.reshape(state_rwkv.shape).astype(dt)
    return (y_prompt, y_sample, st(shifts_p), st(rws_p), st(hgs_p), st(convs_p), st(lrus_p),
            st(shifts_s), sample_rwkv, hg_new.astype(dt), st(convs_s), st(lrus_s))
```
